```python
import math
import jax, jax.numpy as jnp
from jax import lax
import numpy as np

D_MODEL = 1024
BATCH = 8
SEQ = 2048
DEPTH = 2
DEC_BATCH = 128
DEC_SEQ = 4
PAST_LEN = 16384
PAGE_SIZE = 128

H_A = 4
HD_A = 128
D_A = H_A * HD_A
CONV_W = 4
GDN_CHUNK = 64
H_B = 4
DH_B = 64
D_B = H_B * DH_B
SGU_CHUNK = 128
H_C = 4
N_C = 64
D_C = H_C * N_C
RANK_W = 64
RANK_A = 64
RANK_G = 128
D_MIX = D_A + D_B + D_C
AB_SPLITS = (D_A, D_A, D_A, D_A, H_A, H_A, D_B, D_B)
C_SPLITS = (D_C, D_C, D_C, RANK_W, RANK_A, RANK_G)
C_OFF = sum(AB_SPLITS)
N_C_IN = sum(C_SPLITS)
N_IN = C_OFF + N_C_IN
N_GROUPS = 4
EXPERTS_PER_GROUP = 4
N_EXPERTS = N_GROUPS * EXPERTS_PER_GROUP
TOP_K_IN_GROUP = 2
D_FF_E = 512
NORM_EPS = 1e-6
LN_EPS = 1e-5
GN_EPS = 64e-5

kernel_name = "hymba_style_gdn_sgu_rwkv7_hmoe_step"


def rmsnorm(x, w):
    x32 = x.astype(jnp.float32)
    y = x32 * lax.rsqrt(jnp.mean(x32 * x32, axis=-1, keepdims=True) + NORM_EPS)
    return (y * w.astype(jnp.float32)).astype(x.dtype)


def l2norm(x):
    return x * lax.rsqrt(jnp.sum(x * x, axis=-1, keepdims=True) + 1e-6)


def split_cols(x, sizes):
    return jnp.split(x, np.cumsum(sizes)[:-1].tolist(), axis=-1)


def causal_short_conv(x, buf, w):
    L = x.shape[1]
    xp = jnp.concatenate([buf.astype(x.dtype), x], axis=1)
    y = xp[:, 0:L] * w[0]
    for j in range(1, CONV_W):
        y = y + xp[:, j:j + L] * w[j]
    return jax.nn.silu(y), xp[:, L:]


def gated_delta_chunked(q, k, v, g, beta, s0):
    b, L, h, dk = q.shape
    dv = v.shape[-1]
    C = GDN_CHUNK
    n = -(-L // C)
    pad = n * C - L

    def blocks(t):
        t = jnp.pad(t, [(0, 0), (0, pad)] + [(0, 0)] * (t.ndim - 2))
        t = t.reshape((b, n, C) + t.shape[2:])
        return jnp.moveaxis(t, 3, 1)

    q = blocks(q) * (dk ** -0.5)
    k, v, g, beta = blocks(k), blocks(v), blocks(g), blocks(beta)
    gc = jnp.cumsum(g, axis=-1)
    idx = jnp.arange(C)
    causal = idx[:, None] >= idx[None, :]
    strict = idx[:, None] > idx[None, :]
    diff = gc[..., :, None] - gc[..., None, :]
    decay = jnp.where(causal, jnp.exp(jnp.where(causal, diff, 0.0)), 0.0)
    k_beta = k * beta[..., None]
    lower = jnp.where(strict, jnp.einsum('bhnid,bhnjd->bhnij', k_beta, k) * decay, 0.0)
    eye = jnp.eye(C, dtype=q.dtype)
    rhs = jnp.concatenate([v * beta[..., None], k_beta * jnp.exp(gc)[..., None]], axis=-1)
    sol = lax.linalg.triangular_solve(eye + lower, rhs, left_side=True, lower=True,
                                      unit_diagonal=True)
    u, w = sol[..., :dv], sol[..., dv:]
    attn = jnp.einsum('bhnid,bhnjd->bhnij', q, k) * decay
    q_dec = q * jnp.exp(gc)[..., None]
    k_dec = k * jnp.exp(gc[..., -1:] - gc)[..., None]
    g_last = jnp.exp(gc[..., -1])
    xs = tuple(jnp.moveaxis(t, 2, 0) for t in (u, w, attn, q_dec, k_dec, g_last))

    def step(s, inp):
        u_n, w_n, a_n, qd_n, kd_n, gl_n = inp
        v_new = u_n - jnp.einsum('bhck,bhkv->bhcv', w_n, s)
        o_n = jnp.einsum('bhck,bhkv->bhcv', qd_n, s) + jnp.einsum('bhij,bhjv->bhiv', a_n, v_new)
        s = s * gl_n[..., None, None] + jnp.einsum('bhck,bhcv->bhkv', kd_n, v_new)
        return s, o_n

    s_final, o = lax.scan(step, s0, xs)
    o = jnp.transpose(o, (1, 0, 3, 2, 4)).reshape(b, n * C, h, dv)[:, :L]
    return o, s_final


def rwkv7_recurrence(r, w, k, v, kk, a, s0):
    def step(s, inp):
        r_t, w_t, k_t, v_t, kk_t, a_t = inp
        sk = jnp.einsum('bhvk,bhk->bhv', s, kk_t)
        s = (s * w_t[:, :, None, :] - sk[..., None] * (kk_t * a_t)[:, :, None, :]
             + v_t[..., None] * k_t[:, :, None, :])
        return s, jnp.einsum('bhvk,bhk->bhv', s, r_t)

    xs = tuple(jnp.moveaxis(t, 1, 0) for t in (r, w, k, v, kk, a))
    s, y = lax.scan(step, s0, xs)
    return jnp.moveaxis(y, 0, 1), s


def sgu_mix(v, w_s, b_s):
    b, L, hh, d = v.shape
    n = -(-L // SGU_CHUNK)
    vp = jnp.pad(v, ((0, 0), (0, n * SGU_CHUNK - L), (0, 0), (0, 0))).reshape(b, n, SGU_CHUNK, hh, d)
    t = jnp.arange(SGU_CHUNK)
    wm = jnp.where(t[:, None] >= t[None, :], w_s.astype(jnp.float32), 0.0)
    out = jnp.einsum('hts,bnshd->bnthd', wm, vp) + jnp.transpose(b_s.astype(jnp.float32))[:, :, None]
    return out.reshape(b, n * SGU_CHUNK, hh, d)[:, :L]


def token_mixers(h, p, s_gdn, conv_buf, s_rwkv, h_prev):
    f32 = jnp.float32
    b, L, _ = h.shape
    proj = h @ p['w_in']
    qa, ka, va, za, ba, aa, ub, vb = split_cols(proj[..., :C_OFF], AB_SPLITS)

    qkv, new_conv = causal_short_conv(jnp.concatenate([qa, ka, va], axis=-1), conv_buf, p['gdn_conv_w'])
    q, k, v = (t.astype(f32).reshape(b, L, H_A, HD_A) for t in jnp.split(qkv, 3, axis=-1))
    q, k = l2norm(q), l2norm(k)
    beta = jax.nn.sigmoid(ba.astype(f32))
    g = -jnp.exp(p['gdn_a_log'].astype(f32)) * jax.nn.softplus(aa.astype(f32) + p['gdn_dt_bias'].astype(f32))
    o, new_gdn = gated_delta_chunked(q, k, v, g, beta, s_gdn.astype(f32))
    o = o * lax.rsqrt(jnp.mean(o * o, axis=-1, keepdims=True) + NORM_EPS) * p['gdn_norm_w'].astype(f32)
    o_a = (o * jax.nn.silu(za.astype(f32).reshape(b, L, H_A, HD_A))).reshape(b, L, D_A)

    u = jax.nn.gelu(ub.astype(f32), approximate=False)
    vs = jax.nn.gelu(vb.astype(f32), approximate=False).reshape(b, L, H_B, DH_B)
    mu_v = jnp.mean(vs, axis=-1, keepdims=True)
    var_v = jnp.mean(jnp.square(vs - mu_v), axis=-1, keepdims=True)
    vs = ((vs - mu_v) * lax.rsqrt(var_v + LN_EPS) * p['sgu_ln_w'].astype(f32).reshape(H_B, DH_B)
          + p['sgu_ln_b'].astype(f32).reshape(H_B, DH_B))
    o_b = u * sgu_mix(vs, p['sgu_w'], p['sgu_b']).reshape(b, L, D_B)

    pc = proj[..., C_OFF:]
    pc_first = (h_prev.astype(h.dtype) @ p['w_in'][:, C_OFF:])[:, None]
    p_prev = jnp.concatenate([pc_first, pc[:, :-1]], axis=1)
    pm = (pc + p['rwkv_mu'] * (p_prev - pc)).astype(f32)
    r, kc, vc, wd, ad, gd = split_cols(pm, C_SPLITS)
    w_log = -jax.nn.softplus(-(p['rwkv_w0'] + jnp.tanh(wd) @ p['rwkv_w_up'])) - 0.5
    decay = jnp.exp(-jnp.exp(w_log))
    a = jax.nn.sigmoid(p['rwkv_a0'] + ad @ p['rwkv_a_up'])
    gate = jax.nn.sigmoid(gd) @ p['rwkv_g_up']
    heads = lambda t: t.reshape(b, L, H_C, N_C)
    kk = l2norm(heads(kc * p['rwkv_k_k']))
    kc = kc * (1.0 + (a - 1.0) * p['rwkv_k_a'])
    r, kc, vc, a, decay = (heads(t.astype(f32)) for t in (r, kc, vc, a, decay))
    y, new_rwkv = rwkv7_recurrence(r, decay, kc, vc, kk, a, s_rwkv.astype(f32))
    mu_y = jnp.mean(y, axis=-1, keepdims=True)
    var_y = jnp.mean(jnp.square(y - mu_y), axis=-1, keepdims=True)
    y = ((y - mu_y) * lax.rsqrt(var_y + GN_EPS) * p['rwkv_ln_w'].astype(f32).reshape(H_C, N_C)
         + p['rwkv_ln_b'].astype(f32).reshape(H_C, N_C))
    bonus = jnp.sum(r * kc * p['rwkv_r_k'].astype(f32), axis=-1, keepdims=True) * vc
    o_c = (y + bonus).reshape(b, L, D_C) * gate

    mix = jnp.concatenate([o_a, o_b, o_c], axis=-1).astype(h.dtype) @ p['w_out']
    return mix, new_gdn, new_conv, new_rwkv, h[:, -1], vs.reshape(b, L, D_B)


def hier_moe(h, p):
    f32 = jnp.float32
    b, L, _ = h.shape
    gl = (h @ p['router_group_w']).astype(f32) + p['router_group_b'].astype(f32)
    gsel = jnp.argmax(gl, axis=-1)
    gw = jnp.take_along_axis(jax.nn.softmax(gl, axis=-1), gsel[..., None], axis=-1)
    el = ((h @ p['router_expert_w']).astype(f32) + p['router_expert_b'].astype(f32)).reshape(
        b, L, N_GROUPS, EXPERTS_PER_GROUP)
    el_sel = jnp.take_along_axis(el, gsel[..., None, None], axis=2)[..., 0, :]
    top_v, top_i = lax.top_k(el_sel, TOP_K_IN_GROUP)
    tw = jax.nn.softmax(top_v, axis=-1) * gw
    eid = gsel[..., None] * EXPERTS_PER_GROUP + top_i
    gate = jnp.sum(jax.nn.one_hot(eid, N_EXPERTS, dtype=f32) * tw[..., None], axis=-2)
    y = jnp.zeros(h.shape, f32)
    for e in range(N_EXPERTS):
        he = jax.nn.silu(h @ p['expert_w_gate'][e]) * (h @ p['expert_w_up'][e])
        y = y + gate[..., e:e + 1] * (he @ p['expert_w_down'][e]).astype(f32)
    return y.astype(h.dtype)


def forward_group(x, s_gdn, conv_buf, s_rwkv, h_prev, layers, norm_final, keep_chunk_rows):
    new = ([], [], [], [], [])
    for l in range(DEPTH):
        p = layers[l]
        mix, n_gdn, n_conv, n_rwkv, n_shift, v_rows = token_mixers(
            rmsnorm(x, p['norm_mix']), p, s_gdn[l], conv_buf[l], s_rwkv[l], h_prev[l])
        x = x + mix.astype(x.dtype)
        x = x + hier_moe(rmsnorm(x, p['norm_ffn']), p)
        new[0].append(n_gdn)
        new[1].append(n_conv)
        new[2].append(n_rwkv)
        new[3].append(n_shift)
        if keep_chunk_rows:
            new[4].append(v_rows)
    return rmsnorm(x, norm_final), new


def setup_inputs(seed: int = 0) -> dict:
    key = jax.random.key(seed)
    keys = iter(jax.random.split(key, 64))
    f32 = jnp.float32

    def nrm(shape, scale):
        return scale * jax.random.normal(next(keys), shape, f32)

    def unif(shape, lo, hi):
        return jax.random.uniform(next(keys), shape, f32, lo, hi)

    dt = jnp.exp(unif((DEPTH, H_A), math.log(1e-3), math.log(1e-1)))
    return {
        'x_prompt': nrm((BATCH, SEQ, D_MODEL), 1.0),
        'x_sample': nrm((DEC_BATCH, DEC_SEQ, D_MODEL), 1.0),
        'state_gdn': nrm((DEPTH, DEC_BATCH, H_A, HD_A, HD_A), 0.3),
        'state_gdn_conv': nrm((DEPTH, DEC_BATCH, CONV_W - 1, 3 * D_A), 1.0),
        'state_rwkv': nrm((DEPTH, DEC_BATCH, H_C, N_C, N_C), 0.5),
        'state_rwkv_shift': nrm((DEPTH, DEC_BATCH, D_MODEL), 1.0),
        'norm_mix': 1.0 + nrm((DEPTH, D_MODEL), 0.02),
        'norm_ffn': 1.0 + nrm((DEPTH, D_MODEL), 0.02),
        'norm_final': 1.0 + nrm((D_MODEL,), 0.02),
        'w_in': nrm((DEPTH, D_MODEL, N_IN), D_MODEL ** -0.5),
        'gdn_conv_w': nrm((DEPTH, CONV_W, 3 * D_A), CONV_W ** -0.5),
        'gdn_a_log': jnp.log(unif((DEPTH, H_A), 1.0, 16.0)),
        'gdn_dt_bias': jnp.log(jnp.expm1(dt)),
        'gdn_norm_w': 1.0 + nrm((DEPTH, HD_A), 0.02),
        'sgu_ln_w': 1.0 + nrm((DEPTH, D_B), 0.02),
        'sgu_ln_b': nrm((DEPTH, D_B), 0.02),
        'sgu_w': nrm((DEPTH, H_B, SGU_CHUNK, SGU_CHUNK), SGU_CHUNK ** -0.5),
        'sgu_b': 1.0 + nrm((DEPTH, H_B, SGU_CHUNK), 0.02),
        'rwkv_mu': unif((DEPTH, N_C_IN), 0.0, 1.0),
        'rwkv_w0': unif((DEPTH, D_C), -6.0, -1.0),
        'rwkv_w_up': nrm((DEPTH, RANK_W, D_C), 0.1),
        'rwkv_a0': nrm((DEPTH, D_C), 0.1),
        'rwkv_a_up': nrm((DEPTH, RANK_A, D_C), RANK_A ** -0.5),
        'rwkv_g_up': nrm((DEPTH, RANK_G, D_C), RANK_G ** -0.5),
        'rwkv_k_k': 0.85 + nrm((DEPTH, D_C), 0.05),
        'rwkv_k_a': 1.0 + nrm((DEPTH, D_C), 0.05),
        'rwkv_r_k': nrm((DEPTH, H_C, N_C), 0.1),
        'rwkv_ln_w': 1.0 + nrm((DEPTH, D_C), 0.02),
        'rwkv_ln_b': nrm((DEPTH, D_C), 0.02),
        'w_out': nrm((DEPTH, D_MIX, D_MODEL), D_MIX ** -0.5),
        'router_group_w': nrm((DEPTH, D_MODEL, N_GROUPS), D_MODEL ** -0.5),
        'router_group_b': nrm((DEPTH, N_GROUPS), 0.01),
        'router_expert_w': nrm((DEPTH, D_MODEL, N_EXPERTS), D_MODEL ** -0.5),
        'router_expert_b': nrm((DEPTH, N_EXPERTS), 0.01),
        'expert_w_gate': nrm((DEPTH, N_EXPERTS, D_MODEL, D_FF_E), D_MODEL ** -0.5),
        'expert_w_up': nrm((DEPTH, N_EXPERTS, D_MODEL, D_FF_E), D_MODEL ** -0.5),
        'expert_w_down': nrm((DEPTH, N_EXPERTS, D_FF_E, D_MODEL), D_FF_E ** -0.5),
    }


def reference(x_prompt, x_sample, state_gdn, state_gdn_conv, state_rwkv, state_rwkv_shift,
              norm_mix, norm_ffn, norm_final, w_in, gdn_conv_w, gdn_a_log, gdn_dt_bias, gdn_norm_w,
              sgu_ln_w, sgu_ln_b, sgu_w, sgu_b, rwkv_mu, rwkv_w0, rwkv_w_up, rwkv_a0, rwkv_a_up,
              rwkv_g_up, rwkv_k_k, rwkv_k_a, rwkv_r_k, rwkv_ln_w, rwkv_ln_b, w_out,
              router_group_w, router_group_b, router_expert_w, router_expert_b,
              expert_w_gate, expert_w_up, expert_w_down):
    stacked = dict(norm_mix=norm_mix, norm_ffn=norm_ffn, w_in=w_in, gdn_conv_w=gdn_conv_w,
                   gdn_a_log=gdn_a_log, gdn_dt_bias=gdn_dt_bias, gdn_norm_w=gdn_norm_w,
                   sgu_ln_w=sgu_ln_w, sgu_ln_b=sgu_ln_b, sgu_w=sgu_w, sgu_b=sgu_b,
                   rwkv_mu=rwkv_mu, rwkv_w0=rwkv_w0, rwkv_w_up=rwkv_w_up, rwkv_a0=rwkv_a0,
                   rwkv_a_up=rwkv_a_up, rwkv_g_up=rwkv_g_up, rwkv_k_k=rwkv_k_k, rwkv_k_a=rwkv_k_a,
                   rwkv_r_k=rwkv_r_k, rwkv_ln_w=rwkv_ln_w, rwkv_ln_b=rwkv_ln_b, w_out=w_out,
                   router_group_w=router_group_w, router_group_b=router_group_b,
                   router_expert_w=router_expert_w, router_expert_b=router_expert_b,
                   expert_w_gate=expert_w_gate, expert_w_up=expert_w_up, expert_w_down=expert_w_down)
    layers = [{name: arr[l] for name, arr in stacked.items()} for l in range(DEPTH)]
    b = x_prompt.shape[0]
    z_gdn = jnp.zeros((DEPTH, b, H_A, HD_A, HD_A), state_gdn.dtype)
    z_conv = jnp.zeros((DEPTH, b, CONV_W - 1, 3 * D_A), x_prompt.dtype)
    z_rwkv = jnp.zeros((DEPTH, b, H_C, N_C, N_C), state_rwkv.dtype)
    z_shift = jnp.zeros((DEPTH, b, D_MODEL), x_prompt.dtype)
    y_prompt, np_ = forward_group(x_prompt, z_gdn, z_conv, z_rwkv, z_shift, layers, norm_final, False)
    y_sample, ns_ = forward_group(x_sample, state_gdn, state_gdn_conv, state_rwkv, state_rwkv_shift,
                                  layers, norm_final, True)
    gdn_p = jnp.stack(np_[0]).astype(state_gdn.dtype)
    conv_p = jnp.stack(np_[1]).astype(state_gdn_conv.dtype)
    rwkv_p = jnp.stack(np_[2]).astype(state_rwkv.dtype)
    shift_p = jnp.stack(np_[3]).astype(state_rwkv_shift.dtype)
    gdn_s = jnp.stack(ns_[0]).astype(state_gdn.dtype)
    conv_s = jnp.stack(ns_[1]).astype(state_gdn_conv.dtype)
    rwkv_s = jnp.stack(ns_[2]).astype(state_rwkv.dtype)
    shift_s = jnp.stack(ns_[3]).astype(state_rwkv_shift.dtype)
    chunk_v_s = jnp.stack(ns_[4]).astype(x_sample.dtype)
    return (y_prompt, y_sample, gdn_p, conv_p, rwkv_p, shift_p, gdn_s, conv_s, rwkv_s, shift_s, chunk_v_s)
```

```python
import functools
import math

import jax
import jax.numpy as jnp
from jax import lax
from jax.experimental import pallas as pl
from jax.experimental.pallas import tpu as pltpu

f32 = jnp.float32
bf16 = jnp.bfloat16
i32 = jnp.int32

D_MODEL = 1024
BATCH = 8
SEQ = 2048
DEPTH = 2
DEC_BATCH = 128
DEC_SEQ = 4
H_A = 4
HD_A = 128
D_A = 512
CONV_W = 4
D_B = 256
DH_B = 64
H_C = 4
N_C = 64
D_C = 256
N_EXPERTS = 16
D_FF_E = 512
NORM_EPS = 1e-6
LN_EPS = 1e-5
GN_EPS = 64e-5

SEQ_PAD = 8
SEQ_LEAD = SEQ_PAD - DEC_SEQ
T_PROMPT = BATCH * SEQ
T_SAMPLE = DEC_BATCH * SEQ_PAD
T_ALL = T_PROMPT + T_SAMPLE
TB = 128
GDN_CHUNK = 64

OFF_Q, OFF_K, OFF_V, OFF_Z, OFF_BA, OFF_U, OFF_VB, OFF_C = 0, 512, 1024, 1536, 2048, 2176, 2432, 2688
NP_IN = 3840
NC_PAD = NP_IN - OFF_C

VMEM_LIMIT = 48 * 1024 * 1024


NN = (((1,), (0,)), ((), ()))
NT = (((1,), (1,)), ((), ()))
TN = (((0,), (0,)), ((), ()))
BNN = (((2,), (1,)), ((0,), (0,)))


def _bdot(a, b, dims=NN):
    return lax.dot_general(a.astype(bf16), b.astype(bf16), dims, preferred_element_type=f32)


def _split2(x):
    hi = x.astype(bf16)
    lo = (x - hi.astype(f32)).astype(bf16)
    return hi, lo


def _split3(x):
    hi = x.astype(bf16)
    r = x - hi.astype(f32)
    mid = r.astype(bf16)
    lo = (r - mid.astype(f32)).astype(bf16)
    return hi, mid, lo


def _dot01_left(m01, x):
    hi, mid, lo = _split3(x)
    d = lambda p: lax.dot_general(m01, p, NN, preferred_element_type=f32)
    return d(hi) + d(mid) + d(lo)


def _dot01_right(x, m01):
    hi, lo = _split2(x)
    d = lambda p: lax.dot_general(p, m01, NN, preferred_element_type=f32)
    return d(hi) + d(lo)


def _softplus(x):
    return jnp.maximum(x, 0.0) + jnp.log1p(jnp.exp(-jnp.abs(x)))


def _sigmoid(x):
    return 1.0 / (1.0 + jnp.exp(-x))


def _silu(x):
    return x * _sigmoid(x)


def _gelu(x):
    return 0.5 * x * (1.0 + lax.erf(x * (1.0 / math.sqrt(2.0))))


TM_A = 512
N_SLAB = 768


def _inproj_kernel(x_ref, nw_ref, w_ref, proj_ref, h_ref):
    x = x_ref[...]
    h = x * lax.rsqrt(jnp.mean(x * x, axis=-1, keepdims=True) + NORM_EPS) * nw_ref[...]
    h_ref[...] = h
    hb = h.astype(bf16)
    for n in range(NP_IN // N_SLAB):
        sl = pl.ds(n * N_SLAB, N_SLAB)
        proj_ref[:, sl] = jnp.dot(hb, w_ref[:, sl], preferred_element_type=f32)


def _inproj(x_all, nw, w_pad):
    return pl.pallas_call(
        _inproj_kernel,
        grid=(T_ALL // TM_A,),
        in_specs=[pl.BlockSpec((TM_A, D_MODEL), lambda i: (i, 0)),
                  pl.BlockSpec((1, D_MODEL), lambda i: (0, 0)),
                  pl.BlockSpec((D_MODEL, NP_IN), lambda i: (0, 0))],
        out_specs=[pl.BlockSpec((TM_A, NP_IN), lambda i: (i, 0)),
                   pl.BlockSpec((TM_A, D_MODEL), lambda i: (i, 0))],
        out_shape=[jax.ShapeDtypeStruct((T_ALL, NP_IN), f32),
                   jax.ShapeDtypeStruct((T_ALL, D_MODEL), f32)],
        compiler_params=pltpu.CompilerParams(dimension_semantics=("arbitrary",),
                                             vmem_limit_bytes=VMEM_LIMIT),
        name="inproj",
    )(x_all, nw, w_pad)


def _mm_kernel(a_ref, b_ref, o_ref):
    o_ref[...] = jnp.dot(a_ref[...].astype(bf16), b_ref[...], preferred_element_type=f32)


def _mm(a, b):
    return pl.pallas_call(
        _mm_kernel,
        out_shape=jax.ShapeDtypeStruct((a.shape[0], b.shape[1]), f32),
        compiler_params=pltpu.CompilerParams(vmem_limit_bytes=VMEM_LIMIT),
        name="shift_proj",
    )(a, b)


def _neumann(L, iters):
    N = -L
    Q = L
    for _ in range(iters):
        Qb = Q.astype(bf16)
        Q = lax.dot_general(Qb, Qb, BNN, preferred_element_type=f32)
        N = N + Q + lax.dot_general(N.astype(bf16), Q.astype(bf16), BNN, preferred_element_type=f32)
    return N


def _mixer_kernel(*refs, C, G, per_seq):
    it = iter(refs)
    proj_ref = next(it)
    if per_seq:
        gdn_in, conv_in, rwkv_in, pcf_in = next(it), next(it), next(it), next(it)
    convw_ref, v128_ref, v256_ref, sguw_ref, sgub_ref, mu_ref, wup_ref, aup_ref, gup_ref = (
        next(it) for _ in range(9))
    mix_ref, gdn_out, conv_out, rwkv_out = next(it), next(it), next(it), next(it)
    cv_out = next(it) if per_seq else None
    xp, pp = next(it), next(it)

    nchunk = TB // C
    ngrp = C // G
    iters = int(math.log2(G)) - 1

    if per_seq:
        xp[pl.ds(0, 8), :] = jnp.zeros((8, 3 * D_A), f32)
        pp[pl.ds(0, 8), :] = jnp.zeros((8, NC_PAD), f32)
    else:
        @pl.when(pl.program_id(1) == 0)
        def _():
            xp[pl.ds(0, 8), :] = jnp.zeros((8, 3 * D_A), f32)
            pp[pl.ds(0, 8), :] = jnp.zeros((8, NC_PAD), f32)
            gdn_out[...] = jnp.zeros(gdn_out.shape, f32)
            rwkv_out[...] = jnp.zeros(rwkv_out.shape, f32)
    xp[pl.ds(8, TB), :] = proj_ref[:, pl.ds(OFF_Q, 3 * D_A)]
    pp[pl.ds(8, TB), :] = proj_ref[:, pl.ds(OFF_C, NC_PAD)]
    if per_seq:
        for s in range(TB // SEQ_PAD):
            r0 = 8 + s * SEQ_PAD
            xp[pl.ds(r0 + SEQ_LEAD - (CONV_W - 1), CONV_W - 1), :] = conv_in[s]
            pp[pl.ds(r0 + SEQ_LEAD - 1, 1), :] = pcf_in[s]

    rowi = lax.broadcasted_iota(i32, (TB, 1), 0)
    live = (rowi % SEQ_PAD) >= SEQ_LEAD if per_seq else None

    ii = lax.broadcasted_iota(i32, (C, C), 0)
    jj = lax.broadcasted_iota(i32, (C, C), 1)
    same = (ii // G) == (jj // G)
    causal = (ii >= jj) & same
    strict = (ii > jj) & same
    eye = ii == jj
    m_cum = causal.astype(bf16)
    m_grp = same.astype(bf16)

    def conv_cols(r0, c0):
        cs = pl.ds(c0, 128)
        acc = xp[pl.ds(r0 + 8, C), cs] * convw_ref[pl.ds(3, 1), cs]
        for j in range(CONV_W - 1):
            acc = acc + xp[pl.ds(r0 + 5 + j, C), cs] * convw_ref[pl.ds(j, 1), cs]
        return _silu(acc)

    alog_row = v128_ref[pl.ds(0, 1), :]
    dtb_row = v128_ref[pl.ds(1, 1), :]
    gnorm_w = v128_ref[pl.ds(2, 1), :]
    ln_w, ln_b = v256_ref[pl.ds(0, 1), :], v256_ref[pl.ds(1, 1), :]
    w0, a0 = v256_ref[pl.ds(2, 1), :], v256_ref[pl.ds(3, 1), :]
    k_k, k_a, r_k = v256_ref[pl.ds(4, 1), :], v256_ref[pl.ds(5, 1), :], v256_ref[pl.ds(6, 1), :]
    rln_w, rln_b = v256_ref[pl.ds(7, 1), :], v256_ref[pl.ds(8, 1), :]

    l64i = lax.broadcasted_iota(i32, (D_C, D_C), 0) // N_C
    l64j = lax.broadcasted_iota(i32, (D_C, D_C), 1) // N_C
    seg64 = (l64i == l64j).astype(bf16)

    for c in range(nchunk):
        R0 = c * C
        rows = pl.ds(R0, C)
        live_c = live[R0:R0 + C] if per_seq else None

        ba = proj_ref[rows, pl.ds(OFF_BA, 128)]
        beta_all = _sigmoid(ba)
        g_all = -jnp.exp(alog_row) * _softplus(ba + dtb_row)
        if per_seq:
            beta_all = jnp.where(live_c, beta_all, 0.0)
            g_all = jnp.where(live_c, g_all, 0.0)
        gc_all = _dot01_left(m_cum, g_all)
        gl_all = _dot01_left(m_grp, g_all)

        gdn = []
        lmats = []
        for h in range(H_A):
            q = conv_cols(R0, OFF_Q + h * HD_A)
            k = conv_cols(R0, OFF_K + h * HD_A)
            v = conv_cols(R0, OFF_V + h * HD_A)
            q = q * lax.rsqrt(jnp.sum(q * q, axis=-1, keepdims=True) + 1e-6) * (HD_A ** -0.5)
            k = k * lax.rsqrt(jnp.sum(k * k, axis=-1, keepdims=True) + 1e-6)
            if per_seq:
                k = jnp.where(live_c, k, 0.0)
            beta = beta_all[:, h:h + 1]
            gcol = gc_all[:, 4 + h:5 + h]
            glr = gl_all[:, 4 + h:5 + h]
            grow = jnp.sum(jnp.where(eye, jnp.broadcast_to(gcol, (C, C)), 0.0), axis=0, keepdims=True)
            decay = jnp.where(causal, jnp.exp(jnp.where(causal, gcol - grow, 0.0)), 0.0)
            eg = jnp.exp(gcol)
            kb = k * beta
            kq = _bdot(jnp.concatenate([kb, q], axis=0), k, NT)
            lmats.append(jnp.where(strict, kq[:C] * decay, 0.0))
            attn = kq[C:] * decay
            rhs = jnp.concatenate([v * beta, kb * eg], axis=1)
            gdn.append(dict(attn=attn, rhs=rhs, q_dec=q * eg, k_dec=k * jnp.exp(glr - gcol), glr=glr))

        pcur = pp[pl.ds(R0 + 8, C), :]
        pprev = pp[pl.ds(R0 + 7, C), :]
        pm = pcur + mu_ref[...] * (pprev - pcur)
        r_ = pm[:, 0:D_C]
        kc = pm[:, D_C:2 * D_C]
        vc = pm[:, 2 * D_C:3 * D_C]
        wd = pm[:, 3 * D_C:3 * D_C + 128]
        ad = pm[:, 3 * D_C + 128:3 * D_C + 256]
        gd = pm[:, 3 * D_C + 256:3 * D_C + 384]
        w_log = -_softplus(-(w0 + _bdot(jnp.tanh(wd), wup_ref[...]))) - 0.5
        logw = -jnp.exp(w_log)
        a_ = _sigmoid(a0 + _bdot(ad, aup_ref[...]))
        gate = _bdot(_sigmoid(gd), gup_ref[...])
        kk = kc * k_k
        kk = kk * lax.rsqrt(_dot01_right(kk * kk, seg64) + 1e-6)
        kc2 = kc * (1.0 + (a_ - 1.0) * k_a)
        if per_seq:
            logw = jnp.where(live_c, logw, 0.0)
            kk = jnp.where(live_c, kk, 0.0)
            kc2 = jnp.where(live_c, kc2, 0.0)
        b_ = kk * a_
        Gc = _dot01_left(m_cum, logw)
        Gl = _dot01_left(m_grp, logw)
        e_neg = jnp.exp(-Gc)
        e_rem = jnp.exp(Gl - Gc)
        rG = r_ * jnp.exp(Gc)
        kkG = kk * jnp.exp(Gc - logw)
        kN = kc2 * e_neg
        bN = b_ * e_neg
        kdec = kc2 * e_rem
        bdec = b_ * e_rem
        e_last = jnp.exp(Gl)

        rw = []
        for h in range(H_C):
            hs = slice(h * N_C, (h + 1) * N_C)
            aall = _bdot(jnp.concatenate([kkG[:, hs], rG[:, hs]], axis=0),
                         jnp.concatenate([bN[:, hs], kN[:, hs]], axis=0), NT)
            lmats.append(jnp.where(strict, aall[:C, :C], 0.0))
            akk_k = jnp.where(strict, aall[:C, C:], 0.0)
            ar = jnp.concatenate([jnp.where(causal, aall[C:, C:], 0.0),
                                  -jnp.where(causal, aall[C:, :C], 0.0)], axis=1)
            rw.append(dict(akk_k=akk_k, ar=ar))

        nmats = _neumann(jnp.stack(lmats), iters)

        for h in range(H_A):
            d = gdn[h]
            sol = d['rhs'] + _bdot(nmats[h], d['rhs'])
            u, w = sol[:, :HD_A], sol[:, HD_A:]
            rq, vn = [], []
            s_old = []
            for s in range(ngrp):
                gs = slice(s * G, (s + 1) * G)
                S = gdn_in[s, h] if per_seq else gdn_out[0, h]
                s_old.append(S)
                R = _bdot(jnp.concatenate([w[gs], d['q_dec'][gs]], axis=0), S)
                vn.append(u[gs] - R[:G])
                rq.append(R[G:])
            v_new = vn[0] if ngrp == 1 else jnp.concatenate(vn, axis=0)
            o = (rq[0] if ngrp == 1 else jnp.concatenate(rq, axis=0)) + _bdot(d['attn'], v_new)
            for s in range(ngrp):
                gs = slice(s * G, (s + 1) * G)
                g_last = jnp.exp(d['glr'][s * G + G - 1:s * G + G, :])
                S_new = s_old[s] * g_last + _bdot(d['k_dec'][gs], v_new[gs], TN)
                if per_seq:
                    gdn_out[s, h] = S_new
                else:
                    gdn_out[0, h] = S_new
            o = o * lax.rsqrt(jnp.mean(o * o, axis=-1, keepdims=True) + NORM_EPS) * gnorm_w
            z = proj_ref[rows, pl.ds(OFF_Z + h * HD_A, HD_A)]
            mix_ref[rows, pl.ds(h * HD_A, HD_A)] = o * _silu(z)

        ys = []
        for h in range(H_C):
            hs = slice(h * N_C, (h + 1) * N_C)
            d = rw[h]
            V = vc[:, hs]
            x1 = _bdot(d['akk_k'], V)
            both = jnp.concatenate([x1, kkG[:, hs]], axis=1)
            both = both + _bdot(nmats[H_A + h], both)
            u_p, w_p = both[:, :N_C], both[:, N_C:]
            rr, ut = [], []
            s_old = []
            for s in range(ngrp):
                gs = slice(s * G, (s + 1) * G)
                S = rwkv_in[s, h] if per_seq else rwkv_out[0, h]
                s_old.append(S)
                R = _bdot(jnp.concatenate([w_p[gs], rG[gs, hs]], axis=0), S, NT)
                ut.append(u_p[gs] + R[:G])
                rr.append(R[G:])
            Ut = ut[0] if ngrp == 1 else jnp.concatenate(ut, axis=0)
            Rr = rr[0] if ngrp == 1 else jnp.concatenate(rr, axis=0)
            y = Rr + _bdot(d['ar'], jnp.concatenate([V, Ut], axis=0))
            for s in range(ngrp):
                gs = slice(s * G, (s + 1) * G)
                upd = _bdot(jnp.concatenate([V[gs], -Ut[gs]], axis=0),
                            jnp.concatenate([kdec[gs, hs], bdec[gs, hs]], axis=0), TN)
                S_new = s_old[s] * e_last[s * G + G - 1:s * G + G, hs] + upd
                if per_seq:
                    rwkv_out[s, h] = S_new
                else:
                    rwkv_out[0, h] = S_new
            ys.append(y)
        y = jnp.concatenate(ys, axis=1)
        mu_y = _dot01_right(y, seg64) * (1.0 / N_C)
        dy = y - mu_y
        var_y = _dot01_right(dy * dy, seg64) * (1.0 / N_C)
        y = dy * lax.rsqrt(var_y + GN_EPS) * rln_w + rln_b
        bonus = _dot01_right(r_ * kc2 * r_k, seg64) * vc
        mix_ref[rows, pl.ds(D_A + D_B, D_C)] = (y + bonus) * gate

    ug = _gelu(proj_ref[:, pl.ds(OFF_U, D_B)])
    vs = _gelu(proj_ref[:, pl.ds(OFF_VB, D_B)])
    mu_v = _dot01_right(vs, seg64) * (1.0 / DH_B)
    dv = vs - mu_v
    var_v = _dot01_right(dv * dv, seg64) * (1.0 / DH_B)
    vs = dv * lax.rsqrt(var_v + LN_EPS) * ln_w + ln_b
    if per_seq:
        cv_out[...] = vs
    vsb = vs.astype(bf16)
    outs = [lax.dot_general(sguw_ref[h], vsb[:, h * DH_B:(h + 1) * DH_B], NN, preferred_element_type=f32)
            for h in range(4)]
    mixed = jnp.concatenate(outs, axis=1) + sgub_ref[...]
    mix_ref[:, pl.ds(D_A, D_B)] = ug * mixed

    if per_seq:
        for s in range(TB // SEQ_PAD):
            conv_out[s] = xp[pl.ds(8 + (s + 1) * SEQ_PAD - (CONV_W - 1), CONV_W - 1), :]
    else:
        conv_out[0] = xp[pl.ds(8 + TB - (CONV_W - 1), CONV_W - 1), :]
        xp[pl.ds(0, 8), :] = xp[pl.ds(TB, 8), :]
        pp[pl.ds(0, 8), :] = pp[pl.ds(TB, 8), :]


def _mixer_weight_specs(nidx):
    z2 = (lambda b, j: (0, 0)) if nidx == 2 else (lambda i: (0, 0))
    z3 = (lambda b, j: (0, 0, 0)) if nidx == 2 else (lambda i: (0, 0, 0))
    return [pl.BlockSpec((CONV_W, 3 * D_A), z2),
            pl.BlockSpec((8, 128), z2),
            pl.BlockSpec((16, D_C), z2),
            pl.BlockSpec((4, TB, TB), z3),
            pl.BlockSpec((TB, D_B), z2),
            pl.BlockSpec((1, NC_PAD), z2),
            pl.BlockSpec((128, D_C), z2),
            pl.BlockSpec((128, D_C), z2),
            pl.BlockSpec((128, D_C), z2)]


def _mixer_prompt(proj, mw):
    nj = SEQ // TB
    return pl.pallas_call(
        functools.partial(_mixer_kernel, C=GDN_CHUNK, G=GDN_CHUNK, per_seq=False),
        grid=(BATCH, nj),
        in_specs=[pl.BlockSpec((TB, NP_IN), lambda b, j: (b * nj + j, 0))] + _mixer_weight_specs(2),
        out_specs=[pl.BlockSpec((TB, D_MODEL), lambda b, j: (b * nj + j, 0)),
                   pl.BlockSpec((1, H_A, HD_A, HD_A), lambda b, j: (b, 0, 0, 0)),
                   pl.BlockSpec((1, CONV_W - 1, 3 * D_A), lambda b, j: (b, 0, 0)),
                   pl.BlockSpec((1, H_C, N_C, N_C), lambda b, j: (b, 0, 0, 0))],
        out_shape=[jax.ShapeDtypeStruct((T_ALL, D_MODEL), f32),
                   jax.ShapeDtypeStruct((BATCH, H_A, HD_A, HD_A), f32),
                   jax.ShapeDtypeStruct((BATCH, CONV_W - 1, 3 * D_A), f32),
                   jax.ShapeDtypeStruct((BATCH, H_C, N_C, N_C), f32)],
        scratch_shapes=[pltpu.VMEM((TB + 8, 3 * D_A), f32), pltpu.VMEM((TB + 8, NC_PAD), f32)],
        compiler_params=pltpu.CompilerParams(dimension_semantics=("arbitrary", "arbitrary"),
                                             vmem_limit_bytes=VMEM_LIMIT),
        name="mixer_prompt",
    )(proj, *mw)


def _mixer_sample(proj, mix_prev, s_gdn, s_conv, s_rwkv, pcf, mw):
    nseq = TB // SEQ_PAD
    base = T_PROMPT // TB
    n_in = 6 + len(mw)
    return pl.pallas_call(
        functools.partial(_mixer_kernel_sample_wrap, n_skip=1),
        grid=(T_SAMPLE // TB,),
        in_specs=[pl.BlockSpec(memory_space=pl.ANY),
                  pl.BlockSpec((TB, NP_IN), lambda i: (base + i, 0)),
                  pl.BlockSpec((nseq, H_A, HD_A, HD_A), lambda i: (i, 0, 0, 0)),
                  pl.BlockSpec((nseq, CONV_W - 1, 3 * D_A), lambda i: (i, 0, 0)),
                  pl.BlockSpec((nseq, H_C, N_C, N_C), lambda i: (i, 0, 0, 0)),
                  pl.BlockSpec((nseq, 1, NC_PAD), lambda i: (i, 0, 0))] + _mixer_weight_specs(1),
        out_specs=[pl.BlockSpec((TB, D_MODEL), lambda i: (base + i, 0)),
                   pl.BlockSpec((nseq, H_A, HD_A, HD_A), lambda i: (i, 0, 0, 0)),
                   pl.BlockSpec((nseq, CONV_W - 1, 3 * D_A), lambda i: (i, 0, 0)),
                   pl.BlockSpec((nseq, H_C, N_C, N_C), lambda i: (i, 0, 0, 0)),
                   pl.BlockSpec((TB, D_B), lambda i: (i, 0))],
        out_shape=[jax.ShapeDtypeStruct((T_ALL, D_MODEL), f32),
                   jax.ShapeDtypeStruct((DEC_BATCH, H_A, HD_A, HD_A), f32),
                   jax.ShapeDtypeStruct((DEC_BATCH, CONV_W - 1, 3 * D_A), f32),
                   jax.ShapeDtypeStruct((DEC_BATCH, H_C, N_C, N_C), f32),
                   jax.ShapeDtypeStruct((T_SAMPLE, D_B), f32)],
        scratch_shapes=[pltpu.VMEM((TB + 8, 3 * D_A), f32), pltpu.VMEM((TB + 8, NC_PAD), f32)],
        input_output_aliases={0: 0},
        compiler_params=pltpu.CompilerParams(dimension_semantics=("arbitrary",),
                                             vmem_limit_bytes=VMEM_LIMIT),
        name="mixer_sample",
    )(mix_prev, proj, s_gdn, s_conv, s_rwkv, pcf, *mw)


def _mixer_kernel_sample_wrap(*refs, n_skip):
    _mixer_kernel(*refs[n_skip:], C=TB, G=SEQ_PAD, per_seq=True)


TM_C = 512
LANE_E0 = 4


def _outproj_kernel(x_ref, mix_ref, wout_ref, nw_ref, rwh_ref, rwl_ref, rb_ref, x2_ref, h2_ref, gate_ref):
    x2 = x_ref[...] + jnp.dot(mix_ref[...].astype(bf16), wout_ref[...], preferred_element_type=f32)
    x2_ref[...] = x2
    h2 = x2 * lax.rsqrt(jnp.mean(x2 * x2, axis=-1, keepdims=True) + NORM_EPS) * nw_ref[...]
    h2_ref[...] = h2.astype(bf16)
    hh, hl = _split2(h2)
    d = lambda a, b: jnp.dot(a, b, preferred_element_type=f32)
    logits = d(hh, rwh_ref[...]) + d(hl, rwh_ref[...]) + d(hh, rwl_ref[...]) + rb_ref[...]

    lane = lax.broadcasted_iota(i32, logits.shape, 1).astype(f32)
    neg = jnp.float32(-jnp.inf)
    is_g = lane < 4.0
    gl = jnp.where(is_g, logits, neg)
    gmax = jnp.max(gl, axis=-1, keepdims=True)
    gsel = jnp.min(jnp.where(gl == gmax, lane, 128.0), axis=-1, keepdims=True)
    gw = 1.0 / jnp.sum(jnp.where(is_g, jnp.exp(jnp.where(is_g, logits - gmax, 0.0)), 0.0),
                       axis=-1, keepdims=True)
    lo = LANE_E0 + 4.0 * gsel
    in_grp = (lane >= lo) & (lane < lo + 4.0)
    el = jnp.where(in_grp, logits, neg)
    t1 = jnp.max(el, axis=-1, keepdims=True)
    i1 = jnp.min(jnp.where(el == t1, lane, 128.0), axis=-1, keepdims=True)
    el2 = jnp.where(lane == i1, neg, el)
    t2 = jnp.max(el2, axis=-1, keepdims=True)
    i2 = jnp.min(jnp.where(el2 == t2, lane, 128.0), axis=-1, keepdims=True)
    e2 = jnp.exp(t2 - t1)
    den = 1.0 + e2
    gate_ref[...] = jnp.where(lane == i1, gw / den, 0.0) + jnp.where(lane == i2, gw * e2 / den, 0.0)


def _outproj(x_all, mix, wout, nw, rwh, rwl, rb):
    row = lambda i: (i, 0)
    fix = lambda i: (0, 0)
    return pl.pallas_call(
        _outproj_kernel,
        grid=(T_ALL // TM_C,),
        in_specs=[pl.BlockSpec((TM_C, D_MODEL), row), pl.BlockSpec((TM_C, D_MODEL), row),
                  pl.BlockSpec((D_MODEL, D_MODEL), fix), pl.BlockSpec((1, D_MODEL), fix),
                  pl.BlockSpec((D_MODEL, 128), fix), pl.BlockSpec((D_MODEL, 128), fix),
                  pl.BlockSpec((1, 128), fix)],
        out_specs=[pl.BlockSpec((TM_C, D_MODEL), row), pl.BlockSpec((TM_C, D_MODEL), row),
                   pl.BlockSpec((TM_C, 128), row)],
        out_shape=[jax.ShapeDtypeStruct((T_ALL, D_MODEL), f32),
                   jax.ShapeDtypeStruct((T_ALL, D_MODEL), bf16),
                   jax.ShapeDtypeStruct((T_ALL, 128), f32)],
        compiler_params=pltpu.CompilerParams(dimension_semantics=("arbitrary",),
                                             vmem_limit_bytes=VMEM_LIMIT),
        name="outproj_router",
    )(x_all, mix, wout, nw, rwh, rwl, rb)


TM_D = 1024


def _moe_kernel(h2_ref, gate_ref, x2_ref, wgu_ref, wd_ref, nf_ref, o_ref, acc_ref, *, final_norm):
    e = pl.program_id(1)

    @pl.when(e == 0)
    def _():
        acc_ref[...] = jnp.zeros(acc_ref.shape, f32)

    gu = jnp.dot(h2_ref[...], wgu_ref[0], preferred_element_type=f32)
    he = _silu(gu[:, :D_FF_E]) * gu[:, D_FF_E:]
    yd = jnp.dot(he.astype(bf16), wd_ref[0], preferred_element_type=f32)
    g = gate_ref[...]
    lane = lax.broadcasted_iota(i32, g.shape, 1)
    gcol = jnp.sum(jnp.where(lane == LANE_E0 + e, g, 0.0), axis=-1, keepdims=True)
    acc_ref[...] += gcol * yd

    @pl.when(e == N_EXPERTS - 1)
    def _():
        x3 = x2_ref[...] + acc_ref[...]
        if final_norm:
            x3 = x3 * lax.rsqrt(jnp.mean(x3 * x3, axis=-1, keepdims=True) + NORM_EPS) * nf_ref[...]
        o_ref[...] = x3


def _moe(h2, gates, x2, wgu, wd, nf, final_norm):
    row = lambda i, e: (i, 0)
    return pl.pallas_call(
        functools.partial(_moe_kernel, final_norm=final_norm),
        grid=(T_ALL // TM_D, N_EXPERTS),
        in_specs=[pl.BlockSpec((TM_D, D_MODEL), row), pl.BlockSpec((TM_D, 128), row),
                  pl.BlockSpec((TM_D, D_MODEL), row),
                  pl.BlockSpec((1, D_MODEL, 2 * D_FF_E), lambda i, e: (e, 0, 0)),
                  pl.BlockSpec((1, D_FF_E, D_MODEL), lambda i, e: (e, 0, 0)),
                  pl.BlockSpec((1, D_MODEL), lambda i, e: (0, 0))],
        out_specs=pl.BlockSpec((TM_D, D_MODEL), row),
        out_shape=jax.ShapeDtypeStruct((T_ALL, D_MODEL), f32),
        scratch_shapes=[pltpu.VMEM((TM_D, D_MODEL), f32)],
        compiler_params=pltpu.CompilerParams(dimension_semantics=("arbitrary", "arbitrary"),
                                             vmem_limit_bytes=VMEM_LIMIT),
        name="moe",
    )(h2, gates, x2, wgu, wd, nf)


def _pad_cols(a, n):
    return jnp.pad(a, ((0, 0), (0, n - a.shape[1])))


def _pad_rows(a, n):
    return jnp.pad(a, ((0, n - a.shape[0]), (0, 0)))


def _prep_w_in(w):
    c = w[:, 2568:]
    parts = [w[:, 0:2048], _pad_cols(w[:, 2048:2056], 128), w[:, 2056:2568],
             c[:, 0:768], _pad_cols(c[:, 768:832], 128), _pad_cols(c[:, 832:896], 128), c[:, 896:1024]]
    return jnp.concatenate(parts, axis=1).astype(bf16)


def _prep_mu(mu):
    m = mu[None, :]
    return jnp.concatenate([m[:, 0:768], _pad_cols(m[:, 768:832], 128), _pad_cols(m[:, 832:896], 128),
                            m[:, 896:1024]], axis=1)


def _sgu_mats(sgu_w, sgu_b):
    t = jnp.arange(TB)
    wm = jnp.where(t[:, None] >= t[None, :], sgu_w, 0.0)
    bias_p = jnp.repeat(jnp.transpose(sgu_b), DH_B, axis=1)
    small = jnp.zeros((4, SEQ_PAD, SEQ_PAD), f32).at[:, SEQ_LEAD:, SEQ_LEAD:].set(wm[:, :DEC_SEQ, :DEC_SEQ])
    eye16 = jnp.eye(TB // SEQ_PAD, dtype=f32)
    wm_s = jnp.einsum('ab,hij->haibj', eye16, small).reshape(4, TB, TB)
    bias_small = jnp.zeros((SEQ_PAD, D_B), f32).at[SEQ_LEAD:].set(bias_p[:DEC_SEQ])
    bias_s = jnp.tile(bias_small, (TB // SEQ_PAD, 1))
    return wm.astype(bf16), bias_p, wm_s.astype(bf16), bias_s


def _row(a, n):
    return _pad_cols(a.reshape(1, -1), n)


def kernel(x_prompt, x_sample, state_gdn, state_gdn_conv, state_rwkv, state_rwkv_shift, norm_mix, norm_ffn, norm_final, w_in, gdn_conv_w, gdn_a_log, gdn_dt_bias, gdn_norm_w, sgu_ln_w, sgu_ln_b, sgu_w, sgu_b, rwkv_mu, rwkv_w0, rwkv_w_up, rwkv_a0, rwkv_a_up, rwkv_g_up, rwkv_k_k, rwkv_k_a, rwkv_r_k, rwkv_ln_w, rwkv_ln_b, w_out, router_group_w, router_group_b, router_expert_w, router_expert_b, expert_w_gate, expert_w_up, expert_w_down):
    xs = jnp.pad(x_sample, ((0, 0), (SEQ_LEAD, 0), (0, 0))).reshape(T_SAMPLE, D_MODEL)
    x = jnp.concatenate([x_prompt.reshape(T_PROMPT, D_MODEL), xs], axis=0)

    outs = {k: [] for k in ('gdn_p', 'conv_p', 'rwkv_p', 'shift_p', 'gdn_s', 'conv_s', 'rwkv_s', 'shift_s', 'cv_s')}
    for l in range(DEPTH):
        w_pad = _prep_w_in(w_in[l])
        v128 = jnp.concatenate([
            jnp.pad(gdn_a_log[l].reshape(1, H_A), ((0, 0), (4, 120))),
            jnp.pad(gdn_dt_bias[l].reshape(1, H_A), ((0, 0), (4, 120))),
            gdn_norm_w[l].reshape(1, HD_A), jnp.zeros((5, 128), f32)], axis=0)
        v256 = jnp.concatenate([a.reshape(1, D_C) for a in (
            sgu_ln_w[l], sgu_ln_b[l], rwkv_w0[l], rwkv_a0[l], rwkv_k_k[l], rwkv_k_a[l], rwkv_r_k[l],
            rwkv_ln_w[l], rwkv_ln_b[l])] + [jnp.zeros((7, D_C), f32)], axis=0)
        wm_p, bias_p, wm_s, bias_s = _sgu_mats(sgu_w[l], sgu_b[l])
        common = (gdn_conv_w[l], v128, v256)
        tail = (_prep_mu(rwkv_mu[l]), _pad_rows(rwkv_w_up[l], 128).astype(bf16),
                _pad_rows(rwkv_a_up[l], 128).astype(bf16), rwkv_g_up[l].astype(bf16))
        mw_p = common + (wm_p, bias_p) + tail
        mw_s = common + (wm_s, bias_s) + tail

        proj, h = _inproj(x, norm_mix[l].reshape(1, D_MODEL), w_pad)
        pcf = _mm(state_rwkv_shift[l], w_pad[:, OFF_C:]).reshape(DEC_BATCH, 1, NC_PAD)
        mix, gdn_p, conv_p, rwkv_p = _mixer_prompt(proj, mw_p)
        mix, gdn_s, conv_s, rwkv_s, cv_s = _mixer_sample(
            proj, mix, state_gdn[l], state_gdn_conv[l], state_rwkv[l], pcf, mw_s)

        rw = jnp.concatenate([router_group_w[l], router_expert_w[l]], axis=1)
        rw = _pad_cols(jnp.pad(rw, ((0, 0), (0, 0))), 128)
        rwh = rw.astype(bf16)
        rwl = (rw - rwh.astype(f32)).astype(bf16)
        rb = _row(jnp.concatenate([router_group_b[l], router_expert_b[l]]), 128)
        x2, h2, gates = _outproj(x, mix, w_out[l].astype(bf16), norm_ffn[l].reshape(1, D_MODEL), rwh, rwl, rb)
        wgu = jnp.concatenate([expert_w_gate[l], expert_w_up[l]], axis=-1).astype(bf16)
        x = _moe(h2, gates, x2, wgu, expert_w_down[l].astype(bf16), norm_final.reshape(1, D_MODEL),
                 final_norm=(l == DEPTH - 1))

        outs['gdn_p'].append(gdn_p)
        outs['conv_p'].append(conv_p)
        outs['rwkv_p'].append(rwkv_p)
        outs['shift_p'].append(h[:T_PROMPT].reshape(BATCH, SEQ, D_MODEL)[:, -1])
        outs['gdn_s'].append(gdn_s)
        outs['conv_s'].append(conv_s)
        outs['rwkv_s'].append(rwkv_s)
        outs['shift_s'].append(h[T_PROMPT:].reshape(DEC_BATCH, SEQ_PAD, D_MODEL)[:, -1])
        outs['cv_s'].append(cv_s.reshape(DEC_BATCH, SEQ_PAD, D_B)[:, SEQ_LEAD:])

    y_prompt = x[:T_PROMPT].reshape(BATCH, SEQ, D_MODEL)
    y_sample = x[T_PROMPT:].reshape(DEC_BATCH, SEQ_PAD, D_MODEL)[:, SEQ_LEAD:]
    st = lambda k: jnp.stack(outs[k])
    return (y_prompt, y_sample, st('gdn_p'), st('conv_p'), st('rwkv_p'), st('shift_p'),
            st('gdn_s'), st('conv_s'), st('rwkv_s'), st('shift_s'), st('cv_s'))
```

```python
import functools
import math

import jax
import jax.numpy as jnp
from jax import lax
from jax.experimental import pallas as pl
from jax.experimental.pallas import tpu as pltpu

f32 = jnp.float32
bf16 = jnp.bfloat16
i32 = jnp.int32

D_MODEL = 1024
BATCH = 8
SEQ = 2048
DEPTH = 2
DEC_BATCH = 128
DEC_SEQ = 4
H_A = 4
HD_A = 128
D_A = 512
CONV_W = 4
D_B = 256
DH_B = 64
SGU_CHUNK = 128
H_C = 4
N_C = 64
D_C = 256
N_GROUPS = 4
EPG = 4
D_FF_E = 512
NORM_EPS = 1e-6
LN_EPS = 1e-5
GN_EPS = 64e-5

SEQ_PAD = 8
SEQ_LEAD = SEQ_PAD - DEC_SEQ
T_PROMPT = BATCH * SEQ
T_SAMPLE = DEC_BATCH * SEQ_PAD
T_ALL = T_PROMPT + T_SAMPLE
TB_P = 256
TB_S = 128
GDN_CHUNK = 64

OFF_Q, OFF_K, OFF_V, OFF_Z, OFF_BA, OFF_U, OFF_VB, OFF_C = 0, 512, 1024, 1536, 2048, 2176, 2432, 2688
NP_IN = 3840
NC_PAD = NP_IN - OFF_C

XG_W = D_MODEL + 128
LANE_G, LANE_RANK, LANE_E0 = 0, 1, 4

VMEM_LIMIT = 48 * 1024 * 1024


NN = (((1,), (0,)), ((), ()))
NT = (((1,), (1,)), ((), ()))
TN = (((0,), (0,)), ((), ()))
BNN = (((2,), (1,)), ((0,), (0,)))


def _bdot(a, b, dims=NN):
    return lax.dot_general(a.astype(bf16), b.astype(bf16), dims, preferred_element_type=f32)


def _split2(x):
    hi = x.astype(bf16)
    lo = (x - hi.astype(f32)).astype(bf16)
    return hi, lo


def _split3(x):
    hi = x.astype(bf16)
    r = x - hi.astype(f32)
    mid = r.astype(bf16)
    lo = (r - mid.astype(f32)).astype(bf16)
    return hi, mid, lo


def _dot01_left(m01, x):
    hi, mid, lo = _split3(x)
    d = lambda p: lax.dot_general(m01, p, NN, preferred_element_type=f32)
    return d(hi) + d(mid) + d(lo)


def _dot01_right(x, m01):
    hi, lo = _split2(x)
    d = lambda p: lax.dot_general(p, m01, NN, preferred_element_type=f32)
    return d(hi) + d(lo)


def _softplus(x):
    return jnp.maximum(x, 0.0) + jnp.log1p(jnp.exp(-jnp.abs(x)))


def _sigmoid(x):
    return 1.0 / (1.0 + jnp.exp(-x))


def _silu(x):
    return x * _sigmoid(x)


def _gelu(x):
    return 0.5 * x * (1.0 + lax.erf(x * (1.0 / math.sqrt(2.0))))


def _rms(x, w):
    return x * lax.rsqrt(jnp.mean(x * x, axis=-1, keepdims=True) + NORM_EPS) * w


TM_A = 512
N_SLAB = 768


def _inproj_kernel(x_ref, nw_ref, w_ref, proj_ref, h_ref):
    h = _rms(x_ref[...], nw_ref[...])
    h_ref[...] = h
    hb = h.astype(bf16)
    for n in range(NP_IN // N_SLAB):
        sl = pl.ds(n * N_SLAB, N_SLAB)
        proj_ref[:, sl] = jnp.dot(hb, w_ref[:, sl], preferred_element_type=f32)


def _inproj(x_all, nw, w_pad):
    return pl.pallas_call(
        _inproj_kernel,
        grid=(T_ALL // TM_A,),
        in_specs=[pl.BlockSpec((TM_A, D_MODEL), lambda i: (i, 0)),
                  pl.BlockSpec((1, D_MODEL), lambda i: (0, 0)),
                  pl.BlockSpec((D_MODEL, NP_IN), lambda i: (0, 0))],
        out_specs=[pl.BlockSpec((TM_A, NP_IN), lambda i: (i, 0)),
                   pl.BlockSpec((TM_A, D_MODEL), lambda i: (i, 0))],
        out_shape=[jax.ShapeDtypeStruct((T_ALL, NP_IN), f32),
                   jax.ShapeDtypeStruct((T_ALL, D_MODEL), f32)],
        compiler_params=pltpu.CompilerParams(dimension_semantics=("arbitrary",),
                                             vmem_limit_bytes=VMEM_LIMIT),
        name="inproj",
    )(x_all, nw, w_pad)


def _mm_kernel(a_ref, b_ref, o_ref):
    o_ref[...] = jnp.dot(a_ref[...].astype(bf16), b_ref[...], preferred_element_type=f32)


def _mm(a, b):
    return pl.pallas_call(
        _mm_kernel,
        out_shape=jax.ShapeDtypeStruct((a.shape[0], b.shape[1]), f32),
        compiler_params=pltpu.CompilerParams(vmem_limit_bytes=VMEM_LIMIT),
        name="shift_proj",
    )(a, b)


def _neumann(L, iters):
    N = -L
    Q = L
    for _ in range(iters):
        Qb = Q.astype(bf16)
        Q = lax.dot_general(Qb, Qb, BNN, preferred_element_type=f32)
        N = N + Q + lax.dot_general(N.astype(bf16), Q.astype(bf16), BNN, preferred_element_type=f32)
    return N


def _cat(parts, axis):
    return parts[0] if len(parts) == 1 else jnp.concatenate(parts, axis=axis)


def _mixer_kernel(*refs, TB, C, G, per_seq):
    it = iter(refs)
    proj_ref = next(it)
    if per_seq:
        gdn_in, conv_in, rwkv_in, pcf_in = next(it), next(it), next(it), next(it)
    convw_ref, v128_ref, v256_ref, sguw_ref, sgub_ref, mu_ref, wup_ref, aup_ref, gup_ref = (
        next(it) for _ in range(9))
    mix_ref, gdn_out, conv_out, rwkv_out = next(it), next(it), next(it), next(it)
    cv_out = next(it) if per_seq else None
    xp, pp = next(it), next(it)

    nchunk = TB // C
    ngrp = C // G
    iters = int(math.log2(G)) - 1

    if per_seq:
        xp[pl.ds(0, 8), :] = jnp.zeros((8, 3 * D_A), f32)
        pp[pl.ds(0, 8), :] = jnp.zeros((8, NC_PAD), f32)
    else:
        @pl.when(pl.program_id(1) == 0)
        def _():
            xp[pl.ds(0, 8), :] = jnp.zeros((8, 3 * D_A), f32)
            pp[pl.ds(0, 8), :] = jnp.zeros((8, NC_PAD), f32)
            gdn_out[...] = jnp.zeros(gdn_out.shape, f32)
            rwkv_out[...] = jnp.zeros(rwkv_out.shape, f32)
    xp[pl.ds(8, TB), :] = proj_ref[:, pl.ds(OFF_Q, 3 * D_A)]
    pp[pl.ds(8, TB), :] = proj_ref[:, pl.ds(OFF_C, NC_PAD)]
    if per_seq:
        for s in range(TB // SEQ_PAD):
            r0 = 8 + s * SEQ_PAD
            xp[pl.ds(r0 + SEQ_LEAD - (CONV_W - 1), CONV_W - 1), :] = conv_in[s]
            pp[pl.ds(r0 + SEQ_LEAD - 1, 1), :] = pcf_in[s]

    rowi = lax.broadcasted_iota(i32, (TB, 1), 0)
    live = (rowi % SEQ_PAD) >= SEQ_LEAD if per_seq else None

    ii = lax.broadcasted_iota(i32, (C, C), 0)
    jj = lax.broadcasted_iota(i32, (C, C), 1)
    same = (ii // G) == (jj // G)
    causal = (ii >= jj) & same
    strict = (ii > jj) & same
    eye = ii == jj
    m_cum = causal.astype(bf16)
    m_grp = same.astype(bf16)

    def conv_cols(r0, c0):
        cs = pl.ds(c0, 128)
        acc = xp[pl.ds(r0 + 8, C), cs] * convw_ref[pl.ds(3, 1), cs]
        for j in range(CONV_W - 1):
            acc = acc + xp[pl.ds(r0 + 5 + j, C), cs] * convw_ref[pl.ds(j, 1), cs]
        return _silu(acc)

    alog_row = v128_ref[pl.ds(0, 1), :]
    dtb_row = v128_ref[pl.ds(1, 1), :]
    gnorm_w = v128_ref[pl.ds(2, 1), :]
    ln_w, ln_b = v256_ref[pl.ds(0, 1), :], v256_ref[pl.ds(1, 1), :]
    w0, a0 = v256_ref[pl.ds(2, 1), :], v256_ref[pl.ds(3, 1), :]
    k_k, k_a, r_k = v256_ref[pl.ds(4, 1), :], v256_ref[pl.ds(5, 1), :], v256_ref[pl.ds(6, 1), :]
    rln_w, rln_b = v256_ref[pl.ds(7, 1), :], v256_ref[pl.ds(8, 1), :]

    l64i = lax.broadcasted_iota(i32, (D_C, D_C), 0) // N_C
    l64j = lax.broadcasted_iota(i32, (D_C, D_C), 1) // N_C
    seg64 = (l64i == l64j).astype(bf16)

    gdn, rwk, lmats = [None] * nchunk, [None] * nchunk, [None] * (8 * nchunk)

    def phase1(c):
        R0 = c * C
        rows = pl.ds(R0, C)
        live_c = live[R0:R0 + C] if per_seq else None

        ba = proj_ref[rows, pl.ds(OFF_BA, 128)]
        beta_all = _sigmoid(ba)
        g_all = -jnp.exp(alog_row) * _softplus(ba + dtb_row)
        if per_seq:
            beta_all = jnp.where(live_c, beta_all, 0.0)
            g_all = jnp.where(live_c, g_all, 0.0)
        pcur = pp[pl.ds(R0 + 8, C), :]
        pprev = pp[pl.ds(R0 + 7, C), :]
        pm = pcur + mu_ref[...] * (pprev - pcur)
        r_ = pm[:, 0:D_C]
        kc = pm[:, D_C:2 * D_C]
        vc = pm[:, 2 * D_C:3 * D_C]
        wd = pm[:, 3 * D_C:3 * D_C + 128]
        ad = pm[:, 3 * D_C + 128:3 * D_C + 256]
        gd = pm[:, 3 * D_C + 256:3 * D_C + 384]
        wl_mm = _bdot(jnp.tanh(wd), wup_ref[...])
        a_mm = _bdot(ad, aup_ref[...])
        gate = _bdot(_sigmoid(gd), gup_ref[...])
        kk = kc * k_k
        kk_ss = _dot01_right(kk * kk, seg64)
        yield

        gc_all = _dot01_left(m_cum, g_all)
        gl_all = _dot01_left(m_grp, g_all)
        w_log = -_softplus(-(w0 + wl_mm)) - 0.5
        logw = -jnp.exp(w_log)
        a_ = _sigmoid(a0 + a_mm)
        kk = kk * lax.rsqrt(kk_ss + 1e-6)
        kc2 = kc * (1.0 + (a_ - 1.0) * k_a)
        if per_seq:
            logw = jnp.where(live_c, logw, 0.0)
            kk = jnp.where(live_c, kk, 0.0)
            kc2 = jnp.where(live_c, kc2, 0.0)
        b_ = kk * a_
        Gc = _dot01_left(m_cum, logw)
        Gl = _dot01_left(m_grp, logw)
        bonus_ss = _dot01_right(r_ * kc2 * r_k, seg64)
        yield

        heads = []
        for h in range(H_A):
            q = conv_cols(R0, OFF_Q + h * HD_A)
            k = conv_cols(R0, OFF_K + h * HD_A)
            v = conv_cols(R0, OFF_V + h * HD_A)
            q = q * lax.rsqrt(jnp.sum(q * q, axis=-1, keepdims=True) + 1e-6) * (HD_A ** -0.5)
            k = k * lax.rsqrt(jnp.sum(k * k, axis=-1, keepdims=True) + 1e-6)
            if per_seq:
                k = jnp.where(live_c, k, 0.0)
            beta = beta_all[:, h:h + 1]
            gcol = gc_all[:, 4 + h:5 + h]
            glr = gl_all[:, 4 + h:5 + h]
            grow = jnp.sum(jnp.where(eye, jnp.broadcast_to(gcol, (C, C)), 0.0), axis=0, keepdims=True)
            decay = jnp.where(causal, jnp.exp(jnp.where(causal, gcol - grow, 0.0)), 0.0)
            eg = jnp.exp(gcol)
            kb = k * beta
            kq = _bdot(jnp.concatenate([kb, q], axis=0), k, NT)
            heads.append(dict(kq=kq, decay=decay,
                              rhs=jnp.concatenate([v * beta, kb * eg], axis=1),
                              q_dec=q * eg, k_dec=k * jnp.exp(glr - gcol), glr=glr))
        gdn[c] = heads
        e_neg = jnp.exp(-Gc)
        e_rem = jnp.exp(Gl - Gc)
        rG = r_ * jnp.exp(Gc)
        kkG = kk * jnp.exp(Gc - logw)
        kN = kc2 * e_neg
        bN = b_ * e_neg
        aalls = []
        for h in range(H_C):
            hs = slice(h * N_C, (h + 1) * N_C)
            aalls.append(_bdot(jnp.concatenate([kkG[:, hs], rG[:, hs]], axis=0),
                               jnp.concatenate([bN[:, hs], kN[:, hs]], axis=0), NT))
        yield

        for h in range(H_A):
            d = gdn[c][h]
            kq = d.pop('kq')
            decay = d.pop('decay')
            lmats[8 * c + h] = jnp.where(strict, kq[:C] * decay, 0.0)
            d['attn'] = kq[C:] * decay
        heads = []
        for h in range(H_C):
            hs = slice(h * N_C, (h + 1) * N_C)
            aall = aalls[h]
            lmats[8 * c + H_A + h] = jnp.where(strict, aall[:C, :C], 0.0)
            akk_k = jnp.where(strict, aall[:C, C:], 0.0)
            ar = jnp.concatenate([jnp.where(causal, aall[C:, C:], 0.0),
                                  -jnp.where(causal, aall[C:, :C], 0.0)], axis=1)
            heads.append(dict(x1=_bdot(akk_k, vc[:, hs]), ar=ar))
        rwk[c] = dict(heads=heads, vc=vc, rG=rG, kkG=kkG, kdec=kc2 * e_rem, bdec=b_ * e_rem,
                      e_last=jnp.exp(Gl), gate=gate, bonus=bonus_ss * vc)
        yield

    def lockstep(gens):
        gens = list(gens)
        while gens:
            alive = []
            for g in gens:
                try:
                    next(g)
                    alive.append(g)
                except StopIteration:
                    pass
            gens = alive

    lockstep(phase1(c) for c in range(nchunk))

    nmats = _neumann(jnp.stack(lmats), iters)

    for c in range(nchunk):
        for h in range(H_A):
            d = gdn[c][h]
            sol = d['rhs'] + _bdot(nmats[8 * c + h], d['rhs'])
            d['u'], d['w'] = sol[:, :HD_A], sol[:, HD_A:]
        for h in range(H_C):
            hs = slice(h * N_C, (h + 1) * N_C)
            d = rwk[c]['heads'][h]
            both = jnp.concatenate([d['x1'], rwk[c]['kkG'][:, hs]], axis=1)
            both = both + _bdot(nmats[8 * c + H_A + h], both)
            d['u_p'], d['w_p'] = both[:, :N_C], both[:, N_C:]

    def gdn_head(c, h):
        rows = pl.ds(c * C, C)
        d = gdn[c][h]
        rq, vn, s_old = [], [], []
        for s in range(ngrp):
            gs = slice(s * G, (s + 1) * G)
            S = gdn_in[s, h] if per_seq else gdn_out[0, h]
            s_old.append(S)
            R = _bdot(jnp.concatenate([d['w'][gs], d['q_dec'][gs]], axis=0), S)
            vn.append(d['u'][gs] - R[:G])
            rq.append(R[G:])
        yield
        v_new = _cat(vn, 0)
        o = _cat(rq, 0) + _bdot(d['attn'], v_new)
        for s in range(ngrp):
            gs = slice(s * G, (s + 1) * G)
            g_last = jnp.exp(d['glr'][s * G + G - 1:s * G + G, :])
            S_new = s_old[s] * g_last + _bdot(d['k_dec'][gs], v_new[gs], TN)
            if per_seq:
                gdn_out[s, h] = S_new
            else:
                gdn_out[0, h] = S_new
        yield
        o = o * lax.rsqrt(jnp.mean(o * o, axis=-1, keepdims=True) + NORM_EPS) * gnorm_w
        z = proj_ref[rows, pl.ds(OFF_Z + h * HD_A, HD_A)]
        mix_ref[rows, pl.ds(h * HD_A, HD_A)] = o * _silu(z)

    def rwkv_head(c, h, ys):
        rc = rwk[c]
        hs = slice(h * N_C, (h + 1) * N_C)
        d = rc['heads'][h]
        V = rc['vc'][:, hs]
        rr, ut, s_old = [], [], []
        for s in range(ngrp):
            gs = slice(s * G, (s + 1) * G)
            S = rwkv_in[s, h] if per_seq else rwkv_out[0, h]
            s_old.append(S)
            R = _bdot(jnp.concatenate([d['w_p'][gs], rc['rG'][gs, hs]], axis=0), S, NT)
            ut.append(d['u_p'][gs] + R[:G])
            rr.append(R[G:])
        yield
        Ut = _cat(ut, 0)
        ys[h] = _cat(rr, 0) + _bdot(d['ar'], jnp.concatenate([V, Ut], axis=0))
        for s in range(ngrp):
            gs = slice(s * G, (s + 1) * G)
            upd = _bdot(jnp.concatenate([V[gs], -Ut[gs]], axis=0),
                        jnp.concatenate([rc['kdec'][gs, hs], rc['bdec'][gs, hs]], axis=0), TN)
            S_new = s_old[s] * rc['e_last'][s * G + G - 1:s * G + G, hs] + upd
            if per_seq:
                rwkv_out[s, h] = S_new
            else:
                rwkv_out[0, h] = S_new
        yield

    for c in range(nchunk):
        rows = pl.ds(c * C, C)
        rc = rwk[c]
        ys = [None] * H_C
        lockstep([gdn_head(c, h) for h in range(H_A)] + [rwkv_head(c, h, ys) for h in range(H_C)])
        y = jnp.concatenate(ys, axis=1)
        mu_y = _dot01_right(y, seg64) * (1.0 / N_C)
        dy = y - mu_y
        var_y = _dot01_right(dy * dy, seg64) * (1.0 / N_C)
        y = dy * lax.rsqrt(var_y + GN_EPS) * rln_w + rln_b
        mix_ref[rows, pl.ds(D_A + D_B, D_C)] = (y + rc['bonus']) * rc['gate']

    for sc in range(TB // SGU_CHUNK):
        rows = pl.ds(sc * SGU_CHUNK, SGU_CHUNK)
        ug = _gelu(proj_ref[rows, pl.ds(OFF_U, D_B)])
        vs = _gelu(proj_ref[rows, pl.ds(OFF_VB, D_B)])
        mu_v = _dot01_right(vs, seg64) * (1.0 / DH_B)
        dv = vs - mu_v
        var_v = _dot01_right(dv * dv, seg64) * (1.0 / DH_B)
        vs = dv * lax.rsqrt(var_v + LN_EPS) * ln_w + ln_b
        if per_seq:
            cv_out[rows, :] = vs
        vsb = vs.astype(bf16)
        outs = [lax.dot_general(sguw_ref[h], vsb[:, h * DH_B:(h + 1) * DH_B], NN, preferred_element_type=f32)
                for h in range(4)]
        mixed = jnp.concatenate(outs, axis=1) + sgub_ref[...]
        mix_ref[rows, pl.ds(D_A, D_B)] = ug * mixed

    if per_seq:
        for s in range(TB // SEQ_PAD):
            conv_out[s] = xp[pl.ds(8 + (s + 1) * SEQ_PAD - (CONV_W - 1), CONV_W - 1), :]
    else:
        conv_out[0] = xp[pl.ds(8 + TB - (CONV_W - 1), CONV_W - 1), :]
        xp[pl.ds(0, 8), :] = xp[pl.ds(TB, 8), :]
        pp[pl.ds(0, 8), :] = pp[pl.ds(TB, 8), :]


def _mixer_weight_specs(nidx):
    z2 = (lambda b, j: (0, 0)) if nidx == 2 else (lambda i: (0, 0))
    z3 = (lambda b, j: (0, 0, 0)) if nidx == 2 else (lambda i: (0, 0, 0))
    return [pl.BlockSpec((CONV_W, 3 * D_A), z2),
            pl.BlockSpec((8, 128), z2),
            pl.BlockSpec((16, D_C), z2),
            pl.BlockSpec((4, SGU_CHUNK, SGU_CHUNK), z3),
            pl.BlockSpec((SGU_CHUNK, D_B), z2),
            pl.BlockSpec((1, NC_PAD), z2),
            pl.BlockSpec((128, D_C), z2),
            pl.BlockSpec((128, D_C), z2),
            pl.BlockSpec((128, D_C), z2)]


def _mixer_prompt(proj, mw):
    nj = SEQ // TB_P
    return pl.pallas_call(
        functools.partial(_mixer_kernel, TB=TB_P, C=GDN_CHUNK, G=GDN_CHUNK, per_seq=False),
        grid=(BATCH, nj),
        in_specs=[pl.BlockSpec((TB_P, NP_IN), lambda b, j: (b * nj + j, 0))] + _mixer_weight_specs(2),
        out_specs=[pl.BlockSpec((TB_P, D_MODEL), lambda b, j: (b * nj + j, 0)),
                   pl.BlockSpec((1, H_A, HD_A, HD_A), lambda b, j: (b, 0, 0, 0)),
                   pl.BlockSpec((1, CONV_W - 1, 3 * D_A), lambda b, j: (b, 0, 0)),
                   pl.BlockSpec((1, H_C, N_C, N_C), lambda b, j: (b, 0, 0, 0))],
        out_shape=[jax.ShapeDtypeStruct((T_ALL, D_MODEL), f32),
                   jax.ShapeDtypeStruct((BATCH, H_A, HD_A, HD_A), f32),
                   jax.ShapeDtypeStruct((BATCH, CONV_W - 1, 3 * D_A), f32),
                   jax.ShapeDtypeStruct((BATCH, H_C, N_C, N_C), f32)],
        scratch_shapes=[pltpu.VMEM((TB_P + 8, 3 * D_A), f32), pltpu.VMEM((TB_P + 8, NC_PAD), f32)],
        compiler_params=pltpu.CompilerParams(dimension_semantics=("arbitrary", "arbitrary"),
                                             vmem_limit_bytes=VMEM_LIMIT),
        name="mixer_prompt",
    )(proj, *mw)


def _mixer_sample_kernel(mix_prev_ref, *refs):
    del mix_prev_ref
    _mixer_kernel(*refs, TB=TB_S, C=TB_S, G=SEQ_PAD, per_seq=True)


def _mixer_sample(proj, mix_prev, s_gdn, s_conv, s_rwkv, pcf, mw):
    nseq = TB_S // SEQ_PAD
    base = T_PROMPT // TB_S
    return pl.pallas_call(
        _mixer_sample_kernel,
        grid=(T_SAMPLE // TB_S,),
        in_specs=[pl.BlockSpec(memory_space=pl.ANY),
                  pl.BlockSpec((TB_S, NP_IN), lambda i: (base + i, 0)),
                  pl.BlockSpec((nseq, H_A, HD_A, HD_A), lambda i: (i, 0, 0, 0)),
                  pl.BlockSpec((nseq, CONV_W - 1, 3 * D_A), lambda i: (i, 0, 0)),
                  pl.BlockSpec((nseq, H_C, N_C, N_C), lambda i: (i, 0, 0, 0)),
                  pl.BlockSpec((nseq, 1, NC_PAD), lambda i: (i, 0, 0))] + _mixer_weight_specs(1),
        out_specs=[pl.BlockSpec((TB_S, D_MODEL), lambda i: (base + i, 0)),
                   pl.BlockSpec((nseq, H_A, HD_A, HD_A), lambda i: (i, 0, 0, 0)),
                   pl.BlockSpec((nseq, CONV_W - 1, 3 * D_A), lambda i: (i, 0, 0)),
                   pl.BlockSpec((nseq, H_C, N_C, N_C), lambda i: (i, 0, 0, 0)),
                   pl.BlockSpec((TB_S, D_B), lambda i: (i, 0))],
        out_shape=[jax.ShapeDtypeStruct((T_ALL, D_MODEL), f32),
                   jax.ShapeDtypeStruct((DEC_BATCH, H_A, HD_A, HD_A), f32),
                   jax.ShapeDtypeStruct((DEC_BATCH, CONV_W - 1, 3 * D_A), f32),
                   jax.ShapeDtypeStruct((DEC_BATCH, H_C, N_C, N_C), f32),
                   jax.ShapeDtypeStruct((T_SAMPLE, D_B), f32)],
        scratch_shapes=[pltpu.VMEM((TB_S + 8, 3 * D_A), f32), pltpu.VMEM((TB_S + 8, NC_PAD), f32)],
        input_output_aliases={0: 0},
        compiler_params=pltpu.CompilerParams(dimension_semantics=("arbitrary",),
                                             vmem_limit_bytes=VMEM_LIMIT),
        name="mixer_sample",
    )(mix_prev, proj, s_gdn, s_conv, s_rwkv, pcf, *mw)


TM_C = 512


def _outproj_kernel(x_ref, mix_ref, wout_ref, nw_ref, rwh_ref, rwl_ref, rb_ref, xg_ref, cnt_ref, run_ref):
    @pl.when(pl.program_id(0) == 0)
    def _():
        run_ref[...] = jnp.zeros(run_ref.shape, f32)

    x2 = x_ref[...] + jnp.dot(mix_ref[...].astype(bf16), wout_ref[...], preferred_element_type=f32)
    xg_ref[:, pl.ds(0, D_MODEL)] = x2
    h2 = _rms(x2, nw_ref[...])
    hh, hl = _split2(h2)
    d = lambda a, b: jnp.dot(a, b, preferred_element_type=f32)
    logits = d(hh, rwh_ref[...]) + d(hl, rwh_ref[...]) + d(hh, rwl_ref[...]) + rb_ref[...]

    lane = lax.broadcasted_iota(i32, logits.shape, 1).astype(f32)
    neg = jnp.float32(-jnp.inf)
    is_g = lane < float(N_GROUPS)
    gl = jnp.where(is_g, logits, neg)
    gmax = jnp.max(gl, axis=-1, keepdims=True)
    gsel = jnp.min(jnp.where(gl == gmax, lane, 128.0), axis=-1, keepdims=True)
    gw = 1.0 / jnp.sum(jnp.where(is_g, jnp.exp(jnp.where(is_g, logits - gmax, 0.0)), 0.0),
                       axis=-1, keepdims=True)
    lo = LANE_E0 + float(EPG) * gsel
    in_grp = (lane >= lo) & (lane < lo + float(EPG))
    el = jnp.where(in_grp, logits, neg)
    t1 = jnp.max(el, axis=-1, keepdims=True)
    i1 = jnp.min(jnp.where(el == t1, lane, 128.0), axis=-1, keepdims=True)
    el2 = jnp.where(lane == i1, neg, el)
    t2 = jnp.max(el2, axis=-1, keepdims=True)
    i2 = jnp.min(jnp.where(el2 == t2, lane, 128.0), axis=-1, keepdims=True)
    e2 = jnp.exp(t2 - t1)
    den = 1.0 + e2
    gates = jnp.where(lane == i1, gw / den, 0.0) + jnp.where(lane == i2, gw * e2 / den, 0.0)

    onehot = jnp.where(lane == gsel, 1.0, 0.0)
    ri = lax.broadcasted_iota(i32, (TM_C, TM_C), 0)
    ci = lax.broadcasted_iota(i32, (TM_C, TM_C), 1)
    before = lax.dot_general((ri > ci).astype(bf16), onehot.astype(bf16), NN, preferred_element_type=f32)
    rank = jnp.sum((before + run_ref[...]) * onehot, axis=-1, keepdims=True)
    run_ref[...] += jnp.sum(onehot, axis=0, keepdims=True)
    cnt_ref[...] = jnp.broadcast_to(run_ref[...], cnt_ref.shape)
    xg_ref[:, pl.ds(D_MODEL, 128)] = (gates + jnp.where(lane == float(LANE_G), gsel, 0.0)
                                      + jnp.where(lane == float(LANE_RANK), rank, 0.0))


def _outproj(x_all, mix, wout, nw, rwh, rwl, rb):
    row = lambda i: (i, 0)
    fix = lambda i: (0, 0)
    return pl.pallas_call(
        _outproj_kernel,
        grid=(T_ALL // TM_C,),
        in_specs=[pl.BlockSpec((TM_C, D_MODEL), row), pl.BlockSpec((TM_C, D_MODEL), row),
                  pl.BlockSpec((D_MODEL, D_MODEL), fix), pl.BlockSpec((1, D_MODEL), fix),
                  pl.BlockSpec((D_MODEL, 128), fix), pl.BlockSpec((D_MODEL, 128), fix),
                  pl.BlockSpec((1, 128), fix)],
        out_specs=[pl.BlockSpec((TM_C, XG_W), row), pl.BlockSpec((8, 128), fix)],
        out_shape=[jax.ShapeDtypeStruct((T_ALL, XG_W), f32), jax.ShapeDtypeStruct((8, 128), f32)],
        scratch_shapes=[pltpu.VMEM((1, 128), f32)],
        compiler_params=pltpu.CompilerParams(dimension_semantics=("arbitrary",),
                                             vmem_limit_bytes=VMEM_LIMIT),
        name="outproj_router",
    )(x_all, mix, wout, nw, rwh, rwl, rb)


TM_E = 512
NT_E = T_ALL // TM_E + N_GROUPS
T_SORT = NT_E * TM_E


def _dispatch_kernel(pos_ref, xg_ref, zeros_hbm, xs_hbm, sem):
    del zeros_hbm
    base = pl.program_id(0) * TM_E

    def issue(r, c):
        p = pos_ref[base + r]
        pltpu.make_async_copy(xg_ref.at[pl.ds(r, 1), :], xs_hbm.at[pl.ds(p, 1), :], sem).start()
        return c

    lax.fori_loop(0, TM_E, issue, 0)
    pltpu.make_async_copy(xg_ref, xs_hbm.at[pl.ds(0, TM_E), :], sem).wait()


def _dispatch(pos, xg, zeros_sorted):
    return pl.pallas_call(
        _dispatch_kernel,
        grid_spec=pltpu.PrefetchScalarGridSpec(
            num_scalar_prefetch=1,
            grid=(T_ALL // TM_E,),
            in_specs=[pl.BlockSpec((TM_E, XG_W), lambda i, pos: (i, 0)),
                      pl.BlockSpec(memory_space=pl.ANY)],
            out_specs=pl.BlockSpec(memory_space=pl.ANY),
            scratch_shapes=[pltpu.SemaphoreType.DMA(())]),
        out_shape=jax.ShapeDtypeStruct((T_SORT, XG_W), f32),
        input_output_aliases={2: 0},
        compiler_params=pltpu.CompilerParams(dimension_semantics=("arbitrary",),
                                             vmem_limit_bytes=VMEM_LIMIT),
        name="moe_dispatch",
    )(pos, xg, zeros_sorted)


def _combine_kernel(pos_ref, ys_hbm, o_ref, sem):
    base = pl.program_id(0) * TM_E

    def issue(r, c):
        p = pos_ref[base + r]
        pltpu.make_async_copy(ys_hbm.at[pl.ds(p, 1), :], o_ref.at[pl.ds(r, 1), :], sem).start()
        return c

    lax.fori_loop(0, TM_E, issue, 0)
    pltpu.make_async_copy(ys_hbm.at[pl.ds(0, TM_E), :], o_ref, sem).wait()


def _combine(pos, ys):
    return pl.pallas_call(
        _combine_kernel,
        grid_spec=pltpu.PrefetchScalarGridSpec(
            num_scalar_prefetch=1,
            grid=(T_ALL // TM_E,),
            in_specs=[pl.BlockSpec(memory_space=pl.ANY)],
            out_specs=pl.BlockSpec((TM_E, D_MODEL), lambda i, pos: (i, 0)),
            scratch_shapes=[pltpu.SemaphoreType.DMA(())]),
        out_shape=jax.ShapeDtypeStruct((T_ALL, D_MODEL), f32),
        compiler_params=pltpu.CompilerParams(dimension_semantics=("arbitrary",),
                                             vmem_limit_bytes=VMEM_LIMIT),
        name="moe_combine",
    )(pos, ys)


def _experts_kernel(tg_ref, tv_ref, xs_ref, nw_ref, wgu_ref, wd_ref, nf_ref, ys_ref, hbuf, ybuf, *, final_norm):
    t = pl.program_id(0)
    e = pl.program_id(1)
    valid = tv_ref[t] == 1

    @pl.when(valid & (e == 0))
    def _():
        x2 = xs_ref[:, pl.ds(0, D_MODEL)]
        hbuf[...] = _rms(x2, nw_ref[...]).astype(bf16)
        ybuf[...] = x2

    @pl.when(valid)
    def _():
        gu = jnp.dot(hbuf[...], wgu_ref[0], preferred_element_type=f32)
        he = _silu(gu[:, :D_FF_E]) * gu[:, D_FF_E:]
        yd = jnp.dot(he.astype(bf16), wd_ref[0], preferred_element_type=f32)
        g = xs_ref[:, pl.ds(D_MODEL, 128)]
        lane = lax.broadcasted_iota(i32, g.shape, 1)
        gcol = jnp.sum(jnp.where(lane == LANE_E0 + EPG * tg_ref[t] + e, g, 0.0), axis=-1, keepdims=True)
        ybuf[...] += gcol * yd

    @pl.when(e == EPG - 1)
    def _():
        y = ybuf[...]
        if final_norm:
            y = _rms(y, nf_ref[...])
        ys_ref[...] = y


def _experts(tile_group, tile_valid, xs, nw, wgu, wd, nf, final_norm):
    return pl.pallas_call(
        functools.partial(_experts_kernel, final_norm=final_norm),
        grid_spec=pltpu.PrefetchScalarGridSpec(
            num_scalar_prefetch=2,
            grid=(NT_E, EPG),
            in_specs=[pl.BlockSpec((TM_E, XG_W), lambda t, e, tg, tv: (t, 0)),
                      pl.BlockSpec((1, D_MODEL), lambda t, e, tg, tv: (0, 0)),
                      pl.BlockSpec((1, D_MODEL, 2 * D_FF_E), lambda t, e, tg, tv: (tg[t] * EPG + e, 0, 0)),
                      pl.BlockSpec((1, D_FF_E, D_MODEL), lambda t, e, tg, tv: (tg[t] * EPG + e, 0, 0)),
                      pl.BlockSpec((1, D_MODEL), lambda t, e, tg, tv: (0, 0))],
            out_specs=pl.BlockSpec((TM_E, D_MODEL), lambda t, e, tg, tv: (t, 0)),
            scratch_shapes=[pltpu.VMEM((TM_E, D_MODEL), bf16), pltpu.VMEM((TM_E, D_MODEL), f32)]),
        out_shape=jax.ShapeDtypeStruct((T_SORT, D_MODEL), f32),
        compiler_params=pltpu.CompilerParams(dimension_semantics=("arbitrary", "arbitrary"),
                                             vmem_limit_bytes=VMEM_LIMIT),
        name="moe_experts",
    )(tile_group, tile_valid, xs, nw, wgu, wd, nf)


def _route_meta(xg, cnt_rows):
    g = xg[:, D_MODEL + LANE_G].astype(i32)
    rank = xg[:, D_MODEL + LANE_RANK].astype(i32)
    cnt = cnt_rows[0, :N_GROUPS].astype(i32)
    padded = ((cnt + TM_E - 1) // TM_E) * TM_E
    off_end = jnp.cumsum(padded)
    off = off_end - padded
    gid = jnp.arange(N_GROUPS, dtype=i32)
    pos = rank + jnp.sum(jnp.where(g[:, None] == gid[None, :], off[None, :], 0), axis=1)
    tile_start = jnp.arange(NT_E, dtype=i32) * TM_E
    tile_group = jnp.minimum(jnp.sum((tile_start[:, None] >= off_end[None, :]).astype(i32), axis=1), N_GROUPS - 1)
    tile_valid = (tile_start < off_end[-1]).astype(i32)
    return pos, tile_group, tile_valid


def _pad_cols(a, n):
    return jnp.pad(a, ((0, 0), (0, n - a.shape[1])))


def _pad_rows(a, n):
    return jnp.pad(a, ((0, n - a.shape[0]), (0, 0)))


def _prep_w_in(w):
    c = w[:, 2568:]
    parts = [w[:, 0:2048], _pad_cols(w[:, 2048:2056], 128), w[:, 2056:2568],
             c[:, 0:768], _pad_cols(c[:, 768:832], 128), _pad_cols(c[:, 832:896], 128), c[:, 896:1024]]
    return jnp.concatenate(parts, axis=1).astype(bf16)


def _prep_mu(mu):
    m = mu[None, :]
    return jnp.concatenate([m[:, 0:768], _pad_cols(m[:, 768:832], 128), _pad_cols(m[:, 832:896], 128),
                            m[:, 896:1024]], axis=1)


def _sgu_mats(sgu_w, sgu_b):
    t = jnp.arange(SGU_CHUNK)
    wm = jnp.where(t[:, None] >= t[None, :], sgu_w, 0.0)
    bias_p = jnp.repeat(jnp.transpose(sgu_b), DH_B, axis=1)
    small = jnp.zeros((4, SEQ_PAD, SEQ_PAD), f32).at[:, SEQ_LEAD:, SEQ_LEAD:].set(wm[:, :DEC_SEQ, :DEC_SEQ])
    eye16 = jnp.eye(TB_S // SEQ_PAD, dtype=f32)
    wm_s = jnp.einsum('ab,hij->haibj', eye16, small).reshape(4, TB_S, TB_S)
    bias_small = jnp.zeros((SEQ_PAD, D_B), f32).at[SEQ_LEAD:].set(bias_p[:DEC_SEQ])
    bias_s = jnp.tile(bias_small, (TB_S // SEQ_PAD, 1))
    return wm.astype(bf16), bias_p, wm_s.astype(bf16), bias_s


def _row(a, n):
    return _pad_cols(a.reshape(1, -1), n)


def kernel(x_prompt, x_sample, state_gdn, state_gdn_conv, state_rwkv, state_rwkv_shift, norm_mix, norm_ffn, norm_final, w_in, gdn_conv_w, gdn_a_log, gdn_dt_bias, gdn_norm_w, sgu_ln_w, sgu_ln_b, sgu_w, sgu_b, rwkv_mu, rwkv_w0, rwkv_w_up, rwkv_a0, rwkv_a_up, rwkv_g_up, rwkv_k_k, rwkv_k_a, rwkv_r_k, rwkv_ln_w, rwkv_ln_b, w_out, router_group_w, router_group_b, router_expert_w, router_expert_b, expert_w_gate, expert_w_up, expert_w_down):
    xs = jnp.pad(x_sample, ((0, 0), (SEQ_LEAD, 0), (0, 0))).reshape(T_SAMPLE, D_MODEL)
    x = jnp.concatenate([x_prompt.reshape(T_PROMPT, D_MODEL), xs], axis=0)

    outs = {k: [] for k in ('gdn_p', 'conv_p', 'rwkv_p', 'shift_p', 'gdn_s', 'conv_s', 'rwkv_s', 'shift_s', 'cv_s')}
    for l in range(DEPTH):
        w_pad = _prep_w_in(w_in[l])
        v128 = jnp.concatenate([
            jnp.pad(gdn_a_log[l].reshape(1, H_A), ((0, 0), (4, 120))),
            jnp.pad(gdn_dt_bias[l].reshape(1, H_A), ((0, 0), (4, 120))),
            gdn_norm_w[l].reshape(1, HD_A), jnp.zeros((5, 128), f32)], axis=0)
        v256 = jnp.concatenate([a.reshape(1, D_C) for a in (
            sgu_ln_w[l], sgu_ln_b[l], rwkv_w0[l], rwkv_a0[l], rwkv_k_k[l], rwkv_k_a[l], rwkv_r_k[l],
            rwkv_ln_w[l], rwkv_ln_b[l])] + [jnp.zeros((7, D_C), f32)], axis=0)
        wm_p, bias_p, wm_s, bias_s = _sgu_mats(sgu_w[l], sgu_b[l])
        common = (gdn_conv_w[l], v128, v256)
        tail = (_prep_mu(rwkv_mu[l]), _pad_rows(rwkv_w_up[l], 128).astype(bf16),
                _pad_rows(rwkv_a_up[l], 128).astype(bf16), rwkv_g_up[l].astype(bf16))
        mw_p = common + (wm_p, bias_p) + tail
        mw_s = common + (wm_s, bias_s) + tail

        proj, h = _inproj(x, norm_mix[l].reshape(1, D_MODEL), w_pad)
        pcf = _mm(state_rwkv_shift[l], w_pad[:, OFF_C:]).reshape(DEC_BATCH, 1, NC_PAD)
        mix, gdn_p, conv_p, rwkv_p = _mixer_prompt(proj, mw_p)
        mix, gdn_s, conv_s, rwkv_s, cv_s = _mixer_sample(
            proj, mix, state_gdn[l], state_gdn_conv[l], state_rwkv[l], pcf, mw_s)

        rw = _pad_cols(jnp.concatenate([router_group_w[l], router_expert_w[l]], axis=1), 128)
        rwh = rw.astype(bf16)
        rwl = (rw - rwh.astype(f32)).astype(bf16)
        rb = _row(jnp.concatenate([router_group_b[l], router_expert_b[l]]), 128)
        nw_ffn = norm_ffn[l].reshape(1, D_MODEL)
        xg, cnt_rows = _outproj(x, mix, w_out[l].astype(bf16), nw_ffn, rwh, rwl, rb)
        pos, tile_group, tile_valid = _route_meta(xg, cnt_rows)
        xsorted = _dispatch(pos, xg, jnp.zeros((T_SORT, XG_W), f32))
        wgu = jnp.concatenate([expert_w_gate[l], expert_w_up[l]], axis=-1).astype(bf16)
        ys = _experts(tile_group, tile_valid, xsorted, nw_ffn, wgu, expert_w_down[l].astype(bf16),
                      norm_final.reshape(1, D_MODEL), final_norm=(l == DEPTH - 1))
        x = _combine(pos, ys)

        outs['gdn_p'].append(gdn_p)
        outs['conv_p'].append(conv_p)
        outs['rwkv_p'].append(rwkv_p)
        outs['shift_p'].append(h[:T_PROMPT].reshape(BATCH, SEQ, D_MODEL)[:, -1])
        outs['gdn_s'].append(gdn_s)
        outs['conv_s'].append(conv_s)
        outs['rwkv_s'].append(rwkv_s)
        outs['shift_s'].append(h[T_PROMPT:].reshape(DEC_BATCH, SEQ_PAD, D_MODEL)[:, -1])
        outs['cv_s'].append(cv_s.reshape(DEC_BATCH, SEQ_PAD, D_B)[:, SEQ_LEAD:])

    y_prompt = x[:T_PROMPT].reshape(BATCH, SEQ, D_MODEL)
    y_sample = x[T_PROMPT:].reshape(DEC_BATCH, SEQ_PAD, D_MODEL)[:, SEQ_LEAD:]
    st = lambda k: jnp.stack(outs[k])
    return (y_prompt, y_sample, st('gdn_p'), st('conv_p'), st('rwkv_p'), st('shift_p'),
            st('gdn_s'), st('conv_s'), st('rwkv_s'), st('shift_s'), st('cv_s'))
```

```python
import functools
import math

import jax
import jax.numpy as jnp
from jax import lax
from jax.experimental import pallas as pl
from jax.experimental.pallas import tpu as pltpu

f32 = jnp.float32
bf16 = jnp.bfloat16
i32 = jnp.int32

D_MODEL = 1024
BATCH = 8
SEQ = 2048
DEPTH = 2
DEC_BATCH = 128
DEC_SEQ = 4
H_A = 4
HD_A = 128
D_A = 512
CONV_W = 4
D_B = 256
DH_B = 64
SGU_CHUNK = 128
H_C = 4
N_C = 64
D_C = 256
N_GROUPS = 4
EPG = 4
D_FF_E = 512
NORM_EPS = 1e-6
LN_EPS = 1e-5
GN_EPS = 64e-5

SEQ_PAD = 8
SEQ_LEAD = SEQ_PAD - DEC_SEQ
T_PROMPT = BATCH * SEQ
T_SAMPLE = DEC_BATCH * SEQ_PAD
T_ALL = T_PROMPT + T_SAMPLE
TB_P = 256
TB_S = 128
GDN_CHUNK = 64

OFF_Q, OFF_K, OFF_V, OFF_Z, OFF_BA, OFF_U, OFF_VB, OFF_C = 0, 512, 1024, 1536, 2048, 2176, 2432, 2688
NP_IN = 3840
NC_PAD = NP_IN - OFF_C

XG_W = D_MODEL + 128
LANE_G, LANE_RANK, LANE_E0 = 0, 1, 4

VMEM_LIMIT = 48 * 1024 * 1024


NN = (((1,), (0,)), ((), ()))
NT = (((1,), (1,)), ((), ()))
TN = (((0,), (0,)), ((), ()))
BNN = (((2,), (1,)), ((0,), (0,)))


def _bdot(a, b, dims=NN):
    return lax.dot_general(a.astype(bf16), b.astype(bf16), dims, preferred_element_type=f32)


def _split2(x):
    hi = x.astype(bf16)
    lo = (x - hi.astype(f32)).astype(bf16)
    return hi, lo


def _split3(x):
    hi = x.astype(bf16)
    r = x - hi.astype(f32)
    mid = r.astype(bf16)
    lo = (r - mid.astype(f32)).astype(bf16)
    return hi, mid, lo


def _dot01_left(m01, x):
    hi, mid, lo = _split3(x)
    d = lambda p: lax.dot_general(m01, p, NN, preferred_element_type=f32)
    return d(hi) + d(mid) + d(lo)


def _dot01_right(x, m01):
    hi, lo = _split2(x)
    d = lambda p: lax.dot_general(p, m01, NN, preferred_element_type=f32)
    return d(hi) + d(lo)


def _softplus(x):
    return jnp.maximum(x, 0.0) + jnp.log1p(jnp.exp(-jnp.abs(x)))


def _sigmoid(x):
    return 1.0 / (1.0 + jnp.exp(-x))


def _silu(x):
    return x * _sigmoid(x)


def _gelu(x):
    return 0.5 * x * (1.0 + lax.erf(x * (1.0 / math.sqrt(2.0))))


def _rms(x, w):
    return x * lax.rsqrt(jnp.mean(x * x, axis=-1, keepdims=True) + NORM_EPS) * w


TM_A = 512
N_SLAB = 768


def _inproj_kernel(x_ref, nw_ref, w_ref, proj_ref, h8_ref, hscr):
    h = _rms(x_ref[...], nw_ref[...])
    for k in range(D_MODEL // 128):
        hscr[k] = h[:, k * 128:(k + 1) * 128]
        h8_ref[:, pl.ds(k * 128, 128)] = hscr[k, pl.ds(SEQ_PAD - 1, TM_A // SEQ_PAD, stride=SEQ_PAD), :]
    hb = h.astype(bf16)
    for n in range(NP_IN // N_SLAB):
        sl = pl.ds(n * N_SLAB, N_SLAB)
        proj_ref[:, sl] = jnp.dot(hb, w_ref[0, :, sl], preferred_element_type=f32)


def _inproj(x_all, nw, w_pad, l):
    return pl.pallas_call(
        _inproj_kernel,
        grid=(T_ALL // TM_A,),
        in_specs=[pl.BlockSpec((TM_A, D_MODEL), lambda i: (i, 0)),
                  pl.BlockSpec((1, D_MODEL), lambda i: (0, 0)),
                  pl.BlockSpec((1, D_MODEL, NP_IN), lambda i: (l, 0, 0))],
        out_specs=[pl.BlockSpec((TM_A, NP_IN), lambda i: (i, 0)),
                   pl.BlockSpec((TM_A // SEQ_PAD, D_MODEL), lambda i: (i, 0))],
        out_shape=[jax.ShapeDtypeStruct((T_ALL, NP_IN), f32),
                   jax.ShapeDtypeStruct((T_ALL // SEQ_PAD, D_MODEL), f32)],
        scratch_shapes=[pltpu.VMEM((D_MODEL // 128, TM_A, 128), f32)],
        compiler_params=pltpu.CompilerParams(dimension_semantics=("arbitrary",),
                                             vmem_limit_bytes=VMEM_LIMIT),
        name="inproj",
    )(x_all, nw, w_pad)


def _mm_kernel(a_ref, b_ref, o_ref):
    o_ref[...] = jnp.dot(a_ref[...].astype(bf16), b_ref[...], preferred_element_type=f32)


def _mm(a, b):
    return pl.pallas_call(
        _mm_kernel,
        out_shape=jax.ShapeDtypeStruct((a.shape[0], b.shape[1]), f32),
        compiler_params=pltpu.CompilerParams(vmem_limit_bytes=VMEM_LIMIT),
        name="shift_proj",
    )(a, b)


def _neumann(L, iters):
    N = -L
    Q = L
    for _ in range(iters):
        Qb = Q.astype(bf16)
        Q = lax.dot_general(Qb, Qb, BNN, preferred_element_type=f32)
        N = N + Q + lax.dot_general(N.astype(bf16), Q.astype(bf16), BNN, preferred_element_type=f32)
    return N


def _cat(parts, axis):
    return parts[0] if len(parts) == 1 else jnp.concatenate(parts, axis=axis)


def _mixer_kernel(*refs, TB, C, G, per_seq):
    it = iter(refs)
    proj_ref = next(it)
    if per_seq:
        gdn_in, conv_in, rwkv_in, pcf_in = next(it), next(it), next(it), next(it)
    convw_ref, v128_ref, v256_ref, sguw_ref, sgub_ref, mu_ref, wup_ref, aup_ref, gup_ref = (
        next(it) for _ in range(9))
    mix_ref, gdn_out, conv_out, rwkv_out = next(it), next(it), next(it), next(it)
    cv_out = next(it) if per_seq else None
    xp, pp = next(it), next(it)

    nchunk = TB // C
    ngrp = C // G
    iters = int(math.log2(G)) - 1

    if per_seq:
        xp[pl.ds(0, 8), :] = jnp.zeros((8, 3 * D_A), f32)
        pp[pl.ds(0, 8), :] = jnp.zeros((8, NC_PAD), f32)
    else:
        @pl.when(pl.program_id(1) == 0)
        def _():
            xp[pl.ds(0, 8), :] = jnp.zeros((8, 3 * D_A), f32)
            pp[pl.ds(0, 8), :] = jnp.zeros((8, NC_PAD), f32)
            gdn_out[...] = jnp.zeros(gdn_out.shape, f32)
            rwkv_out[...] = jnp.zeros(rwkv_out.shape, f32)
    xp[pl.ds(8, TB), :] = proj_ref[:, pl.ds(OFF_Q, 3 * D_A)]
    pp[pl.ds(8, TB), :] = proj_ref[:, pl.ds(OFF_C, NC_PAD)]
    if per_seq:
        for s in range(TB // SEQ_PAD):
            r0 = 8 + s * SEQ_PAD
            xp[pl.ds(r0 + SEQ_LEAD - (CONV_W - 1), CONV_W - 1), :] = conv_in[s]
            pp[pl.ds(r0 + SEQ_LEAD - 1, 1), :] = pcf_in[s]

    rowi = lax.broadcasted_iota(i32, (TB, 1), 0)
    live = (rowi % SEQ_PAD) >= SEQ_LEAD if per_seq else None

    ii = lax.broadcasted_iota(i32, (C, C), 0)
    jj = lax.broadcasted_iota(i32, (C, C), 1)
    same = (ii // G) == (jj // G)
    causal = (ii >= jj) & same
    strict = (ii > jj) & same
    eye = ii == jj
    m_cum = causal.astype(bf16)
    m_grp = same.astype(bf16)

    def conv_cols(r0, c0):
        cs = pl.ds(c0, 128)
        acc = xp[pl.ds(r0 + 8, C), cs] * convw_ref[pl.ds(3, 1), cs]
        for j in range(CONV_W - 1):
            acc = acc + xp[pl.ds(r0 + 5 + j, C), cs] * convw_ref[pl.ds(j, 1), cs]
        return _silu(acc)

    alog_row = v128_ref[pl.ds(0, 1), :]
    dtb_row = v128_ref[pl.ds(1, 1), :]
    gnorm_w = v128_ref[pl.ds(2, 1), :]
    ln_w, ln_b = v256_ref[pl.ds(0, 1), :], v256_ref[pl.ds(1, 1), :]
    w0, a0 = v256_ref[pl.ds(2, 1), :], v256_ref[pl.ds(3, 1), :]
    k_k, k_a, r_k = v256_ref[pl.ds(4, 1), :], v256_ref[pl.ds(5, 1), :], v256_ref[pl.ds(6, 1), :]
    rln_w, rln_b = v256_ref[pl.ds(7, 1), :], v256_ref[pl.ds(8, 1), :]

    l64i = lax.broadcasted_iota(i32, (D_C, D_C), 0) // N_C
    l64j = lax.broadcasted_iota(i32, (D_C, D_C), 1) // N_C
    seg64 = (l64i == l64j).astype(bf16)

    gdn, rwk, lmats = [None] * nchunk, [None] * nchunk, [None] * (8 * nchunk)

    def phase1(c):
        R0 = c * C
        rows = pl.ds(R0, C)
        live_c = live[R0:R0 + C] if per_seq else None

        ba = proj_ref[rows, pl.ds(OFF_BA, 128)]
        beta_all = _sigmoid(ba)
        g_all = -jnp.exp(alog_row) * _softplus(ba + dtb_row)
        if per_seq:
            beta_all = jnp.where(live_c, beta_all, 0.0)
            g_all = jnp.where(live_c, g_all, 0.0)
        pcur = pp[pl.ds(R0 + 8, C), :]
        pprev = pp[pl.ds(R0 + 7, C), :]
        pm = pcur + mu_ref[...] * (pprev - pcur)
        r_ = pm[:, 0:D_C]
        kc = pm[:, D_C:2 * D_C]
        vc = pm[:, 2 * D_C:3 * D_C]
        wd = pm[:, 3 * D_C:3 * D_C + 128]
        ad = pm[:, 3 * D_C + 128:3 * D_C + 256]
        gd = pm[:, 3 * D_C + 256:3 * D_C + 384]
        wl_mm = _bdot(jnp.tanh(wd), wup_ref[...])
        a_mm = _bdot(ad, aup_ref[...])
        gate = _bdot(_sigmoid(gd), gup_ref[...])
        kk = kc * k_k
        kk_ss = _dot01_right(kk * kk, seg64)
        yield

        gc_all = _dot01_left(m_cum, g_all)
        gl_all = _dot01_left(m_grp, g_all)
        w_log = -_softplus(-(w0 + wl_mm)) - 0.5
        logw = -jnp.exp(w_log)
        a_ = _sigmoid(a0 + a_mm)
        kk = kk * lax.rsqrt(kk_ss + 1e-6)
        kc2 = kc * (1.0 + (a_ - 1.0) * k_a)
        if per_seq:
            logw = jnp.where(live_c, logw, 0.0)
            kk = jnp.where(live_c, kk, 0.0)
            kc2 = jnp.where(live_c, kc2, 0.0)
        b_ = kk * a_
        Gc = _dot01_left(m_cum, logw)
        Gl = _dot01_left(m_grp, logw)
        bonus_ss = _dot01_right(r_ * kc2 * r_k, seg64)
        yield

        heads = []
        for h in range(H_A):
            q = conv_cols(R0, OFF_Q + h * HD_A)
            k = conv_cols(R0, OFF_K + h * HD_A)
            v = conv_cols(R0, OFF_V + h * HD_A)
            q = q * lax.rsqrt(jnp.sum(q * q, axis=-1, keepdims=True) + 1e-6) * (HD_A ** -0.5)
            k = k * lax.rsqrt(jnp.sum(k * k, axis=-1, keepdims=True) + 1e-6)
            if per_seq:
                k = jnp.where(live_c, k, 0.0)
            beta = beta_all[:, h:h + 1]
            gcol = gc_all[:, 4 + h:5 + h]
            glr = gl_all[:, 4 + h:5 + h]
            grow = jnp.sum(jnp.where(eye, jnp.broadcast_to(gcol, (C, C)), 0.0), axis=0, keepdims=True)
            decay = jnp.where(causal, jnp.exp(jnp.where(causal, gcol - grow, 0.0)), 0.0)
            eg = jnp.exp(gcol)
            kb = k * beta
            kq = _bdot(jnp.concatenate([kb, q], axis=0), k, NT)
            heads.append(dict(kq=kq, decay=decay,
                              rhs=jnp.concatenate([v * beta, kb * eg], axis=1),
                              q_dec=q * eg, k_dec=k * jnp.exp(glr - gcol), glr=glr))
        gdn[c] = heads
        e_neg = jnp.exp(-Gc)
        e_rem = jnp.exp(Gl - Gc)
        rG = r_ * jnp.exp(Gc)
        kkG = kk * jnp.exp(Gc - logw)
        kN = kc2 * e_neg
        bN = b_ * e_neg
        aalls = []
        for h in range(H_C):
            hs = slice(h * N_C, (h + 1) * N_C)
            aalls.append(_bdot(jnp.concatenate([kkG[:, hs], rG[:, hs]], axis=0),
                               jnp.concatenate([bN[:, hs], kN[:, hs]], axis=0), NT))
        yield

        for h in range(H_A):
            d = gdn[c][h]
            kq = d.pop('kq')
            decay = d.pop('decay')
            lmats[8 * c + h] = jnp.where(strict, kq[:C] * decay, 0.0)
            d['attn'] = kq[C:] * decay
        heads = []
        for h in range(H_C):
            hs = slice(h * N_C, (h + 1) * N_C)
            aall = aalls[h]
            lmats[8 * c + H_A + h] = jnp.where(strict, aall[:C, :C], 0.0)
            akk_k = jnp.where(strict, aall[:C, C:], 0.0)
            ar = jnp.concatenate([jnp.where(causal, aall[C:, C:], 0.0),
                                  -jnp.where(causal, aall[C:, :C], 0.0)], axis=1)
            heads.append(dict(x1=_bdot(akk_k, vc[:, hs]), ar=ar))
        rwk[c] = dict(heads=heads, vc=vc, rG=rG, kkG=kkG, kdec=kc2 * e_rem, bdec=b_ * e_rem,
                      e_last=jnp.exp(Gl), gate=gate, bonus=bonus_ss * vc)
        yield

    def lockstep(gens):
        gens = list(gens)
        while gens:
            alive = []
            for g in gens:
                try:
                    next(g)
                    alive.append(g)
                except StopIteration:
                    pass
            gens = alive

    lockstep(phase1(c) for c in range(nchunk))

    nmats = _neumann(jnp.stack(lmats), iters)

    for c in range(nchunk):
        for h in range(H_A):
            d = gdn[c][h]
            sol = d['rhs'] + _bdot(nmats[8 * c + h], d['rhs'])
            d['u'], d['w'] = sol[:, :HD_A], sol[:, HD_A:]
        for h in range(H_C):
            hs = slice(h * N_C, (h + 1) * N_C)
            d = rwk[c]['heads'][h]
            both = jnp.concatenate([d['x1'], rwk[c]['kkG'][:, hs]], axis=1)
            both = both + _bdot(nmats[8 * c + H_A + h], both)
            d['u_p'], d['w_p'] = both[:, :N_C], both[:, N_C:]

    def gdn_head(c, h):
        rows = pl.ds(c * C, C)
        d = gdn[c][h]
        rq, vn, s_old = [], [], []
        for s in range(ngrp):
            gs = slice(s * G, (s + 1) * G)
            S = gdn_in[s, h] if per_seq else gdn_out[0, h]
            s_old.append(S)
            R = _bdot(jnp.concatenate([d['w'][gs], d['q_dec'][gs]], axis=0), S)
            vn.append(d['u'][gs] - R[:G])
            rq.append(R[G:])
        yield
        v_new = _cat(vn, 0)
        o = _cat(rq, 0) + _bdot(d['attn'], v_new)
        for s in range(ngrp):
            gs = slice(s * G, (s + 1) * G)
            g_last = jnp.exp(d['glr'][s * G + G - 1:s * G + G, :])
            S_new = s_old[s] * g_last + _bdot(d['k_dec'][gs], v_new[gs], TN)
            if per_seq:
                gdn_out[s, h] = S_new
            else:
                gdn_out[0, h] = S_new
        yield
        o = o * lax.rsqrt(jnp.mean(o * o, axis=-1, keepdims=True) + NORM_EPS) * gnorm_w
        z = proj_ref[rows, pl.ds(OFF_Z + h * HD_A, HD_A)]
        mix_ref[rows, pl.ds(h * HD_A, HD_A)] = o * _silu(z)

    def rwkv_head(c, h, ys):
        rc = rwk[c]
        hs = slice(h * N_C, (h + 1) * N_C)
        d = rc['heads'][h]
        V = rc['vc'][:, hs]
        rr, ut, s_old = [], [], []
        for s in range(ngrp):
            gs = slice(s * G, (s + 1) * G)
            S = rwkv_in[s, h] if per_seq else rwkv_out[0, h]
            s_old.append(S)
            R = _bdot(jnp.concatenate([d['w_p'][gs], rc['rG'][gs, hs]], axis=0), S, NT)
            ut.append(d['u_p'][gs] + R[:G])
            rr.append(R[G:])
        yield
        Ut = _cat(ut, 0)
        ys[h] = _cat(rr, 0) + _bdot(d['ar'], jnp.concatenate([V, Ut], axis=0))
        for s in range(ngrp):
            gs = slice(s * G, (s + 1) * G)
            upd = _bdot(jnp.concatenate([V[gs], -Ut[gs]], axis=0),
                        jnp.concatenate([rc['kdec'][gs, hs], rc['bdec'][gs, hs]], axis=0), TN)
            S_new = s_old[s] * rc['e_last'][s * G + G - 1:s * G + G, hs] + upd
            if per_seq:
                rwkv_out[s, h] = S_new
            else:
                rwkv_out[0, h] = S_new
        yield

    for c in range(nchunk):
        rows = pl.ds(c * C, C)
        rc = rwk[c]
        ys = [None] * H_C
        lockstep([gdn_head(c, h) for h in range(H_A)] + [rwkv_head(c, h, ys) for h in range(H_C)])
        y = jnp.concatenate(ys, axis=1)
        mu_y = _dot01_right(y, seg64) * (1.0 / N_C)
        dy = y - mu_y
        var_y = _dot01_right(dy * dy, seg64) * (1.0 / N_C)
        y = dy * lax.rsqrt(var_y + GN_EPS) * rln_w + rln_b
        mix_ref[rows, pl.ds(D_A + D_B, D_C)] = (y + rc['bonus']) * rc['gate']

    for sc in range(TB // SGU_CHUNK):
        rows = pl.ds(sc * SGU_CHUNK, SGU_CHUNK)
        ug = _gelu(proj_ref[rows, pl.ds(OFF_U, D_B)])
        vs = _gelu(proj_ref[rows, pl.ds(OFF_VB, D_B)])
        mu_v = _dot01_right(vs, seg64) * (1.0 / DH_B)
        dv = vs - mu_v
        var_v = _dot01_right(dv * dv, seg64) * (1.0 / DH_B)
        vs = dv * lax.rsqrt(var_v + LN_EPS) * ln_w + ln_b
        if per_seq:
            cv_out[rows, :] = vs
        vsb = vs.astype(bf16)
        outs = [lax.dot_general(sguw_ref[h], vsb[:, h * DH_B:(h + 1) * DH_B], NN, preferred_element_type=f32)
                for h in range(4)]
        mixed = jnp.concatenate(outs, axis=1) + sgub_ref[...]
        mix_ref[rows, pl.ds(D_A, D_B)] = ug * mixed

    if per_seq:
        for s in range(TB // SEQ_PAD):
            conv_out[s] = xp[pl.ds(8 + (s + 1) * SEQ_PAD - (CONV_W - 1), CONV_W - 1), :]
    else:
        conv_out[0] = xp[pl.ds(8 + TB - (CONV_W - 1), CONV_W - 1), :]
        xp[pl.ds(0, 8), :] = xp[pl.ds(TB, 8), :]
        pp[pl.ds(0, 8), :] = pp[pl.ds(TB, 8), :]


def _mixer_weight_specs(nidx):
    z2 = (lambda b, j: (0, 0)) if nidx == 2 else (lambda i: (0, 0))
    z3 = (lambda b, j: (0, 0, 0)) if nidx == 2 else (lambda i: (0, 0, 0))
    return [pl.BlockSpec((CONV_W, 3 * D_A), z2),
            pl.BlockSpec((8, 128), z2),
            pl.BlockSpec((16, D_C), z2),
            pl.BlockSpec((4, SGU_CHUNK, SGU_CHUNK), z3),
            pl.BlockSpec((SGU_CHUNK, D_B), z2),
            pl.BlockSpec((1, NC_PAD), z2),
            pl.BlockSpec((128, D_C), z2),
            pl.BlockSpec((128, D_C), z2),
            pl.BlockSpec((128, D_C), z2)]


def _mixer_prompt(proj, mw):
    nj = SEQ // TB_P
    return pl.pallas_call(
        functools.partial(_mixer_kernel, TB=TB_P, C=GDN_CHUNK, G=GDN_CHUNK, per_seq=False),
        grid=(BATCH, nj),
        in_specs=[pl.BlockSpec((TB_P, NP_IN), lambda b, j: (b * nj + j, 0))] + _mixer_weight_specs(2),
        out_specs=[pl.BlockSpec((TB_P, D_MODEL), lambda b, j: (b * nj + j, 0)),
                   pl.BlockSpec((1, H_A, HD_A, HD_A), lambda b, j: (b, 0, 0, 0)),
                   pl.BlockSpec((1, CONV_W - 1, 3 * D_A), lambda b, j: (b, 0, 0)),
                   pl.BlockSpec((1, H_C, N_C, N_C), lambda b, j: (b, 0, 0, 0))],
        out_shape=[jax.ShapeDtypeStruct((T_PROMPT, D_MODEL), f32),
                   jax.ShapeDtypeStruct((BATCH, H_A, HD_A, HD_A), f32),
                   jax.ShapeDtypeStruct((BATCH, CONV_W - 1, 3 * D_A), f32),
                   jax.ShapeDtypeStruct((BATCH, H_C, N_C, N_C), f32)],
        scratch_shapes=[pltpu.VMEM((TB_P + 8, 3 * D_A), f32), pltpu.VMEM((TB_P + 8, NC_PAD), f32)],
        compiler_params=pltpu.CompilerParams(dimension_semantics=("arbitrary", "arbitrary"),
                                             vmem_limit_bytes=VMEM_LIMIT),
        name="mixer_prompt",
    )(proj, *mw)


def _mixer_sample(proj, s_gdn, s_conv, s_rwkv, pcf, mw):
    nseq = TB_S // SEQ_PAD
    base = T_PROMPT // TB_S
    return pl.pallas_call(
        functools.partial(_mixer_kernel, TB=TB_S, C=TB_S, G=SEQ_PAD, per_seq=True),
        grid=(T_SAMPLE // TB_S,),
        in_specs=[pl.BlockSpec((TB_S, NP_IN), lambda i: (base + i, 0)),
                  pl.BlockSpec((nseq, H_A, HD_A, HD_A), lambda i: (i, 0, 0, 0)),
                  pl.BlockSpec((nseq, CONV_W - 1, 3 * D_A), lambda i: (i, 0, 0)),
                  pl.BlockSpec((nseq, H_C, N_C, N_C), lambda i: (i, 0, 0, 0)),
                  pl.BlockSpec((nseq, 1, NC_PAD), lambda i: (i, 0, 0))] + _mixer_weight_specs(1),
        out_specs=[pl.BlockSpec((TB_S, D_MODEL), lambda i: (i, 0)),
                   pl.BlockSpec((nseq, H_A, HD_A, HD_A), lambda i: (i, 0, 0, 0)),
                   pl.BlockSpec((nseq, CONV_W - 1, 3 * D_A), lambda i: (i, 0, 0)),
                   pl.BlockSpec((nseq, H_C, N_C, N_C), lambda i: (i, 0, 0, 0)),
                   pl.BlockSpec((TB_S, D_B), lambda i: (i, 0))],
        out_shape=[jax.ShapeDtypeStruct((T_SAMPLE, D_MODEL), f32),
                   jax.ShapeDtypeStruct((DEC_BATCH, H_A, HD_A, HD_A), f32),
                   jax.ShapeDtypeStruct((DEC_BATCH, CONV_W - 1, 3 * D_A), f32),
                   jax.ShapeDtypeStruct((DEC_BATCH, H_C, N_C, N_C), f32),
                   jax.ShapeDtypeStruct((T_SAMPLE, D_B), f32)],
        scratch_shapes=[pltpu.VMEM((TB_S + 8, 3 * D_A), f32), pltpu.VMEM((TB_S + 8, NC_PAD), f32)],
        compiler_params=pltpu.CompilerParams(dimension_semantics=("arbitrary",),
                                             vmem_limit_bytes=VMEM_LIMIT),
        name="mixer_sample",
    )(proj, s_gdn, s_conv, s_rwkv, pcf, *mw)


TM_C = 512


def _outproj_kernel(x_ref, mixp_ref, mixs_ref, wout_ref, nw_ref, rwh_ref, rwl_ref, rb_ref, xg_ref, cnt_ref, run_ref):
    i = pl.program_id(0)

    @pl.when(i == 0)
    def _():
        run_ref[...] = jnp.zeros(run_ref.shape, f32)

    mix = jnp.where(i < T_PROMPT // TM_C, mixp_ref[...], mixs_ref[...])
    x2 = x_ref[...] + jnp.dot(mix.astype(bf16), wout_ref[0], preferred_element_type=f32)
    xg_ref[:, pl.ds(0, D_MODEL)] = x2
    h2 = _rms(x2, nw_ref[...])
    hh, hl = _split2(h2)
    d = lambda a, b: jnp.dot(a, b, preferred_element_type=f32)
    logits = d(hh, rwh_ref[...]) + d(hl, rwh_ref[...]) + d(hh, rwl_ref[...]) + rb_ref[...]

    lane = lax.broadcasted_iota(i32, logits.shape, 1).astype(f32)
    neg = jnp.float32(-jnp.inf)
    is_g = lane < float(N_GROUPS)
    gl = jnp.where(is_g, logits, neg)
    gmax = jnp.max(gl, axis=-1, keepdims=True)
    gsel = jnp.min(jnp.where(gl == gmax, lane, 128.0), axis=-1, keepdims=True)
    gw = 1.0 / jnp.sum(jnp.where(is_g, jnp.exp(jnp.where(is_g, logits - gmax, 0.0)), 0.0),
                       axis=-1, keepdims=True)
    lo = LANE_E0 + float(EPG) * gsel
    in_grp = (lane >= lo) & (lane < lo + float(EPG))
    el = jnp.where(in_grp, logits, neg)
    t1 = jnp.max(el, axis=-1, keepdims=True)
    i1 = jnp.min(jnp.where(el == t1, lane, 128.0), axis=-1, keepdims=True)
    el2 = jnp.where(lane == i1, neg, el)
    t2 = jnp.max(el2, axis=-1, keepdims=True)
    i2 = jnp.min(jnp.where(el2 == t2, lane, 128.0), axis=-1, keepdims=True)
    e2 = jnp.exp(t2 - t1)
    den = 1.0 + e2
    gates = jnp.where(lane == i1, gw / den, 0.0) + jnp.where(lane == i2, gw * e2 / den, 0.0)

    onehot = jnp.where(lane == gsel, 1.0, 0.0)
    ri = lax.broadcasted_iota(i32, (TM_C, TM_C), 0)
    ci = lax.broadcasted_iota(i32, (TM_C, TM_C), 1)
    before = lax.dot_general((ri > ci).astype(bf16), onehot.astype(bf16), NN, preferred_element_type=f32)
    rank = jnp.sum((before + run_ref[...]) * onehot, axis=-1, keepdims=True)
    run_ref[...] += jnp.sum(onehot, axis=0, keepdims=True)
    cnt_ref[...] = jnp.broadcast_to(run_ref[...], cnt_ref.shape)
    xg_ref[:, pl.ds(D_MODEL, 128)] = (gates + jnp.where(lane == float(LANE_G), gsel, 0.0)
                                      + jnp.where(lane == float(LANE_RANK), rank, 0.0))


def _outproj(x_all, mix_p, mix_s, wout, nw, rwh, rwl, rb, l):
    row = lambda i: (i, 0)
    fix = lambda i: (0, 0)
    npt = T_PROMPT // TM_C
    return pl.pallas_call(
        _outproj_kernel,
        grid=(T_ALL // TM_C,),
        in_specs=[pl.BlockSpec((TM_C, D_MODEL), row),
                  pl.BlockSpec((TM_C, D_MODEL), lambda i: (jnp.minimum(i, npt - 1), 0)),
                  pl.BlockSpec((TM_C, D_MODEL), lambda i: (jnp.maximum(i - npt, 0), 0)),
                  pl.BlockSpec((1, D_MODEL, D_MODEL), lambda i: (l, 0, 0)), pl.BlockSpec((1, D_MODEL), fix),
                  pl.BlockSpec((D_MODEL, 128), fix), pl.BlockSpec((D_MODEL, 128), fix),
                  pl.BlockSpec((1, 128), fix)],
        out_specs=[pl.BlockSpec((TM_C, XG_W), row), pl.BlockSpec((8, 128), fix)],
        out_shape=[jax.ShapeDtypeStruct((T_ALL, XG_W), f32), jax.ShapeDtypeStruct((8, 128), f32)],
        scratch_shapes=[pltpu.VMEM((1, 128), f32)],
        compiler_params=pltpu.CompilerParams(dimension_semantics=("arbitrary",),
                                             vmem_limit_bytes=VMEM_LIMIT),
        name="outproj_router",
    )(x_all, mix_p, mix_s, wout, nw, rwh, rwl, rb)


TM_E = 512
NT_E = T_ALL // TM_E + N_GROUPS
T_SORT = NT_E * TM_E
DMA_UNROLL = 8


def _dispatch_kernel(pos_ref, xg_ref, zeros_hbm, xs_hbm, sem):
    del zeros_hbm
    base = pl.program_id(0) * TM_E

    def issue(r, c):
        p = pos_ref[base + r]
        pltpu.make_async_copy(xg_ref.at[pl.ds(r, 1), :], xs_hbm.at[pl.ds(p, 1), :], sem).start()
        return c

    lax.fori_loop(0, TM_E, issue, 0, unroll=DMA_UNROLL)
    pltpu.make_async_copy(xg_ref, xs_hbm.at[pl.ds(0, TM_E), :], sem).wait()


def _dispatch(pos, xg, zeros_sorted):
    return pl.pallas_call(
        _dispatch_kernel,
        grid_spec=pltpu.PrefetchScalarGridSpec(
            num_scalar_prefetch=1,
            grid=(T_ALL // TM_E,),
            in_specs=[pl.BlockSpec((TM_E, XG_W), lambda i, pos: (i, 0)),
                      pl.BlockSpec(memory_space=pl.ANY)],
            out_specs=pl.BlockSpec(memory_space=pl.ANY),
            scratch_shapes=[pltpu.SemaphoreType.DMA(())]),
        out_shape=jax.ShapeDtypeStruct((T_SORT, XG_W), f32),
        input_output_aliases={2: 0},
        compiler_params=pltpu.CompilerParams(dimension_semantics=("arbitrary",),
                                             vmem_limit_bytes=VMEM_LIMIT),
        name="moe_dispatch",
    )(pos, xg, zeros_sorted)


def _gather_rows(pos_ref, base, ys_hbm, o_ref, sem):
    def issue(r, c):
        p = pos_ref[base + r]
        pltpu.make_async_copy(ys_hbm.at[pl.ds(p, 1), :], o_ref.at[pl.ds(r, 1), :], sem).start()
        return c

    lax.fori_loop(0, TM_E, issue, 0, unroll=DMA_UNROLL)
    pltpu.make_async_copy(ys_hbm.at[pl.ds(0, TM_E), :], o_ref, sem).wait()


def _combine_kernel(pos_ref, ys_hbm, o_ref, sem):
    _gather_rows(pos_ref, pl.program_id(0) * TM_E, ys_hbm, o_ref, sem)


def _combine_split_kernel(pos_ref, ys_hbm, op_ref, os_ref, sem):
    i = pl.program_id(0)

    @pl.when(i < T_PROMPT // TM_E)
    def _():
        _gather_rows(pos_ref, i * TM_E, ys_hbm, op_ref, sem)

    @pl.when(i >= T_PROMPT // TM_E)
    def _():
        _gather_rows(pos_ref, i * TM_E, ys_hbm, os_ref, sem)


def _combine(pos, ys, split):
    npt = T_PROMPT // TM_E
    if split:
        out_specs = [pl.BlockSpec((TM_E, D_MODEL), lambda i, pos: (jnp.minimum(i, npt - 1), 0)),
                     pl.BlockSpec((TM_E, D_MODEL), lambda i, pos: (jnp.maximum(i - npt, 0), 0))]
        out_shape = [jax.ShapeDtypeStruct((T_PROMPT, D_MODEL), f32),
                     jax.ShapeDtypeStruct((T_SAMPLE, D_MODEL), f32)]
    else:
        out_specs = pl.BlockSpec((TM_E, D_MODEL), lambda i, pos: (i, 0))
        out_shape = jax.ShapeDtypeStruct((T_ALL, D_MODEL), f32)
    return pl.pallas_call(
        _combine_split_kernel if split else _combine_kernel,
        grid_spec=pltpu.PrefetchScalarGridSpec(
            num_scalar_prefetch=1,
            grid=(T_ALL // TM_E,),
            in_specs=[pl.BlockSpec(memory_space=pl.ANY)],
            out_specs=out_specs,
            scratch_shapes=[pltpu.SemaphoreType.DMA(())]),
        out_shape=out_shape,
        compiler_params=pltpu.CompilerParams(dimension_semantics=("arbitrary",),
                                             vmem_limit_bytes=VMEM_LIMIT),
        name="moe_combine",
    )(pos, ys)


def _experts_kernel(tg_ref, tv_ref, xs_ref, nw_ref, wg_ref, wu_ref, wd_ref, nf_ref, ys_ref, hbuf, ybuf, *,
                    final_norm):
    t = pl.program_id(0)
    e = pl.program_id(1)
    valid = tv_ref[t] == 1

    @pl.when(valid & (e == 0))
    def _():
        x2 = xs_ref[:, pl.ds(0, D_MODEL)]
        hbuf[...] = _rms(x2, nw_ref[...]).astype(bf16)
        ybuf[...] = x2

    @pl.when(valid)
    def _():
        hb = hbuf[...]
        he = (_silu(jnp.dot(hb, wg_ref[0, 0], preferred_element_type=f32))
              * jnp.dot(hb, wu_ref[0, 0], preferred_element_type=f32))
        yd = jnp.dot(he.astype(bf16), wd_ref[0, 0], preferred_element_type=f32)
        g = xs_ref[:, pl.ds(D_MODEL, 128)]
        lane = lax.broadcasted_iota(i32, g.shape, 1)
        gcol = jnp.sum(jnp.where(lane == LANE_E0 + EPG * tg_ref[t] + e, g, 0.0), axis=-1, keepdims=True)
        ybuf[...] += gcol * yd

    @pl.when(e == EPG - 1)
    def _():
        y = ybuf[...]
        if final_norm:
            y = _rms(y, nf_ref[...])
        ys_ref[...] = y


def _experts(tile_group, tile_valid, xs, nw, wg, wu, wd, nf, l, final_norm):
    wsel = lambda t, e, tg, tv: (l, tg[t] * EPG + e, 0, 0)
    return pl.pallas_call(
        functools.partial(_experts_kernel, final_norm=final_norm),
        grid_spec=pltpu.PrefetchScalarGridSpec(
            num_scalar_prefetch=2,
            grid=(NT_E, EPG),
            in_specs=[pl.BlockSpec((TM_E, XG_W), lambda t, e, tg, tv: (t, 0)),
                      pl.BlockSpec((1, D_MODEL), lambda t, e, tg, tv: (0, 0)),
                      pl.BlockSpec((1, 1, D_MODEL, D_FF_E), wsel),
                      pl.BlockSpec((1, 1, D_MODEL, D_FF_E), wsel),
                      pl.BlockSpec((1, 1, D_FF_E, D_MODEL), wsel),
                      pl.BlockSpec((1, D_MODEL), lambda t, e, tg, tv: (0, 0))],
            out_specs=pl.BlockSpec((TM_E, D_MODEL), lambda t, e, tg, tv: (t, 0)),
            scratch_shapes=[pltpu.VMEM((TM_E, D_MODEL), bf16), pltpu.VMEM((TM_E, D_MODEL), f32)]),
        out_shape=jax.ShapeDtypeStruct((T_SORT, D_MODEL), f32),
        compiler_params=pltpu.CompilerParams(dimension_semantics=("arbitrary", "arbitrary"),
                                             vmem_limit_bytes=VMEM_LIMIT),
        name="moe_experts",
    )(tile_group, tile_valid, xs, nw, wg, wu, wd, nf)


def _route_meta(xg, cnt_rows):
    g = xg[:, D_MODEL + LANE_G].astype(i32)
    rank = xg[:, D_MODEL + LANE_RANK].astype(i32)
    cnt = cnt_rows[0, :N_GROUPS].astype(i32)
    padded = ((cnt + TM_E - 1) // TM_E) * TM_E
    off_end = jnp.cumsum(padded)
    off = off_end - padded
    gid = jnp.arange(N_GROUPS, dtype=i32)
    pos = rank + jnp.sum(jnp.where(g[:, None] == gid[None, :], off[None, :], 0), axis=1)
    tile_start = jnp.arange(NT_E, dtype=i32) * TM_E
    tile_group = jnp.minimum(jnp.sum((tile_start[:, None] >= off_end[None, :]).astype(i32), axis=1), N_GROUPS - 1)
    tile_valid = (tile_start < off_end[-1]).astype(i32)
    return pos, tile_group, tile_valid


def _pad_cols(a, n):
    return jnp.pad(a, ((0, 0), (0, n - a.shape[1])))


def _pad_rows(a, n):
    return jnp.pad(a, ((0, n - a.shape[0]), (0, 0)))


def _prep_w_in(w):
    pad_last = lambda a, n: jnp.pad(a, ((0, 0), (0, 0), (0, n - a.shape[-1])))
    c = w[..., 2568:]
    parts = [w[..., 0:2048], pad_last(w[..., 2048:2056], 128), w[..., 2056:2568],
             c[..., 0:768], pad_last(c[..., 768:832], 128), pad_last(c[..., 832:896], 128), c[..., 896:1024]]
    return jnp.concatenate(parts, axis=-1).astype(bf16)


def _prep_mu(mu):
    m = mu[None, :]
    return jnp.concatenate([m[:, 0:768], _pad_cols(m[:, 768:832], 128), _pad_cols(m[:, 832:896], 128),
                            m[:, 896:1024]], axis=1)


def _sgu_mats(sgu_w, sgu_b):
    t = jnp.arange(SGU_CHUNK)
    wm = jnp.where(t[:, None] >= t[None, :], sgu_w, 0.0)
    bias_p = jnp.repeat(jnp.transpose(sgu_b), DH_B, axis=1)
    small = jnp.zeros((4, SEQ_PAD, SEQ_PAD), f32).at[:, SEQ_LEAD:, SEQ_LEAD:].set(wm[:, :DEC_SEQ, :DEC_SEQ])
    eye16 = jnp.eye(TB_S // SEQ_PAD, dtype=f32)
    wm_s = jnp.einsum('ab,hij->haibj', eye16, small).reshape(4, TB_S, TB_S)
    bias_small = jnp.zeros((SEQ_PAD, D_B), f32).at[SEQ_LEAD:].set(bias_p[:DEC_SEQ])
    bias_s = jnp.tile(bias_small, (TB_S // SEQ_PAD, 1))
    return wm.astype(bf16), bias_p, wm_s.astype(bf16), bias_s


def _row(a, n):
    return _pad_cols(a.reshape(1, -1), n)


def kernel(x_prompt, x_sample, state_gdn, state_gdn_conv, state_rwkv, state_rwkv_shift, norm_mix, norm_ffn, norm_final, w_in, gdn_conv_w, gdn_a_log, gdn_dt_bias, gdn_norm_w, sgu_ln_w, sgu_ln_b, sgu_w, sgu_b, rwkv_mu, rwkv_w0, rwkv_w_up, rwkv_a0, rwkv_a_up, rwkv_g_up, rwkv_k_k, rwkv_k_a, rwkv_r_k, rwkv_ln_w, rwkv_ln_b, w_out, router_group_w, router_group_b, router_expert_w, router_expert_b, expert_w_gate, expert_w_up, expert_w_down):
    xs = jnp.pad(x_sample, ((0, 0), (SEQ_LEAD, 0), (0, 0))).reshape(T_SAMPLE, D_MODEL)
    x = jnp.concatenate([x_prompt.reshape(T_PROMPT, D_MODEL), xs], axis=0)

    w_pad = _prep_w_in(w_in)
    w_out_b = w_out.astype(bf16)
    wg_b, wu_b, wd_b = expert_w_gate.astype(bf16), expert_w_up.astype(bf16), expert_w_down.astype(bf16)

    outs = {k: [] for k in ('gdn_p', 'conv_p', 'rwkv_p', 'shift_p', 'gdn_s', 'conv_s', 'rwkv_s', 'shift_s', 'cv_s')}
    for l in range(DEPTH):
        last = l == DEPTH - 1
        v128 = jnp.concatenate([
            jnp.pad(gdn_a_log[l].reshape(1, H_A), ((0, 0), (4, 120))),
            jnp.pad(gdn_dt_bias[l].reshape(1, H_A), ((0, 0), (4, 120))),
            gdn_norm_w[l].reshape(1, HD_A), jnp.zeros((5, 128), f32)], axis=0)
        v256 = jnp.concatenate([a.reshape(1, D_C) for a in (
            sgu_ln_w[l], sgu_ln_b[l], rwkv_w0[l], rwkv_a0[l], rwkv_k_k[l], rwkv_k_a[l], rwkv_r_k[l],
            rwkv_ln_w[l], rwkv_ln_b[l])] + [jnp.zeros((7, D_C), f32)], axis=0)
        wm_p, bias_p, wm_s, bias_s = _sgu_mats(sgu_w[l], sgu_b[l])
        common = (gdn_conv_w[l], v128, v256)
        tail = (_prep_mu(rwkv_mu[l]), _pad_rows(rwkv_w_up[l], 128).astype(bf16),
                _pad_rows(rwkv_a_up[l], 128).astype(bf16), rwkv_g_up[l].astype(bf16))
        mw_p = common + (wm_p, bias_p) + tail
        mw_s = common + (wm_s, bias_s) + tail

        proj, h = _inproj(x, norm_mix[l].reshape(1, D_MODEL), w_pad, l)
        pcf = _mm(state_rwkv_shift[l], w_pad[l, :, OFF_C:]).reshape(DEC_BATCH, 1, NC_PAD)
        mix_p, gdn_p, conv_p, rwkv_p = _mixer_prompt(proj, mw_p)
        mix_s, gdn_s, conv_s, rwkv_s, cv_s = _mixer_sample(
            proj, state_gdn[l], state_gdn_conv[l], state_rwkv[l], pcf, mw_s)

        rw = _pad_cols(jnp.concatenate([router_group_w[l], router_expert_w[l]], axis=1), 128)
        rwh = rw.astype(bf16)
        rwl = (rw - rwh.astype(f32)).astype(bf16)
        rb = _row(jnp.concatenate([router_group_b[l], router_expert_b[l]]), 128)
        nw_ffn = norm_ffn[l].reshape(1, D_MODEL)
        xg, cnt_rows = _outproj(x, mix_p, mix_s, w_out_b, nw_ffn, rwh, rwl, rb, l)
        pos, tile_group, tile_valid = _route_meta(xg, cnt_rows)
        xsorted = _dispatch(pos, xg, jnp.zeros((T_SORT, XG_W), f32))
        ys = _experts(tile_group, tile_valid, xsorted, nw_ffn, wg_b, wu_b, wd_b,
                      norm_final.reshape(1, D_MODEL), l, final_norm=last)
        x = _combine(pos, ys, split=last)

        outs['gdn_p'].append(gdn_p)
        outs['conv_p'].append(conv_p)
        outs['rwkv_p'].append(rwkv_p)
        outs['shift_p'].append(h[:T_PROMPT // SEQ_PAD].reshape(BATCH, SEQ // SEQ_PAD, D_MODEL)[:, -1])
        outs['gdn_s'].append(gdn_s)
        outs['conv_s'].append(conv_s)
        outs['rwkv_s'].append(rwkv_s)
        outs['shift_s'].append(h[T_PROMPT // SEQ_PAD:])
        outs['cv_s'].append(cv_s.reshape(DEC_BATCH, SEQ_PAD, D_B)[:, SEQ_LEAD:])

    y_prompt = x[0].reshape(BATCH, SEQ, D_MODEL)
    y_sample = x[1].reshape(DEC_BATCH, SEQ_PAD, D_MODEL)[:, SEQ_LEAD:]
    st = lambda k: jnp.stack(outs[k])
    return (y_prompt, y_sample, st('gdn_p'), st('conv_p'), st('rwkv_p'), st('shift_p'),
            st('gdn_s'), st('conv_s'), st('rwkv_s'), st('shift_s'), st('cv_s'))
```

```python
import functools
import math

import jax
import jax.numpy as jnp
from jax import lax
from jax.experimental import pallas as pl
from jax.experimental.pallas import tpu as pltpu

f32 = jnp.float32
bf16 = jnp.bfloat16
i32 = jnp.int32

D_MODEL = 1024
BATCH = 8
SEQ = 2048
DEPTH = 2
DEC_BATCH = 128
DEC_SEQ = 4
H_A = 4
HD_A = 128
D_A = 512
CONV_W = 4
D_B = 256
DH_B = 64
SGU_CHUNK = 128
H_C = 4
N_C = 64
D_C = 256
N_GROUPS = 4
EPG = 4
D_FF_E = 512
NORM_EPS = 1e-6
LN_EPS = 1e-5
GN_EPS = 64e-5

SEQ_PAD = 8
SEQ_LEAD = SEQ_PAD - DEC_SEQ
T_PROMPT = BATCH * SEQ
T_SAMPLE = DEC_BATCH * SEQ_PAD
T_ALL = T_PROMPT + T_SAMPLE
TB_P = 256
TB_S = 128
GDN_CHUNK = 64

OFF_Q, OFF_K, OFF_V, OFF_Z, OFF_BA, OFF_U, OFF_VB, OFF_C = 0, 512, 1024, 1536, 2048, 2176, 2432, 2688
NP_IN = 3840
NC_PAD = NP_IN - OFF_C

XG_W = D_MODEL + 128
LANE_G, LANE_RANK, LANE_E0 = 0, 1, 4

VMEM_LIMIT = 48 * 1024 * 1024


NN = (((1,), (0,)), ((), ()))
NT = (((1,), (1,)), ((), ()))
TN = (((0,), (0,)), ((), ()))
BNN = (((2,), (1,)), ((0,), (0,)))


def _bdot(a, b, dims=NN):
    return lax.dot_general(a.astype(bf16), b.astype(bf16), dims, preferred_element_type=f32)


def _split2(x):
    hi = x.astype(bf16)
    lo = (x - hi.astype(f32)).astype(bf16)
    return hi, lo


def _split3(x):
    hi = x.astype(bf16)
    r = x - hi.astype(f32)
    mid = r.astype(bf16)
    lo = (r - mid.astype(f32)).astype(bf16)
    return hi, mid, lo


def _dot01_left(m01, x):
    hi, mid, lo = _split3(x)
    d = lambda p: lax.dot_general(m01, p, NN, preferred_element_type=f32)
    return d(hi) + d(mid) + d(lo)


def _dot01_right(x, m01):
    hi, lo = _split2(x)
    d = lambda p: lax.dot_general(p, m01, NN, preferred_element_type=f32)
    return d(hi) + d(lo)


def _softplus(x):
    return jnp.maximum(x, 0.0) + jnp.log1p(jnp.exp(-jnp.abs(x)))


def _sigmoid(x):
    return 1.0 / (1.0 + jnp.exp(-x))


def _silu(x):
    return x * _sigmoid(x)


def _gelu(x):
    return 0.5 * x * (1.0 + lax.erf(x * (1.0 / math.sqrt(2.0))))


def _rms(x, w):
    return x * lax.rsqrt(jnp.mean(x * x, axis=-1, keepdims=True) + NORM_EPS) * w


TM_A = 512
N_SLAB = 768


def _inproj_kernel(*refs, split):
    if split:
        xp_ref, xs_ref, nw_ref, w_ref, proj_ref, h8_ref, hscr = refs
        x = jnp.where(pl.program_id(0) < T_PROMPT // TM_A, xp_ref[...], xs_ref[...])
    else:
        x_ref, nw_ref, w_ref, proj_ref, h8_ref, hscr = refs
        x = x_ref[...]
    h = _rms(x, nw_ref[...])
    for k in range(D_MODEL // 128):
        hscr[k] = h[:, k * 128:(k + 1) * 128]
        h8_ref[:, pl.ds(k * 128, 128)] = hscr[k, pl.ds(SEQ_PAD - 1, TM_A // SEQ_PAD, stride=SEQ_PAD), :]
    hb = h.astype(bf16)
    for n in range(NP_IN // N_SLAB):
        sl = pl.ds(n * N_SLAB, N_SLAB)
        proj_ref[:, sl] = jnp.dot(hb, w_ref[0, :, sl], preferred_element_type=f32)


def _inproj(xs, nw, w_pad, l):
    split = len(xs) == 2
    npt = T_PROMPT // TM_A
    if split:
        x_specs = [pl.BlockSpec((TM_A, D_MODEL), lambda i: (jnp.minimum(i, npt - 1), 0)),
                   pl.BlockSpec((TM_A, D_MODEL), lambda i: (jnp.maximum(i - npt, 0), 0))]
    else:
        x_specs = [pl.BlockSpec((TM_A, D_MODEL), lambda i: (i, 0))]
    return pl.pallas_call(
        functools.partial(_inproj_kernel, split=split),
        grid=(T_ALL // TM_A,),
        in_specs=x_specs + [pl.BlockSpec((1, D_MODEL), lambda i: (0, 0)),
                            pl.BlockSpec((1, D_MODEL, NP_IN), lambda i: (l, 0, 0))],
        out_specs=[pl.BlockSpec((TM_A, NP_IN), lambda i: (i, 0)),
                   pl.BlockSpec((TM_A // SEQ_PAD, D_MODEL), lambda i: (i, 0))],
        out_shape=[jax.ShapeDtypeStruct((T_ALL, NP_IN), f32),
                   jax.ShapeDtypeStruct((T_ALL // SEQ_PAD, D_MODEL), f32)],
        scratch_shapes=[pltpu.VMEM((D_MODEL // 128, TM_A, 128), f32)],
        compiler_params=pltpu.CompilerParams(dimension_semantics=("arbitrary",),
                                             vmem_limit_bytes=VMEM_LIMIT),
        name="inproj",
    )(*xs, nw, w_pad)


def _mm_kernel(a_ref, b_ref, o_ref):
    o_ref[...] = jnp.dot(a_ref[...].astype(bf16), b_ref[...], preferred_element_type=f32)


def _mm(a, b):
    return pl.pallas_call(
        _mm_kernel,
        out_shape=jax.ShapeDtypeStruct((a.shape[0], b.shape[1]), f32),
        compiler_params=pltpu.CompilerParams(vmem_limit_bytes=VMEM_LIMIT),
        name="shift_proj",
    )(a, b)


def _neumann(L, iters):
    C = L.shape[-1]
    bmm = lambda a, b: lax.dot_general(a.astype(bf16), b.astype(bf16), BNN, preferred_element_type=f32)
    N = -L
    Q = bmm(N, N)
    for _ in range(iters - 1):
        R = bmm(jnp.concatenate([N, Q], axis=1), Q)
        N = N + Q + R[:, :C]
        Q = R[:, C:]
    return N + Q + bmm(N, Q)


def _cat(parts, axis):
    return parts[0] if len(parts) == 1 else jnp.concatenate(parts, axis=axis)


def _mixer_kernel(*refs, TB, C, G, per_seq, n_alias=0):
    it = iter(refs)
    proj_ref = next(it)
    if per_seq:
        gdn_in, conv_in, rwkv_in, pcf_in = next(it), next(it), next(it), next(it)
    convw_ref, v128_ref, v256_ref, sguw_ref, sgub_ref, mu_ref, wup_ref, aup_ref, gup_ref = (
        next(it) for _ in range(9))
    for _ in range(n_alias):
        next(it)
    mix_ref, gdn_out, conv_out, rwkv_out = next(it), next(it), next(it), next(it)
    cv_out = next(it) if per_seq else None
    xp, pp = next(it), next(it)

    nchunk = TB // C
    ngrp = C // G
    iters = int(math.log2(G)) - 1

    if per_seq:
        xp[pl.ds(0, 8), :] = jnp.zeros((8, 3 * D_A), f32)
        pp[pl.ds(0, 8), :] = jnp.zeros((8, NC_PAD), f32)
    else:
        @pl.when(pl.program_id(1) == 0)
        def _():
            xp[pl.ds(0, 8), :] = jnp.zeros((8, 3 * D_A), f32)
            pp[pl.ds(0, 8), :] = jnp.zeros((8, NC_PAD), f32)
            gdn_out[...] = jnp.zeros(gdn_out.shape, f32)
            rwkv_out[...] = jnp.zeros(rwkv_out.shape, f32)
    xp[pl.ds(8, TB), :] = proj_ref[:, pl.ds(OFF_Q, 3 * D_A)]
    pp[pl.ds(8, TB), :] = proj_ref[:, pl.ds(OFF_C, NC_PAD)]
    if per_seq:
        for s in range(TB // SEQ_PAD):
            r0 = 8 + s * SEQ_PAD
            xp[pl.ds(r0 + SEQ_LEAD - (CONV_W - 1), CONV_W - 1), :] = conv_in[s]
            pp[pl.ds(r0 + SEQ_LEAD - 1, 1), :] = pcf_in[s]

    rowi = lax.broadcasted_iota(i32, (TB, 1), 0)
    live = (rowi % SEQ_PAD) >= SEQ_LEAD if per_seq else None

    ii = lax.broadcasted_iota(i32, (C, C), 0)
    jj = lax.broadcasted_iota(i32, (C, C), 1)
    same = (ii // G) == (jj // G)
    causal = (ii >= jj) & same
    strict = (ii > jj) & same
    eye = ii == jj
    m_cum = causal.astype(bf16)
    m_grp = same.astype(bf16)

    def conv_cols(r0, c0):
        cs = pl.ds(c0, 128)
        acc = xp[pl.ds(r0 + 8, C), cs] * convw_ref[pl.ds(3, 1), cs]
        for j in range(CONV_W - 1):
            acc = acc + xp[pl.ds(r0 + 5 + j, C), cs] * convw_ref[pl.ds(j, 1), cs]
        return _silu(acc)

    alog_row = v128_ref[pl.ds(0, 1), :]
    dtb_row = v128_ref[pl.ds(1, 1), :]
    gnorm_w = v128_ref[pl.ds(2, 1), :]
    ln_w, ln_b = v256_ref[pl.ds(0, 1), :], v256_ref[pl.ds(1, 1), :]
    w0, a0 = v256_ref[pl.ds(2, 1), :], v256_ref[pl.ds(3, 1), :]
    k_k, k_a, r_k = v256_ref[pl.ds(4, 1), :], v256_ref[pl.ds(5, 1), :], v256_ref[pl.ds(6, 1), :]
    rln_w, rln_b = v256_ref[pl.ds(7, 1), :], v256_ref[pl.ds(8, 1), :]

    l64i = lax.broadcasted_iota(i32, (D_C, D_C), 0) // N_C
    l64j = lax.broadcasted_iota(i32, (D_C, D_C), 1) // N_C
    seg64 = (l64i == l64j).astype(bf16)

    gdn, rwk, lmats = [None] * nchunk, [None] * nchunk, [None] * (8 * nchunk)

    def phase1(c):
        R0 = c * C
        rows = pl.ds(R0, C)
        live_c = live[R0:R0 + C] if per_seq else None

        ba = proj_ref[rows, pl.ds(OFF_BA, 128)]
        beta_all = _sigmoid(ba)
        g_all = -jnp.exp(alog_row) * _softplus(ba + dtb_row)
        if per_seq:
            beta_all = jnp.where(live_c, beta_all, 0.0)
            g_all = jnp.where(live_c, g_all, 0.0)
        pcur = pp[pl.ds(R0 + 8, C), :]
        pprev = pp[pl.ds(R0 + 7, C), :]
        pm = pcur + mu_ref[...] * (pprev - pcur)
        r_ = pm[:, 0:D_C]
        kc = pm[:, D_C:2 * D_C]
        vc = pm[:, 2 * D_C:3 * D_C]
        wd = pm[:, 3 * D_C:3 * D_C + 128]
        ad = pm[:, 3 * D_C + 128:3 * D_C + 256]
        gd = pm[:, 3 * D_C + 256:3 * D_C + 384]
        wl_mm = _bdot(jnp.tanh(wd), wup_ref[...])
        a_mm = _bdot(ad, aup_ref[...])
        gate = _bdot(_sigmoid(gd), gup_ref[...])
        kk = kc * k_k
        kk_ss = _dot01_right(kk * kk, seg64)
        yield

        gc_all = _dot01_left(m_cum, g_all)
        gl_all = _dot01_left(m_grp, g_all)
        w_log = -_softplus(-(w0 + wl_mm)) - 0.5
        logw = -jnp.exp(w_log)
        a_ = _sigmoid(a0 + a_mm)
        kk = kk * lax.rsqrt(kk_ss + 1e-6)
        kc2 = kc * (1.0 + (a_ - 1.0) * k_a)
        if per_seq:
            logw = jnp.where(live_c, logw, 0.0)
            kk = jnp.where(live_c, kk, 0.0)
            kc2 = jnp.where(live_c, kc2, 0.0)
        b_ = kk * a_
        Gc = _dot01_left(m_cum, logw)
        Gl = _dot01_left(m_grp, logw)
        bonus_ss = _dot01_right(r_ * kc2 * r_k, seg64)
        yield

        heads = []
        for h in range(H_A):
            q = conv_cols(R0, OFF_Q + h * HD_A)
            k = conv_cols(R0, OFF_K + h * HD_A)
            v = conv_cols(R0, OFF_V + h * HD_A)
            q = q * lax.rsqrt(jnp.sum(q * q, axis=-1, keepdims=True) + 1e-6) * (HD_A ** -0.5)
            k = k * lax.rsqrt(jnp.sum(k * k, axis=-1, keepdims=True) + 1e-6)
            if per_seq:
                k = jnp.where(live_c, k, 0.0)
            beta = beta_all[:, h:h + 1]
            gcol = gc_all[:, 4 + h:5 + h]
            glr = gl_all[:, 4 + h:5 + h]
            grow = jnp.sum(jnp.where(eye, jnp.broadcast_to(gcol, (C, C)), 0.0), axis=0, keepdims=True)
            decay = jnp.where(causal, jnp.exp(jnp.where(causal, gcol - grow, 0.0)), 0.0)
            eg = jnp.exp(gcol)
            kb = k * beta
            kq = _bdot(jnp.concatenate([kb, q], axis=0), k, NT)
            heads.append(dict(kq=kq, decay=decay,
                              rhs=jnp.concatenate([v * beta, kb * eg], axis=1),
                              q_dec=q * eg, k_dec=k * jnp.exp(glr - gcol), glr=glr))
        gdn[c] = heads
        e_neg = jnp.exp(-Gc)
        e_rem = jnp.exp(Gl - Gc)
        rG = r_ * jnp.exp(Gc)
        kkG = kk * jnp.exp(Gc - logw)
        kN = kc2 * e_neg
        bN = b_ * e_neg
        aalls = []
        for h in range(H_C):
            hs = slice(h * N_C, (h + 1) * N_C)
            aalls.append(_bdot(jnp.concatenate([kkG[:, hs], rG[:, hs]], axis=0),
                               jnp.concatenate([bN[:, hs], kN[:, hs]], axis=0), NT))
        yield

        for h in range(H_A):
            d = gdn[c][h]
            kq = d.pop('kq')
            decay = d.pop('decay')
            lmats[8 * c + h] = jnp.where(strict, kq[:C] * decay, 0.0)
            d['attn'] = kq[C:] * decay
        heads = []
        for h in range(H_C):
            hs = slice(h * N_C, (h + 1) * N_C)
            aall = aalls[h]
            lmats[8 * c + H_A + h] = jnp.where(strict, aall[:C, :C], 0.0)
            akk_k = jnp.where(strict, aall[:C, C:], 0.0)
            ar = jnp.concatenate([jnp.where(causal, aall[C:, C:], 0.0),
                                  -jnp.where(causal, aall[C:, :C], 0.0)], axis=1)
            heads.append(dict(x1=_bdot(akk_k, vc[:, hs]), ar=ar))
        rwk[c] = dict(heads=heads, vc=vc, rG=rG, kkG=kkG, kdec=kc2 * e_rem, bdec=b_ * e_rem,
                      e_last=jnp.exp(Gl), gate=gate, bonus=bonus_ss * vc)
        yield

    def lockstep(gens):
        gens = list(gens)
        while gens:
            alive = []
            for g in gens:
                try:
                    next(g)
                    alive.append(g)
                except StopIteration:
                    pass
            gens = alive

    lockstep(phase1(c) for c in range(nchunk))

    nmats = _neumann(jnp.stack(lmats), iters)

    for c in range(nchunk):
        for h in range(H_A):
            d = gdn[c][h]
            sol = d['rhs'] + _bdot(nmats[8 * c + h], d['rhs'])
            d['u'], d['w'] = sol[:, :HD_A], sol[:, HD_A:]
        for h in range(H_C):
            hs = slice(h * N_C, (h + 1) * N_C)
            d = rwk[c]['heads'][h]
            both = jnp.concatenate([d['x1'], rwk[c]['kkG'][:, hs]], axis=1)
            both = both + _bdot(nmats[8 * c + H_A + h], both)
            d['u_p'], d['w_p'] = both[:, :N_C], both[:, N_C:]
            d['both'] = both

    if not per_seq:
        for c in range(nchunk):
            for h in range(H_A):
                d = gdn[c][h]
                wu = jnp.concatenate([d['w'], d['u']], axis=1)
                a_wu = _bdot(d['attn'], wu)
                k_wu = _bdot(d['k_dec'], wu, TN)
                d['qe'] = d['q_dec'] - a_wu[:, :HD_A]
                d['o0'] = a_wu[:, HD_A:]
                d['m'] = -k_wu[:, :HD_A]
                d['b'] = k_wu[:, HD_A:]
            for h in range(H_C):
                hs = slice(h * N_C, (h + 1) * N_C)
                rc = rwk[c]
                d = rc['heads'][h]
                V = rc['vc'][:, hs]
                x = _bdot(d['ar'][:, C:], d['both'])
                d['y0'] = _bdot(d['ar'][:, :C], V) + x[:, :N_C]
                d['re'] = rc['rG'][:, hs] + x[:, N_C:]
                d['m'] = _bdot(d['w_p'], rc['bdec'][:, hs], TN)
                d['b'] = _bdot(jnp.concatenate([V, -d['u_p']], axis=0),
                               jnp.concatenate([rc['kdec'][:, hs], rc['bdec'][:, hs]], axis=0), TN)

    def gdn_head_seq(c, h):
        rows = pl.ds(c * C, C)
        d = gdn[c][h]
        S = gdn_out[0, h]
        Sb = S.astype(bf16)
        gdn_out[0, h] = S * jnp.exp(d['glr'][C - 1:C, :]) + _bdot(d['m'], Sb) + d['b']
        o = d['o0'] + _bdot(d['qe'], Sb)
        yield
        o = o * lax.rsqrt(jnp.mean(o * o, axis=-1, keepdims=True) + NORM_EPS) * gnorm_w
        z = proj_ref[rows, pl.ds(OFF_Z + h * HD_A, HD_A)]
        mix_ref[rows, pl.ds(h * HD_A, HD_A)] = o * _silu(z)

    def rwkv_head_seq(c, h, ys):
        rc = rwk[c]
        hs = slice(h * N_C, (h + 1) * N_C)
        d = rc['heads'][h]
        S = rwkv_out[0, h]
        Sb = S.astype(bf16)
        rwkv_out[0, h] = S * rc['e_last'][C - 1:C, hs] + d['b'] - _bdot(Sb, d['m'])
        ys[h] = d['y0'] + _bdot(d['re'], Sb, NT)
        yield

    def gdn_head(c, h):
        rows = pl.ds(c * C, C)
        d = gdn[c][h]
        rq, vn, s_old = [], [], []
        for s in range(ngrp):
            gs = slice(s * G, (s + 1) * G)
            S = gdn_in[s, h] if per_seq else gdn_out[0, h]
            s_old.append(S)
            R = _bdot(jnp.concatenate([d['w'][gs], d['q_dec'][gs]], axis=0), S)
            vn.append(d['u'][gs] - R[:G])
            rq.append(R[G:])
        yield
        v_new = _cat(vn, 0)
        o = _cat(rq, 0) + _bdot(d['attn'], v_new)
        for s in range(ngrp):
            gs = slice(s * G, (s + 1) * G)
            g_last = jnp.exp(d['glr'][s * G + G - 1:s * G + G, :])
            S_new = s_old[s] * g_last + _bdot(d['k_dec'][gs], v_new[gs], TN)
            if per_seq:
                gdn_out[s, h] = S_new
            else:
                gdn_out[0, h] = S_new
        yield
        o = o * lax.rsqrt(jnp.mean(o * o, axis=-1, keepdims=True) + NORM_EPS) * gnorm_w
        z = proj_ref[rows, pl.ds(OFF_Z + h * HD_A, HD_A)]
        mix_ref[rows, pl.ds(h * HD_A, HD_A)] = o * _silu(z)

    def rwkv_head(c, h, ys):
        rc = rwk[c]
        hs = slice(h * N_C, (h + 1) * N_C)
        d = rc['heads'][h]
        V = rc['vc'][:, hs]
        rr, ut, s_old = [], [], []
        for s in range(ngrp):
            gs = slice(s * G, (s + 1) * G)
            S = rwkv_in[s, h] if per_seq else rwkv_out[0, h]
            s_old.append(S)
            R = _bdot(jnp.concatenate([d['w_p'][gs], rc['rG'][gs, hs]], axis=0), S, NT)
            ut.append(d['u_p'][gs] + R[:G])
            rr.append(R[G:])
        yield
        Ut = _cat(ut, 0)
        ys[h] = _cat(rr, 0) + _bdot(d['ar'], jnp.concatenate([V, Ut], axis=0))
        for s in range(ngrp):
            gs = slice(s * G, (s + 1) * G)
            upd = _bdot(jnp.concatenate([V[gs], -Ut[gs]], axis=0),
                        jnp.concatenate([rc['kdec'][gs, hs], rc['bdec'][gs, hs]], axis=0), TN)
            S_new = s_old[s] * rc['e_last'][s * G + G - 1:s * G + G, hs] + upd
            if per_seq:
                rwkv_out[s, h] = S_new
            else:
                rwkv_out[0, h] = S_new
        yield

    for c in range(nchunk):
        rows = pl.ds(c * C, C)
        rc = rwk[c]
        ys = [None] * H_C
        if per_seq:
            lockstep([gdn_head(c, h) for h in range(H_A)] + [rwkv_head(c, h, ys) for h in range(H_C)])
        else:
            lockstep([gdn_head_seq(c, h) for h in range(H_A)] + [rwkv_head_seq(c, h, ys) for h in range(H_C)])
        y = jnp.concatenate(ys, axis=1)
        mu_y = _dot01_right(y, seg64) * (1.0 / N_C)
        dy = y - mu_y
        var_y = _dot01_right(dy * dy, seg64) * (1.0 / N_C)
        y = dy * lax.rsqrt(var_y + GN_EPS) * rln_w + rln_b
        mix_ref[rows, pl.ds(D_A + D_B, D_C)] = (y + rc['bonus']) * rc['gate']

    for sc in range(TB // SGU_CHUNK):
        rows = pl.ds(sc * SGU_CHUNK, SGU_CHUNK)
        ug = _gelu(proj_ref[rows, pl.ds(OFF_U, D_B)])
        vs = _gelu(proj_ref[rows, pl.ds(OFF_VB, D_B)])
        mu_v = _dot01_right(vs, seg64) * (1.0 / DH_B)
        dv = vs - mu_v
        var_v = _dot01_right(dv * dv, seg64) * (1.0 / DH_B)
        vs = dv * lax.rsqrt(var_v + LN_EPS) * ln_w + ln_b
        if per_seq:
            cv_out[rows, :] = vs
        vsb = vs.astype(bf16)
        outs = [lax.dot_general(sguw_ref[h], vsb[:, h * DH_B:(h + 1) * DH_B], NN, preferred_element_type=f32)
                for h in range(4)]
        mixed = jnp.concatenate(outs, axis=1) + sgub_ref[...]
        mix_ref[rows, pl.ds(D_A, D_B)] = ug * mixed

    if per_seq:
        for s in range(TB // SEQ_PAD):
            conv_out[s] = xp[pl.ds(8 + (s + 1) * SEQ_PAD - (CONV_W - 1), CONV_W - 1), :]
    else:
        conv_out[0] = xp[pl.ds(8 + TB - (CONV_W - 1), CONV_W - 1), :]
        xp[pl.ds(0, 8), :] = xp[pl.ds(TB, 8), :]
        pp[pl.ds(0, 8), :] = pp[pl.ds(TB, 8), :]


def _mixer_weight_specs(nidx):
    z2 = (lambda b, j: (0, 0)) if nidx == 2 else (lambda i: (0, 0))
    z3 = (lambda b, j: (0, 0, 0)) if nidx == 2 else (lambda i: (0, 0, 0))
    return [pl.BlockSpec((CONV_W, 3 * D_A), z2),
            pl.BlockSpec((8, 128), z2),
            pl.BlockSpec((16, D_C), z2),
            pl.BlockSpec((4, SGU_CHUNK, SGU_CHUNK), z3),
            pl.BlockSpec((SGU_CHUNK, D_B), z2),
            pl.BlockSpec((1, NC_PAD), z2),
            pl.BlockSpec((128, D_C), z2),
            pl.BlockSpec((128, D_C), z2),
            pl.BlockSpec((128, D_C), z2)]


def _mixer_prompt(proj, mw):
    nj = SEQ // TB_P
    return pl.pallas_call(
        functools.partial(_mixer_kernel, TB=TB_P, C=GDN_CHUNK, G=GDN_CHUNK, per_seq=False),
        grid=(BATCH, nj),
        in_specs=[pl.BlockSpec((TB_P, NP_IN), lambda b, j: (b * nj + j, 0))] + _mixer_weight_specs(2),
        out_specs=[pl.BlockSpec((TB_P, D_MODEL), lambda b, j: (b * nj + j, 0)),
                   pl.BlockSpec((1, H_A, HD_A, HD_A), lambda b, j: (b, 0, 0, 0)),
                   pl.BlockSpec((1, CONV_W - 1, 3 * D_A), lambda b, j: (b, 0, 0)),
                   pl.BlockSpec((1, H_C, N_C, N_C), lambda b, j: (b, 0, 0, 0))],
        out_shape=[jax.ShapeDtypeStruct((T_PROMPT, D_MODEL), f32),
                   jax.ShapeDtypeStruct((BATCH, H_A, HD_A, HD_A), f32),
                   jax.ShapeDtypeStruct((BATCH, CONV_W - 1, 3 * D_A), f32),
                   jax.ShapeDtypeStruct((BATCH, H_C, N_C, N_C), f32)],
        scratch_shapes=[pltpu.VMEM((TB_P + 8, 3 * D_A), f32), pltpu.VMEM((TB_P + 8, NC_PAD), f32)],
        compiler_params=pltpu.CompilerParams(dimension_semantics=("arbitrary", "arbitrary"),
                                             vmem_limit_bytes=VMEM_LIMIT),
        name="mixer_prompt",
    )(proj, *mw)


def _mixer_sample(proj, s_gdn, s_conv, s_rwkv, pcf, mw, l, prev_states):
    nseq = TB_S // SEQ_PAD
    base = T_PROMPT // TB_S
    n_in = 5 + len(mw)
    lay4 = lambda i: (l, i, 0, 0, 0)
    lay3 = lambda i: (l, i, 0, 0)
    return pl.pallas_call(
        functools.partial(_mixer_kernel, TB=TB_S, C=TB_S, G=SEQ_PAD, per_seq=True, n_alias=len(prev_states)),
        grid=(T_SAMPLE // TB_S,),
        in_specs=[pl.BlockSpec((TB_S, NP_IN), lambda i: (base + i, 0)),
                  pl.BlockSpec((None, nseq, H_A, HD_A, HD_A), lay4),
                  pl.BlockSpec((None, nseq, CONV_W - 1, 3 * D_A), lay3),
                  pl.BlockSpec((None, nseq, H_C, N_C, N_C), lay4),
                  pl.BlockSpec((nseq, 1, NC_PAD), lambda i: (i, 0, 0))] + _mixer_weight_specs(1)
                 + [pl.BlockSpec(memory_space=pl.ANY)] * len(prev_states),
        out_specs=[pl.BlockSpec((TB_S, D_MODEL), lambda i: (i, 0)),
                   pl.BlockSpec((None, nseq, H_A, HD_A, HD_A), lay4),
                   pl.BlockSpec((nseq, CONV_W - 1, 3 * D_A), lambda i: (i, 0, 0)),
                   pl.BlockSpec((None, nseq, H_C, N_C, N_C), lay4),
                   pl.BlockSpec((TB_S, D_B), lambda i: (i, 0))],
        out_shape=[jax.ShapeDtypeStruct((T_SAMPLE, D_MODEL), f32),
                   jax.ShapeDtypeStruct((DEPTH, DEC_BATCH, H_A, HD_A, HD_A), f32),
                   jax.ShapeDtypeStruct((DEC_BATCH, CONV_W - 1, 3 * D_A), f32),
                   jax.ShapeDtypeStruct((DEPTH, DEC_BATCH, H_C, N_C, N_C), f32),
                   jax.ShapeDtypeStruct((T_SAMPLE, D_B), f32)],
        scratch_shapes=[pltpu.VMEM((TB_S + 8, 3 * D_A), f32), pltpu.VMEM((TB_S + 8, NC_PAD), f32)],
        input_output_aliases={n_in + k: o for k, o in enumerate((1, 3)[:len(prev_states)])},
        compiler_params=pltpu.CompilerParams(dimension_semantics=("arbitrary",),
                                             vmem_limit_bytes=VMEM_LIMIT),
        name="mixer_sample",
    )(proj, s_gdn, s_conv, s_rwkv, pcf, *mw, *prev_states)


TM_C = 512


def _outproj_kernel(*refs, split):
    i = pl.program_id(0)
    in_prompt = i < T_PROMPT // TM_C
    if split:
        xp_ref, xs_ref = refs[:2]
        x = jnp.where(in_prompt, xp_ref[...], xs_ref[...])
    else:
        x = refs[0][...]
    mixp_ref, mixs_ref, wout_ref, nw_ref, rwh_ref, rwl_ref, rb_ref, xg_ref, cnt_ref, run_ref = refs[-10:]

    @pl.when(i == 0)
    def _():
        run_ref[...] = jnp.zeros(run_ref.shape, f32)

    mix = jnp.where(in_prompt, mixp_ref[...], mixs_ref[...])
    x2 = x + jnp.dot(mix.astype(bf16), wout_ref[0], preferred_element_type=f32)
    xg_ref[:, pl.ds(0, D_MODEL)] = x2
    h2 = _rms(x2, nw_ref[...])
    hh, hl = _split2(h2)
    d = lambda a, b: jnp.dot(a, b, preferred_element_type=f32)
    logits = d(hh, rwh_ref[...]) + d(hl, rwh_ref[...]) + d(hh, rwl_ref[...]) + rb_ref[...]

    lane = lax.broadcasted_iota(i32, logits.shape, 1).astype(f32)
    neg = jnp.float32(-jnp.inf)
    is_g = lane < float(N_GROUPS)
    gl = jnp.where(is_g, logits, neg)
    gmax = jnp.max(gl, axis=-1, keepdims=True)
    gsel = jnp.min(jnp.where(gl == gmax, lane, 128.0), axis=-1, keepdims=True)
    gw = 1.0 / jnp.sum(jnp.where(is_g, jnp.exp(jnp.where(is_g, logits - gmax, 0.0)), 0.0),
                       axis=-1, keepdims=True)
    lo = LANE_E0 + float(EPG) * gsel
    in_grp = (lane >= lo) & (lane < lo + float(EPG))
    el = jnp.where(in_grp, logits, neg)
    t1 = jnp.max(el, axis=-1, keepdims=True)
    i1 = jnp.min(jnp.where(el == t1, lane, 128.0), axis=-1, keepdims=True)
    el2 = jnp.where(lane == i1, neg, el)
    t2 = jnp.max(el2, axis=-1, keepdims=True)
    i2 = jnp.min(jnp.where(el2 == t2, lane, 128.0), axis=-1, keepdims=True)
    e2 = jnp.exp(t2 - t1)
    den = 1.0 + e2
    gates = jnp.where(lane == i1, gw / den, 0.0) + jnp.where(lane == i2, gw * e2 / den, 0.0)

    onehot = jnp.where(lane == gsel, 1.0, 0.0)
    ri = lax.broadcasted_iota(i32, (TM_C, TM_C), 0)
    ci = lax.broadcasted_iota(i32, (TM_C, TM_C), 1)
    before = lax.dot_general((ri > ci).astype(bf16), onehot.astype(bf16), NN, preferred_element_type=f32)
    rank = jnp.sum((before + run_ref[...]) * onehot, axis=-1, keepdims=True)
    run_ref[...] += jnp.sum(onehot, axis=0, keepdims=True)
    cnt_ref[...] = jnp.broadcast_to(run_ref[...], cnt_ref.shape)
    xg_ref[:, pl.ds(D_MODEL, 128)] = (gates + jnp.where(lane == float(LANE_G), gsel, 0.0)
                                      + jnp.where(lane == float(LANE_RANK), rank, 0.0))


def _outproj(xs, mix_p, mix_s, wout, nw, rwh, rwl, rb, l):
    row = lambda i: (i, 0)
    fix = lambda i: (0, 0)
    npt = T_PROMPT // TM_C
    p_rows = pl.BlockSpec((TM_C, D_MODEL), lambda i: (jnp.minimum(i, npt - 1), 0))
    s_rows = pl.BlockSpec((TM_C, D_MODEL), lambda i: (jnp.maximum(i - npt, 0), 0))
    split = len(xs) == 2
    return pl.pallas_call(
        functools.partial(_outproj_kernel, split=split),
        grid=(T_ALL // TM_C,),
        in_specs=([p_rows, s_rows] if split else [pl.BlockSpec((TM_C, D_MODEL), row)]) + [
                  p_rows, s_rows,
                  pl.BlockSpec((1, D_MODEL, D_MODEL), lambda i: (l, 0, 0)), pl.BlockSpec((1, D_MODEL), fix),
                  pl.BlockSpec((D_MODEL, 128), fix), pl.BlockSpec((D_MODEL, 128), fix),
                  pl.BlockSpec((1, 128), fix)],
        out_specs=[pl.BlockSpec((TM_C, XG_W), row), pl.BlockSpec((8, 128), fix)],
        out_shape=[jax.ShapeDtypeStruct((T_ALL, XG_W), f32), jax.ShapeDtypeStruct((8, 128), f32)],
        scratch_shapes=[pltpu.VMEM((1, 128), f32)],
        compiler_params=pltpu.CompilerParams(dimension_semantics=("arbitrary",),
                                             vmem_limit_bytes=VMEM_LIMIT),
        name="outproj_router",
    )(*xs, mix_p, mix_s, wout, nw, rwh, rwl, rb)


TM_E = 512
NT_E = T_ALL // TM_E + N_GROUPS
T_SORT = NT_E * TM_E
DMA_UNROLL = 8


def _dispatch_kernel(pos_ref, last_ref, xg_ref, xs_hbm, zbuf, sem, zsem):
    base = pl.program_id(0) * TM_E

    @pl.when(pl.program_id(0) == 0)
    def _():
        zbuf[...] = jnp.zeros(zbuf.shape, f32)
        for g in range(N_GROUPS):
            dst = xs_hbm.at[pl.ds(pl.multiple_of(last_ref[g] * TM_E, TM_E), TM_E), :]
            cp = pltpu.make_async_copy(zbuf, dst, zsem)
            cp.start()
            cp.wait()

    def issue(r, c):
        p = pos_ref[base + r]
        pltpu.make_async_copy(xg_ref.at[pl.ds(r, 1), :], xs_hbm.at[pl.ds(p, 1), :], sem).start()
        return c

    lax.fori_loop(0, TM_E, issue, 0, unroll=DMA_UNROLL)
    pltpu.make_async_copy(xg_ref, xs_hbm.at[pl.ds(0, TM_E), :], sem).wait()


def _dispatch(pos, last_tile, xg):
    return pl.pallas_call(
        _dispatch_kernel,
        grid_spec=pltpu.PrefetchScalarGridSpec(
            num_scalar_prefetch=2,
            grid=(T_ALL // TM_E,),
            in_specs=[pl.BlockSpec((TM_E, XG_W), lambda i, pos, last: (i, 0))],
            out_specs=pl.BlockSpec(memory_space=pl.ANY),
            scratch_shapes=[pltpu.VMEM((TM_E, XG_W), f32), pltpu.SemaphoreType.DMA(()),
                            pltpu.SemaphoreType.DMA(())]),
        out_shape=jax.ShapeDtypeStruct((T_SORT, XG_W), f32),
        compiler_params=pltpu.CompilerParams(dimension_semantics=("arbitrary",),
                                             vmem_limit_bytes=VMEM_LIMIT),
        name="moe_dispatch",
    )(pos, last_tile, xg)


def _gather_rows(pos_ref, base, ys_hbm, o_ref, sem):
    def issue(r, c):
        p = pos_ref[base + r]
        pltpu.make_async_copy(ys_hbm.at[pl.ds(p, 1), :], o_ref.at[pl.ds(r, 1), :], sem).start()
        return c

    lax.fori_loop(0, TM_E, issue, 0, unroll=DMA_UNROLL)
    pltpu.make_async_copy(ys_hbm.at[pl.ds(0, TM_E), :], o_ref, sem).wait()


def _combine_kernel(pos_ref, ys_hbm, o_ref, sem):
    _gather_rows(pos_ref, pl.program_id(0) * TM_E, ys_hbm, o_ref, sem)


def _combine_split_kernel(pos_ref, ys_hbm, op_ref, os_ref, sem):
    i = pl.program_id(0)

    @pl.when(i < T_PROMPT // TM_E)
    def _():
        _gather_rows(pos_ref, i * TM_E, ys_hbm, op_ref, sem)

    @pl.when(i >= T_PROMPT // TM_E)
    def _():
        _gather_rows(pos_ref, i * TM_E, ys_hbm, os_ref, sem)


def _combine(pos, ys, split):
    npt = T_PROMPT // TM_E
    if split:
        out_specs = [pl.BlockSpec((TM_E, D_MODEL), lambda i, pos: (jnp.minimum(i, npt - 1), 0)),
                     pl.BlockSpec((TM_E, D_MODEL), lambda i, pos: (jnp.maximum(i - npt, 0), 0))]
        out_shape = [jax.ShapeDtypeStruct((T_PROMPT, D_MODEL), f32),
                     jax.ShapeDtypeStruct((T_SAMPLE, D_MODEL), f32)]
    else:
        out_specs = pl.BlockSpec((TM_E, D_MODEL), lambda i, pos: (i, 0))
        out_shape = jax.ShapeDtypeStruct((T_ALL, D_MODEL), f32)
    return pl.pallas_call(
        _combine_split_kernel if split else _combine_kernel,
        grid_spec=pltpu.PrefetchScalarGridSpec(
            num_scalar_prefetch=1,
            grid=(T_ALL // TM_E,),
            in_specs=[pl.BlockSpec(memory_space=pl.ANY)],
            out_specs=out_specs,
            scratch_shapes=[pltpu.SemaphoreType.DMA(())]),
        out_shape=out_shape,
        compiler_params=pltpu.CompilerParams(dimension_semantics=("arbitrary",),
                                             vmem_limit_bytes=VMEM_LIMIT),
        name="moe_combine",
    )(pos, ys)


def _experts_kernel(tg_ref, tv_ref, ti_ref, xs_ref, nw_ref, wg_ref, wu_ref, wd_ref, nf_ref, ys_ref, hbuf, ybuf, *,
                    final_norm):
    del ti_ref
    t = pl.program_id(0)
    e = pl.program_id(1)
    valid = tv_ref[t] == 1

    @pl.when(valid & (e == 0))
    def _():
        x2 = xs_ref[:, pl.ds(0, D_MODEL)]
        hbuf[...] = _rms(x2, nw_ref[...]).astype(bf16)
        ybuf[...] = x2

    @pl.when(valid)
    def _():
        hb = hbuf[...]
        he = (_silu(jnp.dot(hb, wg_ref[0, 0], preferred_element_type=f32))
              * jnp.dot(hb, wu_ref[0, 0], preferred_element_type=f32))
        yd = jnp.dot(he.astype(bf16), wd_ref[0, 0], preferred_element_type=f32)
        g = xs_ref[:, pl.ds(D_MODEL, 128)]
        lane = lax.broadcasted_iota(i32, g.shape, 1)
        gcol = jnp.sum(jnp.where(lane == LANE_E0 + EPG * tg_ref[t] + e, g, 0.0), axis=-1, keepdims=True)
        ybuf[...] += gcol * yd

    @pl.when(e == EPG - 1)
    def _():
        y = ybuf[...]
        if final_norm:
            y = _rms(y, nf_ref[...])
        ys_ref[...] = y


def _experts(tile_group, tile_valid, tile_idx, xs, nw, wg, wu, wd, nf, l, final_norm):
    wsel = lambda t, e, tg, tv, ti: (l, tg[t] * EPG + e, 0, 0)
    return pl.pallas_call(
        functools.partial(_experts_kernel, final_norm=final_norm),
        grid_spec=pltpu.PrefetchScalarGridSpec(
            num_scalar_prefetch=3,
            grid=(NT_E, EPG),
            in_specs=[pl.BlockSpec((TM_E, XG_W), lambda t, e, tg, tv, ti: (ti[t], 0)),
                      pl.BlockSpec((1, D_MODEL), lambda t, e, tg, tv, ti: (0, 0)),
                      pl.BlockSpec((1, 1, D_MODEL, D_FF_E), wsel),
                      pl.BlockSpec((1, 1, D_MODEL, D_FF_E), wsel),
                      pl.BlockSpec((1, 1, D_FF_E, D_MODEL), wsel),
                      pl.BlockSpec((1, D_MODEL), lambda t, e, tg, tv, ti: (0, 0))],
            out_specs=pl.BlockSpec((TM_E, D_MODEL), lambda t, e, tg, tv, ti: (ti[t], 0)),
            scratch_shapes=[pltpu.VMEM((TM_E, D_MODEL), bf16), pltpu.VMEM((TM_E, D_MODEL), f32)]),
        out_shape=jax.ShapeDtypeStruct((T_SORT, D_MODEL), f32),
        compiler_params=pltpu.CompilerParams(dimension_semantics=("arbitrary", "arbitrary"),
                                             vmem_limit_bytes=VMEM_LIMIT),
        name="moe_experts",
    )(tile_group, tile_valid, tile_idx, xs, nw, wg, wu, wd, nf)


def _route_meta(xg, cnt_rows):
    g = xg[:, D_MODEL + LANE_G].astype(i32)
    rank = xg[:, D_MODEL + LANE_RANK].astype(i32)
    cnt = cnt_rows[0, :N_GROUPS].astype(i32)
    padded = ((cnt + TM_E - 1) // TM_E) * TM_E
    off_end = jnp.cumsum(padded)
    off = off_end - padded
    gid = jnp.arange(N_GROUPS, dtype=i32)
    pos = rank + jnp.sum(jnp.where(g[:, None] == gid[None, :], off[None, :], 0), axis=1)
    tile_start = jnp.arange(NT_E, dtype=i32) * TM_E
    tile_group = jnp.minimum(jnp.sum((tile_start[:, None] >= off_end[None, :]).astype(i32), axis=1), N_GROUPS - 1)
    tile_valid = (tile_start < off_end[-1]).astype(i32)
    n_used = off_end[-1] // TM_E
    tile_idx = jnp.minimum(jnp.arange(NT_E, dtype=i32), n_used - 1)
    tile_group = jnp.take(tile_group, tile_idx)
    last_tile = jnp.maximum(off_end // TM_E - 1, 0)
    return pos, tile_group, tile_valid, tile_idx, last_tile


def _pad_cols(a, n):
    return jnp.pad(a, ((0, 0), (0, n - a.shape[1])))


def _pad_rows(a, n):
    return jnp.pad(a, ((0, n - a.shape[0]), (0, 0)))


def _prep_w_in(w):
    pad_last = lambda a, n: jnp.pad(a, ((0, 0), (0, 0), (0, n - a.shape[-1])))
    c = w[..., 2568:]
    parts = [w[..., 0:2048], pad_last(w[..., 2048:2056], 128), w[..., 2056:2568],
             c[..., 0:768], pad_last(c[..., 768:832], 128), pad_last(c[..., 832:896], 128), c[..., 896:1024]]
    return jnp.concatenate(parts, axis=-1).astype(bf16)


def _prep_mu(mu):
    m = mu[None, :]
    return jnp.concatenate([m[:, 0:768], _pad_cols(m[:, 768:832], 128), _pad_cols(m[:, 832:896], 128),
                            m[:, 896:1024]], axis=1)


def _sgu_mats(sgu_w, sgu_b):
    t = jnp.arange(SGU_CHUNK)
    wm = jnp.where(t[:, None] >= t[None, :], sgu_w, 0.0)
    bias_p = jnp.repeat(jnp.transpose(sgu_b), DH_B, axis=1)
    small = jnp.zeros((4, SEQ_PAD, SEQ_PAD), f32).at[:, SEQ_LEAD:, SEQ_LEAD:].set(wm[:, :DEC_SEQ, :DEC_SEQ])
    eye16 = jnp.eye(TB_S // SEQ_PAD, dtype=f32)
    wm_s = jnp.einsum('ab,hij->haibj', eye16, small).reshape(4, TB_S, TB_S)
    bias_small = jnp.zeros((SEQ_PAD, D_B), f32).at[SEQ_LEAD:].set(bias_p[:DEC_SEQ])
    bias_s = jnp.tile(bias_small, (TB_S // SEQ_PAD, 1))
    return wm.astype(bf16), bias_p, wm_s.astype(bf16), bias_s


def _row(a, n):
    return _pad_cols(a.reshape(1, -1), n)


def kernel(x_prompt, x_sample, state_gdn, state_gdn_conv, state_rwkv, state_rwkv_shift, norm_mix, norm_ffn, norm_final, w_in, gdn_conv_w, gdn_a_log, gdn_dt_bias, gdn_norm_w, sgu_ln_w, sgu_ln_b, sgu_w, sgu_b, rwkv_mu, rwkv_w0, rwkv_w_up, rwkv_a0, rwkv_a_up, rwkv_g_up, rwkv_k_k, rwkv_k_a, rwkv_r_k, rwkv_ln_w, rwkv_ln_b, w_out, router_group_w, router_group_b, router_expert_w, router_expert_b, expert_w_gate, expert_w_up, expert_w_down):
    x = (x_prompt.reshape(T_PROMPT, D_MODEL),
         jnp.pad(x_sample, ((0, 0), (SEQ_LEAD, 0), (0, 0))).reshape(T_SAMPLE, D_MODEL))
    sample_states = ()

    w_pad = _prep_w_in(w_in)
    w_out_b = w_out.astype(bf16)
    wg_b, wu_b, wd_b = expert_w_gate.astype(bf16), expert_w_up.astype(bf16), expert_w_down.astype(bf16)

    outs = {k: [] for k in ('gdn_p', 'conv_p', 'rwkv_p', 'shift_p', 'gdn_s', 'conv_s', 'rwkv_s', 'shift_s', 'cv_s')}
    for l in range(DEPTH):
        last = l == DEPTH - 1
        v128 = jnp.concatenate([
            jnp.pad(gdn_a_log[l].reshape(1, H_A), ((0, 0), (4, 120))),
            jnp.pad(gdn_dt_bias[l].reshape(1, H_A), ((0, 0), (4, 120))),
            gdn_norm_w[l].reshape(1, HD_A), jnp.zeros((5, 128), f32)], axis=0)
        v256 = jnp.concatenate([a.reshape(1, D_C) for a in (
            sgu_ln_w[l], sgu_ln_b[l], rwkv_w0[l], rwkv_a0[l], rwkv_k_k[l], rwkv_k_a[l], rwkv_r_k[l],
            rwkv_ln_w[l], rwkv_ln_b[l])] + [jnp.zeros((7, D_C), f32)], axis=0)
        wm_p, bias_p, wm_s, bias_s = _sgu_mats(sgu_w[l], sgu_b[l])
        common = (gdn_conv_w[l], v128, v256)
        tail = (_prep_mu(rwkv_mu[l]), _pad_rows(rwkv_w_up[l], 128).astype(bf16),
                _pad_rows(rwkv_a_up[l], 128).astype(bf16), rwkv_g_up[l].astype(bf16))
        mw_p = common + (wm_p, bias_p) + tail
        mw_s = common + (wm_s, bias_s) + tail

        proj, h = _inproj(x, norm_mix[l].reshape(1, D_MODEL), w_pad, l)
        pcf = _mm(state_rwkv_shift[l], w_pad[l, :, OFF_C:]).reshape(DEC_BATCH, 1, NC_PAD)
        mix_p, gdn_p, conv_p, rwkv_p = _mixer_prompt(proj, mw_p)
        mix_s, gdn_s, conv_s, rwkv_s, cv_s = _mixer_sample(
            proj, state_gdn, state_gdn_conv, state_rwkv, pcf, mw_s, l, sample_states)
        sample_states = (gdn_s, rwkv_s)

        rw = _pad_cols(jnp.concatenate([router_group_w[l], router_expert_w[l]], axis=1), 128)
        rwh = rw.astype(bf16)
        rwl = (rw - rwh.astype(f32)).astype(bf16)
        rb = _row(jnp.concatenate([router_group_b[l], router_expert_b[l]]), 128)
        nw_ffn = norm_ffn[l].reshape(1, D_MODEL)
        xg, cnt_rows = _outproj(x, mix_p, mix_s, w_out_b, nw_ffn, rwh, rwl, rb, l)
        pos, tile_group, tile_valid, tile_idx, last_tile = _route_meta(xg, cnt_rows)
        xsorted = _dispatch(pos, last_tile, xg)
        ys = _experts(tile_group, tile_valid, tile_idx, xsorted, nw_ffn, wg_b, wu_b, wd_b,
                      norm_final.reshape(1, D_MODEL), l, final_norm=last)
        x = _combine(pos, ys, split=last)
        x = x if last else (x,)

        outs['gdn_p'].append(gdn_p)
        outs['conv_p'].append(conv_p)
        outs['rwkv_p'].append(rwkv_p)
        outs['shift_p'].append(h[SEQ // SEQ_PAD - 1:T_PROMPT // SEQ_PAD:SEQ // SEQ_PAD])
        outs['conv_s'].append(conv_s)
        outs['shift_s'].append(h[T_PROMPT // SEQ_PAD:])
        outs['cv_s'].append(cv_s.reshape(DEC_BATCH, SEQ_PAD, D_B)[:, SEQ_LEAD:])

    y_prompt = x[0].reshape(BATCH, SEQ, D_MODEL)
    y_sample = x[1].reshape(DEC_BATCH, SEQ_PAD, D_MODEL)[:, SEQ_LEAD:]
    st = lambda k: jnp.stack(outs[k])
    return (y_prompt, y_sample, st('gdn_p'), st('conv_p'), st('rwkv_p'), st('shift_p'),
            sample_states[0], st('conv_s'), sample_states[1], st('shift_s'), st('cv_s'))
```

```python
import functools
import math

import jax
import jax.numpy as jnp
from jax import lax
from jax.experimental import pallas as pl
from jax.experimental.pallas import tpu as pltpu

f32 = jnp.float32
bf16 = jnp.bfloat16
i32 = jnp.int32

D_MODEL = 1024
BATCH = 8
SEQ = 2048
DEPTH = 2
DEC_BATCH = 128
DEC_SEQ = 4
H_A = 4
HD_A = 128
D_A = 512
CONV_W = 4
D_B = 256
DH_B = 64
SGU_CHUNK = 128
H_C = 4
N_C = 64
D_C = 256
N_GROUPS = 4
EPG = 4
D_FF_E = 512
NORM_EPS = 1e-6
LN_EPS = 1e-5
GN_EPS = 64e-5

SEQ_PAD = 8
SEQ_LEAD = SEQ_PAD - DEC_SEQ
T_PROMPT = BATCH * SEQ
T_SAMPLE = DEC_BATCH * SEQ_PAD
T_ALL = T_PROMPT + T_SAMPLE
TB_P = 256
TB_S = 128
GDN_CHUNK = 64

OFF_Q, OFF_K, OFF_V, OFF_Z, OFF_BA, OFF_U, OFF_VB, OFF_C = 0, 512, 1024, 1536, 2048, 2176, 2432, 2688
NP_IN = 3840
NC_PAD = NP_IN - OFF_C

XG_W = D_MODEL + 128
LANE_G, LANE_RANK, LANE_E0 = 0, 1, 4

VMEM_LIMIT = 48 * 1024 * 1024


NN = (((1,), (0,)), ((), ()))
NT = (((1,), (1,)), ((), ()))
TN = (((0,), (0,)), ((), ()))
BNN = (((2,), (1,)), ((0,), (0,)))


def _bdot(a, b, dims=NN):
    return lax.dot_general(a.astype(bf16), b.astype(bf16), dims, preferred_element_type=f32)


def _split2(x):
    hi = x.astype(bf16)
    lo = (x - hi.astype(f32)).astype(bf16)
    return hi, lo


def _split3(x):
    hi = x.astype(bf16)
    r = x - hi.astype(f32)
    mid = r.astype(bf16)
    lo = (r - mid.astype(f32)).astype(bf16)
    return hi, mid, lo


def _dot01_left(m01, x):
    hi, mid, lo = _split3(x)
    d = lambda p: lax.dot_general(m01, p, NN, preferred_element_type=f32)
    return d(hi) + d(mid) + d(lo)


def _dot01_right(x, m01):
    hi, lo = _split2(x)
    d = lambda p: lax.dot_general(p, m01, NN, preferred_element_type=f32)
    return d(hi) + d(lo)


def _softplus(x):
    return jnp.maximum(x, 0.0) + jnp.log1p(jnp.exp(-jnp.abs(x)))


def _sigmoid(x):
    return 1.0 / (1.0 + jnp.exp(-x))


def _silu(x):
    return x * _sigmoid(x)


def _gelu(x):
    return 0.5 * x * (1.0 + lax.erf(x * (1.0 / math.sqrt(2.0))))


def _rms(x, w):
    return x * lax.rsqrt(jnp.mean(x * x, axis=-1, keepdims=True) + NORM_EPS) * w


TM_A = 512
N_SLAB = 768


def _start_row_copies(idx_ref, base, src_hbm, dst, sem, r0, n, inline):
    def one(r):
        p = idx_ref[base + r]
        pltpu.make_async_copy(src_hbm.at[pl.ds(p, 1), :], dst.at[pl.ds(r, 1), :], sem).start()

    if inline:
        for k in range(n):
            one(r0 + k)
    else:
        def body(r, c):
            one(r)
            return c
        lax.fori_loop(r0, r0 + n, body, 0, unroll=DMA_UNROLL)


def _wait_row_copies(src_hbm, dst, sem):
    pltpu.make_async_copy(src_hbm.at[pl.ds(0, dst.shape[0]), :], dst, sem).wait()


def _inproj_kernel(*refs, split):
    if split:
        xp_ref, xs_ref, nw_ref, w_ref, proj_ref, h8_ref, hscr = refs
        x = jnp.where(pl.program_id(0) < T_PROMPT // TM_A, xp_ref[...], xs_ref[...])
    else:
        x_ref, nw_ref, w_ref, proj_ref, h8_ref, hscr = refs
        x = x_ref[...]
    _inproj_body(x, nw_ref, w_ref, proj_ref, h8_ref, hscr)


def _inproj_gather_kernel(pos_ref, ys_hbm, nw_ref, w_ref, proj_ref, h8_ref, x_ref, hscr, xbuf, sem):
    i = pl.program_id(0)
    n = pl.num_programs(0)
    slot = i % 2

    @pl.when(i == 0)
    def _():
        _start_row_copies(pos_ref, 0, ys_hbm, xbuf.at[0], sem.at[0], 0, TM_A, inline=False)

    _wait_row_copies(ys_hbm, xbuf.at[slot], sem.at[slot])
    x = xbuf[slot]
    x_ref[...] = x
    nxt = jnp.minimum(i + 1, n - 1)
    _start_row_copies(pos_ref, nxt * TM_A, ys_hbm, xbuf.at[1 - slot], sem.at[1 - slot], 0, TM_A, inline=True)
    _inproj_body(x, nw_ref, w_ref, proj_ref, h8_ref, hscr)

    @pl.when(i == n - 1)
    def _():
        _wait_row_copies(ys_hbm, xbuf.at[1 - slot], sem.at[1 - slot])


def _inproj_body(x, nw_ref, w_ref, proj_ref, h8_ref, hscr):
    h = _rms(x, nw_ref[...])
    for k in range(D_MODEL // 128):
        hscr[k] = h[:, k * 128:(k + 1) * 128]
        h8_ref[:, pl.ds(k * 128, 128)] = hscr[k, pl.ds(SEQ_PAD - 1, TM_A // SEQ_PAD, stride=SEQ_PAD), :]
    hb = h.astype(bf16)
    for n in range(NP_IN // N_SLAB):
        sl = pl.ds(n * N_SLAB, N_SLAB)
        proj_ref[:, sl] = jnp.dot(hb, w_ref[0, :, sl], preferred_element_type=f32)


def _inproj_gather(pos, ys, nw, w_pad, l):
    return pl.pallas_call(
        _inproj_gather_kernel,
        grid_spec=pltpu.PrefetchScalarGridSpec(
            num_scalar_prefetch=1,
            grid=(T_ALL // TM_A,),
            in_specs=[pl.BlockSpec(memory_space=pl.ANY),
                      pl.BlockSpec((1, D_MODEL), lambda i, pos: (0, 0)),
                      pl.BlockSpec((1, D_MODEL, NP_IN), lambda i, pos: (l, 0, 0))],
            out_specs=[pl.BlockSpec((TM_A, NP_IN), lambda i, pos: (i, 0)),
                       pl.BlockSpec((TM_A // SEQ_PAD, D_MODEL), lambda i, pos: (i, 0)),
                       pl.BlockSpec((TM_A, D_MODEL), lambda i, pos: (i, 0))],
            scratch_shapes=[pltpu.VMEM((D_MODEL // 128, TM_A, 128), f32),
                            pltpu.VMEM((2, TM_A, D_MODEL), f32),
                            pltpu.SemaphoreType.DMA((2,))]),
        out_shape=[jax.ShapeDtypeStruct((T_ALL, NP_IN), f32),
                   jax.ShapeDtypeStruct((T_ALL // SEQ_PAD, D_MODEL), f32),
                   jax.ShapeDtypeStruct((T_ALL, D_MODEL), f32)],
        compiler_params=pltpu.CompilerParams(dimension_semantics=("arbitrary",),
                                             vmem_limit_bytes=VMEM_LIMIT),
        name="inproj_gather",
    )(pos, ys, nw, w_pad)


def _inproj(xs, nw, w_pad, l):
    split = len(xs) == 2
    npt = T_PROMPT // TM_A
    if split:
        x_specs = [pl.BlockSpec((TM_A, D_MODEL), lambda i: (jnp.minimum(i, npt - 1), 0)),
                   pl.BlockSpec((TM_A, D_MODEL), lambda i: (jnp.maximum(i - npt, 0), 0))]
    else:
        x_specs = [pl.BlockSpec((TM_A, D_MODEL), lambda i: (i, 0))]
    return pl.pallas_call(
        functools.partial(_inproj_kernel, split=split),
        grid=(T_ALL // TM_A,),
        in_specs=x_specs + [pl.BlockSpec((1, D_MODEL), lambda i: (0, 0)),
                            pl.BlockSpec((1, D_MODEL, NP_IN), lambda i: (l, 0, 0))],
        out_specs=[pl.BlockSpec((TM_A, NP_IN), lambda i: (i, 0)),
                   pl.BlockSpec((TM_A // SEQ_PAD, D_MODEL), lambda i: (i, 0))],
        out_shape=[jax.ShapeDtypeStruct((T_ALL, NP_IN), f32),
                   jax.ShapeDtypeStruct((T_ALL // SEQ_PAD, D_MODEL), f32)],
        scratch_shapes=[pltpu.VMEM((D_MODEL // 128, TM_A, 128), f32)],
        compiler_params=pltpu.CompilerParams(dimension_semantics=("arbitrary",),
                                             vmem_limit_bytes=VMEM_LIMIT),
        name="inproj",
    )(*xs, nw, w_pad)


def _mm_kernel(a_ref, b_ref, o_ref):
    o_ref[...] = jnp.dot(a_ref[...].astype(bf16), b_ref[...], preferred_element_type=f32)


def _mm(a, b):
    return pl.pallas_call(
        _mm_kernel,
        out_shape=jax.ShapeDtypeStruct((a.shape[0], b.shape[1]), f32),
        compiler_params=pltpu.CompilerParams(vmem_limit_bytes=VMEM_LIMIT),
        name="shift_proj",
    )(a, b)


def _neumann(L, iters):
    C = L.shape[-1]
    bmm = lambda a, b: lax.dot_general(a.astype(bf16), b.astype(bf16), BNN, preferred_element_type=f32)
    N = -L
    Q = bmm(N, N)
    for _ in range(iters - 1):
        R = bmm(jnp.concatenate([N, Q], axis=1), Q)
        N = N + Q + R[:, :C]
        Q = R[:, C:]
    return N + Q + bmm(N, Q)


def _cat(parts, axis):
    return parts[0] if len(parts) == 1 else jnp.concatenate(parts, axis=axis)


def _mixer_kernel(*refs, TB, C, G, per_seq, n_alias=0):
    it = iter(refs)
    proj_ref = next(it)
    if per_seq:
        gdn_in, conv_in, rwkv_in, pcf_in = next(it), next(it), next(it), next(it)
    convw_ref, v128_ref, v256_ref, sguw_ref, sgub_ref, mu_ref, wup_ref, aup_ref, gup_ref = (
        next(it) for _ in range(9))
    for _ in range(n_alias):
        next(it)
    mix_ref, gdn_out, conv_out, rwkv_out = next(it), next(it), next(it), next(it)
    cv_out = next(it) if per_seq else None
    xp, pp = next(it), next(it)

    nchunk = TB // C
    ngrp = C // G
    iters = int(math.log2(G)) - 1

    if per_seq:
        xp[pl.ds(0, 8), :] = jnp.zeros((8, 3 * D_A), f32)
        pp[pl.ds(0, 8), :] = jnp.zeros((8, NC_PAD), f32)
    else:
        @pl.when(pl.program_id(1) == 0)
        def _():
            xp[pl.ds(0, 8), :] = jnp.zeros((8, 3 * D_A), f32)
            pp[pl.ds(0, 8), :] = jnp.zeros((8, NC_PAD), f32)
            gdn_out[...] = jnp.zeros(gdn_out.shape, f32)
            rwkv_out[...] = jnp.zeros(rwkv_out.shape, f32)
    xp[pl.ds(8, TB), :] = proj_ref[:, pl.ds(OFF_Q, 3 * D_A)]
    pp[pl.ds(8, TB), :] = proj_ref[:, pl.ds(OFF_C, NC_PAD)]
    if per_seq:
        for s in range(TB // SEQ_PAD):
            r0 = 8 + s * SEQ_PAD
            xp[pl.ds(r0 + SEQ_LEAD - (CONV_W - 1), CONV_W - 1), :] = conv_in[s]
            pp[pl.ds(r0 + SEQ_LEAD - 1, 1), :] = pcf_in[s]

    rowi = lax.broadcasted_iota(i32, (TB, 1), 0)
    live = (rowi % SEQ_PAD) >= SEQ_LEAD if per_seq else None

    ii = lax.broadcasted_iota(i32, (C, C), 0)
    jj = lax.broadcasted_iota(i32, (C, C), 1)
    same = (ii // G) == (jj // G)
    causal = (ii >= jj) & same
    strict = (ii > jj) & same
    eye = ii == jj
    m_cum = causal.astype(bf16)
    m_grp = same.astype(bf16)

    def conv_cols(r0, c0):
        cs = pl.ds(c0, 128)
        acc = xp[pl.ds(r0 + 8, C), cs] * convw_ref[pl.ds(3, 1), cs]
        for j in range(CONV_W - 1):
            acc = acc + xp[pl.ds(r0 + 5 + j, C), cs] * convw_ref[pl.ds(j, 1), cs]
        return _silu(acc)

    alog_row = v128_ref[pl.ds(0, 1), :]
    dtb_row = v128_ref[pl.ds(1, 1), :]
    gnorm_w = v128_ref[pl.ds(2, 1), :]
    ln_w, ln_b = v256_ref[pl.ds(0, 1), :], v256_ref[pl.ds(1, 1), :]
    w0, a0 = v256_ref[pl.ds(2, 1), :], v256_ref[pl.ds(3, 1), :]
    k_k, k_a, r_k = v256_ref[pl.ds(4, 1), :], v256_ref[pl.ds(5, 1), :], v256_ref[pl.ds(6, 1), :]
    rln_w, rln_b = v256_ref[pl.ds(7, 1), :], v256_ref[pl.ds(8, 1), :]

    l64i = lax.broadcasted_iota(i32, (D_C, D_C), 0) // N_C
    l64j = lax.broadcasted_iota(i32, (D_C, D_C), 1) // N_C
    seg64 = (l64i == l64j).astype(bf16)

    gdn, rwk, lmats = [None] * nchunk, [None] * nchunk, [None] * (8 * nchunk)

    def phase1(c):
        R0 = c * C
        rows = pl.ds(R0, C)
        live_c = live[R0:R0 + C] if per_seq else None

        ba = proj_ref[rows, pl.ds(OFF_BA, 128)]
        beta_all = _sigmoid(ba)
        g_all = -jnp.exp(alog_row) * _softplus(ba + dtb_row)
        if per_seq:
            beta_all = jnp.where(live_c, beta_all, 0.0)
            g_all = jnp.where(live_c, g_all, 0.0)
        pcur = pp[pl.ds(R0 + 8, C), :]
        pprev = pp[pl.ds(R0 + 7, C), :]
        pm = pcur + mu_ref[...] * (pprev - pcur)
        r_ = pm[:, 0:D_C]
        kc = pm[:, D_C:2 * D_C]
        vc = pm[:, 2 * D_C:3 * D_C]
        wd = pm[:, 3 * D_C:3 * D_C + 128]
        ad = pm[:, 3 * D_C + 128:3 * D_C + 256]
        gd = pm[:, 3 * D_C + 256:3 * D_C + 384]
        wl_mm = _bdot(jnp.tanh(wd), wup_ref[...])
        a_mm = _bdot(ad, aup_ref[...])
        gate = _bdot(_sigmoid(gd), gup_ref[...])
        kk = kc * k_k
        kk_ss = _dot01_right(kk * kk, seg64)
        yield

        gc_all = _dot01_left(m_cum, g_all)
        gl_all = _dot01_left(m_grp, g_all)
        w_log = -_softplus(-(w0 + wl_mm)) - 0.5
        logw = -jnp.exp(w_log)
        a_ = _sigmoid(a0 + a_mm)
        kk = kk * lax.rsqrt(kk_ss + 1e-6)
        kc2 = kc * (1.0 + (a_ - 1.0) * k_a)
        if per_seq:
            logw = jnp.where(live_c, logw, 0.0)
            kk = jnp.where(live_c, kk, 0.0)
            kc2 = jnp.where(live_c, kc2, 0.0)
        b_ = kk * a_
        Gc = _dot01_left(m_cum, logw)
        Gl = _dot01_left(m_grp, logw)
        bonus_ss = _dot01_right(r_ * kc2 * r_k, seg64)
        yield

        heads = []
        for h in range(H_A):
            q = conv_cols(R0, OFF_Q + h * HD_A)
            k = conv_cols(R0, OFF_K + h * HD_A)
            v = conv_cols(R0, OFF_V + h * HD_A)
            q = q * lax.rsqrt(jnp.sum(q * q, axis=-1, keepdims=True) + 1e-6) * (HD_A ** -0.5)
            k = k * lax.rsqrt(jnp.sum(k * k, axis=-1, keepdims=True) + 1e-6)
            if per_seq:
                k = jnp.where(live_c, k, 0.0)
            beta = beta_all[:, h:h + 1]
            gcol = gc_all[:, 4 + h:5 + h]
            glr = gl_all[:, 4 + h:5 + h]
            grow = jnp.sum(jnp.where(eye, jnp.broadcast_to(gcol, (C, C)), 0.0), axis=0, keepdims=True)
            decay = jnp.where(causal, jnp.exp(jnp.where(causal, gcol - grow, 0.0)), 0.0)
            eg = jnp.exp(gcol)
            kb = k * beta
            kq = _bdot(jnp.concatenate([kb, q], axis=0), k, NT)
            heads.append(dict(kq=kq, decay=decay,
                              rhs=jnp.concatenate([v * beta, kb * eg], axis=1),
                              q_dec=q * eg, k_dec=k * jnp.exp(glr - gcol), glr=glr))
        gdn[c] = heads
        e_neg = jnp.exp(-Gc)
        e_rem = jnp.exp(Gl - Gc)
        rG = r_ * jnp.exp(Gc)
        kkG = kk * jnp.exp(Gc - logw)
        kN = kc2 * e_neg
        bN = b_ * e_neg
        aalls = []
        for h in range(H_C):
            hs = slice(h * N_C, (h + 1) * N_C)
            aalls.append(_bdot(jnp.concatenate([kkG[:, hs], rG[:, hs]], axis=0),
                               jnp.concatenate([bN[:, hs], kN[:, hs]], axis=0), NT))
        yield

        for h in range(H_A):
            d = gdn[c][h]
            kq = d.pop('kq')
            decay = d.pop('decay')
            lmats[8 * c + h] = jnp.where(strict, kq[:C] * decay, 0.0)
            d['attn'] = kq[C:] * decay
        heads = []
        for h in range(H_C):
            hs = slice(h * N_C, (h + 1) * N_C)
            aall = aalls[h]
            lmats[8 * c + H_A + h] = jnp.where(strict, aall[:C, :C], 0.0)
            akk_k = jnp.where(strict, aall[:C, C:], 0.0)
            ar = jnp.concatenate([jnp.where(causal, aall[C:, C:], 0.0),
                                  -jnp.where(causal, aall[C:, :C], 0.0)], axis=1)
            heads.append(dict(x1=_bdot(akk_k, vc[:, hs]), ar=ar))
        rwk[c] = dict(heads=heads, vc=vc, rG=rG, kkG=kkG, kdec=kc2 * e_rem, bdec=b_ * e_rem,
                      e_last=jnp.exp(Gl), gate=gate, bonus=bonus_ss * vc)
        yield

    def lockstep(gens):
        gens = list(gens)
        while gens:
            alive = []
            for g in gens:
                try:
                    next(g)
                    alive.append(g)
                except StopIteration:
                    pass
            gens = alive

    lockstep(phase1(c) for c in range(nchunk))

    nmats = _neumann(jnp.stack(lmats), iters)

    for c in range(nchunk):
        for h in range(H_A):
            d = gdn[c][h]
            sol = d['rhs'] + _bdot(nmats[8 * c + h], d['rhs'])
            d['u'], d['w'] = sol[:, :HD_A], sol[:, HD_A:]
        for h in range(H_C):
            hs = slice(h * N_C, (h + 1) * N_C)
            d = rwk[c]['heads'][h]
            both = jnp.concatenate([d['x1'], rwk[c]['kkG'][:, hs]], axis=1)
            both = both + _bdot(nmats[8 * c + H_A + h], both)
            d['u_p'], d['w_p'] = both[:, :N_C], both[:, N_C:]
            d['both'] = both

    if not per_seq:
        for c in range(nchunk):
            for h in range(H_A):
                d = gdn[c][h]
                wu = jnp.concatenate([d['w'], d['u']], axis=1)
                a_wu = _bdot(d['attn'], wu)
                k_wu = _bdot(d['k_dec'], wu, TN)
                d['qe'] = d['q_dec'] - a_wu[:, :HD_A]
                d['o0'] = a_wu[:, HD_A:]
                d['m'] = -k_wu[:, :HD_A]
                d['b'] = k_wu[:, HD_A:]
            for h in range(H_C):
                hs = slice(h * N_C, (h + 1) * N_C)
                rc = rwk[c]
                d = rc['heads'][h]
                V = rc['vc'][:, hs]
                x = _bdot(d['ar'][:, C:], d['both'])
                d['y0'] = _bdot(d['ar'][:, :C], V) + x[:, :N_C]
                d['re'] = rc['rG'][:, hs] + x[:, N_C:]
                d['m'] = _bdot(d['w_p'], rc['bdec'][:, hs], TN)
                d['b'] = _bdot(jnp.concatenate([V, -d['u_p']], axis=0),
                               jnp.concatenate([rc['kdec'][:, hs], rc['bdec'][:, hs]], axis=0), TN)

    def gdn_head_seq(c, h):
        rows = pl.ds(c * C, C)
        d = gdn[c][h]
        S = gdn_out[0, h]
        Sb = S.astype(bf16)
        gdn_out[0, h] = S * jnp.exp(d['glr'][C - 1:C, :]) + _bdot(d['m'], Sb) + d['b']
        o = d['o0'] + _bdot(d['qe'], Sb)
        yield
        o = o * lax.rsqrt(jnp.mean(o * o, axis=-1, keepdims=True) + NORM_EPS) * gnorm_w
        z = proj_ref[rows, pl.ds(OFF_Z + h * HD_A, HD_A)]
        mix_ref[rows, pl.ds(h * HD_A, HD_A)] = o * _silu(z)

    def rwkv_head_seq(c, h, ys):
        rc = rwk[c]
        hs = slice(h * N_C, (h + 1) * N_C)
        d = rc['heads'][h]
        S = rwkv_out[0, h]
        Sb = S.astype(bf16)
        rwkv_out[0, h] = S * rc['e_last'][C - 1:C, hs] + d['b'] - _bdot(Sb, d['m'])
        ys[h] = d['y0'] + _bdot(d['re'], Sb, NT)
        yield

    def gdn_head(c, h):
        rows = pl.ds(c * C, C)
        d = gdn[c][h]
        rq, vn, s_old = [], [], []
        for s in range(ngrp):
            gs = slice(s * G, (s + 1) * G)
            S = gdn_in[s, h] if per_seq else gdn_out[0, h]
            s_old.append(S)
            R = _bdot(jnp.concatenate([d['w'][gs], d['q_dec'][gs]], axis=0), S)
            vn.append(d['u'][gs] - R[:G])
            rq.append(R[G:])
        yield
        v_new = _cat(vn, 0)
        o = _cat(rq, 0) + _bdot(d['attn'], v_new)
        for s in range(ngrp):
            gs = slice(s * G, (s + 1) * G)
            g_last = jnp.exp(d['glr'][s * G + G - 1:s * G + G, :])
            S_new = s_old[s] * g_last + _bdot(d['k_dec'][gs], v_new[gs], TN)
            if per_seq:
                gdn_out[s, h] = S_new
            else:
                gdn_out[0, h] = S_new
        yield
        o = o * lax.rsqrt(jnp.mean(o * o, axis=-1, keepdims=True) + NORM_EPS) * gnorm_w
        z = proj_ref[rows, pl.ds(OFF_Z + h * HD_A, HD_A)]
        mix_ref[rows, pl.ds(h * HD_A, HD_A)] = o * _silu(z)

    def rwkv_head(c, h, ys):
        rc = rwk[c]
        hs = slice(h * N_C, (h + 1) * N_C)
        d = rc['heads'][h]
        V = rc['vc'][:, hs]
        rr, ut, s_old = [], [], []
        for s in range(ngrp):
            gs = slice(s * G, (s + 1) * G)
            S = rwkv_in[s, h] if per_seq else rwkv_out[0, h]
            s_old.append(S)
            R = _bdot(jnp.concatenate([d['w_p'][gs], rc['rG'][gs, hs]], axis=0), S, NT)
            ut.append(d['u_p'][gs] + R[:G])
            rr.append(R[G:])
        yield
        Ut = _cat(ut, 0)
        ys[h] = _cat(rr, 0) + _bdot(d['ar'], jnp.concatenate([V, Ut], axis=0))
        for s in range(ngrp):
            gs = slice(s * G, (s + 1) * G)
            upd = _bdot(jnp.concatenate([V[gs], -Ut[gs]], axis=0),
                        jnp.concatenate([rc['kdec'][gs, hs], rc['bdec'][gs, hs]], axis=0), TN)
            S_new = s_old[s] * rc['e_last'][s * G + G - 1:s * G + G, hs] + upd
            if per_seq:
                rwkv_out[s, h] = S_new
            else:
                rwkv_out[0, h] = S_new
        yield

    for c in range(nchunk):
        rows = pl.ds(c * C, C)
        rc = rwk[c]
        ys = [None] * H_C
        if per_seq:
            lockstep([gdn_head(c, h) for h in range(H_A)] + [rwkv_head(c, h, ys) for h in range(H_C)])
        else:
            lockstep([gdn_head_seq(c, h) for h in range(H_A)] + [rwkv_head_seq(c, h, ys) for h in range(H_C)])
        y = jnp.concatenate(ys, axis=1)
        mu_y = _dot01_right(y, seg64) * (1.0 / N_C)
        dy = y - mu_y
        var_y = _dot01_right(dy * dy, seg64) * (1.0 / N_C)
        y = dy * lax.rsqrt(var_y + GN_EPS) * rln_w + rln_b
        mix_ref[rows, pl.ds(D_A + D_B, D_C)] = (y + rc['bonus']) * rc['gate']

    for sc in range(TB // SGU_CHUNK):
        rows = pl.ds(sc * SGU_CHUNK, SGU_CHUNK)
        ug = _gelu(proj_ref[rows, pl.ds(OFF_U, D_B)])
        vs = _gelu(proj_ref[rows, pl.ds(OFF_VB, D_B)])
        mu_v = _dot01_right(vs, seg64) * (1.0 / DH_B)
        dv = vs - mu_v
        var_v = _dot01_right(dv * dv, seg64) * (1.0 / DH_B)
        vs = dv * lax.rsqrt(var_v + LN_EPS) * ln_w + ln_b
        if per_seq:
            cv_out[rows, :] = vs
        vsb = vs.astype(bf16)
        outs = [lax.dot_general(sguw_ref[h], vsb[:, h * DH_B:(h + 1) * DH_B], NN, preferred_element_type=f32)
                for h in range(4)]
        mixed = jnp.concatenate(outs, axis=1) + sgub_ref[...]
        mix_ref[rows, pl.ds(D_A, D_B)] = ug * mixed

    if per_seq:
        for s in range(TB // SEQ_PAD):
            conv_out[s] = xp[pl.ds(8 + (s + 1) * SEQ_PAD - (CONV_W - 1), CONV_W - 1), :]
    else:
        conv_out[0] = xp[pl.ds(8 + TB - (CONV_W - 1), CONV_W - 1), :]
        xp[pl.ds(0, 8), :] = xp[pl.ds(TB, 8), :]
        pp[pl.ds(0, 8), :] = pp[pl.ds(TB, 8), :]


def _mixer_weight_specs(nidx):
    z2 = (lambda b, j: (0, 0)) if nidx == 2 else (lambda i: (0, 0))
    z3 = (lambda b, j: (0, 0, 0)) if nidx == 2 else (lambda i: (0, 0, 0))
    return [pl.BlockSpec((CONV_W, 3 * D_A), z2),
            pl.BlockSpec((8, 128), z2),
            pl.BlockSpec((16, D_C), z2),
            pl.BlockSpec((4, SGU_CHUNK, SGU_CHUNK), z3),
            pl.BlockSpec((SGU_CHUNK, D_B), z2),
            pl.BlockSpec((1, NC_PAD), z2),
            pl.BlockSpec((128, D_C), z2),
            pl.BlockSpec((128, D_C), z2),
            pl.BlockSpec((128, D_C), z2)]


def _mixer_prompt(proj, mw):
    nj = SEQ // TB_P
    return pl.pallas_call(
        functools.partial(_mixer_kernel, TB=TB_P, C=GDN_CHUNK, G=GDN_CHUNK, per_seq=False),
        grid=(BATCH, nj),
        in_specs=[pl.BlockSpec((TB_P, NP_IN), lambda b, j: (b * nj + j, 0))] + _mixer_weight_specs(2),
        out_specs=[pl.BlockSpec((TB_P, D_MODEL), lambda b, j: (b * nj + j, 0)),
                   pl.BlockSpec((1, H_A, HD_A, HD_A), lambda b, j: (b, 0, 0, 0)),
                   pl.BlockSpec((1, CONV_W - 1, 3 * D_A), lambda b, j: (b, 0, 0)),
                   pl.BlockSpec((1, H_C, N_C, N_C), lambda b, j: (b, 0, 0, 0))],
        out_shape=[jax.ShapeDtypeStruct((T_PROMPT, D_MODEL), f32),
                   jax.ShapeDtypeStruct((BATCH, H_A, HD_A, HD_A), f32),
                   jax.ShapeDtypeStruct((BATCH, CONV_W - 1, 3 * D_A), f32),
                   jax.ShapeDtypeStruct((BATCH, H_C, N_C, N_C), f32)],
        scratch_shapes=[pltpu.VMEM((TB_P + 8, 3 * D_A), f32), pltpu.VMEM((TB_P + 8, NC_PAD), f32)],
        compiler_params=pltpu.CompilerParams(dimension_semantics=("arbitrary", "arbitrary"),
                                             vmem_limit_bytes=VMEM_LIMIT),
        name="mixer_prompt",
    )(proj, *mw)


def _mixer_sample(proj, s_gdn, s_conv, s_rwkv, pcf, mw, l, prev_states):
    nseq = TB_S // SEQ_PAD
    base = T_PROMPT // TB_S
    n_in = 5 + len(mw)
    lay4 = lambda i: (l, i, 0, 0, 0)
    lay3 = lambda i: (l, i, 0, 0)
    return pl.pallas_call(
        functools.partial(_mixer_kernel, TB=TB_S, C=TB_S, G=SEQ_PAD, per_seq=True, n_alias=len(prev_states)),
        grid=(T_SAMPLE // TB_S,),
        in_specs=[pl.BlockSpec((TB_S, NP_IN), lambda i: (base + i, 0)),
                  pl.BlockSpec((None, nseq, H_A, HD_A, HD_A), lay4),
                  pl.BlockSpec((None, nseq, CONV_W - 1, 3 * D_A), lay3),
                  pl.BlockSpec((None, nseq, H_C, N_C, N_C), lay4),
                  pl.BlockSpec((nseq, 1, NC_PAD), lambda i: (i, 0, 0))] + _mixer_weight_specs(1)
                 + [pl.BlockSpec(memory_space=pl.ANY)] * len(prev_states),
        out_specs=[pl.BlockSpec((TB_S, D_MODEL), lambda i: (i, 0)),
                   pl.BlockSpec((None, nseq, H_A, HD_A, HD_A), lay4),
                   pl.BlockSpec((nseq, CONV_W - 1, 3 * D_A), lambda i: (i, 0, 0)),
                   pl.BlockSpec((None, nseq, H_C, N_C, N_C), lay4),
                   pl.BlockSpec((TB_S, D_B), lambda i: (i, 0))],
        out_shape=[jax.ShapeDtypeStruct((T_SAMPLE, D_MODEL), f32),
                   jax.ShapeDtypeStruct((DEPTH, DEC_BATCH, H_A, HD_A, HD_A), f32),
                   jax.ShapeDtypeStruct((DEC_BATCH, CONV_W - 1, 3 * D_A), f32),
                   jax.ShapeDtypeStruct((DEPTH, DEC_BATCH, H_C, N_C, N_C), f32),
                   jax.ShapeDtypeStruct((T_SAMPLE, D_B), f32)],
        scratch_shapes=[pltpu.VMEM((TB_S + 8, 3 * D_A), f32), pltpu.VMEM((TB_S + 8, NC_PAD), f32)],
        input_output_aliases={n_in + k: o for k, o in enumerate((1, 3)[:len(prev_states)])},
        compiler_params=pltpu.CompilerParams(dimension_semantics=("arbitrary",),
                                             vmem_limit_bytes=VMEM_LIMIT),
        name="mixer_sample",
    )(proj, s_gdn, s_conv, s_rwkv, pcf, *mw, *prev_states)


TM_C = 512


def _outproj_kernel(*refs, split):
    i = pl.program_id(0)
    in_prompt = i < T_PROMPT // TM_C
    if split:
        xp_ref, xs_ref = refs[:2]
        x = jnp.where(in_prompt, xp_ref[...], xs_ref[...])
    else:
        x = refs[0][...]
    mixp_ref, mixs_ref, wout_ref, nw_ref, rwh_ref, rwl_ref, rb_ref, xg_ref, cnt_ref, run_ref = refs[-10:]

    @pl.when(i == 0)
    def _():
        run_ref[...] = jnp.zeros(run_ref.shape, f32)

    mix = jnp.where(in_prompt, mixp_ref[...], mixs_ref[...])
    x2 = x + jnp.dot(mix.astype(bf16), wout_ref[0], preferred_element_type=f32)
    xg_ref[:, pl.ds(0, D_MODEL)] = x2
    h2 = _rms(x2, nw_ref[...])
    hh, hl = _split2(h2)
    d = lambda a, b: jnp.dot(a, b, preferred_element_type=f32)
    logits = d(hh, rwh_ref[...]) + d(hl, rwh_ref[...]) + d(hh, rwl_ref[...]) + rb_ref[...]

    lane = lax.broadcasted_iota(i32, logits.shape, 1).astype(f32)
    neg = jnp.float32(-jnp.inf)
    is_g = lane < float(N_GROUPS)
    gl = jnp.where(is_g, logits, neg)
    gmax = jnp.max(gl, axis=-1, keepdims=True)
    gsel = jnp.min(jnp.where(gl == gmax, lane, 128.0), axis=-1, keepdims=True)
    gw = 1.0 / jnp.sum(jnp.where(is_g, jnp.exp(jnp.where(is_g, logits - gmax, 0.0)), 0.0),
                       axis=-1, keepdims=True)
    lo = LANE_E0 + float(EPG) * gsel
    in_grp = (lane >= lo) & (lane < lo + float(EPG))
    el = jnp.where(in_grp, logits, neg)
    t1 = jnp.max(el, axis=-1, keepdims=True)
    i1 = jnp.min(jnp.where(el == t1, lane, 128.0), axis=-1, keepdims=True)
    el2 = jnp.where(lane == i1, neg, el)
    t2 = jnp.max(el2, axis=-1, keepdims=True)
    i2 = jnp.min(jnp.where(el2 == t2, lane, 128.0), axis=-1, keepdims=True)
    e2 = jnp.exp(t2 - t1)
    den = 1.0 + e2
    gates = jnp.where(lane == i1, gw / den, 0.0) + jnp.where(lane == i2, gw * e2 / den, 0.0)

    onehot = jnp.where(lane == gsel, 1.0, 0.0)
    ri = lax.broadcasted_iota(i32, (TM_C, TM_C), 0)
    ci = lax.broadcasted_iota(i32, (TM_C, TM_C), 1)
    before = lax.dot_general((ri > ci).astype(bf16), onehot.astype(bf16), NN, preferred_element_type=f32)
    rank = jnp.sum((before + run_ref[...]) * onehot, axis=-1, keepdims=True)
    run_ref[...] += jnp.sum(onehot, axis=0, keepdims=True)
    cnt_ref[...] = jnp.broadcast_to(run_ref[...], cnt_ref.shape)
    xg_ref[:, pl.ds(D_MODEL, 128)] = (gates + jnp.where(lane == float(LANE_G), gsel, 0.0)
                                      + jnp.where(lane == float(LANE_RANK), rank, 0.0))


def _outproj(xs, mix_p, mix_s, wout, nw, rwh, rwl, rb, l):
    row = lambda i: (i, 0)
    fix = lambda i: (0, 0)
    npt = T_PROMPT // TM_C
    p_rows = pl.BlockSpec((TM_C, D_MODEL), lambda i: (jnp.minimum(i, npt - 1), 0))
    s_rows = pl.BlockSpec((TM_C, D_MODEL), lambda i: (jnp.maximum(i - npt, 0), 0))
    split = len(xs) == 2
    return pl.pallas_call(
        functools.partial(_outproj_kernel, split=split),
        grid=(T_ALL // TM_C,),
        in_specs=([p_rows, s_rows] if split else [pl.BlockSpec((TM_C, D_MODEL), row)]) + [
                  p_rows, s_rows,
                  pl.BlockSpec((1, D_MODEL, D_MODEL), lambda i: (l, 0, 0)), pl.BlockSpec((1, D_MODEL), fix),
                  pl.BlockSpec((D_MODEL, 128), fix), pl.BlockSpec((D_MODEL, 128), fix),
                  pl.BlockSpec((1, 128), fix)],
        out_specs=[pl.BlockSpec((TM_C, XG_W), row), pl.BlockSpec((8, 128), fix)],
        out_shape=[jax.ShapeDtypeStruct((T_ALL, XG_W), f32), jax.ShapeDtypeStruct((8, 128), f32)],
        scratch_shapes=[pltpu.VMEM((1, 128), f32)],
        compiler_params=pltpu.CompilerParams(dimension_semantics=("arbitrary",),
                                             vmem_limit_bytes=VMEM_LIMIT),
        name="outproj_router",
    )(*xs, mix_p, mix_s, wout, nw, rwh, rwl, rb)


TM_E = 512
NT_E = T_ALL // TM_E + N_GROUPS
T_SORT = NT_E * TM_E
DMA_UNROLL = 8


def _invert_kernel(pos_ref, pad_lo_ref, pad_hi_ref, src_ref):
    for g in range(N_GROUPS):
        def fill(p, c):
            src_ref[p] = T_ALL - 1
            return c
        lax.fori_loop(pad_lo_ref[g], pad_hi_ref[g], fill, 0)

    def body(t, c):
        src_ref[pos_ref[t]] = t
        return c
    lax.fori_loop(0, T_ALL, body, 0, unroll=DMA_UNROLL)


def _invert(pos, pad_lo, pad_hi):
    smem = pl.BlockSpec(memory_space=pltpu.SMEM)
    return pl.pallas_call(
        _invert_kernel,
        in_specs=[smem, smem, smem],
        out_specs=smem,
        out_shape=jax.ShapeDtypeStruct((T_SORT,), i32),
        name="moe_invert",
    )(pos, pad_lo, pad_hi)


def _gather_rows(pos_ref, base, ys_hbm, o_ref, sem):
    def issue(r, c):
        p = pos_ref[base + r]
        pltpu.make_async_copy(ys_hbm.at[pl.ds(p, 1), :], o_ref.at[pl.ds(r, 1), :], sem).start()
        return c

    lax.fori_loop(0, TM_E, issue, 0, unroll=DMA_UNROLL)
    pltpu.make_async_copy(ys_hbm.at[pl.ds(0, TM_E), :], o_ref, sem).wait()


def _combine_kernel(pos_ref, ys_hbm, o_ref, sem):
    _gather_rows(pos_ref, pl.program_id(0) * TM_E, ys_hbm, o_ref, sem)


def _combine_split_kernel(pos_ref, ys_hbm, op_ref, os_ref, sem):
    i = pl.program_id(0)

    @pl.when(i < T_PROMPT // TM_E)
    def _():
        _gather_rows(pos_ref, i * TM_E, ys_hbm, op_ref, sem)

    @pl.when(i >= T_PROMPT // TM_E)
    def _():
        _gather_rows(pos_ref, i * TM_E, ys_hbm, os_ref, sem)


def _combine(pos, ys, split):
    npt = T_PROMPT // TM_E
    if split:
        out_specs = [pl.BlockSpec((TM_E, D_MODEL), lambda i, pos: (jnp.minimum(i, npt - 1), 0)),
                     pl.BlockSpec((TM_E, D_MODEL), lambda i, pos: (jnp.maximum(i - npt, 0), 0))]
        out_shape = [jax.ShapeDtypeStruct((T_PROMPT, D_MODEL), f32),
                     jax.ShapeDtypeStruct((T_SAMPLE, D_MODEL), f32)]
    else:
        out_specs = pl.BlockSpec((TM_E, D_MODEL), lambda i, pos: (i, 0))
        out_shape = jax.ShapeDtypeStruct((T_ALL, D_MODEL), f32)
    return pl.pallas_call(
        _combine_split_kernel if split else _combine_kernel,
        grid_spec=pltpu.PrefetchScalarGridSpec(
            num_scalar_prefetch=1,
            grid=(T_ALL // TM_E,),
            in_specs=[pl.BlockSpec(memory_space=pl.ANY)],
            out_specs=out_specs,
            scratch_shapes=[pltpu.SemaphoreType.DMA(())]),
        out_shape=out_shape,
        compiler_params=pltpu.CompilerParams(dimension_semantics=("arbitrary",),
                                             vmem_limit_bytes=VMEM_LIMIT),
        name="moe_combine",
    )(pos, ys)


ROWS_E = TM_E // EPG


def _experts_kernel(tg_ref, tv_ref, ti_ref, src_ref, xg_hbm, nw_ref, wg_ref, wu_ref, wd_ref, nf_ref, ys_ref,
                    xbuf, hbuf, ybuf, gsem, *, final_norm):
    t = pl.program_id(0)
    e = pl.program_id(1)
    valid = tv_ref[t] == 1
    slot = t % 2
    nxt = ti_ref[jnp.minimum(t + 1, NT_E - 1)]

    @pl.when((t == 0) & (e == 0))
    def _():
        _start_row_copies(src_ref, 0, xg_hbm, xbuf.at[0], gsem.at[0], 0, TM_E, inline=False)

    @pl.when(valid & (e == 0))
    def _():
        _wait_row_copies(xg_hbm, xbuf.at[slot], gsem.at[slot])
        x2 = xbuf[slot, :, pl.ds(0, D_MODEL)]
        hbuf[...] = _rms(x2, nw_ref[...]).astype(bf16)
        ybuf[...] = x2

    @pl.when(valid)
    def _():
        _start_row_copies(src_ref, nxt * TM_E, xg_hbm, xbuf.at[1 - slot], gsem.at[1 - slot],
                          e * ROWS_E, ROWS_E, inline=True)
        hb = hbuf[...]
        he = (_silu(jnp.dot(hb, wg_ref[0, 0], preferred_element_type=f32))
              * jnp.dot(hb, wu_ref[0, 0], preferred_element_type=f32))
        yd = jnp.dot(he.astype(bf16), wd_ref[0, 0], preferred_element_type=f32)
        g = xbuf[slot, :, pl.ds(D_MODEL, 128)]
        lane = lax.broadcasted_iota(i32, g.shape, 1)
        gcol = jnp.sum(jnp.where(lane == LANE_E0 + EPG * tg_ref[t] + e, g, 0.0), axis=-1, keepdims=True)
        ybuf[...] += gcol * yd

    @pl.when(e == EPG - 1)
    def _():
        y = ybuf[...]
        if final_norm:
            y = _rms(y, nf_ref[...])
        ys_ref[...] = y

    @pl.when(valid & (nxt == t) & (e == EPG - 1))
    def _():
        _wait_row_copies(xg_hbm, xbuf.at[1 - slot], gsem.at[1 - slot])


def _experts(tile_group, tile_valid, tile_idx, src, xg, nw, wg, wu, wd, nf, l, final_norm):
    wsel = lambda t, e, tg, tv, ti, src: (l, tg[t] * EPG + e, 0, 0)
    return pl.pallas_call(
        functools.partial(_experts_kernel, final_norm=final_norm),
        grid_spec=pltpu.PrefetchScalarGridSpec(
            num_scalar_prefetch=4,
            grid=(NT_E, EPG),
            in_specs=[pl.BlockSpec(memory_space=pl.ANY),
                      pl.BlockSpec((1, D_MODEL), lambda t, e, tg, tv, ti, src: (0, 0)),
                      pl.BlockSpec((1, 1, D_MODEL, D_FF_E), wsel),
                      pl.BlockSpec((1, 1, D_MODEL, D_FF_E), wsel),
                      pl.BlockSpec((1, 1, D_FF_E, D_MODEL), wsel),
                      pl.BlockSpec((1, D_MODEL), lambda t, e, tg, tv, ti, src: (0, 0))],
            out_specs=pl.BlockSpec((TM_E, D_MODEL), lambda t, e, tg, tv, ti, src: (ti[t], 0)),
            scratch_shapes=[pltpu.VMEM((2, TM_E, XG_W), f32), pltpu.VMEM((TM_E, D_MODEL), bf16),
                            pltpu.VMEM((TM_E, D_MODEL), f32), pltpu.SemaphoreType.DMA((2,))]),
        out_shape=jax.ShapeDtypeStruct((T_SORT, D_MODEL), f32),
        compiler_params=pltpu.CompilerParams(dimension_semantics=("arbitrary", "arbitrary"),
                                             vmem_limit_bytes=VMEM_LIMIT),
        name="moe_experts",
    )(tile_group, tile_valid, tile_idx, src, xg, nw, wg, wu, wd, nf)


def _route_meta(xg, cnt_rows):
    g = xg[:, D_MODEL + LANE_G].astype(i32)
    rank = xg[:, D_MODEL + LANE_RANK].astype(i32)
    cnt = cnt_rows[0, :N_GROUPS].astype(i32)
    padded = ((cnt + TM_E - 1) // TM_E) * TM_E
    off_end = jnp.cumsum(padded)
    off = off_end - padded
    gid = jnp.arange(N_GROUPS, dtype=i32)
    pos = rank + jnp.sum(jnp.where(g[:, None] == gid[None, :], off[None, :], 0), axis=1)
    tile_start = jnp.arange(NT_E, dtype=i32) * TM_E
    tile_group = jnp.minimum(jnp.sum((tile_start[:, None] >= off_end[None, :]).astype(i32), axis=1), N_GROUPS - 1)
    tile_valid = (tile_start < off_end[-1]).astype(i32)
    n_used = off_end[-1] // TM_E
    tile_idx = jnp.minimum(jnp.arange(NT_E, dtype=i32), n_used - 1)
    tile_group = jnp.take(tile_group, tile_idx)
    return pos, tile_group, tile_valid, tile_idx, off + cnt, off_end


def _pad_cols(a, n):
    return jnp.pad(a, ((0, 0), (0, n - a.shape[1])))


def _pad_rows(a, n):
    return jnp.pad(a, ((0, n - a.shape[0]), (0, 0)))


def _prep_w_in(w):
    pad_last = lambda a, n: jnp.pad(a, ((0, 0), (0, 0), (0, n - a.shape[-1])))
    c = w[..., 2568:]
    parts = [w[..., 0:2048], pad_last(w[..., 2048:2056], 128), w[..., 2056:2568],
             c[..., 0:768], pad_last(c[..., 768:832], 128), pad_last(c[..., 832:896], 128), c[..., 896:1024]]
    return jnp.concatenate(parts, axis=-1).astype(bf16)


def _prep_mu(mu):
    m = mu[None, :]
    return jnp.concatenate([m[:, 0:768], _pad_cols(m[:, 768:832], 128), _pad_cols(m[:, 832:896], 128),
                            m[:, 896:1024]], axis=1)


def _sgu_mats(sgu_w, sgu_b):
    t = jnp.arange(SGU_CHUNK)
    wm = jnp.where(t[:, None] >= t[None, :], sgu_w, 0.0)
    bias_p = jnp.repeat(jnp.transpose(sgu_b), DH_B, axis=1)
    small = jnp.zeros((4, SEQ_PAD, SEQ_PAD), f32).at[:, SEQ_LEAD:, SEQ_LEAD:].set(wm[:, :DEC_SEQ, :DEC_SEQ])
    eye16 = jnp.eye(TB_S // SEQ_PAD, dtype=f32)
    wm_s = jnp.einsum('ab,hij->haibj', eye16, small).reshape(4, TB_S, TB_S)
    bias_small = jnp.zeros((SEQ_PAD, D_B), f32).at[SEQ_LEAD:].set(bias_p[:DEC_SEQ])
    bias_s = jnp.tile(bias_small, (TB_S // SEQ_PAD, 1))
    return wm.astype(bf16), bias_p, wm_s.astype(bf16), bias_s


def _row(a, n):
    return _pad_cols(a.reshape(1, -1), n)


def kernel(x_prompt, x_sample, state_gdn, state_gdn_conv, state_rwkv, state_rwkv_shift, norm_mix, norm_ffn, norm_final, w_in, gdn_conv_w, gdn_a_log, gdn_dt_bias, gdn_norm_w, sgu_ln_w, sgu_ln_b, sgu_w, sgu_b, rwkv_mu, rwkv_w0, rwkv_w_up, rwkv_a0, rwkv_a_up, rwkv_g_up, rwkv_k_k, rwkv_k_a, rwkv_r_k, rwkv_ln_w, rwkv_ln_b, w_out, router_group_w, router_group_b, router_expert_w, router_expert_b, expert_w_gate, expert_w_up, expert_w_down):
    x = (x_prompt.reshape(T_PROMPT, D_MODEL),
         jnp.pad(x_sample, ((0, 0), (SEQ_LEAD, 0), (0, 0))).reshape(T_SAMPLE, D_MODEL))
    sample_states = ()

    w_pad = _prep_w_in(w_in)
    w_out_b = w_out.astype(bf16)
    wg_b, wu_b, wd_b = expert_w_gate.astype(bf16), expert_w_up.astype(bf16), expert_w_down.astype(bf16)

    outs = {k: [] for k in ('gdn_p', 'conv_p', 'rwkv_p', 'shift_p', 'gdn_s', 'conv_s', 'rwkv_s', 'shift_s', 'cv_s')}
    for l in range(DEPTH):
        last = l == DEPTH - 1
        v128 = jnp.concatenate([
            jnp.pad(gdn_a_log[l].reshape(1, H_A), ((0, 0), (4, 120))),
            jnp.pad(gdn_dt_bias[l].reshape(1, H_A), ((0, 0), (4, 120))),
            gdn_norm_w[l].reshape(1, HD_A), jnp.zeros((5, 128), f32)], axis=0)
        v256 = jnp.concatenate([a.reshape(1, D_C) for a in (
            sgu_ln_w[l], sgu_ln_b[l], rwkv_w0[l], rwkv_a0[l], rwkv_k_k[l], rwkv_k_a[l], rwkv_r_k[l],
            rwkv_ln_w[l], rwkv_ln_b[l])] + [jnp.zeros((7, D_C), f32)], axis=0)
        wm_p, bias_p, wm_s, bias_s = _sgu_mats(sgu_w[l], sgu_b[l])
        common = (gdn_conv_w[l], v128, v256)
        tail = (_prep_mu(rwkv_mu[l]), _pad_rows(rwkv_w_up[l], 128).astype(bf16),
                _pad_rows(rwkv_a_up[l], 128).astype(bf16), rwkv_g_up[l].astype(bf16))
        mw_p = common + (wm_p, bias_p) + tail
        mw_s = common + (wm_s, bias_s) + tail

        if l == 0:
            proj, h = _inproj(x, norm_mix[l].reshape(1, D_MODEL), w_pad, l)
        else:
            proj, h, x_tok = _inproj_gather(pos, ys, norm_mix[l].reshape(1, D_MODEL), w_pad, l)
            x = (x_tok,)
        pcf = _mm(state_rwkv_shift[l], w_pad[l, :, OFF_C:]).reshape(DEC_BATCH, 1, NC_PAD)
        mix_p, gdn_p, conv_p, rwkv_p = _mixer_prompt(proj, mw_p)
        mix_s, gdn_s, conv_s, rwkv_s, cv_s = _mixer_sample(
            proj, state_gdn, state_gdn_conv, state_rwkv, pcf, mw_s, l, sample_states)
        sample_states = (gdn_s, rwkv_s)

        rw = _pad_cols(jnp.concatenate([router_group_w[l], router_expert_w[l]], axis=1), 128)
        rwh = rw.astype(bf16)
        rwl = (rw - rwh.astype(f32)).astype(bf16)
        rb = _row(jnp.concatenate([router_group_b[l], router_expert_b[l]]), 128)
        nw_ffn = norm_ffn[l].reshape(1, D_MODEL)
        xg, cnt_rows = _outproj(x, mix_p, mix_s, w_out_b, nw_ffn, rwh, rwl, rb, l)
        pos, tile_group, tile_valid, tile_idx, pad_lo, pad_hi = _route_meta(xg, cnt_rows)
        src = _invert(pos, pad_lo, pad_hi)
        ys = _experts(tile_group, tile_valid, tile_idx, src, xg, nw_ffn, wg_b, wu_b, wd_b,
                      norm_final.reshape(1, D_MODEL), l, final_norm=last)
        if last:
            x = _combine(pos, ys, split=True)

        outs['gdn_p'].append(gdn_p)
        outs['conv_p'].append(conv_p)
        outs['rwkv_p'].append(rwkv_p)
        outs['shift_p'].append(h[SEQ // SEQ_PAD - 1:T_PROMPT // SEQ_PAD:SEQ // SEQ_PAD])
        outs['conv_s'].append(conv_s)
        outs['shift_s'].append(h[T_PROMPT // SEQ_PAD:])
        outs['cv_s'].append(cv_s.reshape(DEC_BATCH, SEQ_PAD, D_B)[:, SEQ_LEAD:])

    y_prompt = x[0].reshape(BATCH, SEQ, D_MODEL)
    y_sample = x[1].reshape(DEC_BATCH, SEQ_PAD, D_MODEL)[:, SEQ_LEAD:]
    st = lambda k: jnp.stack(outs[k])
    return (y_prompt, y_sample, st('gdn_p'), st('conv_p'), st('rwkv_p'), st('shift_p'),
            sample_states[0], st('conv_s'), sample_states[1], st('shift_s'), st('cv_s'))
```

```python
import functools
import math

import jax
import jax.numpy as jnp
from jax import lax
from jax.experimental import pallas as pl
from jax.experimental.pallas import tpu as pltpu

f32 = jnp.float32
bf16 = jnp.bfloat16
i32 = jnp.int32

D_MODEL = 1024
BATCH = 8
SEQ = 2048
DEPTH = 2
DEC_BATCH = 128
DEC_SEQ = 4
H_A = 4
HD_A = 128
D_A = 512
CONV_W = 4
D_B = 256
DH_B = 64
SGU_CHUNK = 128
H_C = 4
N_C = 64
D_C = 256
N_GROUPS = 4
EPG = 4
D_FF_E = 512
NORM_EPS = 1e-6
LN_EPS = 1e-5
GN_EPS = 64e-5

SEQ_PAD = 8
SEQ_LEAD = SEQ_PAD - DEC_SEQ
T_PROMPT = BATCH * SEQ
T_SAMPLE = DEC_BATCH * SEQ_PAD
T_ALL = T_PROMPT + T_SAMPLE
TB_P = 256
TB_S = 128
GDN_CHUNK = 64

OFF_Q, OFF_K, OFF_V, OFF_Z, OFF_BA, OFF_U, OFF_VB, OFF_C = 0, 512, 1024, 1536, 2048, 2176, 2432, 2688
NP_IN = 3840
NC_PAD = NP_IN - OFF_C

XG_W = D_MODEL + 128
LANE_G, LANE_RANK, LANE_E0 = 0, 1, 4

VMEM_LIMIT = 48 * 1024 * 1024


NN = (((1,), (0,)), ((), ()))
NT = (((1,), (1,)), ((), ()))
TN = (((0,), (0,)), ((), ()))
BNN = (((2,), (1,)), ((0,), (0,)))


def _bdot(a, b, dims=NN):
    return lax.dot_general(a.astype(bf16), b.astype(bf16), dims, preferred_element_type=f32)


def _split2(x):
    hi = x.astype(bf16)
    lo = (x - hi.astype(f32)).astype(bf16)
    return hi, lo


def _split3(x):
    hi = x.astype(bf16)
    r = x - hi.astype(f32)
    mid = r.astype(bf16)
    lo = (r - mid.astype(f32)).astype(bf16)
    return hi, mid, lo


def _dot01_left(m01, x):
    hi, mid, lo = _split3(x)
    d = lambda p: lax.dot_general(m01, p, NN, preferred_element_type=f32)
    return d(hi) + d(mid) + d(lo)


def _dot01_right(x, m01):
    hi, lo = _split2(x)
    d = lambda p: lax.dot_general(p, m01, NN, preferred_element_type=f32)
    return d(hi) + d(lo)


def _softplus(x):
    return jnp.maximum(x, 0.0) + jnp.log1p(jnp.exp(-jnp.abs(x)))


def _sigmoid(x):
    return 0.5 * jnp.tanh(0.5 * x) + 0.5


def _silu(x):
    return x * _sigmoid(x)


def _gelu(x):
    return 0.5 * x * (1.0 + lax.erf(x * (1.0 / math.sqrt(2.0))))


def _rms(x, w):
    return x * lax.rsqrt(jnp.mean(x * x, axis=-1, keepdims=True) + NORM_EPS) * w


TM_A = 512
N_SLAB = 768


def _start_row_copies(idx_ref, base, src_hbm, dst, sem, r0, n, inline):
    def one(r):
        p = idx_ref[base + r]
        pltpu.make_async_copy(src_hbm.at[pl.ds(p, 1), :], dst.at[pl.ds(r, 1), :], sem).start()

    if inline:
        for k in range(n):
            one(r0 + k)
    else:
        def body(r, c):
            one(r)
            return c
        lax.fori_loop(r0, r0 + n, body, 0, unroll=DMA_UNROLL)


def _wait_row_copies(src_hbm, dst, sem):
    pltpu.make_async_copy(src_hbm.at[pl.ds(0, dst.shape[0]), :], dst, sem).wait()


def _inproj_kernel(*refs, split):
    if split:
        xp_ref, xs_ref, nw_ref, w_ref, proj_ref, h8_ref, hscr = refs
        x = jnp.where(pl.program_id(0) < T_PROMPT // TM_A, xp_ref[...], xs_ref[...])
    else:
        x_ref, nw_ref, w_ref, proj_ref, h8_ref, hscr = refs
        x = x_ref[...]
    _inproj_body(x, nw_ref, w_ref, proj_ref, h8_ref, hscr)


def _inproj_gather_kernel(pos_ref, ys_hbm, nw_ref, w_ref, proj_ref, h8_ref, x_ref, hscr, xbuf, sem):
    i = pl.program_id(0)
    n = pl.num_programs(0)
    slot = i % 2

    @pl.when(i == 0)
    def _():
        _start_row_copies(pos_ref, 0, ys_hbm, xbuf.at[0], sem.at[0], 0, TM_A, inline=False)

    _wait_row_copies(ys_hbm, xbuf.at[slot], sem.at[slot])
    x = xbuf[slot]
    x_ref[...] = x
    nxt = jnp.minimum(i + 1, n - 1)
    _start_row_copies(pos_ref, nxt * TM_A, ys_hbm, xbuf.at[1 - slot], sem.at[1 - slot], 0, TM_A, inline=True)
    _inproj_body(x, nw_ref, w_ref, proj_ref, h8_ref, hscr)

    @pl.when(i == n - 1)
    def _():
        _wait_row_copies(ys_hbm, xbuf.at[1 - slot], sem.at[1 - slot])


def _inproj_body(x, nw_ref, w_ref, proj_ref, h8_ref, hscr):
    h = _rms(x, nw_ref[...])
    for k in range(D_MODEL // 128):
        hscr[k] = h[:, k * 128:(k + 1) * 128]
        h8_ref[:, pl.ds(k * 128, 128)] = hscr[k, pl.ds(SEQ_PAD - 1, TM_A // SEQ_PAD, stride=SEQ_PAD), :]
    hb = h.astype(bf16)
    for n in range(NP_IN // N_SLAB):
        sl = pl.ds(n * N_SLAB, N_SLAB)
        proj_ref[:, sl] = jnp.dot(hb, w_ref[0, :, sl], preferred_element_type=f32)


def _inproj_gather(pos, ys, nw, w_pad, l):
    return pl.pallas_call(
        _inproj_gather_kernel,
        grid_spec=pltpu.PrefetchScalarGridSpec(
            num_scalar_prefetch=1,
            grid=(T_ALL // TM_A,),
            in_specs=[pl.BlockSpec(memory_space=pl.ANY),
                      pl.BlockSpec((1, D_MODEL), lambda i, pos: (0, 0)),
                      pl.BlockSpec((1, D_MODEL, NP_IN), lambda i, pos: (l, 0, 0))],
            out_specs=[pl.BlockSpec((TM_A, NP_IN), lambda i, pos: (i, 0)),
                       pl.BlockSpec((TM_A // SEQ_PAD, D_MODEL), lambda i, pos: (i, 0)),
                       pl.BlockSpec((TM_A, D_MODEL), lambda i, pos: (i, 0))],
            scratch_shapes=[pltpu.VMEM((D_MODEL // 128, TM_A, 128), f32),
                            pltpu.VMEM((2, TM_A, D_MODEL), f32),
                            pltpu.SemaphoreType.DMA((2,))]),
        out_shape=[jax.ShapeDtypeStruct((T_ALL, NP_IN), f32),
                   jax.ShapeDtypeStruct((T_ALL // SEQ_PAD, D_MODEL), f32),
                   jax.ShapeDtypeStruct((T_ALL, D_MODEL), f32)],
        compiler_params=pltpu.CompilerParams(dimension_semantics=("arbitrary",),
                                             vmem_limit_bytes=VMEM_LIMIT),
        name="inproj_gather",
    )(pos, ys, nw, w_pad)


def _inproj(xs, nw, w_pad, l):
    split = len(xs) == 2
    npt = T_PROMPT // TM_A
    if split:
        x_specs = [pl.BlockSpec((TM_A, D_MODEL), lambda i: (jnp.minimum(i, npt - 1), 0)),
                   pl.BlockSpec((TM_A, D_MODEL), lambda i: (jnp.maximum(i - npt, 0), 0))]
    else:
        x_specs = [pl.BlockSpec((TM_A, D_MODEL), lambda i: (i, 0))]
    return pl.pallas_call(
        functools.partial(_inproj_kernel, split=split),
        grid=(T_ALL // TM_A,),
        in_specs=x_specs + [pl.BlockSpec((1, D_MODEL), lambda i: (0, 0)),
                            pl.BlockSpec((1, D_MODEL, NP_IN), lambda i: (l, 0, 0))],
        out_specs=[pl.BlockSpec((TM_A, NP_IN), lambda i: (i, 0)),
                   pl.BlockSpec((TM_A // SEQ_PAD, D_MODEL), lambda i: (i, 0))],
        out_shape=[jax.ShapeDtypeStruct((T_ALL, NP_IN), f32),
                   jax.ShapeDtypeStruct((T_ALL // SEQ_PAD, D_MODEL), f32)],
        scratch_shapes=[pltpu.VMEM((D_MODEL // 128, TM_A, 128), f32)],
        compiler_params=pltpu.CompilerParams(dimension_semantics=("arbitrary",),
                                             vmem_limit_bytes=VMEM_LIMIT),
        name="inproj",
    )(*xs, nw, w_pad)


def _mm_kernel(a_ref, b_ref, o_ref):
    o_ref[...] = jnp.dot(a_ref[...].astype(bf16), b_ref[...], preferred_element_type=f32)


def _mm(a, b):
    return pl.pallas_call(
        _mm_kernel,
        out_shape=jax.ShapeDtypeStruct((a.shape[0], b.shape[1]), f32),
        compiler_params=pltpu.CompilerParams(vmem_limit_bytes=VMEM_LIMIT),
        name="shift_proj",
    )(a, b)


def _cat(parts, axis):
    return parts[0] if len(parts) == 1 else jnp.concatenate(parts, axis=axis)


def _mixer_kernel(*refs, TB, C, G, per_seq, n_alias=0):
    it = iter(refs)
    proj_ref = next(it)
    if per_seq:
        gdn_in, conv_in, rwkv_in, pcf_in = next(it), next(it), next(it), next(it)
    convw_ref, v128_ref, v256_ref, sguw_ref, sgub_ref, mu_ref, wup_ref, aup_ref, gup_ref = (
        next(it) for _ in range(9))
    for _ in range(n_alias):
        next(it)
    mix_ref, gdn_out, conv_out, rwkv_out = next(it), next(it), next(it), next(it)
    cv_out = next(it) if per_seq else None
    xp, pp = next(it), next(it)

    nchunk = TB // C
    ngrp = C // G
    iters = int(math.log2(G)) - 1

    if per_seq:
        xp[pl.ds(0, 8), :] = jnp.zeros((8, 3 * D_A), f32)
        pp[pl.ds(0, 8), :] = jnp.zeros((8, NC_PAD), f32)
    else:
        @pl.when(pl.program_id(1) == 0)
        def _():
            xp[pl.ds(0, 8), :] = jnp.zeros((8, 3 * D_A), f32)
            pp[pl.ds(0, 8), :] = jnp.zeros((8, NC_PAD), f32)
            gdn_out[...] = jnp.zeros(gdn_out.shape, f32)
            rwkv_out[...] = jnp.zeros(rwkv_out.shape, f32)
    xp[pl.ds(8, TB), :] = proj_ref[:, pl.ds(OFF_Q, 3 * D_A)]
    pp[pl.ds(8, TB), :] = proj_ref[:, pl.ds(OFF_C, NC_PAD)]
    if per_seq:
        for s in range(TB // SEQ_PAD):
            r0 = 8 + s * SEQ_PAD
            xp[pl.ds(r0 + SEQ_LEAD - (CONV_W - 1), CONV_W - 1), :] = conv_in[s]
            pp[pl.ds(r0 + SEQ_LEAD - 1, 1), :] = pcf_in[s]

    rowi = lax.broadcasted_iota(i32, (TB, 1), 0)
    live = (rowi % SEQ_PAD) >= SEQ_LEAD if per_seq else None

    ii = lax.broadcasted_iota(i32, (C, C), 0)
    jj = lax.broadcasted_iota(i32, (C, C), 1)
    same = (ii // G) == (jj // G)
    causal = (ii >= jj) & same
    strict = (ii > jj) & same
    eye = ii == jj
    m_cum = causal.astype(bf16)
    m_grp = same.astype(bf16)

    def conv_cols(r0, c0):
        cs = pl.ds(c0, 128)
        acc = xp[pl.ds(r0 + 8, C), cs] * convw_ref[pl.ds(3, 1), cs]
        for j in range(CONV_W - 1):
            acc = acc + xp[pl.ds(r0 + 5 + j, C), cs] * convw_ref[pl.ds(j, 1), cs]
        return _silu(acc)

    alog_row = v128_ref[pl.ds(0, 1), :]
    dtb_row = v128_ref[pl.ds(1, 1), :]
    gnorm_w = v128_ref[pl.ds(2, 1), :]
    ln_w, ln_b = v256_ref[pl.ds(0, 1), :], v256_ref[pl.ds(1, 1), :]
    w0, a0 = v256_ref[pl.ds(2, 1), :], v256_ref[pl.ds(3, 1), :]
    k_k, k_a, r_k = v256_ref[pl.ds(4, 1), :], v256_ref[pl.ds(5, 1), :], v256_ref[pl.ds(6, 1), :]
    rln_w, rln_b = v256_ref[pl.ds(7, 1), :], v256_ref[pl.ds(8, 1), :]

    l64i = lax.broadcasted_iota(i32, (D_C, D_C), 0) // N_C
    l64j = lax.broadcasted_iota(i32, (D_C, D_C), 1) // N_C
    seg64 = (l64i == l64j).astype(bf16)

    gdn, rwk, lmats = [None] * nchunk, [None] * nchunk, [None] * (8 * nchunk)

    def phase1(c):
        R0 = c * C
        rows = pl.ds(R0, C)
        live_c = live[R0:R0 + C] if per_seq else None

        ba = proj_ref[rows, pl.ds(OFF_BA, 128)]
        beta_all = _sigmoid(ba)
        g_all = -jnp.exp(alog_row) * _softplus(ba + dtb_row)
        if per_seq:
            beta_all = jnp.where(live_c, beta_all, 0.0)
            g_all = jnp.where(live_c, g_all, 0.0)
        pcur = pp[pl.ds(R0 + 8, C), :]
        pprev = pp[pl.ds(R0 + 7, C), :]
        pm = pcur + mu_ref[...] * (pprev - pcur)
        r_ = pm[:, 0:D_C]
        kc = pm[:, D_C:2 * D_C]
        vc = pm[:, 2 * D_C:3 * D_C]
        wd = pm[:, 3 * D_C:3 * D_C + 128]
        ad = pm[:, 3 * D_C + 128:3 * D_C + 256]
        gd = pm[:, 3 * D_C + 256:3 * D_C + 384]
        wl_mm = _bdot(jnp.tanh(wd), wup_ref[...])
        a_mm = _bdot(ad, aup_ref[...])
        gate = _bdot(_sigmoid(gd), gup_ref[...])
        kk = kc * k_k
        kk_ss = _dot01_right(kk * kk, seg64)
        yield

        gc_all = _dot01_left(m_cum, g_all)
        total = lambda cum, x: _dot01_left(m_grp, x)
        gl_all = total(gc_all, g_all)
        w_log = -_softplus(-(w0 + wl_mm)) - 0.5
        logw = -jnp.exp(w_log)
        a_ = _sigmoid(a0 + a_mm)
        kk = kk * lax.rsqrt(kk_ss + 1e-6)
        kc2 = kc * (1.0 + (a_ - 1.0) * k_a)
        if per_seq:
            logw = jnp.where(live_c, logw, 0.0)
            kk = jnp.where(live_c, kk, 0.0)
            kc2 = jnp.where(live_c, kc2, 0.0)
        b_ = kk * a_
        Gc = _dot01_left(m_cum, logw)
        Gl = total(Gc, logw)
        bonus_ss = _dot01_right(r_ * kc2 * r_k, seg64)
        yield

        heads = []
        for h in range(H_A):
            q = conv_cols(R0, OFF_Q + h * HD_A)
            k = conv_cols(R0, OFF_K + h * HD_A)
            v = conv_cols(R0, OFF_V + h * HD_A)
            q = q * lax.rsqrt(jnp.sum(q * q, axis=-1, keepdims=True) + 1e-6) * (HD_A ** -0.5)
            k = k * lax.rsqrt(jnp.sum(k * k, axis=-1, keepdims=True) + 1e-6)
            if per_seq:
                k = jnp.where(live_c, k, 0.0)
            beta = beta_all[:, h:h + 1]
            gcol = gc_all[:, 4 + h:5 + h]
            glr = gl_all[:, 4 + h:5 + h]
            grow = jnp.sum(jnp.where(eye, jnp.broadcast_to(gcol, (C, C)), 0.0), axis=0, keepdims=True)
            decay = jnp.where(causal, jnp.exp(jnp.where(causal, gcol - grow, 0.0)), 0.0)
            eg = jnp.exp(gcol)
            kb = k * beta
            kq = _bdot(jnp.concatenate([kb, q], axis=0), k, NT)
            heads.append(dict(kq=kq, decay=decay,
                              rhs=jnp.concatenate([v * beta, kb * eg], axis=1),
                              q_dec=q * eg, k_dec=k * jnp.exp(glr - gcol), glr=glr))
            yield
        gdn[c] = heads
        e_neg = jnp.exp(-Gc)
        e_rem = jnp.exp(Gl - Gc)
        rG = r_ * jnp.exp(Gc)
        kkG = kk * jnp.exp(Gc - logw)
        kN = kc2 * e_neg
        bN = b_ * e_neg
        aalls = []
        for h in range(H_C):
            hs = slice(h * N_C, (h + 1) * N_C)
            aalls.append(_bdot(jnp.concatenate([kkG[:, hs], rG[:, hs]], axis=0),
                               jnp.concatenate([bN[:, hs], kN[:, hs]], axis=0), NT))
        yield

        for h in range(H_A):
            d = gdn[c][h]
            kq = d.pop('kq')
            decay = d.pop('decay')
            lmats[8 * c + h] = jnp.where(strict, kq[:C] * decay, 0.0)
            d['attn'] = kq[C:] * decay
        yield
        heads = []
        for h in range(H_C):
            hs = slice(h * N_C, (h + 1) * N_C)
            aall = aalls[h]
            lmats[8 * c + H_A + h] = jnp.where(strict, aall[:C, :C], 0.0)
            akk_k = jnp.where(strict, aall[:C, C:], 0.0)
            ar = jnp.concatenate([jnp.where(causal, aall[C:, C:], 0.0),
                                  -jnp.where(causal, aall[C:, :C], 0.0)], axis=1)
            heads.append(dict(x1=_bdot(akk_k, vc[:, hs]), ar=ar))
        rwk[c] = dict(heads=heads, vc=vc, rG=rG, kkG=kkG, kdec=kc2 * e_rem, bdec=b_ * e_rem,
                      e_last=jnp.exp(Gl), gate=gate, bonus=bonus_ss * vc)
        yield

    def par(*gens):
        gens = list(gens)
        while gens:
            alive = []
            for g in gens:
                try:
                    next(g)
                    alive.append(g)
                except StopIteration:
                    pass
            gens = alive
            yield

    def seq(*gens):
        for g in gens:
            yield from g

    def lockstep(gens):
        for _ in par(*gens):
            pass

    nmats = [None] * (8 * nchunk)

    def inverses(chunks):
        idx = [8 * c + k for c in chunks for k in range(8)]
        bmm = lambda a, b: lax.dot_general(a.astype(bf16), b.astype(bf16), BNN, preferred_element_type=f32)
        pack = 2 * C <= 128
        if pack:
            N = -jnp.stack([jnp.concatenate([lmats[i], lmats[j]], axis=1) for i, j in zip(idx[0::2], idx[1::2])])
            left = lax.broadcasted_iota(i32, N.shape, 2) < C
            rhs = lambda X: jnp.concatenate([jnp.where(left, X, 0.0), jnp.where(left, 0.0, X)], axis=1)
        else:
            N = -jnp.stack([lmats[i] for i in idx])
            rhs = lambda X: X
        Q = bmm(N, rhs(N))
        yield
        for _ in range(iters - 1):
            R = bmm(jnp.concatenate([N, Q], axis=1), rhs(Q))
            N = N + Q + R[:, :C]
            Q = R[:, C:]
            yield
        N = N + Q + bmm(N, rhs(Q))
        if pack:
            for p, (i, j) in enumerate(zip(idx[0::2], idx[1::2])):
                nmats[i], nmats[j] = N[p][:, :C], N[p][:, C:]
        else:
            for p, i in enumerate(idx):
                nmats[i] = N[p]
        yield

    def phase3(c):
        for h in range(H_A):
            d = gdn[c][h]
            sol = d['rhs'] + _bdot(nmats[8 * c + h], d['rhs'])
            d['u'], d['w'] = sol[:, :HD_A], sol[:, HD_A:]
        for h in range(H_C):
            hs = slice(h * N_C, (h + 1) * N_C)
            d = rwk[c]['heads'][h]
            both = jnp.concatenate([d['x1'], rwk[c]['kkG'][:, hs]], axis=1)
            both = both + _bdot(nmats[8 * c + H_A + h], both)
            d['u_p'], d['w_p'] = both[:, :N_C], both[:, N_C:]
            d['both'] = both
        yield
        if per_seq:
            return
        for h in range(H_A):
            d = gdn[c][h]
            wu = jnp.concatenate([d['w'], d['u']], axis=1)
            a_wu = _bdot(d['attn'], wu)
            k_wu = _bdot(d['k_dec'], wu, TN)
            d['qe'] = d['q_dec'] - a_wu[:, :HD_A]
            d['o0'] = a_wu[:, HD_A:]
            d['m'] = -k_wu[:, :HD_A]
            d['b'] = k_wu[:, HD_A:]
        yield
        for h in range(H_C):
            hs = slice(h * N_C, (h + 1) * N_C)
            rc = rwk[c]
            d = rc['heads'][h]
            V = rc['vc'][:, hs]
            x = _bdot(d['ar'][:, C:], d['both'])
            d['y0'] = _bdot(d['ar'][:, :C], V) + x[:, :N_C]
            d['re'] = rc['rG'][:, hs] + x[:, N_C:]
            d['m'] = _bdot(d['w_p'], rc['bdec'][:, hs], TN)
            d['b'] = _bdot(jnp.concatenate([V, -d['u_p']], axis=0),
                           jnp.concatenate([rc['kdec'][:, hs], rc['bdec'][:, hs]], axis=0), TN)
        yield

    def gdn_head_seq(c, h):
        rows = pl.ds(c * C, C)
        d = gdn[c][h]
        S = gdn_out[0, h]
        Sb = S.astype(bf16)
        gdn_out[0, h] = S * jnp.exp(d['glr'][C - 1:C, :]) + _bdot(d['m'], Sb) + d['b']
        o = d['o0'] + _bdot(d['qe'], Sb)
        yield
        o = o * lax.rsqrt(jnp.mean(o * o, axis=-1, keepdims=True) + NORM_EPS) * gnorm_w
        z = proj_ref[rows, pl.ds(OFF_Z + h * HD_A, HD_A)]
        mix_ref[rows, pl.ds(h * HD_A, HD_A)] = o * _silu(z)

    def rwkv_head_seq(c, h, ys):
        rc = rwk[c]
        hs = slice(h * N_C, (h + 1) * N_C)
        d = rc['heads'][h]
        S = rwkv_out[0, h]
        Sb = S.astype(bf16)
        rwkv_out[0, h] = S * rc['e_last'][C - 1:C, hs] + d['b'] - _bdot(Sb, d['m'])
        ys[h] = d['y0'] + _bdot(d['re'], Sb, NT)
        yield

    def gdn_head(c, h):
        rows = pl.ds(c * C, C)
        d = gdn[c][h]
        rq, vn, s_old = [], [], []
        for s in range(ngrp):
            gs = slice(s * G, (s + 1) * G)
            S = gdn_in[s, h] if per_seq else gdn_out[0, h]
            s_old.append(S)
            R = _bdot(jnp.concatenate([d['w'][gs], d['q_dec'][gs]], axis=0), S)
            vn.append(d['u'][gs] - R[:G])
            rq.append(R[G:])
        yield
        v_new = _cat(vn, 0)
        o = _cat(rq, 0) + _bdot(d['attn'], v_new)
        for s in range(ngrp):
            gs = slice(s * G, (s + 1) * G)
            g_last = jnp.exp(d['glr'][s * G + G - 1:s * G + G, :])
            S_new = s_old[s] * g_last + _bdot(d['k_dec'][gs], v_new[gs], TN)
            if per_seq:
                gdn_out[s, h] = S_new
            else:
                gdn_out[0, h] = S_new
        yield
        o = o * lax.rsqrt(jnp.mean(o * o, axis=-1, keepdims=True) + NORM_EPS) * gnorm_w
        z = proj_ref[rows, pl.ds(OFF_Z + h * HD_A, HD_A)]
        mix_ref[rows, pl.ds(h * HD_A, HD_A)] = o * _silu(z)

    def rwkv_head(c, h, ys):
        rc = rwk[c]
        hs = slice(h * N_C, (h + 1) * N_C)
        d = rc['heads'][h]
        V = rc['vc'][:, hs]
        rr, ut, s_old = [], [], []
        for s in range(ngrp):
            gs = slice(s * G, (s + 1) * G)
            S = rwkv_in[s, h] if per_seq else rwkv_out[0, h]
            s_old.append(S)
            R = _bdot(jnp.concatenate([d['w_p'][gs], rc['rG'][gs, hs]], axis=0), S, NT)
            ut.append(d['u_p'][gs] + R[:G])
            rr.append(R[G:])
        yield
        Ut = _cat(ut, 0)
        ys[h] = _cat(rr, 0) + _bdot(d['ar'], jnp.concatenate([V, Ut], axis=0))
        for s in range(ngrp):
            gs = slice(s * G, (s + 1) * G)
            upd = _bdot(jnp.concatenate([V[gs], -Ut[gs]], axis=0),
                        jnp.concatenate([rc['kdec'][gs, hs], rc['bdec'][gs, hs]], axis=0), TN)
            S_new = s_old[s] * rc['e_last'][s * G + G - 1:s * G + G, hs] + upd
            if per_seq:
                rwkv_out[s, h] = S_new
            else:
                rwkv_out[0, h] = S_new
        yield

    def phase4(c):
        rows = pl.ds(c * C, C)
        rc = rwk[c]
        ys = [None] * H_C
        if per_seq:
            heads = [gdn_head(c, h) for h in range(H_A)] + [rwkv_head(c, h, ys) for h in range(H_C)]
        else:
            heads = [gdn_head_seq(c, h) for h in range(H_A)] + [rwkv_head_seq(c, h, ys) for h in range(H_C)]
        yield from par(*heads)
        y = jnp.concatenate(ys, axis=1)
        mu_y = _dot01_right(y, seg64) * (1.0 / N_C)
        dy = y - mu_y
        var_y = _dot01_right(dy * dy, seg64) * (1.0 / N_C)
        y = dy * lax.rsqrt(var_y + GN_EPS) * rln_w + rln_b
        mix_ref[rows, pl.ds(D_A + D_B, D_C)] = (y + rc['bonus']) * rc['gate']
        yield

    def sgu(sc):
        rows = pl.ds(sc * SGU_CHUNK, SGU_CHUNK)
        ug = _gelu(proj_ref[rows, pl.ds(OFF_U, D_B)])
        vs = _gelu(proj_ref[rows, pl.ds(OFF_VB, D_B)])
        yield
        mu_v = _dot01_right(vs, seg64) * (1.0 / DH_B)
        dv = vs - mu_v
        var_v = _dot01_right(dv * dv, seg64) * (1.0 / DH_B)
        vs = dv * lax.rsqrt(var_v + LN_EPS) * ln_w + ln_b
        if per_seq:
            cv_out[rows, :] = vs
        yield
        vsb = vs.astype(bf16)
        outs = [lax.dot_general(sguw_ref[h], vsb[:, h * DH_B:(h + 1) * DH_B], NN, preferred_element_type=f32)
                for h in range(4)]
        mixed = jnp.concatenate(outs, axis=1) + sgub_ref[...]
        mix_ref[rows, pl.ds(D_A, D_B)] = ug * mixed
        yield

    first = list(range(nchunk))[:max(nchunk // 2, 1)]
    second = list(range(nchunk))[len(first):]
    sgus = [sgu(sc) for sc in range(TB // SGU_CHUNK)]
    lockstep([phase1(c) for c in first])
    lockstep([phase1(c) for c in second]
             + [seq(inverses(first), par(*[phase3(c) for c in first]))])
    if second:
        lockstep([seq(inverses(second), par(*[phase3(c) for c in second])),
                  seq(*[phase4(c) for c in first]), seq(*sgus)])
        lockstep([seq(*[phase4(c) for c in second])])
    else:
        lockstep([seq(*[phase4(c) for c in first]), seq(*sgus)])

    if per_seq:
        for s in range(TB // SEQ_PAD):
            conv_out[s] = xp[pl.ds(8 + (s + 1) * SEQ_PAD - (CONV_W - 1), CONV_W - 1), :]
    else:
        conv_out[0] = xp[pl.ds(8 + TB - (CONV_W - 1), CONV_W - 1), :]
        xp[pl.ds(0, 8), :] = xp[pl.ds(TB, 8), :]
        pp[pl.ds(0, 8), :] = pp[pl.ds(TB, 8), :]


def _mixer_weight_specs(nidx):
    z2 = (lambda b, j: (0, 0)) if nidx == 2 else (lambda i: (0, 0))
    z3 = (lambda b, j: (0, 0, 0)) if nidx == 2 else (lambda i: (0, 0, 0))
    return [pl.BlockSpec((CONV_W, 3 * D_A), z2),
            pl.BlockSpec((8, 128), z2),
            pl.BlockSpec((16, D_C), z2),
            pl.BlockSpec((4, SGU_CHUNK, SGU_CHUNK), z3),
            pl.BlockSpec((SGU_CHUNK, D_B), z2),
            pl.BlockSpec((1, NC_PAD), z2),
            pl.BlockSpec((128, D_C), z2),
            pl.BlockSpec((128, D_C), z2),
            pl.BlockSpec((128, D_C), z2)]


def _mixer_prompt(proj, mw):
    nj = SEQ // TB_P
    return pl.pallas_call(
        functools.partial(_mixer_kernel, TB=TB_P, C=GDN_CHUNK, G=GDN_CHUNK, per_seq=False),
        grid=(BATCH, nj),
        in_specs=[pl.BlockSpec((TB_P, NP_IN), lambda b, j: (b * nj + j, 0))] + _mixer_weight_specs(2),
        out_specs=[pl.BlockSpec((TB_P, D_MODEL), lambda b, j: (b * nj + j, 0)),
                   pl.BlockSpec((1, H_A, HD_A, HD_A), lambda b, j: (b, 0, 0, 0)),
                   pl.BlockSpec((1, CONV_W - 1, 3 * D_A), lambda b, j: (b, 0, 0)),
                   pl.BlockSpec((1, H_C, N_C, N_C), lambda b, j: (b, 0, 0, 0))],
        out_shape=[jax.ShapeDtypeStruct((T_PROMPT, D_MODEL), f32),
                   jax.ShapeDtypeStruct((BATCH, H_A, HD_A, HD_A), f32),
                   jax.ShapeDtypeStruct((BATCH, CONV_W - 1, 3 * D_A), f32),
                   jax.ShapeDtypeStruct((BATCH, H_C, N_C, N_C), f32)],
        scratch_shapes=[pltpu.VMEM((TB_P + 8, 3 * D_A), f32), pltpu.VMEM((TB_P + 8, NC_PAD), f32)],
        compiler_params=pltpu.CompilerParams(dimension_semantics=("arbitrary", "arbitrary"),
                                             vmem_limit_bytes=VMEM_LIMIT),
        name="mixer_prompt",
    )(proj, *mw)


def _mixer_sample(proj, s_gdn, s_conv, s_rwkv, pcf, mw, l, prev_states):
    nseq = TB_S // SEQ_PAD
    base = T_PROMPT // TB_S
    n_in = 5 + len(mw)
    lay4 = lambda i: (l, i, 0, 0, 0)
    lay3 = lambda i: (l, i, 0, 0)
    return pl.pallas_call(
        functools.partial(_mixer_kernel, TB=TB_S, C=TB_S, G=SEQ_PAD, per_seq=True, n_alias=len(prev_states)),
        grid=(T_SAMPLE // TB_S,),
        in_specs=[pl.BlockSpec((TB_S, NP_IN), lambda i: (base + i, 0)),
                  pl.BlockSpec((None, nseq, H_A, HD_A, HD_A), lay4),
                  pl.BlockSpec((None, nseq, CONV_W - 1, 3 * D_A), lay3),
                  pl.BlockSpec((None, nseq, H_C, N_C, N_C), lay4),
                  pl.BlockSpec((nseq, 1, NC_PAD), lambda i: (i, 0, 0))] + _mixer_weight_specs(1)
                 + [pl.BlockSpec(memory_space=pl.ANY)] * len(prev_states),
        out_specs=[pl.BlockSpec((TB_S, D_MODEL), lambda i: (i, 0)),
                   pl.BlockSpec((None, nseq, H_A, HD_A, HD_A), lay4),
                   pl.BlockSpec((nseq, CONV_W - 1, 3 * D_A), lambda i: (i, 0, 0)),
                   pl.BlockSpec((None, nseq, H_C, N_C, N_C), lay4),
                   pl.BlockSpec((TB_S, D_B), lambda i: (i, 0))],
        out_shape=[jax.ShapeDtypeStruct((T_SAMPLE, D_MODEL), f32),
                   jax.ShapeDtypeStruct((DEPTH, DEC_BATCH, H_A, HD_A, HD_A), f32),
                   jax.ShapeDtypeStruct((DEC_BATCH, CONV_W - 1, 3 * D_A), f32),
                   jax.ShapeDtypeStruct((DEPTH, DEC_BATCH, H_C, N_C, N_C), f32),
                   jax.ShapeDtypeStruct((T_SAMPLE, D_B), f32)],
        scratch_shapes=[pltpu.VMEM((TB_S + 8, 3 * D_A), f32), pltpu.VMEM((TB_S + 8, NC_PAD), f32)],
        input_output_aliases={n_in + k: o for k, o in enumerate((1, 3)[:len(prev_states)])},
        compiler_params=pltpu.CompilerParams(dimension_semantics=("arbitrary",),
                                             vmem_limit_bytes=VMEM_LIMIT),
        name="mixer_sample",
    )(proj, s_gdn, s_conv, s_rwkv, pcf, *mw, *prev_states)


TM_C = 512


def _outproj_kernel(*refs, split):
    i = pl.program_id(0)
    in_prompt = i < T_PROMPT // TM_C
    if split:
        xp_ref, xs_ref = refs[:2]
        x = jnp.where(in_prompt, xp_ref[...], xs_ref[...])
    else:
        x = refs[0][...]
    mixp_ref, mixs_ref, wout_ref, nw_ref, rwh_ref, rwl_ref, rb_ref, xg_ref, cnt_ref, run_ref = refs[-10:]

    @pl.when(i == 0)
    def _():
        run_ref[...] = jnp.zeros(run_ref.shape, f32)

    mix = jnp.where(in_prompt, mixp_ref[...], mixs_ref[...])
    x2 = x + jnp.dot(mix.astype(bf16), wout_ref[0], preferred_element_type=f32)
    xg_ref[:, pl.ds(0, D_MODEL)] = x2
    h2 = _rms(x2, nw_ref[...])
    hh, hl = _split2(h2)
    d = lambda a, b: jnp.dot(a, b, preferred_element_type=f32)
    both = d(hh, jnp.concatenate([rwh_ref[...], rwl_ref[...]], axis=1))
    logits = both[:, :128] + both[:, 128:] + d(hl, rwh_ref[...]) + rb_ref[...]

    lane = lax.broadcasted_iota(i32, logits.shape, 1).astype(f32)
    neg = jnp.float32(-jnp.inf)
    is_g = lane < float(N_GROUPS)
    gl = jnp.where(is_g, logits, neg)
    gmax = jnp.max(gl, axis=-1, keepdims=True)
    gsel = jnp.min(jnp.where(gl == gmax, lane, 128.0), axis=-1, keepdims=True)
    gw = 1.0 / jnp.sum(jnp.where(is_g, jnp.exp(jnp.where(is_g, logits - gmax, 0.0)), 0.0),
                       axis=-1, keepdims=True)
    lo = LANE_E0 + float(EPG) * gsel
    in_grp = (lane >= lo) & (lane < lo + float(EPG))
    el = jnp.where(in_grp, logits, neg)
    t1 = jnp.max(el, axis=-1, keepdims=True)
    i1 = jnp.min(jnp.where(el == t1, lane, 128.0), axis=-1, keepdims=True)
    el2 = jnp.where(lane == i1, neg, el)
    t2 = jnp.max(el2, axis=-1, keepdims=True)
    i2 = jnp.min(jnp.where(el2 == t2, lane, 128.0), axis=-1, keepdims=True)
    e2 = jnp.exp(t2 - t1)
    den = 1.0 + e2
    gates = jnp.where(lane == i1, gw / den, 0.0) + jnp.where(lane == i2, gw * e2 / den, 0.0)

    onehot = jnp.where(lane == gsel, 1.0, 0.0)
    ri = lax.broadcasted_iota(i32, (TM_C, TM_C), 0)
    ci = lax.broadcasted_iota(i32, (TM_C, TM_C), 1)
    before = lax.dot_general((ri > ci).astype(bf16), onehot.astype(bf16), NN, preferred_element_type=f32)
    rank = jnp.sum((before + run_ref[...]) * onehot, axis=-1, keepdims=True)
    run_ref[...] += jnp.sum(onehot, axis=0, keepdims=True)
    cnt_ref[...] = jnp.broadcast_to(run_ref[...], cnt_ref.shape)
    xg_ref[:, pl.ds(D_MODEL, 128)] = (gates + jnp.where(lane == float(LANE_G), gsel, 0.0)
                                      + jnp.where(lane == float(LANE_RANK), rank, 0.0))


def _outproj(xs, mix_p, mix_s, wout, nw, rwh, rwl, rb, l):
    row = lambda i: (i, 0)
    fix = lambda i: (0, 0)
    npt = T_PROMPT // TM_C
    p_rows = pl.BlockSpec((TM_C, D_MODEL), lambda i: (jnp.minimum(i, npt - 1), 0))
    s_rows = pl.BlockSpec((TM_C, D_MODEL), lambda i: (jnp.maximum(i - npt, 0), 0))
    split = len(xs) == 2
    return pl.pallas_call(
        functools.partial(_outproj_kernel, split=split),
        grid=(T_ALL // TM_C,),
        in_specs=([p_rows, s_rows] if split else [pl.BlockSpec((TM_C, D_MODEL), row)]) + [
                  p_rows, s_rows,
                  pl.BlockSpec((1, D_MODEL, D_MODEL), lambda i: (l, 0, 0)), pl.BlockSpec((1, D_MODEL), fix),
                  pl.BlockSpec((D_MODEL, 128), fix), pl.BlockSpec((D_MODEL, 128), fix),
                  pl.BlockSpec((1, 128), fix)],
        out_specs=[pl.BlockSpec((TM_C, XG_W), row), pl.BlockSpec((8, 128), fix)],
        out_shape=[jax.ShapeDtypeStruct((T_ALL, XG_W), f32), jax.ShapeDtypeStruct((8, 128), f32)],
        scratch_shapes=[pltpu.VMEM((1, 128), f32)],
        compiler_params=pltpu.CompilerParams(dimension_semantics=("arbitrary",),
                                             vmem_limit_bytes=VMEM_LIMIT),
        name="outproj_router",
    )(*xs, mix_p, mix_s, wout, nw, rwh, rwl, rb)


TM_E = 512
NT_E = T_ALL // TM_E + N_GROUPS
T_SORT = NT_E * TM_E
DMA_UNROLL = 8


def _invert_kernel(pos_ref, pad_lo_ref, pad_hi_ref, src_ref):
    for g in range(N_GROUPS):
        def fill(p, c):
            src_ref[p] = T_ALL - 1
            return c
        lax.fori_loop(pad_lo_ref[g], pad_hi_ref[g], fill, 0)

    def body(t, c):
        src_ref[pos_ref[t]] = t
        return c
    lax.fori_loop(0, T_ALL, body, 0, unroll=DMA_UNROLL)


def _invert(pos, pad_lo, pad_hi):
    smem = pl.BlockSpec(memory_space=pltpu.SMEM)
    return pl.pallas_call(
        _invert_kernel,
        in_specs=[smem, smem, smem],
        out_specs=smem,
        out_shape=jax.ShapeDtypeStruct((T_SORT,), i32),
        name="moe_invert",
    )(pos, pad_lo, pad_hi)


def _gather_rows(pos_ref, base, ys_hbm, o_ref, sem):
    def issue(r, c):
        p = pos_ref[base + r]
        pltpu.make_async_copy(ys_hbm.at[pl.ds(p, 1), :], o_ref.at[pl.ds(r, 1), :], sem).start()
        return c

    lax.fori_loop(0, TM_E, issue, 0, unroll=DMA_UNROLL)
    pltpu.make_async_copy(ys_hbm.at[pl.ds(0, TM_E), :], o_ref, sem).wait()


def _combine_kernel(pos_ref, ys_hbm, o_ref, sem):
    _gather_rows(pos_ref, pl.program_id(0) * TM_E, ys_hbm, o_ref, sem)


def _combine_split_kernel(pos_ref, ys_hbm, op_ref, os_ref, sem):
    i = pl.program_id(0)

    @pl.when(i < T_PROMPT // TM_E)
    def _():
        _gather_rows(pos_ref, i * TM_E, ys_hbm, op_ref, sem)

    @pl.when(i >= T_PROMPT // TM_E)
    def _():
        _gather_rows(pos_ref, i * TM_E, ys_hbm, os_ref, sem)


def _combine(pos, ys, split):
    npt = T_PROMPT // TM_E
    if split:
        out_specs = [pl.BlockSpec((TM_E, D_MODEL), lambda i, pos: (jnp.minimum(i, npt - 1), 0)),
                     pl.BlockSpec((TM_E, D_MODEL), lambda i, pos: (jnp.maximum(i - npt, 0), 0))]
        out_shape = [jax.ShapeDtypeStruct((T_PROMPT, D_MODEL), f32),
                     jax.ShapeDtypeStruct((T_SAMPLE, D_MODEL), f32)]
    else:
        out_specs = pl.BlockSpec((TM_E, D_MODEL), lambda i, pos: (i, 0))
        out_shape = jax.ShapeDtypeStruct((T_ALL, D_MODEL), f32)
    return pl.pallas_call(
        _combine_split_kernel if split else _combine_kernel,
        grid_spec=pltpu.PrefetchScalarGridSpec(
            num_scalar_prefetch=1,
            grid=(T_ALL // TM_E,),
            in_specs=[pl.BlockSpec(memory_space=pl.ANY)],
            out_specs=out_specs,
            scratch_shapes=[pltpu.SemaphoreType.DMA(())]),
        out_shape=out_shape,
        compiler_params=pltpu.CompilerParams(dimension_semantics=("arbitrary",),
                                             vmem_limit_bytes=VMEM_LIMIT),
        name="moe_combine",
    )(pos, ys)


ROWS_E = TM_E // EPG


def _experts_kernel(tg_ref, tv_ref, ti_ref, src_ref, xg_hbm, nw_ref, wg_ref, wu_ref, wd_ref, nf_ref, ys_ref,
                    xbuf, hbuf, ybuf, gsem, *, final_norm):
    t = pl.program_id(0)
    e = pl.program_id(1)
    valid = tv_ref[t] == 1
    slot = t % 2
    nxt = ti_ref[jnp.minimum(t + 1, NT_E - 1)]

    @pl.when((t == 0) & (e == 0))
    def _():
        _start_row_copies(src_ref, 0, xg_hbm, xbuf.at[0], gsem.at[0], 0, TM_E, inline=False)

    @pl.when(valid & (e == 0))
    def _():
        _wait_row_copies(xg_hbm, xbuf.at[slot], gsem.at[slot])
        x2 = xbuf[slot, :, pl.ds(0, D_MODEL)]
        hbuf[...] = _rms(x2, nw_ref[...]).astype(bf16)
        ybuf[...] = x2

    @pl.when(valid)
    def _():
        _start_row_copies(src_ref, nxt * TM_E, xg_hbm, xbuf.at[1 - slot], gsem.at[1 - slot],
                          e * ROWS_E, ROWS_E, inline=True)
        hb = hbuf[...]
        he = (_silu(jnp.dot(hb, wg_ref[0, 0], preferred_element_type=f32))
              * jnp.dot(hb, wu_ref[0, 0], preferred_element_type=f32))
        yd = jnp.dot(he.astype(bf16), wd_ref[0, 0], preferred_element_type=f32)
        g = xbuf[slot, :, pl.ds(D_MODEL, 128)]
        lane = lax.broadcasted_iota(i32, g.shape, 1)
        gcol = jnp.sum(jnp.where(lane == LANE_E0 + EPG * tg_ref[t] + e, g, 0.0), axis=-1, keepdims=True)
        ybuf[...] += gcol * yd

    @pl.when(e == EPG - 1)
    def _():
        y = ybuf[...]
        if final_norm:
            y = _rms(y, nf_ref[...])
        ys_ref[...] = y

    @pl.when(valid & (nxt == t) & (e == EPG - 1))
    def _():
        _wait_row_copies(xg_hbm, xbuf.at[1 - slot], gsem.at[1 - slot])


def _experts(tile_group, tile_valid, tile_idx, src, xg, nw, wg, wu, wd, nf, l, final_norm):
    wsel = lambda t, e, tg, tv, ti, src: (l, tg[t] * EPG + e, 0, 0)
    return pl.pallas_call(
        functools.partial(_experts_kernel, final_norm=final_norm),
        grid_spec=pltpu.PrefetchScalarGridSpec(
            num_scalar_prefetch=4,
            grid=(NT_E, EPG),
            in_specs=[pl.BlockSpec(memory_space=pl.ANY),
                      pl.BlockSpec((1, D_MODEL), lambda t, e, tg, tv, ti, src: (0, 0)),
                      pl.BlockSpec((1, 1, D_MODEL, D_FF_E), wsel),
                      pl.BlockSpec((1, 1, D_MODEL, D_FF_E), wsel),
                      pl.BlockSpec((1, 1, D_FF_E, D_MODEL), wsel),
                      pl.BlockSpec((1, D_MODEL), lambda t, e, tg, tv, ti, src: (0, 0))],
            out_specs=pl.BlockSpec((TM_E, D_MODEL), lambda t, e, tg, tv, ti, src: (ti[t], 0)),
            scratch_shapes=[pltpu.VMEM((2, TM_E, XG_W), f32), pltpu.VMEM((TM_E, D_MODEL), bf16),
                            pltpu.VMEM((TM_E, D_MODEL), f32), pltpu.SemaphoreType.DMA((2,))]),
        out_shape=jax.ShapeDtypeStruct((T_SORT, D_MODEL), f32),
        compiler_params=pltpu.CompilerParams(dimension_semantics=("arbitrary", "arbitrary"),
                                             vmem_limit_bytes=VMEM_LIMIT),
        name="moe_experts",
    )(tile_group, tile_valid, tile_idx, src, xg, nw, wg, wu, wd, nf)


def _route_meta(xg, cnt_rows):
    g = xg[:, D_MODEL + LANE_G].astype(i32)
    rank = xg[:, D_MODEL + LANE_RANK].astype(i32)
    cnt = cnt_rows[0, :N_GROUPS].astype(i32)
    padded = ((cnt + TM_E - 1) // TM_E) * TM_E
    off_end = jnp.cumsum(padded)
    off = off_end - padded
    gid = jnp.arange(N_GROUPS, dtype=i32)
    pos = rank + jnp.sum(jnp.where(g[:, None] == gid[None, :], off[None, :], 0), axis=1)
    tile_start = jnp.arange(NT_E, dtype=i32) * TM_E
    tile_group = jnp.minimum(jnp.sum((tile_start[:, None] >= off_end[None, :]).astype(i32), axis=1), N_GROUPS - 1)
    tile_valid = (tile_start < off_end[-1]).astype(i32)
    n_used = off_end[-1] // TM_E
    tile_idx = jnp.minimum(jnp.arange(NT_E, dtype=i32), n_used - 1)
    tile_group = jnp.take(tile_group, tile_idx)
    return pos, tile_group, tile_valid, tile_idx, off + cnt, off_end


def _pad_cols(a, n):
    return jnp.pad(a, ((0, 0), (0, n - a.shape[1])))


def _pad_rows(a, n):
    return jnp.pad(a, ((0, n - a.shape[0]), (0, 0)))


def _prep_w_in(w):
    pad_last = lambda a, n: jnp.pad(a, ((0, 0), (0, 0), (0, n - a.shape[-1])))
    c = w[..., 2568:]
    parts = [w[..., 0:2048], pad_last(w[..., 2048:2056], 128), w[..., 2056:2568],
             c[..., 0:768], pad_last(c[..., 768:832], 128), pad_last(c[..., 832:896], 128), c[..., 896:1024]]
    return jnp.concatenate(parts, axis=-1).astype(bf16)


def _prep_mu(mu):
    m = mu[None, :]
    return jnp.concatenate([m[:, 0:768], _pad_cols(m[:, 768:832], 128), _pad_cols(m[:, 832:896], 128),
                            m[:, 896:1024]], axis=1)


def _sgu_mats(sgu_w, sgu_b):
    t = jnp.arange(SGU_CHUNK)
    wm = jnp.where(t[:, None] >= t[None, :], sgu_w, 0.0)
    bias_p = jnp.repeat(jnp.transpose(sgu_b), DH_B, axis=1)
    small = jnp.zeros((4, SEQ_PAD, SEQ_PAD), f32).at[:, SEQ_LEAD:, SEQ_LEAD:].set(wm[:, :DEC_SEQ, :DEC_SEQ])
    eye16 = jnp.eye(TB_S // SEQ_PAD, dtype=f32)
    wm_s = jnp.einsum('ab,hij->haibj', eye16, small).reshape(4, TB_S, TB_S)
    bias_small = jnp.zeros((SEQ_PAD, D_B), f32).at[SEQ_LEAD:].set(bias_p[:DEC_SEQ])
    bias_s = jnp.tile(bias_small, (TB_S // SEQ_PAD, 1))
    return wm.astype(bf16), bias_p, wm_s.astype(bf16), bias_s


def _row(a, n):
    return _pad_cols(a.reshape(1, -1), n)


def kernel(x_prompt, x_sample, state_gdn, state_gdn_conv, state_rwkv, state_rwkv_shift, norm_mix, norm_ffn, norm_final, w_in, gdn_conv_w, gdn_a_log, gdn_dt_bias, gdn_norm_w, sgu_ln_w, sgu_ln_b, sgu_w, sgu_b, rwkv_mu, rwkv_w0, rwkv_w_up, rwkv_a0, rwkv_a_up, rwkv_g_up, rwkv_k_k, rwkv_k_a, rwkv_r_k, rwkv_ln_w, rwkv_ln_b, w_out, router_group_w, router_group_b, router_expert_w, router_expert_b, expert_w_gate, expert_w_up, expert_w_down):
    x = (x_prompt.reshape(T_PROMPT, D_MODEL),
         jnp.pad(x_sample, ((0, 0), (SEQ_LEAD, 0), (0, 0))).reshape(T_SAMPLE, D_MODEL))
    sample_states = ()

    w_pad = _prep_w_in(w_in)
    w_out_b = w_out.astype(bf16)
    wg_b, wu_b, wd_b = expert_w_gate.astype(bf16), expert_w_up.astype(bf16), expert_w_down.astype(bf16)

    outs = {k: [] for k in ('gdn_p', 'conv_p', 'rwkv_p', 'shift_p', 'gdn_s', 'conv_s', 'rwkv_s', 'shift_s', 'cv_s')}
    for l in range(DEPTH):
        last = l == DEPTH - 1
        v128 = jnp.concatenate([
            jnp.pad(gdn_a_log[l].reshape(1, H_A), ((0, 0), (4, 120))),
            jnp.pad(gdn_dt_bias[l].reshape(1, H_A), ((0, 0), (4, 120))),
            gdn_norm_w[l].reshape(1, HD_A), jnp.zeros((5, 128), f32)], axis=0)
        v256 = jnp.concatenate([a.reshape(1, D_C) for a in (
            sgu_ln_w[l], sgu_ln_b[l], rwkv_w0[l], rwkv_a0[l], rwkv_k_k[l], rwkv_k_a[l], rwkv_r_k[l],
            rwkv_ln_w[l], rwkv_ln_b[l])] + [jnp.zeros((7, D_C), f32)], axis=0)
        wm_p, bias_p, wm_s, bias_s = _sgu_mats(sgu_w[l], sgu_b[l])
        common = (gdn_conv_w[l], v128, v256)
        tail = (_prep_mu(rwkv_mu[l]), _pad_rows(rwkv_w_up[l], 128).astype(bf16),
                _pad_rows(rwkv_a_up[l], 128).astype(bf16), rwkv_g_up[l].astype(bf16))
        mw_p = common + (wm_p, bias_p) + tail
        mw_s = common + (wm_s, bias_s) + tail

        if l == 0:
            proj, h = _inproj(x, norm_mix[l].reshape(1, D_MODEL), w_pad, l)
        else:
            proj, h, x_tok = _inproj_gather(pos, ys, norm_mix[l].reshape(1, D_MODEL), w_pad, l)
            x = (x_tok,)
        pcf = _mm(state_rwkv_shift[l], w_pad[l, :, OFF_C:]).reshape(DEC_BATCH, 1, NC_PAD)
        mix_p, gdn_p, conv_p, rwkv_p = _mixer_prompt(proj, mw_p)
        mix_s, gdn_s, conv_s, rwkv_s, cv_s = _mixer_sample(
            proj, state_gdn, state_gdn_conv, state_rwkv, pcf, mw_s, l, sample_states)
        sample_states = (gdn_s, rwkv_s)

        rw = _pad_cols(jnp.concatenate([router_group_w[l], router_expert_w[l]], axis=1), 128)
        rwh = rw.astype(bf16)
        rwl = (rw - rwh.astype(f32)).astype(bf16)
        rb = _row(jnp.concatenate([router_group_b[l], router_expert_b[l]]), 128)
        nw_ffn = norm_ffn[l].reshape(1, D_MODEL)
        xg, cnt_rows = _outproj(x, mix_p, mix_s, w_out_b, nw_ffn, rwh, rwl, rb, l)
        pos, tile_group, tile_valid, tile_idx, pad_lo, pad_hi = _route_meta(xg, cnt_rows)
        src = _invert(pos, pad_lo, pad_hi)
        ys = _experts(tile_group, tile_valid, tile_idx, src, xg, nw_ffn, wg_b, wu_b, wd_b,
                      norm_final.reshape(1, D_MODEL), l, final_norm=last)
        if last:
            x = _combine(pos, ys, split=True)

        outs['gdn_p'].append(gdn_p)
        outs['conv_p'].append(conv_p)
        outs['rwkv_p'].append(rwkv_p)
        outs['shift_p'].append(h[SEQ // SEQ_PAD - 1:T_PROMPT // SEQ_PAD:SEQ // SEQ_PAD])
        outs['conv_s'].append(conv_s)
        outs['shift_s'].append(h[T_PROMPT // SEQ_PAD:])
        outs['cv_s'].append(cv_s.reshape(DEC_BATCH, SEQ_PAD, D_B)[:, SEQ_LEAD:])

    y_prompt = x[0].reshape(BATCH, SEQ, D_MODEL)
    y_sample = x[1].reshape(DEC_BATCH, SEQ_PAD, D_MODEL)[:, SEQ_LEAD:]
    st = lambda k: jnp.stack(outs[k])
    return (y_prompt, y_sample, st('gdn_p'), st('conv_p'), st('rwkv_p'), st('shift_p'),
            sample_states[0], st('conv_s'), sample_states[1], st('shift_s'), st('cv_s'))
```

```python
import functools
import math

import jax
import jax.numpy as jnp
from jax import lax
from jax.experimental import pallas as pl
from jax.experimental.pallas import tpu as pltpu

f32 = jnp.float32
bf16 = jnp.bfloat16
i32 = jnp.int32

D_MODEL = 1024
BATCH = 8
SEQ = 2048
DEPTH = 2
DEC_BATCH = 128
DEC_SEQ = 4
H_A = 4
HD_A = 128
D_A = 512
CONV_W = 4
D_B = 256
DH_B = 64
SGU_CHUNK = 128
H_C = 4
N_C = 64
D_C = 256
N_GROUPS = 4
EPG = 4
D_FF_E = 512
NORM_EPS = 1e-6
LN_EPS = 1e-5
GN_EPS = 64e-5

SEQ_PAD = 8
SEQ_LEAD = SEQ_PAD - DEC_SEQ
T_PROMPT = BATCH * SEQ
T_SAMPLE = DEC_BATCH * SEQ_PAD
T_ALL = T_PROMPT + T_SAMPLE
TB_P = 256
TB_S = 128
GDN_CHUNK = 64

OFF_Q, OFF_K, OFF_V, OFF_Z, OFF_BA, OFF_U, OFF_VB, OFF_C = 0, 512, 1024, 1536, 2048, 2176, 2432, 2688
NP_IN = 3840
NC_PAD = NP_IN - OFF_C

N_PAIRS = 6
N_BUCKETS = N_GROUPS * N_PAIRS
PAIRS = ((0, 1), (0, 2), (0, 3), (1, 2), (1, 3), (2, 3))
XG_W = D_MODEL + 128
LANE_G, LANE_RANK, LANE_E0 = 0, 1, 4

VMEM_LIMIT = 48 * 1024 * 1024


NN = (((1,), (0,)), ((), ()))
NT = (((1,), (1,)), ((), ()))
TN = (((0,), (0,)), ((), ()))
BNN = (((2,), (1,)), ((0,), (0,)))


def _bdot(a, b, dims=NN):
    return lax.dot_general(a.astype(bf16), b.astype(bf16), dims, preferred_element_type=f32)


def _split2(x):
    hi = x.astype(bf16)
    lo = (x - hi.astype(f32)).astype(bf16)
    return hi, lo


def _split3(x):
    hi = x.astype(bf16)
    r = x - hi.astype(f32)
    mid = r.astype(bf16)
    lo = (r - mid.astype(f32)).astype(bf16)
    return hi, mid, lo


def _dot01_left(m01, x):
    hi, mid, lo = _split3(x)
    d = lambda p: lax.dot_general(m01, p, NN, preferred_element_type=f32)
    return d(hi) + d(mid) + d(lo)


def _dot01_right(x, m01):
    hi, lo = _split2(x)
    d = lambda p: lax.dot_general(p, m01, NN, preferred_element_type=f32)
    return d(hi) + d(lo)


def _softplus(x):
    return jnp.maximum(x, 0.0) + jnp.log1p(jnp.exp(-jnp.abs(x)))


def _sigmoid(x):
    return 0.5 * jnp.tanh(0.5 * x) + 0.5


def _silu(x):
    return x * _sigmoid(x)


def _gelu(x):
    return 0.5 * x * (1.0 + lax.erf(x * (1.0 / math.sqrt(2.0))))


def _rms(x, w):
    return x * lax.rsqrt(jnp.mean(x * x, axis=-1, keepdims=True) + NORM_EPS) * w


TM_A = 512
N_SLAB = 768


def _start_row_copies(idx_ref, base, src_hbm, dst, sem, r0, n, inline):
    def one(r):
        p = idx_ref[base + r]
        pltpu.make_async_copy(src_hbm.at[pl.ds(p, 1), :], dst.at[pl.ds(r, 1), :], sem).start()

    if inline:
        for k in range(n):
            one(r0 + k)
    else:
        def body(k, c):
            one(r0 + k)
            return c
        lax.fori_loop(0, n, body, 0, unroll=DMA_UNROLL)


def _wait_row_copies(src_hbm, dst, sem):
    pltpu.make_async_copy(src_hbm.at[pl.ds(0, dst.shape[0]), :], dst, sem).wait()


def _inproj_kernel(*refs, split):
    if split:
        xp_ref, xs_ref, nw_ref, w_ref, proj_ref, h8_ref, hscr = refs
        x = jnp.where(pl.program_id(0) < T_PROMPT // TM_A, xp_ref[...], xs_ref[...])
    else:
        x_ref, nw_ref, w_ref, proj_ref, h8_ref, hscr = refs
        x = x_ref[...]
    _inproj_body(x, nw_ref, w_ref, proj_ref, h8_ref, hscr)


def _inproj_gather_kernel(pos_ref, ys_hbm, nw_ref, w_ref, proj_ref, h8_ref, x_ref, hscr, xbuf, sem):
    i = pl.program_id(0)
    n = pl.num_programs(0)
    slot = i % 2

    @pl.when(i == 0)
    def _():
        _start_row_copies(pos_ref, 0, ys_hbm, xbuf.at[0], sem.at[0], 0, TM_A, inline=False)

    _wait_row_copies(ys_hbm, xbuf.at[slot], sem.at[slot])
    x = xbuf[slot]
    x_ref[...] = x
    nxt = jnp.minimum(i + 1, n - 1)
    _start_row_copies(pos_ref, nxt * TM_A, ys_hbm, xbuf.at[1 - slot], sem.at[1 - slot], 0, TM_A, inline=True)
    _inproj_body(x, nw_ref, w_ref, proj_ref, h8_ref, hscr)

    @pl.when(i == n - 1)
    def _():
        _wait_row_copies(ys_hbm, xbuf.at[1 - slot], sem.at[1 - slot])


def _inproj_body(x, nw_ref, w_ref, proj_ref, h8_ref, hscr):
    h = _rms(x, nw_ref[...])
    for k in range(D_MODEL // 128):
        hscr[k] = h[:, k * 128:(k + 1) * 128]
        h8_ref[:, pl.ds(k * 128, 128)] = hscr[k, pl.ds(SEQ_PAD - 1, TM_A // SEQ_PAD, stride=SEQ_PAD), :]
    hb = h.astype(bf16)
    for n in range(NP_IN // N_SLAB):
        sl = pl.ds(n * N_SLAB, N_SLAB)
        proj_ref[:, sl] = jnp.dot(hb, w_ref[0, :, sl], preferred_element_type=f32)


def _inproj_gather(pos, ys, nw, w_pad, l):
    return pl.pallas_call(
        _inproj_gather_kernel,
        grid_spec=pltpu.PrefetchScalarGridSpec(
            num_scalar_prefetch=1,
            grid=(T_ALL // TM_A,),
            in_specs=[pl.BlockSpec(memory_space=pl.ANY),
                      pl.BlockSpec((1, D_MODEL), lambda i, pos: (0, 0)),
                      pl.BlockSpec((1, D_MODEL, NP_IN), lambda i, pos: (l, 0, 0))],
            out_specs=[pl.BlockSpec((TM_A, NP_IN), lambda i, pos: (i, 0)),
                       pl.BlockSpec((TM_A // SEQ_PAD, D_MODEL), lambda i, pos: (i, 0)),
                       pl.BlockSpec((TM_A, D_MODEL), lambda i, pos: (i, 0))],
            scratch_shapes=[pltpu.VMEM((D_MODEL // 128, TM_A, 128), f32),
                            pltpu.VMEM((2, TM_A, D_MODEL), f32),
                            pltpu.SemaphoreType.DMA((2,))]),
        out_shape=[jax.ShapeDtypeStruct((T_ALL, NP_IN), f32),
                   jax.ShapeDtypeStruct((T_ALL // SEQ_PAD, D_MODEL), f32),
                   jax.ShapeDtypeStruct((T_ALL, D_MODEL), f32)],
        compiler_params=pltpu.CompilerParams(dimension_semantics=("arbitrary",),
                                             vmem_limit_bytes=VMEM_LIMIT),
        name="inproj_gather",
    )(pos, ys, nw, w_pad)


def _inproj(xs, nw, w_pad, l):
    split = len(xs) == 2
    npt = T_PROMPT // TM_A
    if split:
        x_specs = [pl.BlockSpec((TM_A, D_MODEL), lambda i: (jnp.minimum(i, npt - 1), 0)),
                   pl.BlockSpec((TM_A, D_MODEL), lambda i: (jnp.maximum(i - npt, 0), 0))]
    else:
        x_specs = [pl.BlockSpec((TM_A, D_MODEL), lambda i: (i, 0))]
    return pl.pallas_call(
        functools.partial(_inproj_kernel, split=split),
        grid=(T_ALL // TM_A,),
        in_specs=x_specs + [pl.BlockSpec((1, D_MODEL), lambda i: (0, 0)),
                            pl.BlockSpec((1, D_MODEL, NP_IN), lambda i: (l, 0, 0))],
        out_specs=[pl.BlockSpec((TM_A, NP_IN), lambda i: (i, 0)),
                   pl.BlockSpec((TM_A // SEQ_PAD, D_MODEL), lambda i: (i, 0))],
        out_shape=[jax.ShapeDtypeStruct((T_ALL, NP_IN), f32),
                   jax.ShapeDtypeStruct((T_ALL // SEQ_PAD, D_MODEL), f32)],
        scratch_shapes=[pltpu.VMEM((D_MODEL // 128, TM_A, 128), f32)],
        compiler_params=pltpu.CompilerParams(dimension_semantics=("arbitrary",),
                                             vmem_limit_bytes=VMEM_LIMIT),
        name="inproj",
    )(*xs, nw, w_pad)


def _mm_kernel(a_ref, b_ref, o_ref):
    o_ref[...] = jnp.dot(a_ref[...].astype(bf16), b_ref[...], preferred_element_type=f32)


def _mm(a, b):
    return pl.pallas_call(
        _mm_kernel,
        out_shape=jax.ShapeDtypeStruct((a.shape[0], b.shape[1]), f32),
        compiler_params=pltpu.CompilerParams(vmem_limit_bytes=VMEM_LIMIT),
        name="shift_proj",
    )(a, b)


def _cat(parts, axis):
    return parts[0] if len(parts) == 1 else jnp.concatenate(parts, axis=axis)


def _mixer_kernel(*refs, TB, C, G, per_seq, n_alias=0):
    it = iter(refs)
    proj_ref = next(it)
    if per_seq:
        gdn_in, conv_in, rwkv_in, pcf_in = next(it), next(it), next(it), next(it)
    convw_ref, v128_ref, v256_ref, sguw_ref, sgub_ref, mu_ref, wup_ref, aup_ref, gup_ref = (
        next(it) for _ in range(9))
    for _ in range(n_alias):
        next(it)
    mix_ref, gdn_out, conv_out, rwkv_out = next(it), next(it), next(it), next(it)
    cv_out = next(it) if per_seq else None
    xp, pp = next(it), next(it)

    nchunk = TB // C
    ngrp = C // G
    iters = int(math.log2(G)) - 1

    if per_seq:
        xp[pl.ds(0, 8), :] = jnp.zeros((8, 3 * D_A), f32)
        pp[pl.ds(0, 8), :] = jnp.zeros((8, NC_PAD), f32)
    else:
        @pl.when(pl.program_id(1) == 0)
        def _():
            xp[pl.ds(0, 8), :] = jnp.zeros((8, 3 * D_A), f32)
            pp[pl.ds(0, 8), :] = jnp.zeros((8, NC_PAD), f32)
            gdn_out[...] = jnp.zeros(gdn_out.shape, f32)
            rwkv_out[...] = jnp.zeros(rwkv_out.shape, f32)
    xp[pl.ds(8, TB), :] = proj_ref[:, pl.ds(OFF_Q, 3 * D_A)]
    pp[pl.ds(8, TB), :] = proj_ref[:, pl.ds(OFF_C, NC_PAD)]
    if per_seq:
        for s in range(TB // SEQ_PAD):
            r0 = 8 + s * SEQ_PAD
            xp[pl.ds(r0 + SEQ_LEAD - (CONV_W - 1), CONV_W - 1), :] = conv_in[s]
            pp[pl.ds(r0 + SEQ_LEAD - 1, 1), :] = pcf_in[s]

    rowi = lax.broadcasted_iota(i32, (TB, 1), 0)
    live = (rowi % SEQ_PAD) >= SEQ_LEAD if per_seq else None

    ii = lax.broadcasted_iota(i32, (C, C), 0)
    jj = lax.broadcasted_iota(i32, (C, C), 1)
    same = (ii // G) == (jj // G)
    causal = (ii >= jj) & same
    strict = (ii > jj) & same
    eye = ii == jj
    m_cum = causal.astype(bf16)
    m_grp = same.astype(bf16)

    def conv_cols(r0, c0):
        cs = pl.ds(c0, 128)
        acc = xp[pl.ds(r0 + 8, C), cs] * convw_ref[pl.ds(3, 1), cs]
        for j in range(CONV_W - 1):
            acc = acc + xp[pl.ds(r0 + 5 + j, C), cs] * convw_ref[pl.ds(j, 1), cs]
        return _silu(acc)

    alog_row = v128_ref[pl.ds(0, 1), :]
    dtb_row = v128_ref[pl.ds(1, 1), :]
    gnorm_w = v128_ref[pl.ds(2, 1), :]
    ln_w, ln_b = v256_ref[pl.ds(0, 1), :], v256_ref[pl.ds(1, 1), :]
    w0, a0 = v256_ref[pl.ds(2, 1), :], v256_ref[pl.ds(3, 1), :]
    k_k, k_a, r_k = v256_ref[pl.ds(4, 1), :], v256_ref[pl.ds(5, 1), :], v256_ref[pl.ds(6, 1), :]
    rln_w, rln_b = v256_ref[pl.ds(7, 1), :], v256_ref[pl.ds(8, 1), :]

    l64i = lax.broadcasted_iota(i32, (D_C, D_C), 0) // N_C
    l64j = lax.broadcasted_iota(i32, (D_C, D_C), 1) // N_C
    seg64 = (l64i == l64j).astype(bf16)

    gdn, rwk, lmats = [None] * nchunk, [None] * nchunk, [None] * (8 * nchunk)

    def phase1(c):
        R0 = c * C
        rows = pl.ds(R0, C)
        live_c = live[R0:R0 + C] if per_seq else None

        ba = proj_ref[rows, pl.ds(OFF_BA, 128)]
        beta_all = _sigmoid(ba)
        g_all = -jnp.exp(alog_row) * _softplus(ba + dtb_row)
        if per_seq:
            beta_all = jnp.where(live_c, beta_all, 0.0)
            g_all = jnp.where(live_c, g_all, 0.0)
        pcur = pp[pl.ds(R0 + 8, C), :]
        pprev = pp[pl.ds(R0 + 7, C), :]
        pm = pcur + mu_ref[...] * (pprev - pcur)
        r_ = pm[:, 0:D_C]
        kc = pm[:, D_C:2 * D_C]
        vc = pm[:, 2 * D_C:3 * D_C]
        wd = pm[:, 3 * D_C:3 * D_C + 128]
        ad = pm[:, 3 * D_C + 128:3 * D_C + 256]
        gd = pm[:, 3 * D_C + 256:3 * D_C + 384]
        wl_mm = _bdot(jnp.tanh(wd), wup_ref[...])
        a_mm = _bdot(ad, aup_ref[...])
        gate = _bdot(_sigmoid(gd), gup_ref[...])
        kk = kc * k_k
        kk_ss = _dot01_right(kk * kk, seg64)
        yield

        gc_all = _dot01_left(m_cum, g_all)
        total = lambda cum, x: _dot01_left(m_grp, x)
        gl_all = total(gc_all, g_all)
        w_log = -_softplus(-(w0 + wl_mm)) - 0.5
        logw = -jnp.exp(w_log)
        a_ = _sigmoid(a0 + a_mm)
        kk = kk * lax.rsqrt(kk_ss + 1e-6)
        kc2 = kc * (1.0 + (a_ - 1.0) * k_a)
        if per_seq:
            logw = jnp.where(live_c, logw, 0.0)
            kk = jnp.where(live_c, kk, 0.0)
            kc2 = jnp.where(live_c, kc2, 0.0)
        b_ = kk * a_
        Gc = _dot01_left(m_cum, logw)
        Gl = total(Gc, logw)
        bonus_ss = _dot01_right(r_ * kc2 * r_k, seg64)
        yield

        heads = []
        for h in range(H_A):
            q = conv_cols(R0, OFF_Q + h * HD_A)
            k = conv_cols(R0, OFF_K + h * HD_A)
            v = conv_cols(R0, OFF_V + h * HD_A)
            q = q * lax.rsqrt(jnp.sum(q * q, axis=-1, keepdims=True) + 1e-6) * (HD_A ** -0.5)
            k = k * lax.rsqrt(jnp.sum(k * k, axis=-1, keepdims=True) + 1e-6)
            if per_seq:
                k = jnp.where(live_c, k, 0.0)
            beta = beta_all[:, h:h + 1]
            gcol = gc_all[:, 4 + h:5 + h]
            glr = gl_all[:, 4 + h:5 + h]
            grow = jnp.sum(jnp.where(eye, jnp.broadcast_to(gcol, (C, C)), 0.0), axis=0, keepdims=True)
            decay = jnp.where(causal, jnp.exp(jnp.where(causal, gcol - grow, 0.0)), 0.0)
            eg = jnp.exp(gcol)
            kb = k * beta
            kq = _bdot(jnp.concatenate([kb, q], axis=0), k, NT)
            heads.append(dict(kq=kq, decay=decay,
                              rhs=jnp.concatenate([v * beta, kb * eg], axis=1),
                              q_dec=q * eg, k_dec=k * jnp.exp(glr - gcol), glr=glr))
            yield
        gdn[c] = heads
        e_neg = jnp.exp(-Gc)
        e_rem = jnp.exp(Gl - Gc)
        rG = r_ * jnp.exp(Gc)
        kkG = kk * jnp.exp(Gc - logw)
        kN = kc2 * e_neg
        bN = b_ * e_neg
        aalls = []
        for h in range(H_C):
            hs = slice(h * N_C, (h + 1) * N_C)
            aalls.append(_bdot(jnp.concatenate([kkG[:, hs], rG[:, hs]], axis=0),
                               jnp.concatenate([bN[:, hs], kN[:, hs]], axis=0), NT))
        yield

        for h in range(H_A):
            d = gdn[c][h]
            kq = d.pop('kq')
            decay = d.pop('decay')
            lmats[8 * c + h] = jnp.where(strict, kq[:C] * decay, 0.0)
            d['attn'] = kq[C:] * decay
        yield
        heads = []
        for h in range(H_C):
            hs = slice(h * N_C, (h + 1) * N_C)
            aall = aalls[h]
            lmats[8 * c + H_A + h] = jnp.where(strict, aall[:C, :C], 0.0)
            akk_k = jnp.where(strict, aall[:C, C:], 0.0)
            ar = jnp.concatenate([jnp.where(causal, aall[C:, C:], 0.0),
                                  -jnp.where(causal, aall[C:, :C], 0.0)], axis=1)
            heads.append(dict(x1=_bdot(akk_k, vc[:, hs]), ar=ar))
        rwk[c] = dict(heads=heads, vc=vc, rG=rG, kkG=kkG, kdec=kc2 * e_rem, bdec=b_ * e_rem,
                      e_last=jnp.exp(Gl), gate=gate, bonus=bonus_ss * vc)
        yield

    def par(*gens):
        gens = list(gens)
        while gens:
            alive = []
            for g in gens:
                try:
                    next(g)
                    alive.append(g)
                except StopIteration:
                    pass
            gens = alive
            yield

    def seq(*gens):
        for g in gens:
            yield from g

    def lockstep(gens):
        for _ in par(*gens):
            pass

    nmats = [None] * (8 * nchunk)

    def inverses(chunks):
        idx = [8 * c + k for c in chunks for k in range(8)]
        bmm = lambda a, b: lax.dot_general(a.astype(bf16), b.astype(bf16), BNN, preferred_element_type=f32)
        pack = 2 * C <= 128
        if pack:
            N = -jnp.stack([jnp.concatenate([lmats[i], lmats[j]], axis=1) for i, j in zip(idx[0::2], idx[1::2])])
            left = lax.broadcasted_iota(i32, N.shape, 2) < C
            rhs = lambda X: jnp.concatenate([jnp.where(left, X, 0.0), jnp.where(left, 0.0, X)], axis=1)
        else:
            N = -jnp.stack([lmats[i] for i in idx])
            rhs = lambda X: X
        Q = bmm(N, rhs(N))
        yield
        for _ in range(iters - 1):
            R = bmm(jnp.concatenate([N, Q], axis=1), rhs(Q))
            N = N + Q + R[:, :C]
            Q = R[:, C:]
            yield
        N = N + Q + bmm(N, rhs(Q))
        if pack:
            for p, (i, j) in enumerate(zip(idx[0::2], idx[1::2])):
                nmats[i], nmats[j] = N[p][:, :C], N[p][:, C:]
        else:
            for p, i in enumerate(idx):
                nmats[i] = N[p]
        yield

    def phase3(c):
        for h in range(H_A):
            d = gdn[c][h]
            sol = d['rhs'] + _bdot(nmats[8 * c + h], d['rhs'])
            d['u'], d['w'] = sol[:, :HD_A], sol[:, HD_A:]
        for h in range(H_C):
            hs = slice(h * N_C, (h + 1) * N_C)
            d = rwk[c]['heads'][h]
            both = jnp.concatenate([d['x1'], rwk[c]['kkG'][:, hs]], axis=1)
            both = both + _bdot(nmats[8 * c + H_A + h], both)
            d['u_p'], d['w_p'] = both[:, :N_C], both[:, N_C:]
            d['both'] = both
        yield
        if per_seq:
            return
        for h in range(H_A):
            d = gdn[c][h]
            wu = jnp.concatenate([d['w'], d['u']], axis=1)
            a_wu = _bdot(d['attn'], wu)
            k_wu = _bdot(d['k_dec'], wu, TN)
            d['qe'] = d['q_dec'] - a_wu[:, :HD_A]
            d['o0'] = a_wu[:, HD_A:]
            d['m'] = -k_wu[:, :HD_A]
            d['b'] = k_wu[:, HD_A:]
        yield
        for h in range(H_C):
            hs = slice(h * N_C, (h + 1) * N_C)
            rc = rwk[c]
            d = rc['heads'][h]
            V = rc['vc'][:, hs]
            x = _bdot(d['ar'][:, C:], d['both'])
            d['y0'] = _bdot(d['ar'][:, :C], V) + x[:, :N_C]
            d['re'] = rc['rG'][:, hs] + x[:, N_C:]
            d['m'] = _bdot(d['w_p'], rc['bdec'][:, hs], TN)
            d['b'] = _bdot(jnp.concatenate([V, -d['u_p']], axis=0),
                           jnp.concatenate([rc['kdec'][:, hs], rc['bdec'][:, hs]], axis=0), TN)
        yield

    def gdn_head_seq(c, h):
        rows = pl.ds(c * C, C)
        d = gdn[c][h]
        S = gdn_out[0, h]
        Sb = S.astype(bf16)
        gdn_out[0, h] = S * jnp.exp(d['glr'][C - 1:C, :]) + _bdot(d['m'], Sb) + d['b']
        o = d['o0'] + _bdot(d['qe'], Sb)
        yield
        o = o * lax.rsqrt(jnp.mean(o * o, axis=-1, keepdims=True) + NORM_EPS) * gnorm_w
        z = proj_ref[rows, pl.ds(OFF_Z + h * HD_A, HD_A)]
        mix_ref[rows, pl.ds(h * HD_A, HD_A)] = o * _silu(z)

    def rwkv_head_seq(c, h, ys):
        rc = rwk[c]
        hs = slice(h * N_C, (h + 1) * N_C)
        d = rc['heads'][h]
        S = rwkv_out[0, h]
        Sb = S.astype(bf16)
        rwkv_out[0, h] = S * rc['e_last'][C - 1:C, hs] + d['b'] - _bdot(Sb, d['m'])
        ys[h] = d['y0'] + _bdot(d['re'], Sb, NT)
        yield

    def gdn_head(c, h):
        rows = pl.ds(c * C, C)
        d = gdn[c][h]
        rq, vn, s_old = [], [], []
        for s in range(ngrp):
            gs = slice(s * G, (s + 1) * G)
            S = gdn_in[s, h] if per_seq else gdn_out[0, h]
            s_old.append(S)
            R = _bdot(jnp.concatenate([d['w'][gs], d['q_dec'][gs]], axis=0), S)
            vn.append(d['u'][gs] - R[:G])
            rq.append(R[G:])
        yield
        v_new = _cat(vn, 0)
        o = _cat(rq, 0) + _bdot(d['attn'], v_new)
        for s in range(ngrp):
            gs = slice(s * G, (s + 1) * G)
            g_last = jnp.exp(d['glr'][s * G + G - 1:s * G + G, :])
            S_new = s_old[s] * g_last + _bdot(d['k_dec'][gs], v_new[gs], TN)
            if per_seq:
                gdn_out[s, h] = S_new
            else:
                gdn_out[0, h] = S_new
        yield
        o = o * lax.rsqrt(jnp.mean(o * o, axis=-1, keepdims=True) + NORM_EPS) * gnorm_w
        z = proj_ref[rows, pl.ds(OFF_Z + h * HD_A, HD_A)]
        mix_ref[rows, pl.ds(h * HD_A, HD_A)] = o * _silu(z)

    def rwkv_head(c, h, ys):
        rc = rwk[c]
        hs = slice(h * N_C, (h + 1) * N_C)
        d = rc['heads'][h]
        V = rc['vc'][:, hs]
        rr, ut, s_old = [], [], []
        for s in range(ngrp):
            gs = slice(s * G, (s + 1) * G)
            S = rwkv_in[s, h] if per_seq else rwkv_out[0, h]
            s_old.append(S)
            R = _bdot(jnp.concatenate([d['w_p'][gs], rc['rG'][gs, hs]], axis=0), S, NT)
            ut.append(d['u_p'][gs] + R[:G])
            rr.append(R[G:])
        yield
        Ut = _cat(ut, 0)
        ys[h] = _cat(rr, 0) + _bdot(d['ar'], jnp.concatenate([V, Ut], axis=0))
        for s in range(ngrp):
            gs = slice(s * G, (s + 1) * G)
            upd = _bdot(jnp.concatenate([V[gs], -Ut[gs]], axis=0),
                        jnp.concatenate([rc['kdec'][gs, hs], rc['bdec'][gs, hs]], axis=0), TN)
            S_new = s_old[s] * rc['e_last'][s * G + G - 1:s * G + G, hs] + upd
            if per_seq:
                rwkv_out[s, h] = S_new
            else:
                rwkv_out[0, h] = S_new
        yield

    def phase4(c):
        rows = pl.ds(c * C, C)
        rc = rwk[c]
        ys = [None] * H_C
        if per_seq:
            heads = [gdn_head(c, h) for h in range(H_A)] + [rwkv_head(c, h, ys) for h in range(H_C)]
        else:
            heads = [gdn_head_seq(c, h) for h in range(H_A)] + [rwkv_head_seq(c, h, ys) for h in range(H_C)]
        yield from par(*heads)
        y = jnp.concatenate(ys, axis=1)
        mu_y = _dot01_right(y, seg64) * (1.0 / N_C)
        dy = y - mu_y
        var_y = _dot01_right(dy * dy, seg64) * (1.0 / N_C)
        y = dy * lax.rsqrt(var_y + GN_EPS) * rln_w + rln_b
        mix_ref[rows, pl.ds(D_A + D_B, D_C)] = (y + rc['bonus']) * rc['gate']
        yield

    def sgu(sc):
        rows = pl.ds(sc * SGU_CHUNK, SGU_CHUNK)
        ug = _gelu(proj_ref[rows, pl.ds(OFF_U, D_B)])
        vs = _gelu(proj_ref[rows, pl.ds(OFF_VB, D_B)])
        yield
        mu_v = _dot01_right(vs, seg64) * (1.0 / DH_B)
        dv = vs - mu_v
        var_v = _dot01_right(dv * dv, seg64) * (1.0 / DH_B)
        vs = dv * lax.rsqrt(var_v + LN_EPS) * ln_w + ln_b
        if per_seq:
            cv_out[rows, :] = vs
        yield
        vsb = vs.astype(bf16)
        outs = [lax.dot_general(sguw_ref[h], vsb[:, h * DH_B:(h + 1) * DH_B], NN, preferred_element_type=f32)
                for h in range(4)]
        mixed = jnp.concatenate(outs, axis=1) + sgub_ref[...]
        mix_ref[rows, pl.ds(D_A, D_B)] = ug * mixed
        yield

    first = list(range(nchunk))[:max(nchunk // 2, 1)]
    second = list(range(nchunk))[len(first):]
    sgus = [sgu(sc) for sc in range(TB // SGU_CHUNK)]
    lockstep([phase1(c) for c in first])
    lockstep([phase1(c) for c in second]
             + [seq(inverses(first), par(*[phase3(c) for c in first]))])
    if second:
        lockstep([seq(inverses(second), par(*[phase3(c) for c in second])),
                  seq(*[phase4(c) for c in first]), seq(*sgus)])
        lockstep([seq(*[phase4(c) for c in second])])
    else:
        lockstep([seq(*[phase4(c) for c in first]), seq(*sgus)])

    if per_seq:
        for s in range(TB // SEQ_PAD):
            conv_out[s] = xp[pl.ds(8 + (s + 1) * SEQ_PAD - (CONV_W - 1), CONV_W - 1), :]
    else:
        conv_out[0] = xp[pl.ds(8 + TB - (CONV_W - 1), CONV_W - 1), :]
        xp[pl.ds(0, 8), :] = xp[pl.ds(TB, 8), :]
        pp[pl.ds(0, 8), :] = pp[pl.ds(TB, 8), :]


def _mixer_weight_specs(nidx):
    z2 = (lambda b, j: (0, 0)) if nidx == 2 else (lambda i: (0, 0))
    z3 = (lambda b, j: (0, 0, 0)) if nidx == 2 else (lambda i: (0, 0, 0))
    return [pl.BlockSpec((CONV_W, 3 * D_A), z2),
            pl.BlockSpec((8, 128), z2),
            pl.BlockSpec((16, D_C), z2),
            pl.BlockSpec((4, SGU_CHUNK, SGU_CHUNK), z3),
            pl.BlockSpec((SGU_CHUNK, D_B), z2),
            pl.BlockSpec((1, NC_PAD), z2),
            pl.BlockSpec((128, D_C), z2),
            pl.BlockSpec((128, D_C), z2),
            pl.BlockSpec((128, D_C), z2)]


def _mixer_prompt(proj, mw):
    nj = SEQ // TB_P
    return pl.pallas_call(
        functools.partial(_mixer_kernel, TB=TB_P, C=GDN_CHUNK, G=GDN_CHUNK, per_seq=False),
        grid=(BATCH, nj),
        in_specs=[pl.BlockSpec((TB_P, NP_IN), lambda b, j: (b * nj + j, 0))] + _mixer_weight_specs(2),
        out_specs=[pl.BlockSpec((TB_P, D_MODEL), lambda b, j: (b * nj + j, 0)),
                   pl.BlockSpec((1, H_A, HD_A, HD_A), lambda b, j: (b, 0, 0, 0)),
                   pl.BlockSpec((1, CONV_W - 1, 3 * D_A), lambda b, j: (b, 0, 0)),
                   pl.BlockSpec((1, H_C, N_C, N_C), lambda b, j: (b, 0, 0, 0))],
        out_shape=[jax.ShapeDtypeStruct((T_PROMPT, D_MODEL), f32),
                   jax.ShapeDtypeStruct((BATCH, H_A, HD_A, HD_A), f32),
                   jax.ShapeDtypeStruct((BATCH, CONV_W - 1, 3 * D_A), f32),
                   jax.ShapeDtypeStruct((BATCH, H_C, N_C, N_C), f32)],
        scratch_shapes=[pltpu.VMEM((TB_P + 8, 3 * D_A), f32), pltpu.VMEM((TB_P + 8, NC_PAD), f32)],
        compiler_params=pltpu.CompilerParams(dimension_semantics=("arbitrary", "arbitrary"),
                                             vmem_limit_bytes=VMEM_LIMIT),
        name="mixer_prompt",
    )(proj, *mw)


def _mixer_sample(proj, s_gdn, s_conv, s_rwkv, pcf, mw, l, prev_states):
    nseq = TB_S // SEQ_PAD
    base = T_PROMPT // TB_S
    n_in = 5 + len(mw)
    lay4 = lambda i: (l, i, 0, 0, 0)
    lay3 = lambda i: (l, i, 0, 0)
    return pl.pallas_call(
        functools.partial(_mixer_kernel, TB=TB_S, C=TB_S, G=SEQ_PAD, per_seq=True, n_alias=len(prev_states)),
        grid=(T_SAMPLE // TB_S,),
        in_specs=[pl.BlockSpec((TB_S, NP_IN), lambda i: (base + i, 0)),
                  pl.BlockSpec((None, nseq, H_A, HD_A, HD_A), lay4),
                  pl.BlockSpec((None, nseq, CONV_W - 1, 3 * D_A), lay3),
                  pl.BlockSpec((None, nseq, H_C, N_C, N_C), lay4),
                  pl.BlockSpec((nseq, 1, NC_PAD), lambda i: (i, 0, 0))] + _mixer_weight_specs(1)
                 + [pl.BlockSpec(memory_space=pl.ANY)] * len(prev_states),
        out_specs=[pl.BlockSpec((TB_S, D_MODEL), lambda i: (i, 0)),
                   pl.BlockSpec((None, nseq, H_A, HD_A, HD_A), lay4),
                   pl.BlockSpec((nseq, CONV_W - 1, 3 * D_A), lambda i: (i, 0, 0)),
                   pl.BlockSpec((None, nseq, H_C, N_C, N_C), lay4),
                   pl.BlockSpec((TB_S, D_B), lambda i: (i, 0))],
        out_shape=[jax.ShapeDtypeStruct((T_SAMPLE, D_MODEL), f32),
                   jax.ShapeDtypeStruct((DEPTH, DEC_BATCH, H_A, HD_A, HD_A), f32),
                   jax.ShapeDtypeStruct((DEC_BATCH, CONV_W - 1, 3 * D_A), f32),
                   jax.ShapeDtypeStruct((DEPTH, DEC_BATCH, H_C, N_C, N_C), f32),
                   jax.ShapeDtypeStruct((T_SAMPLE, D_B), f32)],
        scratch_shapes=[pltpu.VMEM((TB_S + 8, 3 * D_A), f32), pltpu.VMEM((TB_S + 8, NC_PAD), f32)],
        input_output_aliases={n_in + k: o for k, o in enumerate((1, 3)[:len(prev_states)])},
        compiler_params=pltpu.CompilerParams(dimension_semantics=("arbitrary",),
                                             vmem_limit_bytes=VMEM_LIMIT),
        name="mixer_sample",
    )(proj, s_gdn, s_conv, s_rwkv, pcf, *mw, *prev_states)


TM_C = 512


def _outproj_kernel(*refs, split):
    i = pl.program_id(0)
    in_prompt = i < T_PROMPT // TM_C
    if split:
        xp_ref, xs_ref = refs[:2]
        x = jnp.where(in_prompt, xp_ref[...], xs_ref[...])
    else:
        x = refs[0][...]
    mixp_ref, mixs_ref, wout_ref, nw_ref, rwh_ref, rwl_ref, rb_ref, xg_ref, cnt_ref, run_ref = refs[-10:]

    @pl.when(i == 0)
    def _():
        run_ref[...] = jnp.zeros(run_ref.shape, f32)

    mix = jnp.where(in_prompt, mixp_ref[...], mixs_ref[...])
    x2 = x + jnp.dot(mix.astype(bf16), wout_ref[0], preferred_element_type=f32)
    xg_ref[:, pl.ds(0, D_MODEL)] = x2
    h2 = _rms(x2, nw_ref[...])
    hh, hl = _split2(h2)
    d = lambda a, b: jnp.dot(a, b, preferred_element_type=f32)
    both = d(hh, jnp.concatenate([rwh_ref[...], rwl_ref[...]], axis=1))
    logits = both[:, :128] + both[:, 128:] + d(hl, rwh_ref[...]) + rb_ref[...]

    lane = lax.broadcasted_iota(i32, logits.shape, 1).astype(f32)
    neg = jnp.float32(-jnp.inf)
    is_g = lane < float(N_GROUPS)
    gl = jnp.where(is_g, logits, neg)
    gmax = jnp.max(gl, axis=-1, keepdims=True)
    gsel = jnp.min(jnp.where(gl == gmax, lane, 128.0), axis=-1, keepdims=True)
    gw = 1.0 / jnp.sum(jnp.where(is_g, jnp.exp(jnp.where(is_g, logits - gmax, 0.0)), 0.0),
                       axis=-1, keepdims=True)
    lo = LANE_E0 + float(EPG) * gsel
    in_grp = (lane >= lo) & (lane < lo + float(EPG))
    el = jnp.where(in_grp, logits, neg)
    t1 = jnp.max(el, axis=-1, keepdims=True)
    i1 = jnp.min(jnp.where(el == t1, lane, 128.0), axis=-1, keepdims=True)
    el2 = jnp.where(lane == i1, neg, el)
    t2 = jnp.max(el2, axis=-1, keepdims=True)
    i2 = jnp.min(jnp.where(el2 == t2, lane, 128.0), axis=-1, keepdims=True)
    e2 = jnp.exp(t2 - t1)
    den = 1.0 + e2
    gates = jnp.where(lane == i1, gw / den, 0.0) + jnp.where(lane == i2, gw * e2 / den, 0.0)

    ea, eb = i1 - lo, i2 - lo
    e_lo, e_hi = jnp.minimum(ea, eb), jnp.maximum(ea, eb)
    bucket = float(N_PAIRS) * gsel + e_lo * (7.0 - e_lo) * 0.5 + (e_hi - e_lo - 1.0)
    onehot = jnp.where(lane == bucket, 1.0, 0.0)
    ri = lax.broadcasted_iota(i32, (TM_C, TM_C), 0)
    ci = lax.broadcasted_iota(i32, (TM_C, TM_C), 1)
    before = lax.dot_general((ri > ci).astype(bf16), onehot.astype(bf16), NN, preferred_element_type=f32)
    rank = jnp.sum((before + run_ref[...]) * onehot, axis=-1, keepdims=True)
    run_ref[...] += jnp.sum(onehot, axis=0, keepdims=True)
    cnt_ref[...] = jnp.broadcast_to(run_ref[...], cnt_ref.shape)
    xg_ref[:, pl.ds(D_MODEL, 128)] = (gates + jnp.where(lane == float(LANE_G), bucket, 0.0)
                                      + jnp.where(lane == float(LANE_RANK), rank, 0.0))


def _outproj(xs, mix_p, mix_s, wout, nw, rwh, rwl, rb, l):
    row = lambda i: (i, 0)
    fix = lambda i: (0, 0)
    npt = T_PROMPT // TM_C
    p_rows = pl.BlockSpec((TM_C, D_MODEL), lambda i: (jnp.minimum(i, npt - 1), 0))
    s_rows = pl.BlockSpec((TM_C, D_MODEL), lambda i: (jnp.maximum(i - npt, 0), 0))
    split = len(xs) == 2
    return pl.pallas_call(
        functools.partial(_outproj_kernel, split=split),
        grid=(T_ALL // TM_C,),
        in_specs=([p_rows, s_rows] if split else [pl.BlockSpec((TM_C, D_MODEL), row)]) + [
                  p_rows, s_rows,
                  pl.BlockSpec((1, D_MODEL, D_MODEL), lambda i: (l, 0, 0)), pl.BlockSpec((1, D_MODEL), fix),
                  pl.BlockSpec((D_MODEL, 128), fix), pl.BlockSpec((D_MODEL, 128), fix),
                  pl.BlockSpec((1, 128), fix)],
        out_specs=[pl.BlockSpec((TM_C, XG_W), row), pl.BlockSpec((8, 128), fix)],
        out_shape=[jax.ShapeDtypeStruct((T_ALL, XG_W), f32), jax.ShapeDtypeStruct((8, 128), f32)],
        scratch_shapes=[pltpu.VMEM((1, 128), f32)],
        compiler_params=pltpu.CompilerParams(dimension_semantics=("arbitrary",),
                                             vmem_limit_bytes=VMEM_LIMIT),
        name="outproj_router",
    )(*xs, mix_p, mix_s, wout, nw, rwh, rwl, rb)


TM_E = 512
NT_E = T_ALL // TM_E + N_GROUPS
T_SORT = NT_E * TM_E
DMA_UNROLL = 8


def _invert_kernel(pos_ref, pad_lo_ref, pad_hi_ref, src_ref):
    for g in range(N_GROUPS):
        def fill(p, c):
            src_ref[p] = T_ALL - 1
            return c
        lax.fori_loop(pad_lo_ref[g], pad_hi_ref[g], fill, 0)

    def body(t, c):
        src_ref[pos_ref[t]] = t
        return c
    lax.fori_loop(0, T_ALL, body, 0, unroll=DMA_UNROLL)


def _invert(pos, pad_lo, pad_hi):
    smem = pl.BlockSpec(memory_space=pltpu.SMEM)
    return pl.pallas_call(
        _invert_kernel,
        in_specs=[smem, smem, smem],
        out_specs=smem,
        out_shape=jax.ShapeDtypeStruct((T_SORT,), i32),
        name="moe_invert",
    )(pos, pad_lo, pad_hi)


def _gather_rows(pos_ref, base, ys_hbm, o_ref, sem):
    def issue(r, c):
        p = pos_ref[base + r]
        pltpu.make_async_copy(ys_hbm.at[pl.ds(p, 1), :], o_ref.at[pl.ds(r, 1), :], sem).start()
        return c

    lax.fori_loop(0, TM_E, issue, 0, unroll=DMA_UNROLL)
    pltpu.make_async_copy(ys_hbm.at[pl.ds(0, TM_E), :], o_ref, sem).wait()


def _combine_kernel(pos_ref, ys_hbm, o_ref, sem):
    _gather_rows(pos_ref, pl.program_id(0) * TM_E, ys_hbm, o_ref, sem)


def _combine_split_kernel(pos_ref, ys_hbm, op_ref, os_ref, sem):
    i = pl.program_id(0)

    @pl.when(i < T_PROMPT // TM_E)
    def _():
        _gather_rows(pos_ref, i * TM_E, ys_hbm, op_ref, sem)

    @pl.when(i >= T_PROMPT // TM_E)
    def _():
        _gather_rows(pos_ref, i * TM_E, ys_hbm, os_ref, sem)


def _combine(pos, ys, split):
    npt = T_PROMPT // TM_E
    if split:
        out_specs = [pl.BlockSpec((TM_E, D_MODEL), lambda i, pos: (jnp.minimum(i, npt - 1), 0)),
                     pl.BlockSpec((TM_E, D_MODEL), lambda i, pos: (jnp.maximum(i - npt, 0), 0))]
        out_shape = [jax.ShapeDtypeStruct((T_PROMPT, D_MODEL), f32),
                     jax.ShapeDtypeStruct((T_SAMPLE, D_MODEL), f32)]
    else:
        out_specs = pl.BlockSpec((TM_E, D_MODEL), lambda i, pos: (i, 0))
        out_shape = jax.ShapeDtypeStruct((T_ALL, D_MODEL), f32)
    return pl.pallas_call(
        _combine_split_kernel if split else _combine_kernel,
        grid_spec=pltpu.PrefetchScalarGridSpec(
            num_scalar_prefetch=1,
            grid=(T_ALL // TM_E,),
            in_specs=[pl.BlockSpec(memory_space=pl.ANY)],
            out_specs=out_specs,
            scratch_shapes=[pltpu.SemaphoreType.DMA(())]),
        out_shape=out_shape,
        compiler_params=pltpu.CompilerParams(dimension_semantics=("arbitrary",),
                                             vmem_limit_bytes=VMEM_LIMIT),
        name="moe_combine",
    )(pos, ys)


ROWS_E = TM_E // EPG


def _experts_kernel(tg_ref, tv_ref, ti_ref, tu_ref, tw_ref, src_ref, xg_hbm, nw_ref, wg_ref, wu_ref, wd_ref, nf_ref,
                    ys_ref, xbuf, hbuf, ybuf, gsem, *, final_norm):
    del tw_ref
    t = pl.program_id(0)
    e = pl.program_id(1)
    valid = tv_ref[t] == 1
    used = tu_ref[t * EPG + e] == 1
    slot = t % 2
    nxt = ti_ref[jnp.minimum(t + 1, NT_E - 1)]

    @pl.when((t == 0) & (e == 0))
    def _():
        _start_row_copies(src_ref, 0, xg_hbm, xbuf.at[0], gsem.at[0], 0, TM_E, inline=False)

    @pl.when(valid & (e == 0))
    def _():
        _wait_row_copies(xg_hbm, xbuf.at[slot], gsem.at[slot])
        x2 = xbuf[slot, :, pl.ds(0, D_MODEL)]
        hbuf[...] = _rms(x2, nw_ref[...]).astype(bf16)
        ybuf[...] = x2

    @pl.when(valid & jnp.logical_not(used))
    def _():
        _start_row_copies(src_ref, nxt * TM_E, xg_hbm, xbuf.at[1 - slot], gsem.at[1 - slot],
                          e * ROWS_E, ROWS_E, inline=False)

    @pl.when(valid & used)
    def _():
        _start_row_copies(src_ref, nxt * TM_E, xg_hbm, xbuf.at[1 - slot], gsem.at[1 - slot],
                          e * ROWS_E, ROWS_E, inline=True)
        hb = hbuf[...]
        he = (_silu(jnp.dot(hb, wg_ref[0, 0], preferred_element_type=f32))
              * jnp.dot(hb, wu_ref[0, 0], preferred_element_type=f32))
        yd = jnp.dot(he.astype(bf16), wd_ref[0, 0], preferred_element_type=f32)
        g = xbuf[slot, :, pl.ds(D_MODEL, 128)]
        lane = lax.broadcasted_iota(i32, g.shape, 1)
        gcol = jnp.sum(jnp.where(lane == LANE_E0 + EPG * tg_ref[t] + e, g, 0.0), axis=-1, keepdims=True)
        ybuf[...] += gcol * yd

    @pl.when(e == EPG - 1)
    def _():
        y = ybuf[...]
        if final_norm:
            y = _rms(y, nf_ref[...])
        ys_ref[...] = y

    @pl.when(valid & (nxt == t) & (e == EPG - 1))
    def _():
        _wait_row_copies(xg_hbm, xbuf.at[1 - slot], gsem.at[1 - slot])


def _experts(tile_group, tile_valid, tile_idx, tile_used, tile_wexp, src, xg, nw, wg, wu, wd, nf, l, final_norm):
    wsel = lambda t, e, tg, tv, ti, tu, tw, src: (l, tg[t] * EPG + tw[t * EPG + e], 0, 0)
    return pl.pallas_call(
        functools.partial(_experts_kernel, final_norm=final_norm),
        grid_spec=pltpu.PrefetchScalarGridSpec(
            num_scalar_prefetch=6,
            grid=(NT_E, EPG),
            in_specs=[pl.BlockSpec(memory_space=pl.ANY),
                      pl.BlockSpec((1, D_MODEL), lambda t, e, tg, tv, ti, tu, tw, src: (0, 0)),
                      pl.BlockSpec((1, 1, D_MODEL, D_FF_E), wsel),
                      pl.BlockSpec((1, 1, D_MODEL, D_FF_E), wsel),
                      pl.BlockSpec((1, 1, D_FF_E, D_MODEL), wsel),
                      pl.BlockSpec((1, D_MODEL), lambda t, e, tg, tv, ti, tu, tw, src: (0, 0))],
            out_specs=pl.BlockSpec((TM_E, D_MODEL), lambda t, e, tg, tv, ti, tu, tw, src: (ti[t], 0)),
            scratch_shapes=[pltpu.VMEM((2, TM_E, XG_W), f32), pltpu.VMEM((TM_E, D_MODEL), bf16),
                            pltpu.VMEM((TM_E, D_MODEL), f32), pltpu.SemaphoreType.DMA((2,))]),
        out_shape=jax.ShapeDtypeStruct((T_SORT, D_MODEL), f32),
        compiler_params=pltpu.CompilerParams(dimension_semantics=("arbitrary", "arbitrary"),
                                             vmem_limit_bytes=VMEM_LIMIT),
        name="moe_experts",
    )(tile_group, tile_valid, tile_idx, tile_used, tile_wexp, src, xg, nw, wg, wu, wd, nf)


def _route_meta(xg, cnt_rows):
    b = xg[:, D_MODEL + LANE_G].astype(i32)
    rank = xg[:, D_MODEL + LANE_RANK].astype(i32)
    bcnt = cnt_rows[0, :N_BUCKETS].astype(i32).reshape(N_GROUPS, N_PAIRS)
    cnt = jnp.sum(bcnt, axis=1)
    padded = ((cnt + TM_E - 1) // TM_E) * TM_E
    off_end = jnp.cumsum(padded)
    off = off_end - padded
    bstart = (off[:, None] + jnp.cumsum(bcnt, axis=1) - bcnt).reshape(N_BUCKETS)
    bend = bstart + bcnt.reshape(N_BUCKETS)
    bid = jnp.arange(N_BUCKETS, dtype=i32)
    pos = rank + jnp.sum(jnp.where(b[:, None] == bid[None, :], bstart[None, :], 0), axis=1)
    tile_start = jnp.arange(NT_E, dtype=i32) * TM_E
    tile_group = jnp.minimum(jnp.sum((tile_start[:, None] >= off_end[None, :]).astype(i32), axis=1), N_GROUPS - 1)
    tile_valid = (tile_start < off_end[-1]).astype(i32)
    n_used = off_end[-1] // TM_E
    tile_idx = jnp.minimum(jnp.arange(NT_E, dtype=i32), n_used - 1)
    tile_group = jnp.take(tile_group, tile_idx)
    ts = (tile_idx * TM_E)[:, None]
    overlap = (bstart[None, :] < ts + TM_E) & (bend[None, :] > ts) & (bend > bstart)[None, :]
    pair_has = jnp.array([[int(e in p) for e in range(EPG)] for p in PAIRS] * N_GROUPS, dtype=i32)
    used = jnp.sum(overlap.astype(i32)[:, :, None] * pair_has[None, :, :], axis=1) > 0
    eidx = jnp.arange(EPG, dtype=i32)[None, :]
    prev = lax.cummax(jnp.where(used, eidx, -1), axis=1)
    nxt = lax.cummin(jnp.where(used, eidx, EPG), axis=1, reverse=True)
    wexp = jnp.where(prev >= 0, prev, jnp.minimum(nxt, EPG - 1))
    return (pos, tile_group, tile_valid, tile_idx, off + cnt, off_end,
            used.astype(i32).reshape(-1), wexp.reshape(-1))


def _pad_cols(a, n):
    return jnp.pad(a, ((0, 0), (0, n - a.shape[1])))


def _pad_rows(a, n):
    return jnp.pad(a, ((0, n - a.shape[0]), (0, 0)))


def _prep_w_in(w):
    pad_last = lambda a, n: jnp.pad(a, ((0, 0), (0, 0), (0, n - a.shape[-1])))
    c = w[..., 2568:]
    parts = [w[..., 0:2048], pad_last(w[..., 2048:2056], 128), w[..., 2056:2568],
             c[..., 0:768], pad_last(c[..., 768:832], 128), pad_last(c[..., 832:896], 128), c[..., 896:1024]]
    return jnp.concatenate(parts, axis=-1).astype(bf16)


def _prep_mu(mu):
    m = mu[None, :]
    return jnp.concatenate([m[:, 0:768], _pad_cols(m[:, 768:832], 128), _pad_cols(m[:, 832:896], 128),
                            m[:, 896:1024]], axis=1)


def _sgu_mats(sgu_w, sgu_b):
    t = jnp.arange(SGU_CHUNK)
    wm = jnp.where(t[:, None] >= t[None, :], sgu_w, 0.0)
    bias_p = jnp.repeat(jnp.transpose(sgu_b), DH_B, axis=1)
    small = jnp.zeros((4, SEQ_PAD, SEQ_PAD), f32).at[:, SEQ_LEAD:, SEQ_LEAD:].set(wm[:, :DEC_SEQ, :DEC_SEQ])
    eye16 = jnp.eye(TB_S // SEQ_PAD, dtype=f32)
    wm_s = jnp.einsum('ab,hij->haibj', eye16, small).reshape(4, TB_S, TB_S)
    bias_small = jnp.zeros((SEQ_PAD, D_B), f32).at[SEQ_LEAD:].set(bias_p[:DEC_SEQ])
    bias_s = jnp.tile(bias_small, (TB_S // SEQ_PAD, 1))
    return wm.astype(bf16), bias_p, wm_s.astype(bf16), bias_s


def _row(a, n):
    return _pad_cols(a.reshape(1, -1), n)


def kernel(x_prompt, x_sample, state_gdn, state_gdn_conv, state_rwkv, state_rwkv_shift, norm_mix, norm_ffn, norm_final, w_in, gdn_conv_w, gdn_a_log, gdn_dt_bias, gdn_norm_w, sgu_ln_w, sgu_ln_b, sgu_w, sgu_b, rwkv_mu, rwkv_w0, rwkv_w_up, rwkv_a0, rwkv_a_up, rwkv_g_up, rwkv_k_k, rwkv_k_a, rwkv_r_k, rwkv_ln_w, rwkv_ln_b, w_out, router_group_w, router_group_b, router_expert_w, router_expert_b, expert_w_gate, expert_w_up, expert_w_down):
    x = (x_prompt.reshape(T_PROMPT, D_MODEL),
         jnp.pad(x_sample, ((0, 0), (SEQ_LEAD, 0), (0, 0))).reshape(T_SAMPLE, D_MODEL))
    sample_states = ()

    w_pad = _prep_w_in(w_in)
    w_out_b = w_out.astype(bf16)
    wg_b, wu_b, wd_b = expert_w_gate.astype(bf16), expert_w_up.astype(bf16), expert_w_down.astype(bf16)

    outs = {k: [] for k in ('gdn_p', 'conv_p', 'rwkv_p', 'shift_p', 'gdn_s', 'conv_s', 'rwkv_s', 'shift_s', 'cv_s')}
    for l in range(DEPTH):
        last = l == DEPTH - 1
        v128 = jnp.concatenate([
            jnp.pad(gdn_a_log[l].reshape(1, H_A), ((0, 0), (4, 120))),
            jnp.pad(gdn_dt_bias[l].reshape(1, H_A), ((0, 0), (4, 120))),
            gdn_norm_w[l].reshape(1, HD_A), jnp.zeros((5, 128), f32)], axis=0)
        v256 = jnp.concatenate([a.reshape(1, D_C) for a in (
            sgu_ln_w[l], sgu_ln_b[l], rwkv_w0[l], rwkv_a0[l], rwkv_k_k[l], rwkv_k_a[l], rwkv_r_k[l],
            rwkv_ln_w[l], rwkv_ln_b[l])] + [jnp.zeros((7, D_C), f32)], axis=0)
        wm_p, bias_p, wm_s, bias_s = _sgu_mats(sgu_w[l], sgu_b[l])
        common = (gdn_conv_w[l], v128, v256)
        tail = (_prep_mu(rwkv_mu[l]), _pad_rows(rwkv_w_up[l], 128).astype(bf16),
                _pad_rows(rwkv_a_up[l], 128).astype(bf16), rwkv_g_up[l].astype(bf16))
        mw_p = common + (wm_p, bias_p) + tail
        mw_s = common + (wm_s, bias_s) + tail

        if l == 0:
            proj, h = _inproj(x, norm_mix[l].reshape(1, D_MODEL), w_pad, l)
        else:
            proj, h, x_tok = _inproj_gather(pos, ys, norm_mix[l].reshape(1, D_MODEL), w_pad, l)
            x = (x_tok,)
        pcf = _mm(state_rwkv_shift[l], w_pad[l, :, OFF_C:]).reshape(DEC_BATCH, 1, NC_PAD)
        mix_p, gdn_p, conv_p, rwkv_p = _mixer_prompt(proj, mw_p)
        mix_s, gdn_s, conv_s, rwkv_s, cv_s = _mixer_sample(
            proj, state_gdn, state_gdn_conv, state_rwkv, pcf, mw_s, l, sample_states)
        sample_states = (gdn_s, rwkv_s)

        rw = _pad_cols(jnp.concatenate([router_group_w[l], router_expert_w[l]], axis=1), 128)
        rwh = rw.astype(bf16)
        rwl = (rw - rwh.astype(f32)).astype(bf16)
        rb = _row(jnp.concatenate([router_group_b[l], router_expert_b[l]]), 128)
        nw_ffn = norm_ffn[l].reshape(1, D_MODEL)
        xg, cnt_rows = _outproj(x, mix_p, mix_s, w_out_b, nw_ffn, rwh, rwl, rb, l)
        pos, tile_group, tile_valid, tile_idx, pad_lo, pad_hi, tile_used, tile_wexp = _route_meta(xg, cnt_rows)
        src = _invert(pos, pad_lo, pad_hi)
        ys = _experts(tile_group, tile_valid, tile_idx, tile_used, tile_wexp, src, xg, nw_ffn, wg_b, wu_b, wd_b,
                      norm_final.reshape(1, D_MODEL), l, final_norm=last)
        if last:
            x = _combine(pos, ys, split=True)

        outs['gdn_p'].append(gdn_p)
        outs['conv_p'].append(conv_p)
        outs['rwkv_p'].append(rwkv_p)
        outs['shift_p'].append(h[SEQ // SEQ_PAD - 1:T_PROMPT // SEQ_PAD:SEQ // SEQ_PAD])
        outs['conv_s'].append(conv_s)
        outs['shift_s'].append(h[T_PROMPT // SEQ_PAD:])
        outs['cv_s'].append(cv_s.reshape(DEC_BATCH, SEQ_PAD, D_B)[:, SEQ_LEAD:])

    y_prompt = x[0].reshape(BATCH, SEQ, D_MODEL)
    y_sample = x[1].reshape(DEC_BATCH, SEQ_PAD, D_MODEL)[:, SEQ_LEAD:]
    st = lambda k: jnp.stack(outs[k])
    return (y_prompt, y_sample, st('gdn_p'), st('conv_p'), st('rwkv_p'), st('shift_p'),
            sample_states[0], st('conv_s'), sample_states[1], st('shift_s'), st('cv_s'))
```

```python
import functools
import math

import jax
import jax.numpy as jnp
from jax import lax
from jax.experimental import pallas as pl
from jax.experimental.pallas import tpu as pltpu

f32 = jnp.float32
bf16 = jnp.bfloat16
i32 = jnp.int32

D_MODEL = 1024
BATCH = 8
SEQ = 2048
DEPTH = 2
DEC_BATCH = 128
DEC_SEQ = 4
H_A = 4
HD_A = 128
D_A = 512
CONV_W = 4
D_B = 256
DH_B = 64
SGU_CHUNK = 128
H_C = 4
N_C = 64
D_C = 256
N_GROUPS = 4
EPG = 4
D_FF_E = 512
NORM_EPS = 1e-6
LN_EPS = 1e-5
GN_EPS = 64e-5

SEQ_PAD = 8
SEQ_LEAD = SEQ_PAD - DEC_SEQ
T_PROMPT = BATCH * SEQ
T_SAMPLE = DEC_BATCH * SEQ_PAD
T_ALL = T_PROMPT + T_SAMPLE
TB_P = 256
TB_S = 128
GDN_CHUNK = 64

OFF_Q, OFF_K, OFF_V, OFF_Z, OFF_BA, OFF_U, OFF_VB, OFF_C = 0, 512, 1024, 1536, 2048, 2176, 2432, 2688
NP_IN = 3840
NC_PAD = NP_IN - OFF_C

N_PAIRS = 6
N_BUCKETS = N_GROUPS * N_PAIRS
PAIRS = ((0, 1), (0, 2), (0, 3), (1, 2), (1, 3), (2, 3))
XG_W = D_MODEL + 128
LANE_G, LANE_RANK, LANE_E0 = 0, 1, 4

VMEM_LIMIT = 48 * 1024 * 1024


NN = (((1,), (0,)), ((), ()))
NT = (((1,), (1,)), ((), ()))
TN = (((0,), (0,)), ((), ()))
BNN = (((2,), (1,)), ((0,), (0,)))


def _bdot(a, b, dims=NN):
    return lax.dot_general(a.astype(bf16), b.astype(bf16), dims, preferred_element_type=f32)


def _split2(x):
    hi = x.astype(bf16)
    lo = (x - hi.astype(f32)).astype(bf16)
    return hi, lo


def _split3(x):
    hi = x.astype(bf16)
    r = x - hi.astype(f32)
    mid = r.astype(bf16)
    lo = (r - mid.astype(f32)).astype(bf16)
    return hi, mid, lo


def _dot01_left(m01, x):
    hi, mid, lo = _split3(x)
    d = lambda p: lax.dot_general(m01, p, NN, preferred_element_type=f32)
    return d(hi) + d(mid) + d(lo)


def _dot01_right(x, m01):
    hi, lo = _split2(x)
    d = lambda p: lax.dot_general(p, m01, NN, preferred_element_type=f32)
    return d(hi) + d(lo)


def _softplus(x):
    return jnp.maximum(x, 0.0) + jnp.log1p(jnp.exp(-jnp.abs(x)))


def _sigmoid(x):
    return 0.5 * jnp.tanh(0.5 * x) + 0.5


def _silu(x):
    return x * _sigmoid(x)


def _gelu(x):
    return 0.5 * x * (1.0 + lax.erf(x * (1.0 / math.sqrt(2.0))))


def _rms(x, w):
    return x * lax.rsqrt(jnp.mean(x * x, axis=-1, keepdims=True) + NORM_EPS) * w


TM_A = 512
N_SLAB = 768


def _start_row_copies(idx_ref, base, src_hbm, dst, sem, r0, n, inline):
    def one(r):
        p = idx_ref[base + r]
        pltpu.make_async_copy(src_hbm.at[pl.ds(p, 1), :], dst.at[pl.ds(r, 1), :], sem).start()

    if inline:
        for k in range(n):
            one(r0 + k)
    else:
        def body(k, c):
            one(r0 + k)
            return c
        lax.fori_loop(0, n, body, 0, unroll=DMA_UNROLL)


def _wait_row_copies(src_hbm, dst, sem):
    pltpu.make_async_copy(src_hbm.at[pl.ds(0, dst.shape[0]), :], dst, sem).wait()


def _inproj_kernel(*refs, split):
    if split:
        xp_ref, xs_ref, nw_ref, w_ref, proj_ref, h8_ref, hscr = refs
        x = jnp.where(pl.program_id(0) < T_PROMPT // TM_A, xp_ref[...], xs_ref[...])
    else:
        x_ref, nw_ref, w_ref, proj_ref, h8_ref, hscr = refs
        x = x_ref[...]
    _inproj_body(x, nw_ref, w_ref, proj_ref, h8_ref, hscr)


def _inproj_gather_kernel(pos_ref, ys_hbm, nw_ref, w_ref, proj_ref, h8_ref, x_ref, hscr, xbuf, sem):
    i = pl.program_id(0)
    n = pl.num_programs(0)
    slot = i % 2

    @pl.when(i == 0)
    def _():
        _start_row_copies(pos_ref, 0, ys_hbm, xbuf.at[0], sem.at[0], 0, TM_A, inline=False)

    _wait_row_copies(ys_hbm, xbuf.at[slot], sem.at[slot])
    x = xbuf[slot]
    x_ref[...] = x
    nxt = jnp.minimum(i + 1, n - 1)
    _start_row_copies(pos_ref, nxt * TM_A, ys_hbm, xbuf.at[1 - slot], sem.at[1 - slot], 0, TM_A, inline=True)
    _inproj_body(x, nw_ref, w_ref, proj_ref, h8_ref, hscr)

    @pl.when(i == n - 1)
    def _():
        _wait_row_copies(ys_hbm, xbuf.at[1 - slot], sem.at[1 - slot])


def _inproj_body(x, nw_ref, w_ref, proj_ref, h8_ref, hscr):
    h = _rms(x, nw_ref[...])
    for k in range(D_MODEL // 128):
        hscr[k] = h[:, k * 128:(k + 1) * 128]
        h8_ref[:, pl.ds(k * 128, 128)] = hscr[k, pl.ds(SEQ_PAD - 1, TM_A // SEQ_PAD, stride=SEQ_PAD), :]
    hb = h.astype(bf16)
    for n in range(NP_IN // N_SLAB):
        sl = pl.ds(n * N_SLAB, N_SLAB)
        proj_ref[:, sl] = jnp.dot(hb, w_ref[0, :, sl], preferred_element_type=f32)


def _inproj_gather(pos, ys, nw, w_pad, l):
    return pl.pallas_call(
        _inproj_gather_kernel,
        grid_spec=pltpu.PrefetchScalarGridSpec(
            num_scalar_prefetch=1,
            grid=(T_ALL // TM_A,),
            in_specs=[pl.BlockSpec(memory_space=pl.ANY),
                      pl.BlockSpec((1, D_MODEL), lambda i, pos: (0, 0)),
                      pl.BlockSpec((1, D_MODEL, NP_IN), lambda i, pos: (l, 0, 0))],
            out_specs=[pl.BlockSpec((TM_A, NP_IN), lambda i, pos: (i, 0)),
                       pl.BlockSpec((TM_A // SEQ_PAD, D_MODEL), lambda i, pos: (i, 0)),
                       pl.BlockSpec((TM_A, D_MODEL), lambda i, pos: (i, 0))],
            scratch_shapes=[pltpu.VMEM((D_MODEL // 128, TM_A, 128), f32),
                            pltpu.VMEM((2, TM_A, D_MODEL), f32),
                            pltpu.SemaphoreType.DMA((2,))]),
        out_shape=[jax.ShapeDtypeStruct((T_ALL, NP_IN), f32),
                   jax.ShapeDtypeStruct((T_ALL // SEQ_PAD, D_MODEL), f32),
                   jax.ShapeDtypeStruct((T_ALL, D_MODEL), f32)],
        compiler_params=pltpu.CompilerParams(dimension_semantics=("arbitrary",),
                                             vmem_limit_bytes=VMEM_LIMIT),
        name="inproj_gather",
    )(pos, ys, nw, w_pad)


def _inproj(xs, nw, w_pad, l):
    split = len(xs) == 2
    npt = T_PROMPT // TM_A
    if split:
        x_specs = [pl.BlockSpec((TM_A, D_MODEL), lambda i: (jnp.minimum(i, npt - 1), 0)),
                   pl.BlockSpec((TM_A, D_MODEL), lambda i: (jnp.maximum(i - npt, 0), 0))]
    else:
        x_specs = [pl.BlockSpec((TM_A, D_MODEL), lambda i: (i, 0))]
    return pl.pallas_call(
        functools.partial(_inproj_kernel, split=split),
        grid=(T_ALL // TM_A,),
        in_specs=x_specs + [pl.BlockSpec((1, D_MODEL), lambda i: (0, 0)),
                            pl.BlockSpec((1, D_MODEL, NP_IN), lambda i: (l, 0, 0))],
        out_specs=[pl.BlockSpec((TM_A, NP_IN), lambda i: (i, 0)),
                   pl.BlockSpec((TM_A // SEQ_PAD, D_MODEL), lambda i: (i, 0))],
        out_shape=[jax.ShapeDtypeStruct((T_ALL, NP_IN), f32),
                   jax.ShapeDtypeStruct((T_ALL // SEQ_PAD, D_MODEL), f32)],
        scratch_shapes=[pltpu.VMEM((D_MODEL // 128, TM_A, 128), f32)],
        compiler_params=pltpu.CompilerParams(dimension_semantics=("arbitrary",),
                                             vmem_limit_bytes=VMEM_LIMIT),
        name="inproj",
    )(*xs, nw, w_pad)


def _mm_kernel(a_ref, b_ref, o_ref):
    o_ref[...] = jnp.dot(a_ref[...].astype(bf16), b_ref[...], preferred_element_type=f32)


def _mm(a, b):
    return pl.pallas_call(
        _mm_kernel,
        out_shape=jax.ShapeDtypeStruct((a.shape[0], b.shape[1]), f32),
        compiler_params=pltpu.CompilerParams(vmem_limit_bytes=VMEM_LIMIT),
        name="shift_proj",
    )(a, b)


def _cat(parts, axis):
    return parts[0] if len(parts) == 1 else jnp.concatenate(parts, axis=axis)


def _mixer_kernel(*refs, TB, C, G, per_seq, n_alias=0):
    it = iter(refs)
    proj_ref = next(it)
    if per_seq:
        gdn_in, conv_in, rwkv_in, pcf_in = next(it), next(it), next(it), next(it)
    convw_ref, v128_ref, v256_ref, sguw_ref, sgub_ref, mu_ref, wup_ref, aup_ref, gup_ref = (
        next(it) for _ in range(9))
    for _ in range(n_alias):
        next(it)
    mix_ref, gdn_out, conv_out, rwkv_out = next(it), next(it), next(it), next(it)
    cv_out = next(it) if per_seq else None
    xp, pp = next(it), next(it)

    nchunk = TB // C
    ngrp = C // G
    iters = int(math.log2(G)) - 1

    if per_seq:
        xp[pl.ds(0, 8), :] = jnp.zeros((8, 3 * D_A), f32)
        pp[pl.ds(0, 8), :] = jnp.zeros((8, NC_PAD), f32)
    else:
        @pl.when(pl.program_id(1) == 0)
        def _():
            xp[pl.ds(0, 8), :] = jnp.zeros((8, 3 * D_A), f32)
            pp[pl.ds(0, 8), :] = jnp.zeros((8, NC_PAD), f32)
            gdn_out[...] = jnp.zeros(gdn_out.shape, f32)
            rwkv_out[...] = jnp.zeros(rwkv_out.shape, f32)
    xp[pl.ds(8, TB), :] = proj_ref[:, pl.ds(OFF_Q, 3 * D_A)]
    pp[pl.ds(8, TB), :] = proj_ref[:, pl.ds(OFF_C, NC_PAD)]
    if per_seq:
        for s in range(TB // SEQ_PAD):
            r0 = 8 + s * SEQ_PAD
            xp[pl.ds(r0 + SEQ_LEAD - (CONV_W - 1), CONV_W - 1), :] = conv_in[s]
            pp[pl.ds(r0 + SEQ_LEAD - 1, 1), :] = pcf_in[s]

    rowi = lax.broadcasted_iota(i32, (TB, 1), 0)
    live = (rowi % SEQ_PAD) >= SEQ_LEAD if per_seq else None

    ii = lax.broadcasted_iota(i32, (C, C), 0)
    jj = lax.broadcasted_iota(i32, (C, C), 1)
    same = (ii // G) == (jj // G)
    causal = (ii >= jj) & same
    strict = (ii > jj) & same
    eye = ii == jj
    m_cum = causal.astype(bf16)
    m_grp = same.astype(bf16)

    def conv_cols(r0, c0):
        cs = pl.ds(c0, 128)
        acc = xp[pl.ds(r0 + 8, C), cs] * convw_ref[pl.ds(3, 1), cs]
        for j in range(CONV_W - 1):
            acc = acc + xp[pl.ds(r0 + 5 + j, C), cs] * convw_ref[pl.ds(j, 1), cs]
        return _silu(acc)

    alog_row = v128_ref[pl.ds(0, 1), :]
    dtb_row = v128_ref[pl.ds(1, 1), :]
    gnorm_w = v128_ref[pl.ds(2, 1), :]
    ln_w, ln_b = v256_ref[pl.ds(0, 1), :], v256_ref[pl.ds(1, 1), :]
    w0, a0 = v256_ref[pl.ds(2, 1), :], v256_ref[pl.ds(3, 1), :]
    k_k, k_a, r_k = v256_ref[pl.ds(4, 1), :], v256_ref[pl.ds(5, 1), :], v256_ref[pl.ds(6, 1), :]
    rln_w, rln_b = v256_ref[pl.ds(7, 1), :], v256_ref[pl.ds(8, 1), :]

    l64i = lax.broadcasted_iota(i32, (D_C, D_C), 0) // N_C
    l64j = lax.broadcasted_iota(i32, (D_C, D_C), 1) // N_C
    seg64 = (l64i == l64j).astype(bf16)

    gdn, rwk, lmats = [None] * nchunk, [None] * nchunk, [None] * (8 * nchunk)

    def phase1(c):
        R0 = c * C
        rows = pl.ds(R0, C)
        live_c = live[R0:R0 + C] if per_seq else None

        ba = proj_ref[rows, pl.ds(OFF_BA, 128)]
        beta_all = _sigmoid(ba)
        g_all = -jnp.exp(alog_row) * _softplus(ba + dtb_row)
        if per_seq:
            beta_all = jnp.where(live_c, beta_all, 0.0)
            g_all = jnp.where(live_c, g_all, 0.0)
        pcur = pp[pl.ds(R0 + 8, C), :]
        pprev = pp[pl.ds(R0 + 7, C), :]
        pm = pcur + mu_ref[...] * (pprev - pcur)
        r_ = pm[:, 0:D_C]
        kc = pm[:, D_C:2 * D_C]
        vc = pm[:, 2 * D_C:3 * D_C]
        wd = pm[:, 3 * D_C:3 * D_C + 128]
        ad = pm[:, 3 * D_C + 128:3 * D_C + 256]
        gd = pm[:, 3 * D_C + 256:3 * D_C + 384]
        wl_mm = _bdot(jnp.tanh(wd), wup_ref[...])
        a_mm = _bdot(ad, aup_ref[...])
        gate = _bdot(_sigmoid(gd), gup_ref[...])
        kk = kc * k_k
        kk_ss = _dot01_right(kk * kk, seg64)
        yield

        gc_all = _dot01_left(m_cum, g_all)
        total = lambda cum, x: _dot01_left(m_grp, x)
        gl_all = total(gc_all, g_all)
        w_log = -_softplus(-(w0 + wl_mm)) - 0.5
        logw = -jnp.exp(w_log)
        a_ = _sigmoid(a0 + a_mm)
        kk = kk * lax.rsqrt(kk_ss + 1e-6)
        kc2 = kc * (1.0 + (a_ - 1.0) * k_a)
        if per_seq:
            logw = jnp.where(live_c, logw, 0.0)
            kk = jnp.where(live_c, kk, 0.0)
            kc2 = jnp.where(live_c, kc2, 0.0)
        b_ = kk * a_
        Gc = _dot01_left(m_cum, logw)
        Gl = total(Gc, logw)
        bonus_ss = _dot01_right(r_ * kc2 * r_k, seg64)
        yield

        heads = []
        for h in range(H_A):
            q = conv_cols(R0, OFF_Q + h * HD_A)
            k = conv_cols(R0, OFF_K + h * HD_A)
            v = conv_cols(R0, OFF_V + h * HD_A)
            q = q * lax.rsqrt(jnp.sum(q * q, axis=-1, keepdims=True) + 1e-6) * (HD_A ** -0.5)
            k = k * lax.rsqrt(jnp.sum(k * k, axis=-1, keepdims=True) + 1e-6)
            if per_seq:
                k = jnp.where(live_c, k, 0.0)
            beta = beta_all[:, h:h + 1]
            gcol = gc_all[:, 4 + h:5 + h]
            glr = gl_all[:, 4 + h:5 + h]
            grow = jnp.sum(jnp.where(eye, jnp.broadcast_to(gcol, (C, C)), 0.0), axis=0, keepdims=True)
            decay = jnp.where(causal, jnp.exp(jnp.where(causal, gcol - grow, 0.0)), 0.0)
            eg = jnp.exp(gcol)
            kb = k * beta
            kq = _bdot(jnp.concatenate([kb, q], axis=0), k, NT)
            heads.append(dict(kq=kq, decay=decay,
                              rhs=jnp.concatenate([v * beta, kb * eg], axis=1),
                              q_dec=q * eg, k_dec=k * jnp.exp(glr - gcol), glr=glr))
            yield
        gdn[c] = heads
        e_neg = jnp.exp(-Gc)
        e_rem = jnp.exp(Gl - Gc)
        rG = r_ * jnp.exp(Gc)
        kkG = kk * jnp.exp(Gc - logw)
        kN = kc2 * e_neg
        bN = b_ * e_neg
        aalls = []
        for h in range(H_C):
            hs = slice(h * N_C, (h + 1) * N_C)
            aalls.append(_bdot(jnp.concatenate([kkG[:, hs], rG[:, hs]], axis=0),
                               jnp.concatenate([bN[:, hs], kN[:, hs]], axis=0), NT))
        yield

        for h in range(H_A):
            d = gdn[c][h]
            kq = d.pop('kq')
            decay = d.pop('decay')
            lmats[8 * c + h] = jnp.where(strict, kq[:C] * decay, 0.0)
            d['attn'] = kq[C:] * decay
        yield
        heads = []
        for h in range(H_C):
            hs = slice(h * N_C, (h + 1) * N_C)
            aall = aalls[h]
            lmats[8 * c + H_A + h] = jnp.where(strict, aall[:C, :C], 0.0)
            akk_k = jnp.where(strict, aall[:C, C:], 0.0)
            ar = jnp.concatenate([jnp.where(causal, aall[C:, C:], 0.0),
                                  -jnp.where(causal, aall[C:, :C], 0.0)], axis=1)
            heads.append(dict(x1=_bdot(akk_k, vc[:, hs]), ar=ar))
        rwk[c] = dict(heads=heads, vc=vc, rG=rG, kkG=kkG, kdec=kc2 * e_rem, bdec=b_ * e_rem,
                      e_last=jnp.exp(Gl), gate=gate, bonus=bonus_ss * vc)
        yield

    def par(*gens):
        gens = list(gens)
        while gens:
            alive = []
            for g in gens:
                try:
                    next(g)
                    alive.append(g)
                except StopIteration:
                    pass
            gens = alive
            yield

    def seq(*gens):
        for g in gens:
            yield from g

    def lockstep(gens):
        for _ in par(*gens):
            pass

    nmats = [None] * (8 * nchunk)

    def inverses(chunks):
        idx = [8 * c + k for c in chunks for k in range(8)]
        bmm = lambda a, b: lax.dot_general(a.astype(bf16), b.astype(bf16), BNN, preferred_element_type=f32)
        pack = 2 * C <= 128
        if pack:
            N = -jnp.stack([jnp.concatenate([lmats[i], lmats[j]], axis=1) for i, j in zip(idx[0::2], idx[1::2])])
            left = lax.broadcasted_iota(i32, N.shape, 2) < C
            rhs = lambda X: jnp.concatenate([jnp.where(left, X, 0.0), jnp.where(left, 0.0, X)], axis=1)
        else:
            N = -jnp.stack([lmats[i] for i in idx])
            rhs = lambda X: X
        Q = bmm(N, rhs(N))
        yield
        for _ in range(iters - 1):
            R = bmm(jnp.concatenate([N, Q], axis=1), rhs(Q))
            N = N + Q + R[:, :C]
            Q = R[:, C:]
            yield
        N = N + Q + bmm(N, rhs(Q))
        if pack:
            for p, (i, j) in enumerate(zip(idx[0::2], idx[1::2])):
                nmats[i], nmats[j] = N[p][:, :C], N[p][:, C:]
        else:
            for p, i in enumerate(idx):
                nmats[i] = N[p]
        yield

    def phase3(c):
        for h in range(H_A):
            d = gdn[c][h]
            sol = d['rhs'] + _bdot(nmats[8 * c + h], d['rhs'])
            d['u'], d['w'] = sol[:, :HD_A], sol[:, HD_A:]
        for h in range(H_C):
            hs = slice(h * N_C, (h + 1) * N_C)
            d = rwk[c]['heads'][h]
            both = jnp.concatenate([d['x1'], rwk[c]['kkG'][:, hs]], axis=1)
            both = both + _bdot(nmats[8 * c + H_A + h], both)
            d['u_p'], d['w_p'] = both[:, :N_C], both[:, N_C:]
            d['both'] = both
        yield
        if per_seq:
            return
        for h in range(H_A):
            d = gdn[c][h]
            wu = jnp.concatenate([d['w'], d['u']], axis=1)
            a_wu = _bdot(d['attn'], wu)
            k_wu = _bdot(d['k_dec'], wu, TN)
            d['qe'] = d['q_dec'] - a_wu[:, :HD_A]
            d['o0'] = a_wu[:, HD_A:]
            d['m'] = -k_wu[:, :HD_A]
            d['b'] = k_wu[:, HD_A:]
        yield
        for h in range(H_C):
            hs = slice(h * N_C, (h + 1) * N_C)
            rc = rwk[c]
            d = rc['heads'][h]
            V = rc['vc'][:, hs]
            x = _bdot(d['ar'][:, C:], d['both'])
            d['y0'] = _bdot(d['ar'][:, :C], V) + x[:, :N_C]
            d['re'] = rc['rG'][:, hs] + x[:, N_C:]
            d['m'] = _bdot(d['w_p'], rc['bdec'][:, hs], TN)
            d['b'] = _bdot(jnp.concatenate([V, -d['u_p']], axis=0),
                           jnp.concatenate([rc['kdec'][:, hs], rc['bdec'][:, hs]], axis=0), TN)
        yield

    def gdn_head_seq(c, h):
        rows = pl.ds(c * C, C)
        d = gdn[c][h]
        S = gdn_out[0, h]
        Sb = S.astype(bf16)
        gdn_out[0, h] = S * jnp.exp(d['glr'][C - 1:C, :]) + _bdot(d['m'], Sb) + d['b']
        o = d['o0'] + _bdot(d['qe'], Sb)
        yield
        o = o * lax.rsqrt(jnp.mean(o * o, axis=-1, keepdims=True) + NORM_EPS) * gnorm_w
        z = proj_ref[rows, pl.ds(OFF_Z + h * HD_A, HD_A)]
        mix_ref[rows, pl.ds(h * HD_A, HD_A)] = o * _silu(z)

    def rwkv_head_seq(c, h, ys):
        rc = rwk[c]
        hs = slice(h * N_C, (h + 1) * N_C)
        d = rc['heads'][h]
        S = rwkv_out[0, h]
        Sb = S.astype(bf16)
        rwkv_out[0, h] = S * rc['e_last'][C - 1:C, hs] + d['b'] - _bdot(Sb, d['m'])
        ys[h] = d['y0'] + _bdot(d['re'], Sb, NT)
        yield

    def gdn_head(c, h):
        rows = pl.ds(c * C, C)
        d = gdn[c][h]
        rq, vn, s_old = [], [], []
        for s in range(ngrp):
            gs = slice(s * G, (s + 1) * G)
            S = gdn_in[s, h] if per_seq else gdn_out[0, h]
            s_old.append(S)
            R = _bdot(jnp.concatenate([d['w'][gs], d['q_dec'][gs]], axis=0), S)
            vn.append(d['u'][gs] - R[:G])
            rq.append(R[G:])
        yield
        v_new = _cat(vn, 0)
        o = _cat(rq, 0) + _bdot(d['attn'], v_new)
        for s in range(ngrp):
            gs = slice(s * G, (s + 1) * G)
            g_last = jnp.exp(d['glr'][s * G + G - 1:s * G + G, :])
            S_new = s_old[s] * g_last + _bdot(d['k_dec'][gs], v_new[gs], TN)
            if per_seq:
                gdn_out[s, h] = S_new
            else:
                gdn_out[0, h] = S_new
        yield
        o = o * lax.rsqrt(jnp.mean(o * o, axis=-1, keepdims=True) + NORM_EPS) * gnorm_w
        z = proj_ref[rows, pl.ds(OFF_Z + h * HD_A, HD_A)]
        mix_ref[rows, pl.ds(h * HD_A, HD_A)] = o * _silu(z)

    def rwkv_head(c, h, ys):
        rc = rwk[c]
        hs = slice(h * N_C, (h + 1) * N_C)
        d = rc['heads'][h]
        V = rc['vc'][:, hs]
        rr, ut, s_old = [], [], []
        for s in range(ngrp):
            gs = slice(s * G, (s + 1) * G)
            S = rwkv_in[s, h] if per_seq else rwkv_out[0, h]
            s_old.append(S)
            R = _bdot(jnp.concatenate([d['w_p'][gs], rc['rG'][gs, hs]], axis=0), S, NT)
            ut.append(d['u_p'][gs] + R[:G])
            rr.append(R[G:])
        yield
        Ut = _cat(ut, 0)
        ys[h] = _cat(rr, 0) + _bdot(d['ar'], jnp.concatenate([V, Ut], axis=0))
        for s in range(ngrp):
            gs = slice(s * G, (s + 1) * G)
            upd = _bdot(jnp.concatenate([V[gs], -Ut[gs]], axis=0),
                        jnp.concatenate([rc['kdec'][gs, hs], rc['bdec'][gs, hs]], axis=0), TN)
            S_new = s_old[s] * rc['e_last'][s * G + G - 1:s * G + G, hs] + upd
            if per_seq:
                rwkv_out[s, h] = S_new
            else:
                rwkv_out[0, h] = S_new
        yield

    def phase4(c):
        rows = pl.ds(c * C, C)
        rc = rwk[c]
        ys = [None] * H_C
        if per_seq:
            heads = [gdn_head(c, h) for h in range(H_A)] + [rwkv_head(c, h, ys) for h in range(H_C)]
        else:
            heads = [gdn_head_seq(c, h) for h in range(H_A)] + [rwkv_head_seq(c, h, ys) for h in range(H_C)]
        yield from par(*heads)
        y = jnp.concatenate(ys, axis=1)
        mu_y = _dot01_right(y, seg64) * (1.0 / N_C)
        dy = y - mu_y
        var_y = _dot01_right(dy * dy, seg64) * (1.0 / N_C)
        y = dy * lax.rsqrt(var_y + GN_EPS) * rln_w + rln_b
        mix_ref[rows, pl.ds(D_A + D_B, D_C)] = (y + rc['bonus']) * rc['gate']
        yield

    def sgu(sc):
        rows = pl.ds(sc * SGU_CHUNK, SGU_CHUNK)
        ug = _gelu(proj_ref[rows, pl.ds(OFF_U, D_B)])
        vs = _gelu(proj_ref[rows, pl.ds(OFF_VB, D_B)])
        yield
        mu_v = _dot01_right(vs, seg64) * (1.0 / DH_B)
        dv = vs - mu_v
        var_v = _dot01_right(dv * dv, seg64) * (1.0 / DH_B)
        vs = dv * lax.rsqrt(var_v + LN_EPS) * ln_w + ln_b
        if per_seq:
            cv_out[rows, :] = vs
        yield
        vsb = vs.astype(bf16)
        outs = [lax.dot_general(sguw_ref[h], vsb[:, h * DH_B:(h + 1) * DH_B], NN, preferred_element_type=f32)
                for h in range(4)]
        mixed = jnp.concatenate(outs, axis=1) + sgub_ref[...]
        mix_ref[rows, pl.ds(D_A, D_B)] = ug * mixed
        yield

    first = list(range(nchunk))[:max(nchunk // 2, 1)]
    second = list(range(nchunk))[len(first):]
    sgus = [sgu(sc) for sc in range(TB // SGU_CHUNK)]
    lockstep([phase1(c) for c in first])
    lockstep([phase1(c) for c in second]
             + [seq(inverses(first), par(*[phase3(c) for c in first]))])
    if second:
        lockstep([seq(inverses(second), par(*[phase3(c) for c in second])),
                  seq(*[phase4(c) for c in first]), seq(*sgus)])
        lockstep([seq(*[phase4(c) for c in second])])
    else:
        lockstep([seq(*[phase4(c) for c in first]), seq(*sgus)])

    if per_seq:
        for s in range(TB // SEQ_PAD):
            conv_out[s] = xp[pl.ds(8 + (s + 1) * SEQ_PAD - (CONV_W - 1), CONV_W - 1), :]
    else:
        conv_out[0] = xp[pl.ds(8 + TB - (CONV_W - 1), CONV_W - 1), :]
        xp[pl.ds(0, 8), :] = xp[pl.ds(TB, 8), :]
        pp[pl.ds(0, 8), :] = pp[pl.ds(TB, 8), :]


def _mixer_weight_specs(nidx):
    z2 = (lambda b, j: (0, 0)) if nidx == 2 else (lambda i: (0, 0))
    z3 = (lambda b, j: (0, 0, 0)) if nidx == 2 else (lambda i: (0, 0, 0))
    return [pl.BlockSpec((CONV_W, 3 * D_A), z2),
            pl.BlockSpec((8, 128), z2),
            pl.BlockSpec((16, D_C), z2),
            pl.BlockSpec((4, SGU_CHUNK, SGU_CHUNK), z3),
            pl.BlockSpec((SGU_CHUNK, D_B), z2),
            pl.BlockSpec((1, NC_PAD), z2),
            pl.BlockSpec((128, D_C), z2),
            pl.BlockSpec((128, D_C), z2),
            pl.BlockSpec((128, D_C), z2)]


def _mixer_prompt(proj, mw):
    nj = SEQ // TB_P
    return pl.pallas_call(
        functools.partial(_mixer_kernel, TB=TB_P, C=GDN_CHUNK, G=GDN_CHUNK, per_seq=False),
        grid=(BATCH, nj),
        in_specs=[pl.BlockSpec((TB_P, NP_IN), lambda b, j: (b * nj + j, 0))] + _mixer_weight_specs(2),
        out_specs=[pl.BlockSpec((TB_P, D_MODEL), lambda b, j: (b * nj + j, 0)),
                   pl.BlockSpec((1, H_A, HD_A, HD_A), lambda b, j: (b, 0, 0, 0)),
                   pl.BlockSpec((1, CONV_W - 1, 3 * D_A), lambda b, j: (b, 0, 0)),
                   pl.BlockSpec((1, H_C, N_C, N_C), lambda b, j: (b, 0, 0, 0))],
        out_shape=[jax.ShapeDtypeStruct((T_PROMPT, D_MODEL), f32),
                   jax.ShapeDtypeStruct((BATCH, H_A, HD_A, HD_A), f32),
                   jax.ShapeDtypeStruct((BATCH, CONV_W - 1, 3 * D_A), f32),
                   jax.ShapeDtypeStruct((BATCH, H_C, N_C, N_C), f32)],
        scratch_shapes=[pltpu.VMEM((TB_P + 8, 3 * D_A), f32), pltpu.VMEM((TB_P + 8, NC_PAD), f32)],
        compiler_params=pltpu.CompilerParams(dimension_semantics=("arbitrary", "arbitrary"),
                                             vmem_limit_bytes=VMEM_LIMIT),
        name="mixer_prompt",
    )(proj, *mw)


def _mixer_sample(proj, s_gdn, s_conv, s_rwkv, pcf, mw, l, prev_states):
    nseq = TB_S // SEQ_PAD
    base = T_PROMPT // TB_S
    n_in = 5 + len(mw)
    lay4 = lambda i: (l, i, 0, 0, 0)
    lay3 = lambda i: (l, i, 0, 0)
    return pl.pallas_call(
        functools.partial(_mixer_kernel, TB=TB_S, C=TB_S, G=SEQ_PAD, per_seq=True, n_alias=len(prev_states)),
        grid=(T_SAMPLE // TB_S,),
        in_specs=[pl.BlockSpec((TB_S, NP_IN), lambda i: (base + i, 0)),
                  pl.BlockSpec((None, nseq, H_A, HD_A, HD_A), lay4),
                  pl.BlockSpec((None, nseq, CONV_W - 1, 3 * D_A), lay3),
                  pl.BlockSpec((None, nseq, H_C, N_C, N_C), lay4),
                  pl.BlockSpec((nseq, 1, NC_PAD), lambda i: (i, 0, 0))] + _mixer_weight_specs(1)
                 + [pl.BlockSpec(memory_space=pl.ANY)] * len(prev_states),
        out_specs=[pl.BlockSpec((TB_S, D_MODEL), lambda i: (i, 0)),
                   pl.BlockSpec((None, nseq, H_A, HD_A, HD_A), lay4),
                   pl.BlockSpec((nseq, CONV_W - 1, 3 * D_A), lambda i: (i, 0, 0)),
                   pl.BlockSpec((None, nseq, H_C, N_C, N_C), lay4),
                   pl.BlockSpec((TB_S, D_B), lambda i: (i, 0))],
        out_shape=[jax.ShapeDtypeStruct((T_SAMPLE, D_MODEL), f32),
                   jax.ShapeDtypeStruct((DEPTH, DEC_BATCH, H_A, HD_A, HD_A), f32),
                   jax.ShapeDtypeStruct((DEC_BATCH, CONV_W - 1, 3 * D_A), f32),
                   jax.ShapeDtypeStruct((DEPTH, DEC_BATCH, H_C, N_C, N_C), f32),
                   jax.ShapeDtypeStruct((T_SAMPLE, D_B), f32)],
        scratch_shapes=[pltpu.VMEM((TB_S + 8, 3 * D_A), f32), pltpu.VMEM((TB_S + 8, NC_PAD), f32)],
        input_output_aliases={n_in + k: o for k, o in enumerate((1, 3)[:len(prev_states)])},
        compiler_params=pltpu.CompilerParams(dimension_semantics=("arbitrary",),
                                             vmem_limit_bytes=VMEM_LIMIT),
        name="mixer_sample",
    )(proj, s_gdn, s_conv, s_rwkv, pcf, *mw, *prev_states)


TM_C = 512


def _outproj_kernel(*refs, split):
    i = pl.program_id(0)
    in_prompt = i < T_PROMPT // TM_C
    if split:
        xp_ref, xs_ref = refs[:2]
        x = jnp.where(in_prompt, xp_ref[...], xs_ref[...])
    else:
        x = refs[0][...]
    mixp_ref, mixs_ref, wout_ref, nw_ref, rwh_ref, rwl_ref, rb_ref, xg_ref, cnt_ref, run_ref = refs[-10:]

    @pl.when(i == 0)
    def _():
        run_ref[...] = jnp.zeros(run_ref.shape, f32)

    mix = jnp.where(in_prompt, mixp_ref[...], mixs_ref[...])
    x2 = x + jnp.dot(mix.astype(bf16), wout_ref[0], preferred_element_type=f32)
    xg_ref[:, pl.ds(0, D_MODEL)] = x2
    h2 = _rms(x2, nw_ref[...])
    hh, hl = _split2(h2)
    d = lambda a, b: jnp.dot(a, b, preferred_element_type=f32)
    both = d(hh, jnp.concatenate([rwh_ref[...], rwl_ref[...]], axis=1))
    logits = both[:, :128] + both[:, 128:] + d(hl, rwh_ref[...]) + rb_ref[...]

    lane = lax.broadcasted_iota(i32, logits.shape, 1).astype(f32)
    neg = jnp.float32(-jnp.inf)
    is_g = lane < float(N_GROUPS)
    gl = jnp.where(is_g, logits, neg)
    gmax = jnp.max(gl, axis=-1, keepdims=True)
    gsel = jnp.min(jnp.where(gl == gmax, lane, 128.0), axis=-1, keepdims=True)
    gw = 1.0 / jnp.sum(jnp.where(is_g, jnp.exp(jnp.where(is_g, logits - gmax, 0.0)), 0.0),
                       axis=-1, keepdims=True)
    lo = LANE_E0 + float(EPG) * gsel
    in_grp = (lane >= lo) & (lane < lo + float(EPG))
    el = jnp.where(in_grp, logits, neg)
    t1 = jnp.max(el, axis=-1, keepdims=True)
    i1 = jnp.min(jnp.where(el == t1, lane, 128.0), axis=-1, keepdims=True)
    el2 = jnp.where(lane == i1, neg, el)
    t2 = jnp.max(el2, axis=-1, keepdims=True)
    i2 = jnp.min(jnp.where(el2 == t2, lane, 128.0), axis=-1, keepdims=True)
    e2 = jnp.exp(t2 - t1)
    den = 1.0 + e2
    gates = jnp.where(lane == i1, gw / den, 0.0) + jnp.where(lane == i2, gw * e2 / den, 0.0)

    ea, eb = i1 - lo, i2 - lo
    e_lo, e_hi = jnp.minimum(ea, eb), jnp.maximum(ea, eb)
    bucket = float(N_PAIRS) * gsel + e_lo * (7.0 - e_lo) * 0.5 + (e_hi - e_lo - 1.0)
    onehot = jnp.where(lane == bucket, 1.0, 0.0)
    ri = lax.broadcasted_iota(i32, (TM_C, TM_C), 0)
    ci = lax.broadcasted_iota(i32, (TM_C, TM_C), 1)
    before = lax.dot_general((ri > ci).astype(bf16), onehot.astype(bf16), NN, preferred_element_type=f32)
    rank = jnp.sum((before + run_ref[...]) * onehot, axis=-1, keepdims=True)
    run_ref[...] += jnp.sum(onehot, axis=0, keepdims=True)
    cnt_ref[...] = jnp.broadcast_to(run_ref[...], cnt_ref.shape)
    xg_ref[:, pl.ds(D_MODEL, 128)] = (gates + jnp.where(lane == float(LANE_G), bucket, 0.0)
                                      + jnp.where(lane == float(LANE_RANK), rank, 0.0))


def _outproj(xs, mix_p, mix_s, wout, nw, rwh, rwl, rb, l):
    row = lambda i: (i, 0)
    fix = lambda i: (0, 0)
    npt = T_PROMPT // TM_C
    p_rows = pl.BlockSpec((TM_C, D_MODEL), lambda i: (jnp.minimum(i, npt - 1), 0))
    s_rows = pl.BlockSpec((TM_C, D_MODEL), lambda i: (jnp.maximum(i - npt, 0), 0))
    split = len(xs) == 2
    return pl.pallas_call(
        functools.partial(_outproj_kernel, split=split),
        grid=(T_ALL // TM_C,),
        in_specs=([p_rows, s_rows] if split else [pl.BlockSpec((TM_C, D_MODEL), row)]) + [
                  p_rows, s_rows,
                  pl.BlockSpec((1, D_MODEL, D_MODEL), lambda i: (l, 0, 0)), pl.BlockSpec((1, D_MODEL), fix),
                  pl.BlockSpec((D_MODEL, 128), fix), pl.BlockSpec((D_MODEL, 128), fix),
                  pl.BlockSpec((1, 128), fix)],
        out_specs=[pl.BlockSpec((TM_C, XG_W), row), pl.BlockSpec((8, 128), fix)],
        out_shape=[jax.ShapeDtypeStruct((T_ALL, XG_W), f32), jax.ShapeDtypeStruct((8, 128), f32)],
        scratch_shapes=[pltpu.VMEM((1, 128), f32)],
        compiler_params=pltpu.CompilerParams(dimension_semantics=("arbitrary",),
                                             vmem_limit_bytes=VMEM_LIMIT),
        name="outproj_router",
    )(*xs, mix_p, mix_s, wout, nw, rwh, rwl, rb)


TM_E = 512
NT_E = T_ALL // TM_E + N_GROUPS
T_SORT = NT_E * TM_E
DMA_UNROLL = 8


def _invert_kernel(pos_ref, pad_lo_ref, pad_hi_ref, src_ref):
    for g in range(N_GROUPS):
        def fill(p, c):
            src_ref[p] = T_ALL - 1
            return c
        lax.fori_loop(pad_lo_ref[g], pad_hi_ref[g], fill, 0)

    def body(t, c):
        src_ref[pos_ref[t]] = t
        return c
    lax.fori_loop(0, T_ALL, body, 0, unroll=DMA_UNROLL)


def _invert(pos, pad_lo, pad_hi):
    smem = pl.BlockSpec(memory_space=pltpu.SMEM)
    return pl.pallas_call(
        _invert_kernel,
        in_specs=[smem, smem, smem],
        out_specs=smem,
        out_shape=jax.ShapeDtypeStruct((T_SORT,), i32),
        name="moe_invert",
    )(pos, pad_lo, pad_hi)


def _gather_rows(pos_ref, base, ys_hbm, o_ref, sem):
    def issue(r, c):
        p = pos_ref[base + r]
        pltpu.make_async_copy(ys_hbm.at[pl.ds(p, 1), :], o_ref.at[pl.ds(r, 1), :], sem).start()
        return c

    lax.fori_loop(0, TM_E, issue, 0, unroll=DMA_UNROLL)
    pltpu.make_async_copy(ys_hbm.at[pl.ds(0, TM_E), :], o_ref, sem).wait()


def _combine_kernel(pos_ref, ys_hbm, o_ref, sem):
    _gather_rows(pos_ref, pl.program_id(0) * TM_E, ys_hbm, o_ref, sem)


def _combine_split_kernel(pos_ref, ys_hbm, op_ref, os_ref, sem):
    i = pl.program_id(0)

    @pl.when(i < T_PROMPT // TM_E)
    def _():
        _gather_rows(pos_ref, i * TM_E, ys_hbm, op_ref, sem)

    @pl.when(i >= T_PROMPT // TM_E)
    def _():
        _gather_rows(pos_ref, i * TM_E, ys_hbm, os_ref, sem)


def _combine(pos, ys, split):
    npt = T_PROMPT // TM_E
    if split:
        out_specs = [pl.BlockSpec((TM_E, D_MODEL), lambda i, pos: (jnp.minimum(i, npt - 1), 0)),
                     pl.BlockSpec((TM_E, D_MODEL), lambda i, pos: (jnp.maximum(i - npt, 0), 0))]
        out_shape = [jax.ShapeDtypeStruct((T_PROMPT, D_MODEL), f32),
                     jax.ShapeDtypeStruct((T_SAMPLE, D_MODEL), f32)]
    else:
        out_specs = pl.BlockSpec((TM_E, D_MODEL), lambda i, pos: (i, 0))
        out_shape = jax.ShapeDtypeStruct((T_ALL, D_MODEL), f32)
    return pl.pallas_call(
        _combine_split_kernel if split else _combine_kernel,
        grid_spec=pltpu.PrefetchScalarGridSpec(
            num_scalar_prefetch=1,
            grid=(T_ALL // TM_E,),
            in_specs=[pl.BlockSpec(memory_space=pl.ANY)],
            out_specs=out_specs,
            scratch_shapes=[pltpu.SemaphoreType.DMA(())]),
        out_shape=out_shape,
        compiler_params=pltpu.CompilerParams(dimension_semantics=("arbitrary",),
                                             vmem_limit_bytes=VMEM_LIMIT),
        name="moe_combine",
    )(pos, ys)


ROWS_E = TM_E // EPG
N_XBUF = 3


def _experts_kernel(tg_ref, tv_ref, ti_ref, tu_ref, tw_ref, src_ref, xg_hbm, nw_ref, wg_ref, wu_ref, wd_ref, nf_ref,
                    ys_ref, xbuf, hbuf, ybuf, gsem, *, final_norm):
    del tw_ref
    t = pl.program_id(0)
    e = pl.program_id(1)
    valid = tv_ref[t] == 1
    used = tu_ref[t * EPG + e] == 1
    slot = t % N_XBUF
    ahead = ti_ref[jnp.minimum(t + 2, NT_E - 1)]
    aslot = (t + 2) % N_XBUF
    is_last = ti_ref[jnp.minimum(t + 1, NT_E - 1)] == t

    @pl.when((t == 0) & (e == 0))
    def _():
        _start_row_copies(src_ref, 0, xg_hbm, xbuf.at[0], gsem.at[0], 0, TM_E, inline=False)
        _start_row_copies(src_ref, ti_ref[1] * TM_E, xg_hbm, xbuf.at[1], gsem.at[1], 0, TM_E, inline=False)

    @pl.when(valid & (e == 0))
    def _():
        _wait_row_copies(xg_hbm, xbuf.at[slot], gsem.at[slot])
        x2 = xbuf[slot, :, pl.ds(0, D_MODEL)]
        hbuf[...] = _rms(x2, nw_ref[...]).astype(bf16)
        ybuf[...] = x2

    @pl.when(valid & jnp.logical_not(used))
    def _():
        _start_row_copies(src_ref, ahead * TM_E, xg_hbm, xbuf.at[aslot], gsem.at[aslot],
                          e * ROWS_E, ROWS_E, inline=False)

    @pl.when(valid & used)
    def _():
        _start_row_copies(src_ref, ahead * TM_E, xg_hbm, xbuf.at[aslot], gsem.at[aslot],
                          e * ROWS_E, ROWS_E, inline=True)
        hb = hbuf[...]
        he = (_silu(jnp.dot(hb, wg_ref[0, 0], preferred_element_type=f32))
              * jnp.dot(hb, wu_ref[0, 0], preferred_element_type=f32))
        yd = jnp.dot(he.astype(bf16), wd_ref[0, 0], preferred_element_type=f32)
        g = xbuf[slot, :, pl.ds(D_MODEL, 128)]
        lane = lax.broadcasted_iota(i32, g.shape, 1)
        gcol = jnp.sum(jnp.where(lane == LANE_E0 + EPG * tg_ref[t] + e, g, 0.0), axis=-1, keepdims=True)
        ybuf[...] += gcol * yd

    @pl.when(e == EPG - 1)
    def _():
        y = ybuf[...]
        if final_norm:
            y = _rms(y, nf_ref[...])
        ys_ref[...] = y

    @pl.when(valid & is_last & (e == EPG - 1))
    def _():
        for k in (1, 2):
            s = (t + k) % N_XBUF
            _wait_row_copies(xg_hbm, xbuf.at[s], gsem.at[s])


def _experts(tile_group, tile_valid, tile_idx, tile_used, tile_wexp, src, xg, nw, wg, wu, wd, nf, l, final_norm):
    wsel = lambda t, e, tg, tv, ti, tu, tw, src: (l, tg[t] * EPG + tw[t * EPG + e], 0, 0)
    return pl.pallas_call(
        functools.partial(_experts_kernel, final_norm=final_norm),
        grid_spec=pltpu.PrefetchScalarGridSpec(
            num_scalar_prefetch=6,
            grid=(NT_E, EPG),
            in_specs=[pl.BlockSpec(memory_space=pl.ANY),
                      pl.BlockSpec((1, D_MODEL), lambda t, e, tg, tv, ti, tu, tw, src: (0, 0)),
                      pl.BlockSpec((1, 1, D_MODEL, D_FF_E), wsel),
                      pl.BlockSpec((1, 1, D_MODEL, D_FF_E), wsel),
                      pl.BlockSpec((1, 1, D_FF_E, D_MODEL), wsel),
                      pl.BlockSpec((1, D_MODEL), lambda t, e, tg, tv, ti, tu, tw, src: (0, 0))],
            out_specs=pl.BlockSpec((TM_E, D_MODEL), lambda t, e, tg, tv, ti, tu, tw, src: (ti[t], 0)),
            scratch_shapes=[pltpu.VMEM((N_XBUF, TM_E, XG_W), f32), pltpu.VMEM((TM_E, D_MODEL), bf16),
                            pltpu.VMEM((TM_E, D_MODEL), f32), pltpu.SemaphoreType.DMA((N_XBUF,))]),
        out_shape=jax.ShapeDtypeStruct((T_SORT, D_MODEL), f32),
        compiler_params=pltpu.CompilerParams(dimension_semantics=("arbitrary", "arbitrary"),
                                             vmem_limit_bytes=VMEM_LIMIT),
        name="moe_experts",
    )(tile_group, tile_valid, tile_idx, tile_used, tile_wexp, src, xg, nw, wg, wu, wd, nf)


def _route_meta(xg, cnt_rows):
    b = xg[:, D_MODEL + LANE_G].astype(i32)
    rank = xg[:, D_MODEL + LANE_RANK].astype(i32)
    bcnt = cnt_rows[0, :N_BUCKETS].astype(i32).reshape(N_GROUPS, N_PAIRS)
    cnt = jnp.sum(bcnt, axis=1)
    padded = ((cnt + TM_E - 1) // TM_E) * TM_E
    off_end = jnp.cumsum(padded)
    off = off_end - padded
    bstart = (off[:, None] + jnp.cumsum(bcnt, axis=1) - bcnt).reshape(N_BUCKETS)
    bend = bstart + bcnt.reshape(N_BUCKETS)
    bid = jnp.arange(N_BUCKETS, dtype=i32)
    pos = rank + jnp.sum(jnp.where(b[:, None] == bid[None, :], bstart[None, :], 0), axis=1)
    tile_start = jnp.arange(NT_E, dtype=i32) * TM_E
    tile_group = jnp.minimum(jnp.sum((tile_start[:, None] >= off_end[None, :]).astype(i32), axis=1), N_GROUPS - 1)
    tile_valid = (tile_start < off_end[-1]).astype(i32)
    n_used = off_end[-1] // TM_E
    tile_idx = jnp.minimum(jnp.arange(NT_E, dtype=i32), n_used - 1)
    tile_group = jnp.take(tile_group, tile_idx)
    ts = (tile_idx * TM_E)[:, None]
    overlap = (bstart[None, :] < ts + TM_E) & (bend[None, :] > ts) & (bend > bstart)[None, :]
    pair_has = jnp.array([[int(e in p) for e in range(EPG)] for p in PAIRS] * N_GROUPS, dtype=i32)
    used = jnp.sum(overlap.astype(i32)[:, :, None] * pair_has[None, :, :], axis=1) > 0
    eidx = jnp.arange(EPG, dtype=i32)[None, :]
    prev = lax.cummax(jnp.where(used, eidx, -1), axis=1)
    nxt = lax.cummin(jnp.where(used, eidx, EPG), axis=1, reverse=True)
    wexp = jnp.where(prev >= 0, prev, jnp.minimum(nxt, EPG - 1))
    return (pos, tile_group, tile_valid, tile_idx, off + cnt, off_end,
            used.astype(i32).reshape(-1), wexp.reshape(-1))


def _pad_cols(a, n):
    return jnp.pad(a, ((0, 0), (0, n - a.shape[1])))


def _pad_rows(a, n):
    return jnp.pad(a, ((0, n - a.shape[0]), (0, 0)))


def _prep_w_in(w):
    pad_last = lambda a, n: jnp.pad(a, ((0, 0), (0, 0), (0, n - a.shape[-1])))
    c = w[..., 2568:]
    parts = [w[..., 0:2048], pad_last(w[..., 2048:2056], 128), w[..., 2056:2568],
             c[..., 0:768], pad_last(c[..., 768:832], 128), pad_last(c[..., 832:896], 128), c[..., 896:1024]]
    return jnp.concatenate(parts, axis=-1).astype(bf16)


def _prep_mu(mu):
    m = mu[None, :]
    return jnp.concatenate([m[:, 0:768], _pad_cols(m[:, 768:832], 128), _pad_cols(m[:, 832:896], 128),
                            m[:, 896:1024]], axis=1)


def _sgu_mats(sgu_w, sgu_b):
    t = jnp.arange(SGU_CHUNK)
    wm = jnp.where(t[:, None] >= t[None, :], sgu_w, 0.0)
    bias_p = jnp.repeat(jnp.transpose(sgu_b), DH_B, axis=1)
    small = jnp.zeros((4, SEQ_PAD, SEQ_PAD), f32).at[:, SEQ_LEAD:, SEQ_LEAD:].set(wm[:, :DEC_SEQ, :DEC_SEQ])
    eye16 = jnp.eye(TB_S // SEQ_PAD, dtype=f32)
    wm_s = jnp.einsum('ab,hij->haibj', eye16, small).reshape(4, TB_S, TB_S)
    bias_small = jnp.zeros((SEQ_PAD, D_B), f32).at[SEQ_LEAD:].set(bias_p[:DEC_SEQ])
    bias_s = jnp.tile(bias_small, (TB_S // SEQ_PAD, 1))
    return wm.astype(bf16), bias_p, wm_s.astype(bf16), bias_s


def _row(a, n):
    return _pad_cols(a.reshape(1, -1), n)


def kernel(x_prompt, x_sample, state_gdn, state_gdn_conv, state_rwkv, state_rwkv_shift, norm_mix, norm_ffn, norm_final, w_in, gdn_conv_w, gdn_a_log, gdn_dt_bias, gdn_norm_w, sgu_ln_w, sgu_ln_b, sgu_w, sgu_b, rwkv_mu, rwkv_w0, rwkv_w_up, rwkv_a0, rwkv_a_up, rwkv_g_up, rwkv_k_k, rwkv_k_a, rwkv_r_k, rwkv_ln_w, rwkv_ln_b, w_out, router_group_w, router_group_b, router_expert_w, router_expert_b, expert_w_gate, expert_w_up, expert_w_down):
    x = (x_prompt.reshape(T_PROMPT, D_MODEL),
         jnp.pad(x_sample, ((0, 0), (SEQ_LEAD, 0), (0, 0))).reshape(T_SAMPLE, D_MODEL))
    sample_states = ()

    w_pad = _prep_w_in(w_in)
    w_out_b = w_out.astype(bf16)
    wg_b, wu_b, wd_b = expert_w_gate.astype(bf16), expert_w_up.astype(bf16), expert_w_down.astype(bf16)

    outs = {k: [] for k in ('gdn_p', 'conv_p', 'rwkv_p', 'shift_p', 'gdn_s', 'conv_s', 'rwkv_s', 'shift_s', 'cv_s')}
    for l in range(DEPTH):
        last = l == DEPTH - 1
        v128 = jnp.concatenate([
            jnp.pad(gdn_a_log[l].reshape(1, H_A), ((0, 0), (4, 120))),
            jnp.pad(gdn_dt_bias[l].reshape(1, H_A), ((0, 0), (4, 120))),
            gdn_norm_w[l].reshape(1, HD_A), jnp.zeros((5, 128), f32)], axis=0)
        v256 = jnp.concatenate([a.reshape(1, D_C) for a in (
            sgu_ln_w[l], sgu_ln_b[l], rwkv_w0[l], rwkv_a0[l], rwkv_k_k[l], rwkv_k_a[l], rwkv_r_k[l],
            rwkv_ln_w[l], rwkv_ln_b[l])] + [jnp.zeros((7, D_C), f32)], axis=0)
        wm_p, bias_p, wm_s, bias_s = _sgu_mats(sgu_w[l], sgu_b[l])
        common = (gdn_conv_w[l], v128, v256)
        tail = (_prep_mu(rwkv_mu[l]), _pad_rows(rwkv_w_up[l], 128).astype(bf16),
                _pad_rows(rwkv_a_up[l], 128).astype(bf16), rwkv_g_up[l].astype(bf16))
        mw_p = common + (wm_p, bias_p) + tail
        mw_s = common + (wm_s, bias_s) + tail

        if l == 0:
            proj, h = _inproj(x, norm_mix[l].reshape(1, D_MODEL), w_pad, l)
        else:
            proj, h, x_tok = _inproj_gather(pos, ys, norm_mix[l].reshape(1, D_MODEL), w_pad, l)
            x = (x_tok,)
        pcf = _mm(state_rwkv_shift[l], w_pad[l, :, OFF_C:]).reshape(DEC_BATCH, 1, NC_PAD)
        mix_p, gdn_p, conv_p, rwkv_p = _mixer_prompt(proj, mw_p)
        mix_s, gdn_s, conv_s, rwkv_s, cv_s = _mixer_sample(
            proj, state_gdn, state_gdn_conv, state_rwkv, pcf, mw_s, l, sample_states)
        sample_states = (gdn_s, rwkv_s)

        rw = _pad_cols(jnp.concatenate([router_group_w[l], router_expert_w[l]], axis=1), 128)
        rwh = rw.astype(bf16)
        rwl = (rw - rwh.astype(f32)).astype(bf16)
        rb = _row(jnp.concatenate([router_group_b[l], router_expert_b[l]]), 128)
        nw_ffn = norm_ffn[l].reshape(1, D_MODEL)
        xg, cnt_rows = _outproj(x, mix_p, mix_s, w_out_b, nw_ffn, rwh, rwl, rb, l)
        pos, tile_group, tile_valid, tile_idx, pad_lo, pad_hi, tile_used, tile_wexp = _route_meta(xg, cnt_rows)
        src = _invert(pos, pad_lo, pad_hi)
        ys = _experts(tile_group, tile_valid, tile_idx, tile_used, tile_wexp, src, xg, nw_ffn, wg_b, wu_b, wd_b,
                      norm_final.reshape(1, D_MODEL), l, final_norm=last)
        if last:
            x = _combine(pos, ys, split=True)

        outs['gdn_p'].append(gdn_p)
        outs['conv_p'].append(conv_p)
        outs['rwkv_p'].append(rwkv_p)
        outs['shift_p'].append(h[SEQ // SEQ_PAD - 1:T_PROMPT // SEQ_PAD:SEQ // SEQ_PAD])
        outs['conv_s'].append(conv_s)
        outs['shift_s'].append(h[T_PROMPT // SEQ_PAD:])
        outs['cv_s'].append(cv_s.reshape(DEC_BATCH, SEQ_PAD, D_B)[:, SEQ_LEAD:])

    y_prompt = x[0].reshape(BATCH, SEQ, D_MODEL)
    y_sample = x[1].reshape(DEC_BATCH, SEQ_PAD, D_MODEL)[:, SEQ_LEAD:]
    st = lambda k: jnp.stack(outs[k])
    return (y_prompt, y_sample, st('gdn_p'), st('conv_p'), st('rwkv_p'), st('shift_p'),
            sample_states[0], st('conv_s'), sample_states[1], st('shift_s'), st('cv_s'))
```

```python
import functools
import math

import jax
import jax.numpy as jnp
from jax import lax
from jax.experimental import pallas as pl
from jax.experimental.pallas import tpu as pltpu

f32 = jnp.float32
bf16 = jnp.bfloat16
i32 = jnp.int32

D_MODEL = 1024
BATCH = 8
SEQ = 2048
DEPTH = 2
DEC_BATCH = 128
DEC_SEQ = 4
H_A = 4
HD_A = 128
D_A = 512
CONV_W = 4
D_B = 256
DH_B = 64
SGU_CHUNK = 128
H_C = 4
N_C = 64
D_C = 256
N_GROUPS = 4
EPG = 4
D_FF_E = 512
NORM_EPS = 1e-6
LN_EPS = 1e-5
GN_EPS = 64e-5

SEQ_PAD = 8
SEQ_LEAD = SEQ_PAD - DEC_SEQ
T_PROMPT = BATCH * SEQ
T_SAMPLE = DEC_BATCH * SEQ_PAD
T_ALL = T_PROMPT + T_SAMPLE
TB_P = 512
TB_S = 128
GDN_CHUNK = 64

OFF_Q, OFF_K, OFF_V, OFF_Z, OFF_BA, OFF_U, OFF_VB, OFF_C = 0, 512, 1024, 1536, 2048, 2176, 2432, 2688
NP_IN = 3840
NC_PAD = NP_IN - OFF_C

N_PAIRS = 6
N_BUCKETS = N_GROUPS * N_PAIRS
PAIRS = ((0, 1), (0, 2), (0, 3), (1, 2), (1, 3), (2, 3))
XG_W = D_MODEL + 128
LANE_G, LANE_RANK, LANE_E0 = 0, 1, 4

VMEM_LIMIT = 48 * 1024 * 1024


NN = (((1,), (0,)), ((), ()))
NT = (((1,), (1,)), ((), ()))
TN = (((0,), (0,)), ((), ()))
BNN = (((2,), (1,)), ((0,), (0,)))


def _bdot(a, b, dims=NN):
    return lax.dot_general(a.astype(bf16), b.astype(bf16), dims, preferred_element_type=f32)


def _split2(x):
    hi = x.astype(bf16)
    lo = (x - hi.astype(f32)).astype(bf16)
    return hi, lo


def _split3(x):
    hi = x.astype(bf16)
    r = x - hi.astype(f32)
    mid = r.astype(bf16)
    lo = (r - mid.astype(f32)).astype(bf16)
    return hi, mid, lo


def _dot01_left(m01, x):
    hi, mid, lo = _split3(x)
    d = lambda p: lax.dot_general(m01, p, NN, preferred_element_type=f32)
    return d(hi) + d(mid) + d(lo)


def _dot01_right(x, m01):
    hi, lo = _split2(x)
    d = lambda p: lax.dot_general(p, m01, NN, preferred_element_type=f32)
    return d(hi) + d(lo)


def _softplus(x):
    return jnp.maximum(x, 0.0) + jnp.log(1.0 + jnp.exp(-jnp.abs(x)))


def _sigmoid(x):
    return 0.5 * jnp.tanh(0.5 * x) + 0.5


def _silu(x):
    return x * _sigmoid(x)


def _gelu(x):
    return 0.5 * x * (1.0 + lax.erf(x * (1.0 / math.sqrt(2.0))))


def _rms(x, w):
    return x * lax.rsqrt(jnp.mean(x * x, axis=-1, keepdims=True) + NORM_EPS) * w


TM_A = 512
N_SLAB = 768


def _start_row_copies(idx_ref, base, src_hbm, dst, sem, r0, n, inline):
    def one(r):
        p = idx_ref[base + r]
        pltpu.make_async_copy(src_hbm.at[pl.ds(p, 1), :], dst.at[pl.ds(r, 1), :], sem).start()

    if inline:
        for k in range(n):
            one(r0 + k)
    else:
        def body(k, c):
            one(r0 + k)
            return c
        lax.fori_loop(0, n, body, 0, unroll=DMA_UNROLL)


def _wait_row_copies(src_hbm, dst, sem):
    pltpu.make_async_copy(src_hbm.at[pl.ds(0, dst.shape[0]), :], dst, sem).wait()


def _inproj_kernel(*refs, split):
    if split:
        xp_ref, xs_ref, nw_ref, w_ref, proj_ref, h8_ref, hscr = refs
        x = jnp.where(pl.program_id(0) < T_PROMPT // TM_A, xp_ref[...], xs_ref[...])
    else:
        x_ref, nw_ref, w_ref, proj_ref, h8_ref, hscr = refs
        x = x_ref[...]
    _inproj_body(x, nw_ref, w_ref, proj_ref, h8_ref, hscr)


def _inproj_gather_kernel(pos_ref, ys_hbm, nw_ref, w_ref, proj_ref, h8_ref, x_ref, hscr, xbuf, sem):
    i = pl.program_id(0)
    n = pl.num_programs(0)
    slot = i % 2

    @pl.when(i == 0)
    def _():
        _start_row_copies(pos_ref, 0, ys_hbm, xbuf.at[0], sem.at[0], 0, TM_A, inline=False)

    _wait_row_copies(ys_hbm, xbuf.at[slot], sem.at[slot])
    x = xbuf[slot]
    x_ref[...] = x
    nxt = jnp.minimum(i + 1, n - 1)
    _start_row_copies(pos_ref, nxt * TM_A, ys_hbm, xbuf.at[1 - slot], sem.at[1 - slot], 0, TM_A, inline=True)
    _inproj_body(x, nw_ref, w_ref, proj_ref, h8_ref, hscr)

    @pl.when(i == n - 1)
    def _():
        _wait_row_copies(ys_hbm, xbuf.at[1 - slot], sem.at[1 - slot])


def _inproj_body(x, nw_ref, w_ref, proj_ref, h8_ref, hscr):
    h = _rms(x, nw_ref[...])
    for k in range(D_MODEL // 128):
        hscr[k] = h[:, k * 128:(k + 1) * 128]
        h8_ref[:, pl.ds(k * 128, 128)] = hscr[k, pl.ds(SEQ_PAD - 1, TM_A // SEQ_PAD, stride=SEQ_PAD), :]
    hb = h.astype(bf16)
    for n in range(NP_IN // N_SLAB):
        sl = pl.ds(n * N_SLAB, N_SLAB)
        proj_ref[:, sl] = jnp.dot(hb, w_ref[0, :, sl], preferred_element_type=f32)


def _inproj_gather(pos, ys, nw, w_pad, l):
    return pl.pallas_call(
        _inproj_gather_kernel,
        grid_spec=pltpu.PrefetchScalarGridSpec(
            num_scalar_prefetch=1,
            grid=(T_ALL // TM_A,),
            in_specs=[pl.BlockSpec(memory_space=pl.ANY),
                      pl.BlockSpec((1, D_MODEL), lambda i, pos: (0, 0)),
                      pl.BlockSpec((1, D_MODEL, NP_IN), lambda i, pos: (l, 0, 0))],
            out_specs=[pl.BlockSpec((TM_A, NP_IN), lambda i, pos: (i, 0)),
                       pl.BlockSpec((TM_A // SEQ_PAD, D_MODEL), lambda i, pos: (i, 0)),
                       pl.BlockSpec((TM_A, D_MODEL), lambda i, pos: (i, 0))],
            scratch_shapes=[pltpu.VMEM((D_MODEL // 128, TM_A, 128), f32),
                            pltpu.VMEM((2, TM_A, D_MODEL), f32),
                            pltpu.SemaphoreType.DMA((2,))]),
        out_shape=[jax.ShapeDtypeStruct((T_ALL, NP_IN), f32),
                   jax.ShapeDtypeStruct((T_ALL // SEQ_PAD, D_MODEL), f32),
                   jax.ShapeDtypeStruct((T_ALL, D_MODEL), f32)],
        compiler_params=pltpu.CompilerParams(dimension_semantics=("arbitrary",),
                                             vmem_limit_bytes=VMEM_LIMIT),
        name="inproj_gather",
    )(pos, ys, nw, w_pad)


def _inproj(xs, nw, w_pad, l):
    split = len(xs) == 2
    npt = T_PROMPT // TM_A
    if split:
        x_specs = [pl.BlockSpec((TM_A, D_MODEL), lambda i: (jnp.minimum(i, npt - 1), 0)),
                   pl.BlockSpec((TM_A, D_MODEL), lambda i: (jnp.maximum(i - npt, 0), 0))]
    else:
        x_specs = [pl.BlockSpec((TM_A, D_MODEL), lambda i: (i, 0))]
    return pl.pallas_call(
        functools.partial(_inproj_kernel, split=split),
        grid=(T_ALL // TM_A,),
        in_specs=x_specs + [pl.BlockSpec((1, D_MODEL), lambda i: (0, 0)),
                            pl.BlockSpec((1, D_MODEL, NP_IN), lambda i: (l, 0, 0))],
        out_specs=[pl.BlockSpec((TM_A, NP_IN), lambda i: (i, 0)),
                   pl.BlockSpec((TM_A // SEQ_PAD, D_MODEL), lambda i: (i, 0))],
        out_shape=[jax.ShapeDtypeStruct((T_ALL, NP_IN), f32),
                   jax.ShapeDtypeStruct((T_ALL // SEQ_PAD, D_MODEL), f32)],
        scratch_shapes=[pltpu.VMEM((D_MODEL // 128, TM_A, 128), f32)],
        compiler_params=pltpu.CompilerParams(dimension_semantics=("arbitrary",),
                                             vmem_limit_bytes=VMEM_LIMIT),
        name="inproj",
    )(*xs, nw, w_pad)


def _mm_kernel(a_ref, b_ref, o_ref):
    o_ref[...] = jnp.dot(a_ref[...].astype(bf16), b_ref[...], preferred_element_type=f32)


def _mm(a, b):
    return pl.pallas_call(
        _mm_kernel,
        out_shape=jax.ShapeDtypeStruct((a.shape[0], b.shape[1]), f32),
        compiler_params=pltpu.CompilerParams(vmem_limit_bytes=VMEM_LIMIT),
        name="shift_proj",
    )(a, b)


def _cat(parts, axis):
    return parts[0] if len(parts) == 1 else jnp.concatenate(parts, axis=axis)


def _mixer_kernel(*refs, TB, C, G, per_seq, n_alias=0):
    it = iter(refs)
    proj_ref = next(it)
    if per_seq:
        gdn_in, conv_in, rwkv_in, pcf_in = next(it), next(it), next(it), next(it)
    convw_ref, v128_ref, v256_ref, sguw_ref, sgub_ref, mu_ref, wup_ref, aup_ref, gup_ref = (
        next(it) for _ in range(9))
    for _ in range(n_alias):
        next(it)
    mix_ref, gdn_out, conv_out, rwkv_out = next(it), next(it), next(it), next(it)
    cv_out = next(it) if per_seq else None
    xp, pp = next(it), next(it)

    nchunk = TB // C
    ngrp = C // G
    iters = int(math.log2(G)) - 1

    if per_seq:
        xp[pl.ds(0, 8), :] = jnp.zeros((8, 3 * D_A), f32)
        pp[pl.ds(0, 8), :] = jnp.zeros((8, NC_PAD), f32)
    else:
        @pl.when(pl.program_id(1) == 0)
        def _():
            xp[pl.ds(0, 8), :] = jnp.zeros((8, 3 * D_A), f32)
            pp[pl.ds(0, 8), :] = jnp.zeros((8, NC_PAD), f32)
            gdn_out[...] = jnp.zeros(gdn_out.shape, f32)
            rwkv_out[...] = jnp.zeros(rwkv_out.shape, f32)
    xp[pl.ds(8, TB), :] = proj_ref[:, pl.ds(OFF_Q, 3 * D_A)]
    pp[pl.ds(8, TB), :] = proj_ref[:, pl.ds(OFF_C, NC_PAD)]
    if per_seq:
        for s in range(TB // SEQ_PAD):
            r0 = 8 + s * SEQ_PAD
            xp[pl.ds(r0 + SEQ_LEAD - (CONV_W - 1), CONV_W - 1), :] = conv_in[s]
            pp[pl.ds(r0 + SEQ_LEAD - 1, 1), :] = pcf_in[s]

    rowi = lax.broadcasted_iota(i32, (TB, 1), 0)
    live = (rowi % SEQ_PAD) >= SEQ_LEAD if per_seq else None

    ii = lax.broadcasted_iota(i32, (C, C), 0)
    jj = lax.broadcasted_iota(i32, (C, C), 1)
    same = (ii // G) == (jj // G)
    causal = (ii >= jj) & same
    strict = (ii > jj) & same
    eye = ii == jj
    m_cum = causal.astype(bf16)
    m_grp = same.astype(bf16)

    def conv_cols(r0, c0):
        cs = pl.ds(c0, 128)
        acc = xp[pl.ds(r0 + 8, C), cs] * convw_ref[pl.ds(3, 1), cs]
        for j in range(CONV_W - 1):
            acc = acc + xp[pl.ds(r0 + 5 + j, C), cs] * convw_ref[pl.ds(j, 1), cs]
        return _silu(acc)

    alog_row = v128_ref[pl.ds(0, 1), :]
    dtb_row = v128_ref[pl.ds(1, 1), :]
    gnorm_w = v128_ref[pl.ds(2, 1), :]
    ln_w, ln_b = v256_ref[pl.ds(0, 1), :], v256_ref[pl.ds(1, 1), :]
    w0, a0 = v256_ref[pl.ds(2, 1), :], v256_ref[pl.ds(3, 1), :]
    k_k, k_a, r_k = v256_ref[pl.ds(4, 1), :], v256_ref[pl.ds(5, 1), :], v256_ref[pl.ds(6, 1), :]
    rln_w, rln_b = v256_ref[pl.ds(7, 1), :], v256_ref[pl.ds(8, 1), :]

    l64i = lax.broadcasted_iota(i32, (D_C, D_C), 0) // N_C
    l64j = lax.broadcasted_iota(i32, (D_C, D_C), 1) // N_C
    seg64 = (l64i == l64j).astype(bf16)
    gdn, rwk, lmats = [None] * nchunk, [None] * nchunk, [None] * (8 * nchunk)

    def phase1(c):
        R0 = c * C
        rows = pl.ds(R0, C)
        live_c = live[R0:R0 + C] if per_seq else None

        ba = proj_ref[rows, pl.ds(OFF_BA, 128)]
        beta_all = _sigmoid(ba)
        g_all = -jnp.exp(alog_row) * _softplus(ba + dtb_row)
        if per_seq:
            beta_all = jnp.where(live_c, beta_all, 0.0)
            g_all = jnp.where(live_c, g_all, 0.0)
        pcur = pp[pl.ds(R0 + 8, C), :]
        pprev = pp[pl.ds(R0 + 7, C), :]
        pm = pcur + mu_ref[...] * (pprev - pcur)
        r_ = pm[:, 0:D_C]
        kc = pm[:, D_C:2 * D_C]
        vc = pm[:, 2 * D_C:3 * D_C]
        wd = pm[:, 3 * D_C:3 * D_C + 128]
        ad = pm[:, 3 * D_C + 128:3 * D_C + 256]
        gd = pm[:, 3 * D_C + 256:3 * D_C + 384]
        wl_mm = _bdot(jnp.tanh(wd), wup_ref[...])
        a_mm = _bdot(ad, aup_ref[...])
        gate = _bdot(_sigmoid(gd), gup_ref[...])
        kk = kc * k_k
        kk_ss = _dot01_right(kk * kk, seg64)
        yield

        gc_all = _dot01_left(m_cum, g_all)
        total = lambda x: jnp.sum(x, axis=0, keepdims=True) if ngrp == 1 else _dot01_left(m_grp, x)
        gl_all = total(g_all)
        w_log = -_softplus(-(w0 + wl_mm)) - 0.5
        logw = -jnp.exp(w_log)
        a_ = _sigmoid(a0 + a_mm)
        kk = kk * lax.rsqrt(kk_ss + 1e-6)
        kc2 = kc * (1.0 + (a_ - 1.0) * k_a)
        if per_seq:
            logw = jnp.where(live_c, logw, 0.0)
            kk = jnp.where(live_c, kk, 0.0)
            kc2 = jnp.where(live_c, kc2, 0.0)
        b_ = kk * a_
        lw_hi, lw_lo = _split2(logw)
        Gc = (lax.dot_general(m_cum, lw_hi, NN, preferred_element_type=f32)
              + lax.dot_general(m_cum, lw_lo, NN, preferred_element_type=f32))
        Gl = total(logw)
        bonus_ss = _dot01_right(r_ * kc2 * r_k, seg64)
        yield

        heads = []
        for h in range(H_A):
            q = conv_cols(R0, OFF_Q + h * HD_A)
            k = conv_cols(R0, OFF_K + h * HD_A)
            v = conv_cols(R0, OFF_V + h * HD_A)
            q = q * lax.rsqrt(jnp.sum(q * q, axis=-1, keepdims=True) + 1e-6) * (HD_A ** -0.5)
            k = k * lax.rsqrt(jnp.sum(k * k, axis=-1, keepdims=True) + 1e-6)
            if per_seq:
                k = jnp.where(live_c, k, 0.0)
            beta = beta_all[:, h:h + 1]
            gcol = gc_all[:, 4 + h:5 + h]
            glr = gl_all[:, 4 + h:5 + h]
            grow = jnp.sum(jnp.where(eye, jnp.broadcast_to(gcol, (C, C)), 0.0), axis=0, keepdims=True)
            decay = jnp.where(causal, jnp.exp(jnp.where(causal, gcol - grow, 0.0)), 0.0)
            eg = jnp.exp(gcol)
            kb = k * beta
            kq = _bdot(jnp.concatenate([kb, q], axis=0), k, NT)
            heads.append(dict(kq=kq, decay=decay,
                              rhs=jnp.concatenate([v * beta, kb * eg], axis=1),
                              q_dec=q * eg, k_dec=k * jnp.exp(glr - gcol), glr=glr))
            yield
        gdn[c] = heads
        e_neg = jnp.exp(-Gc)
        e_rem = jnp.exp(Gl - Gc)
        rG = r_ * jnp.exp(Gc)
        kkG = kk * jnp.exp(Gc - logw)
        kN = kc2 * e_neg
        bN = b_ * e_neg
        aalls = []
        for h in range(H_C):
            hs = slice(h * N_C, (h + 1) * N_C)
            aalls.append(_bdot(jnp.concatenate([kkG[:, hs], rG[:, hs]], axis=0),
                               jnp.concatenate([bN[:, hs], kN[:, hs]], axis=0), NT))
        yield

        for h in range(H_A):
            d = gdn[c][h]
            kq = d.pop('kq')
            decay = d.pop('decay')
            lmats[8 * c + h] = jnp.where(strict, kq[:C] * decay, 0.0)
            d['attn'] = kq[C:] * decay
        yield
        heads = []
        for h in range(H_C):
            hs = slice(h * N_C, (h + 1) * N_C)
            aall = aalls[h]
            lmats[8 * c + H_A + h] = jnp.where(strict, aall[:C, :C], 0.0)
            akk_k = jnp.where(strict, aall[:C, C:], 0.0)
            ar = jnp.concatenate([jnp.where(causal, aall[C:, C:], 0.0),
                                  -jnp.where(causal, aall[C:, :C], 0.0)], axis=1)
            heads.append(dict(x1=_bdot(akk_k, vc[:, hs]), ar=ar))
        rwk[c] = dict(heads=heads, vc=vc, rG=rG, kkG=kkG, kdec=kc2 * e_rem, bdec=b_ * e_rem,
                      e_last=jnp.exp(Gl), gate=gate, bonus=bonus_ss * vc)
        yield

    def par(*gens):
        gens = list(gens)
        while gens:
            alive = []
            for g in gens:
                try:
                    next(g)
                    alive.append(g)
                except StopIteration:
                    pass
            gens = alive
            yield

    def seq(*gens):
        for g in gens:
            yield from g

    def lockstep(gens):
        for _ in par(*gens):
            pass

    nmats = [None] * (8 * nchunk)

    def inverses(chunks):
        idx = [8 * c + k for c in chunks for k in range(8)]
        bmm = lambda a, b: lax.dot_general(a.astype(bf16), b.astype(bf16), BNN, preferred_element_type=f32)
        pack = 2 * C <= 128
        if pack:
            N = -jnp.stack([jnp.concatenate([lmats[i], lmats[j]], axis=1) for i, j in zip(idx[0::2], idx[1::2])])
            left = lax.broadcasted_iota(i32, N.shape, 2) < C
            rhs = lambda X: jnp.concatenate([jnp.where(left, X, 0.0), jnp.where(left, 0.0, X)], axis=1)
        else:
            N = -jnp.stack([lmats[i] for i in idx])
            rhs = lambda X: X
        Q = bmm(N, rhs(N))
        yield
        for _ in range(iters - 1):
            R = bmm(jnp.concatenate([N, Q], axis=1), rhs(Q))
            N = N + Q + R[:, :C]
            Q = R[:, C:]
            yield
        N = N + Q + bmm(N, rhs(Q))
        if pack:
            for p, (i, j) in enumerate(zip(idx[0::2], idx[1::2])):
                nmats[i], nmats[j] = N[p][:, :C], N[p][:, C:]
        else:
            for p, i in enumerate(idx):
                nmats[i] = N[p]
        yield

    def phase3(c):
        for h in range(H_A):
            d = gdn[c][h]
            sol = d['rhs'] + _bdot(nmats[8 * c + h], d['rhs'])
            d['u'], d['w'] = sol[:, :HD_A], sol[:, HD_A:]
        for h in range(H_C):
            hs = slice(h * N_C, (h + 1) * N_C)
            d = rwk[c]['heads'][h]
            both = jnp.concatenate([d['x1'], rwk[c]['kkG'][:, hs]], axis=1)
            both = both + _bdot(nmats[8 * c + H_A + h], both)
            d['u_p'], d['w_p'] = both[:, :N_C], both[:, N_C:]
            d['both'] = both
        yield
        if per_seq:
            return
        for h in range(H_A):
            d = gdn[c][h]
            wu = jnp.concatenate([d['w'], d['u']], axis=1)
            a_wu = _bdot(d['attn'], wu)
            k_wu = _bdot(d['k_dec'], wu, TN)
            d['qe'] = d['q_dec'] - a_wu[:, :HD_A]
            d['o0'] = a_wu[:, HD_A:]
            d['m'] = -k_wu[:, :HD_A]
            d['b'] = k_wu[:, HD_A:]
        yield
        for h in range(H_C):
            hs = slice(h * N_C, (h + 1) * N_C)
            rc = rwk[c]
            d = rc['heads'][h]
            V = rc['vc'][:, hs]
            x = _bdot(d['ar'][:, C:], d['both'])
            d['y0'] = _bdot(d['ar'][:, :C], V) + x[:, :N_C]
            d['re'] = rc['rG'][:, hs] + x[:, N_C:]
            d['m'] = _bdot(d['w_p'], rc['bdec'][:, hs], TN)
            d['b'] = _bdot(jnp.concatenate([V, -d['u_p']], axis=0),
                           jnp.concatenate([rc['kdec'][:, hs], rc['bdec'][:, hs]], axis=0), TN)
        yield

    def gdn_head_seq(c, h):
        rows = pl.ds(c * C, C)
        d = gdn[c][h]
        S = gdn_out[0, h]
        Sb = S.astype(bf16)
        gdn_out[0, h] = S * jnp.exp(d['glr'][-1:, :]) + _bdot(d['m'], Sb) + d['b']
        o = d['o0'] + _bdot(d['qe'], Sb)
        yield
        o = o * lax.rsqrt(jnp.mean(o * o, axis=-1, keepdims=True) + NORM_EPS) * gnorm_w
        z = proj_ref[rows, pl.ds(OFF_Z + h * HD_A, HD_A)]
        mix_ref[rows, pl.ds(h * HD_A, HD_A)] = o * _silu(z)

    def rwkv_head_seq(c, h, ys):
        rc = rwk[c]
        hs = slice(h * N_C, (h + 1) * N_C)
        d = rc['heads'][h]
        S = rwkv_out[0, h]
        Sb = S.astype(bf16)
        rwkv_out[0, h] = S * rc['e_last'][-1:, hs] + d['b'] - _bdot(Sb, d['m'])
        ys[h] = d['y0'] + _bdot(d['re'], Sb, NT)
        yield

    def gdn_head(c, h):
        rows = pl.ds(c * C, C)
        d = gdn[c][h]
        rq, vn, s_old = [], [], []
        for s in range(ngrp):
            gs = slice(s * G, (s + 1) * G)
            S = gdn_in[s, h] if per_seq else gdn_out[0, h]
            s_old.append(S)
            R = _bdot(jnp.concatenate([d['w'][gs], d['q_dec'][gs]], axis=0), S)
            vn.append(d['u'][gs] - R[:G])
            rq.append(R[G:])
        yield
        v_new = _cat(vn, 0)
        o = _cat(rq, 0) + _bdot(d['attn'], v_new)
        for s in range(ngrp):
            gs = slice(s * G, (s + 1) * G)
            g_last = jnp.exp(d['glr'][s * G + G - 1:s * G + G, :])
            S_new = s_old[s] * g_last + _bdot(d['k_dec'][gs], v_new[gs], TN)
            if per_seq:
                gdn_out[s, h] = S_new
            else:
                gdn_out[0, h] = S_new
        yield
        o = o * lax.rsqrt(jnp.mean(o * o, axis=-1, keepdims=True) + NORM_EPS) * gnorm_w
        z = proj_ref[rows, pl.ds(OFF_Z + h * HD_A, HD_A)]
        mix_ref[rows, pl.ds(h * HD_A, HD_A)] = o * _silu(z)

    def rwkv_head(c, h, ys):
        rc = rwk[c]
        hs = slice(h * N_C, (h + 1) * N_C)
        d = rc['heads'][h]
        V = rc['vc'][:, hs]
        rr, ut, s_old = [], [], []
        for s in range(ngrp):
            gs = slice(s * G, (s + 1) * G)
            S = rwkv_in[s, h] if per_seq else rwkv_out[0, h]
            s_old.append(S)
            R = _bdot(jnp.concatenate([d['w_p'][gs], rc['rG'][gs, hs]], axis=0), S, NT)
            ut.append(d['u_p'][gs] + R[:G])
            rr.append(R[G:])
        yield
        Ut = _cat(ut, 0)
        ys[h] = _cat(rr, 0) + _bdot(d['ar'], jnp.concatenate([V, Ut], axis=0))
        for s in range(ngrp):
            gs = slice(s * G, (s + 1) * G)
            upd = _bdot(jnp.concatenate([V[gs], -Ut[gs]], axis=0),
                        jnp.concatenate([rc['kdec'][gs, hs], rc['bdec'][gs, hs]], axis=0), TN)
            S_new = s_old[s] * rc['e_last'][s * G + G - 1:s * G + G, hs] + upd
            if per_seq:
                rwkv_out[s, h] = S_new
            else:
                rwkv_out[0, h] = S_new
        yield

    def phase4(c):
        rows = pl.ds(c * C, C)
        rc = rwk[c]
        ys = [None] * H_C
        if per_seq:
            heads = [gdn_head(c, h) for h in range(H_A)] + [rwkv_head(c, h, ys) for h in range(H_C)]
        else:
            heads = [gdn_head_seq(c, h) for h in range(H_A)] + [rwkv_head_seq(c, h, ys) for h in range(H_C)]
        yield from par(*heads)
        y = jnp.concatenate(ys, axis=1)
        mu_y = _dot01_right(y, seg64) * (1.0 / N_C)
        dy = y - mu_y
        var_y = _dot01_right(dy * dy, seg64) * (1.0 / N_C)
        y = dy * lax.rsqrt(var_y + GN_EPS) * rln_w + rln_b
        mix_ref[rows, pl.ds(D_A + D_B, D_C)] = (y + rc['bonus']) * rc['gate']
        yield

    def sgu(sc):
        rows = pl.ds(sc * SGU_CHUNK, SGU_CHUNK)
        ug = _gelu(proj_ref[rows, pl.ds(OFF_U, D_B)])
        vs = _gelu(proj_ref[rows, pl.ds(OFF_VB, D_B)])
        yield
        mu_v = _dot01_right(vs, seg64) * (1.0 / DH_B)
        dv = vs - mu_v
        var_v = _dot01_right(dv * dv, seg64) * (1.0 / DH_B)
        vs = dv * lax.rsqrt(var_v + LN_EPS) * ln_w + ln_b
        if per_seq:
            cv_out[rows, :] = vs
        yield
        vsb = vs.astype(bf16)
        outs = [lax.dot_general(sguw_ref[h], vsb[:, h * DH_B:(h + 1) * DH_B], NN, preferred_element_type=f32)
                for h in range(4)]
        mixed = jnp.concatenate(outs, axis=1) + sgub_ref[...]
        mix_ref[rows, pl.ds(D_A, D_B)] = ug * mixed
        yield

    first = list(range(nchunk))[:max(nchunk // 2, 1)]
    second = list(range(nchunk))[len(first):]
    sgus = [sgu(sc) for sc in range(TB // SGU_CHUNK)]
    lockstep([phase1(c) for c in first])
    lockstep([phase1(c) for c in second]
             + [seq(inverses(first), par(*[phase3(c) for c in first]))])
    if second:
        lockstep([seq(inverses(second), par(*[phase3(c) for c in second])),
                  seq(*[phase4(c) for c in first]), seq(*sgus)])
        lockstep([seq(*[phase4(c) for c in second])])
    else:
        lockstep([seq(*[phase4(c) for c in first]), seq(*sgus)])

    if per_seq:
        for s in range(TB // SEQ_PAD):
            conv_out[s] = xp[pl.ds(8 + (s + 1) * SEQ_PAD - (CONV_W - 1), CONV_W - 1), :]
    else:
        conv_out[0] = xp[pl.ds(8 + TB - (CONV_W - 1), CONV_W - 1), :]
        xp[pl.ds(0, 8), :] = xp[pl.ds(TB, 8), :]
        pp[pl.ds(0, 8), :] = pp[pl.ds(TB, 8), :]


def _mixer_weight_specs(nidx):
    z2 = (lambda b, j: (0, 0)) if nidx == 2 else (lambda i: (0, 0))
    z3 = (lambda b, j: (0, 0, 0)) if nidx == 2 else (lambda i: (0, 0, 0))
    return [pl.BlockSpec((CONV_W, 3 * D_A), z2),
            pl.BlockSpec((8, 128), z2),
            pl.BlockSpec((16, D_C), z2),
            pl.BlockSpec((4, SGU_CHUNK, SGU_CHUNK), z3),
            pl.BlockSpec((SGU_CHUNK, D_B), z2),
            pl.BlockSpec((1, NC_PAD), z2),
            pl.BlockSpec((128, D_C), z2),
            pl.BlockSpec((128, D_C), z2),
            pl.BlockSpec((128, D_C), z2)]


def _mixer_prompt(proj, mw):
    nj = SEQ // TB_P
    return pl.pallas_call(
        functools.partial(_mixer_kernel, TB=TB_P, C=GDN_CHUNK, G=GDN_CHUNK, per_seq=False),
        grid=(BATCH, nj),
        in_specs=[pl.BlockSpec((TB_P, NP_IN), lambda b, j: (b * nj + j, 0))] + _mixer_weight_specs(2),
        out_specs=[pl.BlockSpec((TB_P, D_MODEL), lambda b, j: (b * nj + j, 0)),
                   pl.BlockSpec((1, H_A, HD_A, HD_A), lambda b, j: (b, 0, 0, 0)),
                   pl.BlockSpec((1, CONV_W - 1, 3 * D_A), lambda b, j: (b, 0, 0)),
                   pl.BlockSpec((1, H_C, N_C, N_C), lambda b, j: (b, 0, 0, 0))],
        out_shape=[jax.ShapeDtypeStruct((T_PROMPT, D_MODEL), f32),
                   jax.ShapeDtypeStruct((BATCH, H_A, HD_A, HD_A), f32),
                   jax.ShapeDtypeStruct((BATCH, CONV_W - 1, 3 * D_A), f32),
                   jax.ShapeDtypeStruct((BATCH, H_C, N_C, N_C), f32)],
        scratch_shapes=[pltpu.VMEM((TB_P + 8, 3 * D_A), f32), pltpu.VMEM((TB_P + 8, NC_PAD), f32)],
        compiler_params=pltpu.CompilerParams(dimension_semantics=("arbitrary", "arbitrary"),
                                             vmem_limit_bytes=VMEM_LIMIT),
        name="mixer_prompt",
    )(proj, *mw)


def _mixer_sample(proj, s_gdn, s_conv, s_rwkv, pcf, mw, l, prev_states):
    nseq = TB_S // SEQ_PAD
    base = T_PROMPT // TB_S
    n_in = 5 + len(mw)
    lay4 = lambda i: (l, i, 0, 0, 0)
    lay3 = lambda i: (l, i, 0, 0)
    return pl.pallas_call(
        functools.partial(_mixer_kernel, TB=TB_S, C=TB_S, G=SEQ_PAD, per_seq=True, n_alias=len(prev_states)),
        grid=(T_SAMPLE // TB_S,),
        in_specs=[pl.BlockSpec((TB_S, NP_IN), lambda i: (base + i, 0)),
                  pl.BlockSpec((None, nseq, H_A, HD_A, HD_A), lay4),
                  pl.BlockSpec((None, nseq, CONV_W - 1, 3 * D_A), lay3),
                  pl.BlockSpec((None, nseq, H_C, N_C, N_C), lay4),
                  pl.BlockSpec((nseq, 1, NC_PAD), lambda i: (i, 0, 0))] + _mixer_weight_specs(1)
                 + [pl.BlockSpec(memory_space=pl.ANY)] * len(prev_states),
        out_specs=[pl.BlockSpec((TB_S, D_MODEL), lambda i: (i, 0)),
                   pl.BlockSpec((None, nseq, H_A, HD_A, HD_A), lay4),
                   pl.BlockSpec((nseq, CONV_W - 1, 3 * D_A), lambda i: (i, 0, 0)),
                   pl.BlockSpec((None, nseq, H_C, N_C, N_C), lay4),
                   pl.BlockSpec((TB_S, D_B), lambda i: (i, 0))],
        out_shape=[jax.ShapeDtypeStruct((T_SAMPLE, D_MODEL), f32),
                   jax.ShapeDtypeStruct((DEPTH, DEC_BATCH, H_A, HD_A, HD_A), f32),
                   jax.ShapeDtypeStruct((DEC_BATCH, CONV_W - 1, 3 * D_A), f32),
                   jax.ShapeDtypeStruct((DEPTH, DEC_BATCH, H_C, N_C, N_C), f32),
                   jax.ShapeDtypeStruct((T_SAMPLE, D_B), f32)],
        scratch_shapes=[pltpu.VMEM((TB_S + 8, 3 * D_A), f32), pltpu.VMEM((TB_S + 8, NC_PAD), f32)],
        input_output_aliases={n_in + k: o for k, o in enumerate((1, 3)[:len(prev_states)])},
        compiler_params=pltpu.CompilerParams(dimension_semantics=("arbitrary",),
                                             vmem_limit_bytes=VMEM_LIMIT),
        name="mixer_sample",
    )(proj, s_gdn, s_conv, s_rwkv, pcf, *mw, *prev_states)


TM_C = 512


def _outproj_kernel(*refs, split):
    i = pl.program_id(0)
    in_prompt = i < T_PROMPT // TM_C
    if split:
        xp_ref, xs_ref = refs[:2]
        x = jnp.where(in_prompt, xp_ref[...], xs_ref[...])
    else:
        x = refs[0][...]
    mixp_ref, mixs_ref, wout_ref, nw_ref, rwc_ref, tri_ref, rb_ref, xg_ref, cnt_ref, run_ref = refs[-10:]

    @pl.when(i == 0)
    def _():
        run_ref[...] = jnp.zeros(run_ref.shape, f32)

    mix = jnp.where(in_prompt, mixp_ref[...], mixs_ref[...])
    x2 = x + jnp.dot(mix.astype(bf16), wout_ref[0], preferred_element_type=f32)
    xg_ref[:, pl.ds(0, D_MODEL)] = x2
    h2 = _rms(x2, nw_ref[...])
    hh, hl = _split2(h2)
    d = lambda a, b: jnp.dot(a, b, preferred_element_type=f32)
    both = d(hh, rwc_ref[...])
    logits = both[:, :128] + both[:, 128:] + d(hl, rwc_ref[:, pl.ds(0, 128)]) + rb_ref[...]

    lane = lax.broadcasted_iota(i32, logits.shape, 1).astype(f32)
    neg = jnp.float32(-jnp.inf)
    is_g = lane < float(N_GROUPS)
    gl = jnp.where(is_g, logits, neg)
    gmax = jnp.max(gl, axis=-1, keepdims=True)
    gsel = jnp.min(jnp.where(gl == gmax, lane, 128.0), axis=-1, keepdims=True)
    gw = 1.0 / jnp.sum(jnp.where(is_g, jnp.exp(jnp.where(is_g, logits - gmax, 0.0)), 0.0),
                       axis=-1, keepdims=True)
    lo = LANE_E0 + float(EPG) * gsel
    in_grp = (lane >= lo) & (lane < lo + float(EPG))
    el = jnp.where(in_grp, logits, neg)
    t1 = jnp.max(el, axis=-1, keepdims=True)
    i1 = jnp.min(jnp.where(el == t1, lane, 128.0), axis=-1, keepdims=True)
    el2 = jnp.where(lane == i1, neg, el)
    t2 = jnp.max(el2, axis=-1, keepdims=True)
    i2 = jnp.min(jnp.where(el2 == t2, lane, 128.0), axis=-1, keepdims=True)
    e2 = jnp.exp(t2 - t1)
    den = 1.0 + e2
    gates = jnp.where(lane == i1, gw / den, 0.0) + jnp.where(lane == i2, gw * e2 / den, 0.0)

    ea, eb = i1 - lo, i2 - lo
    e_lo, e_hi = jnp.minimum(ea, eb), jnp.maximum(ea, eb)
    bucket = float(N_PAIRS) * gsel + e_lo * (7.0 - e_lo) * 0.5 + (e_hi - e_lo - 1.0)
    onehot = jnp.where(lane == bucket, 1.0, 0.0)
    before = lax.dot_general(tri_ref[...], onehot.astype(bf16), NN, preferred_element_type=f32)
    rank = jnp.sum((before + run_ref[...]) * onehot, axis=-1, keepdims=True)
    run_ref[...] += jnp.sum(onehot, axis=0, keepdims=True)
    cnt_ref[...] = jnp.broadcast_to(run_ref[...], cnt_ref.shape)
    xg_ref[:, pl.ds(D_MODEL, 128)] = (gates + jnp.where(lane == float(LANE_G), bucket, 0.0)
                                      + jnp.where(lane == float(LANE_RANK), rank, 0.0))


def _outproj(xs, mix_p, mix_s, wout, nw, rwh, rwl, rb, l):
    rwc = jnp.concatenate([rwh, rwl], axis=1)
    ids = jnp.arange(TM_C)
    tri = (ids[:, None] > ids[None, :]).astype(bf16)
    row = lambda i: (i, 0)
    fix = lambda i: (0, 0)
    npt = T_PROMPT // TM_C
    p_rows = pl.BlockSpec((TM_C, D_MODEL), lambda i: (jnp.minimum(i, npt - 1), 0))
    s_rows = pl.BlockSpec((TM_C, D_MODEL), lambda i: (jnp.maximum(i - npt, 0), 0))
    split = len(xs) == 2
    return pl.pallas_call(
        functools.partial(_outproj_kernel, split=split),
        grid=(T_ALL // TM_C,),
        in_specs=([p_rows, s_rows] if split else [pl.BlockSpec((TM_C, D_MODEL), row)]) + [
                  p_rows, s_rows,
                  pl.BlockSpec((1, D_MODEL, D_MODEL), lambda i: (l, 0, 0)), pl.BlockSpec((1, D_MODEL), fix),
                  pl.BlockSpec((D_MODEL, 256), fix), pl.BlockSpec((TM_C, TM_C), fix),
                  pl.BlockSpec((1, 128), fix)],
        out_specs=[pl.BlockSpec((TM_C, XG_W), row), pl.BlockSpec((8, 128), fix)],
        out_shape=[jax.ShapeDtypeStruct((T_ALL, XG_W), f32), jax.ShapeDtypeStruct((8, 128), f32)],
        scratch_shapes=[pltpu.VMEM((1, 128), f32)],
        compiler_params=pltpu.CompilerParams(dimension_semantics=("arbitrary",),
                                             vmem_limit_bytes=VMEM_LIMIT),
        name="outproj_router",
    )(*xs, mix_p, mix_s, wout, nw, rwc, tri, rb)


TM_E = 512
NT_E = T_ALL // TM_E + N_GROUPS
T_SORT = NT_E * TM_E
DMA_UNROLL = 8


def _invert_kernel(pos_ref, pad_lo_ref, pad_hi_ref, src_ref):
    for g in range(N_GROUPS):
        def fill(p, c):
            src_ref[p] = T_ALL - 1
            return c
        lax.fori_loop(pad_lo_ref[g], pad_hi_ref[g], fill, 0)

    def body(t, c):
        src_ref[pos_ref[t]] = t
        return c
    lax.fori_loop(0, T_ALL, body, 0, unroll=DMA_UNROLL)


def _invert(pos, pad_lo, pad_hi):
    smem = pl.BlockSpec(memory_space=pltpu.SMEM)
    return pl.pallas_call(
        _invert_kernel,
        in_specs=[smem, smem, smem],
        out_specs=smem,
        out_shape=jax.ShapeDtypeStruct((T_SORT,), i32),
        name="moe_invert",
    )(pos, pad_lo, pad_hi)


def _gather_rows(pos_ref, base, ys_hbm, o_ref, sem):
    def issue(r, c):
        p = pos_ref[base + r]
        pltpu.make_async_copy(ys_hbm.at[pl.ds(p, 1), :], o_ref.at[pl.ds(r, 1), :], sem).start()
        return c

    lax.fori_loop(0, TM_E, issue, 0, unroll=DMA_UNROLL)
    pltpu.make_async_copy(ys_hbm.at[pl.ds(0, TM_E), :], o_ref, sem).wait()


def _combine_kernel(pos_ref, ys_hbm, o_ref, sem):
    _gather_rows(pos_ref, pl.program_id(0) * TM_E, ys_hbm, o_ref, sem)


def _combine_split_kernel(pos_ref, ys_hbm, op_ref, os_ref, sem):
    i = pl.program_id(0)

    @pl.when(i < T_PROMPT // TM_E)
    def _():
        _gather_rows(pos_ref, i * TM_E, ys_hbm, op_ref, sem)

    @pl.when(i >= T_PROMPT // TM_E)
    def _():
        _gather_rows(pos_ref, i * TM_E, ys_hbm, os_ref, sem)


def _combine(pos, ys, split):
    npt = T_PROMPT // TM_E
    if split:
        out_specs = [pl.BlockSpec((TM_E, D_MODEL), lambda i, pos: (jnp.minimum(i, npt - 1), 0)),
                     pl.BlockSpec((TM_E, D_MODEL), lambda i, pos: (jnp.maximum(i - npt, 0), 0))]
        out_shape = [jax.ShapeDtypeStruct((T_PROMPT, D_MODEL), f32),
                     jax.ShapeDtypeStruct((T_SAMPLE, D_MODEL), f32)]
    else:
        out_specs = pl.BlockSpec((TM_E, D_MODEL), lambda i, pos: (i, 0))
        out_shape = jax.ShapeDtypeStruct((T_ALL, D_MODEL), f32)
    return pl.pallas_call(
        _combine_split_kernel if split else _combine_kernel,
        grid_spec=pltpu.PrefetchScalarGridSpec(
            num_scalar_prefetch=1,
            grid=(T_ALL // TM_E,),
            in_specs=[pl.BlockSpec(memory_space=pl.ANY)],
            out_specs=out_specs,
            scratch_shapes=[pltpu.SemaphoreType.DMA(())]),
        out_shape=out_shape,
        compiler_params=pltpu.CompilerParams(dimension_semantics=("arbitrary",),
                                             vmem_limit_bytes=VMEM_LIMIT),
        name="moe_combine",
    )(pos, ys)


ROWS_E = TM_E // EPG
N_XBUF = 3


def _experts_kernel(tg_ref, tv_ref, ti_ref, tu_ref, tw_ref, src_ref, xg_hbm, nw_ref, wg_ref, wu_ref, wd_ref, nf_ref,
                    ys_ref, xbuf, hbuf, ybuf, gsem, *, final_norm):
    del tw_ref
    t = pl.program_id(0)
    e = pl.program_id(1)
    valid = tv_ref[t] == 1
    used = tu_ref[t * EPG + e] == 1
    slot = t % N_XBUF
    ahead = ti_ref[jnp.minimum(t + 2, NT_E - 1)]
    aslot = (t + 2) % N_XBUF
    is_last = ti_ref[jnp.minimum(t + 1, NT_E - 1)] == t

    @pl.when((t == 0) & (e == 0))
    def _():
        _start_row_copies(src_ref, 0, xg_hbm, xbuf.at[0], gsem.at[0], 0, TM_E, inline=False)
        _start_row_copies(src_ref, ti_ref[1] * TM_E, xg_hbm, xbuf.at[1], gsem.at[1], 0, TM_E, inline=False)

    @pl.when(valid & (e == 0))
    def _():
        _wait_row_copies(xg_hbm, xbuf.at[slot], gsem.at[slot])
        x2 = xbuf[slot, :, pl.ds(0, D_MODEL)]
        hbuf[...] = _rms(x2, nw_ref[...]).astype(bf16)
        ybuf[...] = x2

    @pl.when(valid & jnp.logical_not(used))
    def _():
        _start_row_copies(src_ref, ahead * TM_E, xg_hbm, xbuf.at[aslot], gsem.at[aslot],
                          e * ROWS_E, ROWS_E, inline=False)

    @pl.when(valid & used)
    def _():
        _start_row_copies(src_ref, ahead * TM_E, xg_hbm, xbuf.at[aslot], gsem.at[aslot],
                          e * ROWS_E, ROWS_E, inline=True)
        hb = hbuf[...]
        he = (_silu(jnp.dot(hb, wg_ref[0, 0], preferred_element_type=f32))
              * jnp.dot(hb, wu_ref[0, 0], preferred_element_type=f32))
        yd = jnp.dot(he.astype(bf16), wd_ref[0, 0], preferred_element_type=f32)
        g = xbuf[slot, :, pl.ds(D_MODEL, 128)]
        lane = lax.broadcasted_iota(i32, g.shape, 1)
        gcol = jnp.sum(jnp.where(lane == LANE_E0 + EPG * tg_ref[t] + e, g, 0.0), axis=-1, keepdims=True)
        ybuf[...] += gcol * yd

    @pl.when(e == EPG - 1)
    def _():
        y = ybuf[...]
        if final_norm:
            y = _rms(y, nf_ref[...])
        ys_ref[...] = y

    @pl.when(valid & is_last & (e == EPG - 1))
    def _():
        for k in (1, 2):
            s = (t + k) % N_XBUF
            _wait_row_copies(xg_hbm, xbuf.at[s], gsem.at[s])


def _experts(tile_group, tile_valid, tile_idx, tile_used, tile_wexp, src, xg, nw, wg, wu, wd, nf, l, final_norm):
    wsel = lambda t, e, tg, tv, ti, tu, tw, src: (l, tg[t] * EPG + tw[t * EPG + e], 0, 0)
    return pl.pallas_call(
        functools.partial(_experts_kernel, final_norm=final_norm),
        grid_spec=pltpu.PrefetchScalarGridSpec(
            num_scalar_prefetch=6,
            grid=(NT_E, EPG),
            in_specs=[pl.BlockSpec(memory_space=pl.ANY),
                      pl.BlockSpec((1, D_MODEL), lambda t, e, tg, tv, ti, tu, tw, src: (0, 0)),
                      pl.BlockSpec((1, 1, D_MODEL, D_FF_E), wsel),
                      pl.BlockSpec((1, 1, D_MODEL, D_FF_E), wsel),
                      pl.BlockSpec((1, 1, D_FF_E, D_MODEL), wsel),
                      pl.BlockSpec((1, D_MODEL), lambda t, e, tg, tv, ti, tu, tw, src: (0, 0))],
            out_specs=pl.BlockSpec((TM_E, D_MODEL), lambda t, e, tg, tv, ti, tu, tw, src: (ti[t], 0)),
            scratch_shapes=[pltpu.VMEM((N_XBUF, TM_E, XG_W), f32), pltpu.VMEM((TM_E, D_MODEL), bf16),
                            pltpu.VMEM((TM_E, D_MODEL), f32), pltpu.SemaphoreType.DMA((N_XBUF,))]),
        out_shape=jax.ShapeDtypeStruct((T_SORT, D_MODEL), f32),
        compiler_params=pltpu.CompilerParams(dimension_semantics=("arbitrary", "arbitrary"),
                                             vmem_limit_bytes=VMEM_LIMIT),
        name="moe_experts",
    )(tile_group, tile_valid, tile_idx, tile_used, tile_wexp, src, xg, nw, wg, wu, wd, nf)


def _route_meta(xg, cnt_rows):
    b = xg[:, D_MODEL + LANE_G].astype(i32)
    rank = xg[:, D_MODEL + LANE_RANK].astype(i32)
    bcnt = cnt_rows[0, :N_BUCKETS].astype(i32).reshape(N_GROUPS, N_PAIRS)
    cnt = jnp.sum(bcnt, axis=1)
    padded = ((cnt + TM_E - 1) // TM_E) * TM_E
    off_end = jnp.cumsum(padded)
    off = off_end - padded
    bstart = (off[:, None] + jnp.cumsum(bcnt, axis=1) - bcnt).reshape(N_BUCKETS)
    bend = bstart + bcnt.reshape(N_BUCKETS)
    bid = jnp.arange(N_BUCKETS, dtype=i32)
    pos = rank + jnp.sum(jnp.where(b[:, None] == bid[None, :], bstart[None, :], 0), axis=1)
    tile_start = jnp.arange(NT_E, dtype=i32) * TM_E
    tile_group = jnp.minimum(jnp.sum((tile_start[:, None] >= off_end[None, :]).astype(i32), axis=1), N_GROUPS - 1)
    tile_valid = (tile_start < off_end[-1]).astype(i32)
    n_used = off_end[-1] // TM_E
    tile_idx = jnp.minimum(jnp.arange(NT_E, dtype=i32), n_used - 1)
    tile_group = jnp.take(tile_group, tile_idx)
    ts = (tile_idx * TM_E)[:, None]
    overlap = (bstart[None, :] < ts + TM_E) & (bend[None, :] > ts) & (bend > bstart)[None, :]
    pair_has = jnp.array([[int(e in p) for e in range(EPG)] for p in PAIRS] * N_GROUPS, dtype=i32)
    used = jnp.sum(overlap.astype(i32)[:, :, None] * pair_has[None, :, :], axis=1) > 0
    eidx = jnp.arange(EPG, dtype=i32)[None, :]
    prev = lax.cummax(jnp.where(used, eidx, -1), axis=1)
    nxt = lax.cummin(jnp.where(used, eidx, EPG), axis=1, reverse=True)
    wexp = jnp.where(prev >= 0, prev, jnp.minimum(nxt, EPG - 1))
    return (pos, tile_group, tile_valid, tile_idx, off + cnt, off_end,
            used.astype(i32).reshape(-1), wexp.reshape(-1))


def _pad_cols(a, n):
    return jnp.pad(a, ((0, 0), (0, n - a.shape[1])))


def _pad_rows(a, n):
    return jnp.pad(a, ((0, n - a.shape[0]), (0, 0)))


def _prep_w_in(w):
    pad_last = lambda a, n: jnp.pad(a, ((0, 0), (0, 0), (0, n - a.shape[-1])))
    c = w[..., 2568:]
    parts = [w[..., 0:2048], pad_last(w[..., 2048:2056], 128), w[..., 2056:2568],
             c[..., 0:768], pad_last(c[..., 768:832], 128), pad_last(c[..., 832:896], 128), c[..., 896:1024]]
    return jnp.concatenate(parts, axis=-1).astype(bf16)


def _prep_mu(mu):
    m = mu[None, :]
    return jnp.concatenate([m[:, 0:768], _pad_cols(m[:, 768:832], 128), _pad_cols(m[:, 832:896], 128),
                            m[:, 896:1024]], axis=1)


def _sgu_mats(sgu_w, sgu_b):
    t = jnp.arange(SGU_CHUNK)
    wm = jnp.where(t[:, None] >= t[None, :], sgu_w, 0.0)
    bias_p = jnp.repeat(jnp.transpose(sgu_b), DH_B, axis=1)
    small = jnp.zeros((4, SEQ_PAD, SEQ_PAD), f32).at[:, SEQ_LEAD:, SEQ_LEAD:].set(wm[:, :DEC_SEQ, :DEC_SEQ])
    eye16 = jnp.eye(TB_S // SEQ_PAD, dtype=f32)
    wm_s = jnp.einsum('ab,hij->haibj', eye16, small).reshape(4, TB_S, TB_S)
    bias_small = jnp.zeros((SEQ_PAD, D_B), f32).at[SEQ_LEAD:].set(bias_p[:DEC_SEQ])
    bias_s = jnp.tile(bias_small, (TB_S // SEQ_PAD, 1))
    return wm.astype(bf16), bias_p, wm_s.astype(bf16), bias_s


def _row(a, n):
    return _pad_cols(a.reshape(1, -1), n)


def kernel(x_prompt, x_sample, state_gdn, state_gdn_conv, state_rwkv, state_rwkv_shift, norm_mix, norm_ffn, norm_final, w_in, gdn_conv_w, gdn_a_log, gdn_dt_bias, gdn_norm_w, sgu_ln_w, sgu_ln_b, sgu_w, sgu_b, rwkv_mu, rwkv_w0, rwkv_w_up, rwkv_a0, rwkv_a_up, rwkv_g_up, rwkv_k_k, rwkv_k_a, rwkv_r_k, rwkv_ln_w, rwkv_ln_b, w_out, router_group_w, router_group_b, router_expert_w, router_expert_b, expert_w_gate, expert_w_up, expert_w_down):
    x = (x_prompt.reshape(T_PROMPT, D_MODEL),
         jnp.pad(x_sample, ((0, 0), (SEQ_LEAD, 0), (0, 0))).reshape(T_SAMPLE, D_MODEL))
    sample_states = ()

    w_pad = _prep_w_in(w_in)
    w_out_b = w_out.astype(bf16)
    wg_b, wu_b, wd_b = expert_w_gate.astype(bf16), expert_w_up.astype(bf16), expert_w_down.astype(bf16)

    outs = {k: [] for k in ('gdn_p', 'conv_p', 'rwkv_p', 'shift_p', 'gdn_s', 'conv_s', 'rwkv_s', 'shift_s', 'cv_s')}
    for l in range(DEPTH):
        last = l == DEPTH - 1
        v128 = jnp.concatenate([
            jnp.pad(gdn_a_log[l].reshape(1, H_A), ((0, 0), (4, 120))),
            jnp.pad(gdn_dt_bias[l].reshape(1, H_A), ((0, 0), (4, 120))),
            gdn_norm_w[l].reshape(1, HD_A), jnp.zeros((5, 128), f32)], axis=0)
        v256 = jnp.concatenate([a.reshape(1, D_C) for a in (
            sgu_ln_w[l], sgu_ln_b[l], rwkv_w0[l], rwkv_a0[l], rwkv_k_k[l], rwkv_k_a[l], rwkv_r_k[l],
            rwkv_ln_w[l], rwkv_ln_b[l])] + [jnp.zeros((7, D_C), f32)], axis=0)
        wm_p, bias_p, wm_s, bias_s = _sgu_mats(sgu_w[l], sgu_b[l])
        common = (gdn_conv_w[l], v128, v256)
        tail = (_prep_mu(rwkv_mu[l]), _pad_rows(rwkv_w_up[l], 128).astype(bf16),
                _pad_rows(rwkv_a_up[l], 128).astype(bf16), rwkv_g_up[l].astype(bf16))
        mw_p = common + (wm_p, bias_p) + tail
        mw_s = common + (wm_s, bias_s) + tail

        if l == 0:
            proj, h = _inproj(x, norm_mix[l].reshape(1, D_MODEL), w_pad, l)
        else:
            proj, h, x_tok = _inproj_gather(pos, ys, norm_mix[l].reshape(1, D_MODEL), w_pad, l)
            x = (x_tok,)
        pcf = _mm(state_rwkv_shift[l], w_pad[l, :, OFF_C:]).reshape(DEC_BATCH, 1, NC_PAD)
        mix_p, gdn_p, conv_p, rwkv_p = _mixer_prompt(proj, mw_p)
        mix_s, gdn_s, conv_s, rwkv_s, cv_s = _mixer_sample(
            proj, state_gdn, state_gdn_conv, state_rwkv, pcf, mw_s, l, sample_states)
        sample_states = (gdn_s, rwkv_s)

        rw = _pad_cols(jnp.concatenate([router_group_w[l], router_expert_w[l]], axis=1), 128)
        rwh = rw.astype(bf16)
        rwl = (rw - rwh.astype(f32)).astype(bf16)
        rb = _row(jnp.concatenate([router_group_b[l], router_expert_b[l]]), 128)
        nw_ffn = norm_ffn[l].reshape(1, D_MODEL)
        xg, cnt_rows = _outproj(x, mix_p, mix_s, w_out_b, nw_ffn, rwh, rwl, rb, l)
        pos, tile_group, tile_valid, tile_idx, pad_lo, pad_hi, tile_used, tile_wexp = _route_meta(xg, cnt_rows)
        src = _invert(pos, pad_lo, pad_hi)
        ys = _experts(tile_group, tile_valid, tile_idx, tile_used, tile_wexp, src, xg, nw_ffn, wg_b, wu_b, wd_b,
                      norm_final.reshape(1, D_MODEL), l, final_norm=last)
        if last:
            x = _combine(pos, ys, split=True)

        outs['gdn_p'].append(gdn_p)
        outs['conv_p'].append(conv_p)
        outs['rwkv_p'].append(rwkv_p)
        outs['shift_p'].append(h[SEQ // SEQ_PAD - 1:T_PROMPT // SEQ_PAD:SEQ // SEQ_PAD])
        outs['conv_s'].append(conv_s)
        outs['shift_s'].append(h[T_PROMPT // SEQ_PAD:])
        outs['cv_s'].append(cv_s.reshape(DEC_BATCH, SEQ_PAD, D_B)[:, SEQ_LEAD:])

    y_prompt = x[0].reshape(BATCH, SEQ, D_MODEL)
    y_sample = x[1].reshape(DEC_BATCH, SEQ_PAD, D_MODEL)[:, SEQ_LEAD:]
    st = lambda k: jnp.stack(outs[k])
    return (y_prompt, y_sample, st('gdn_p'), st('conv_p'), st('rwkv_p'), st('shift_p'),
            sample_states[0], st('conv_s'), sample_states[1], st('shift_s'), st('cv_s'))
```

```python
import functools
import math

import jax
import jax.numpy as jnp
from jax import lax
from jax.experimental import pallas as pl
from jax.experimental.pallas import tpu as pltpu

f32 = jnp.float32
bf16 = jnp.bfloat16
i32 = jnp.int32

D_MODEL = 1024
BATCH = 8
SEQ = 2048
DEPTH = 2
DEC_BATCH = 128
DEC_SEQ = 4
H_A = 4
HD_A = 128
D_A = 512
CONV_W = 4
D_B = 256
DH_B = 64
SGU_CHUNK = 128
H_C = 4
N_C = 64
D_C = 256
N_GROUPS = 4
EPG = 4
D_FF_E = 512
NORM_EPS = 1e-6
LN_EPS = 1e-5
GN_EPS = 64e-5

SEQ_PAD = 8
SEQ_LEAD = SEQ_PAD - DEC_SEQ
T_PROMPT = BATCH * SEQ
T_SAMPLE = DEC_BATCH * SEQ_PAD
T_ALL = T_PROMPT + T_SAMPLE
TB_P = 512
TB_S = 128
GDN_CHUNK = 64

OFF_Q, OFF_K, OFF_V, OFF_Z, OFF_BA, OFF_U, OFF_VB, OFF_C = 0, 512, 1024, 1536, 2048, 2176, 2432, 2688
NP_IN = 3840
NC_PAD = NP_IN - OFF_C

N_PAIRS = 6
N_BUCKETS = N_GROUPS * N_PAIRS
PAIRS = ((0, 1), (0, 2), (0, 3), (1, 2), (1, 3), (2, 3))
XG_W = D_MODEL + 128
LANE_G, LANE_RANK, LANE_E0 = 0, 1, 4

VMEM_LIMIT = 48 * 1024 * 1024


NN = (((1,), (0,)), ((), ()))
NT = (((1,), (1,)), ((), ()))
TN = (((0,), (0,)), ((), ()))
BNN = (((2,), (1,)), ((0,), (0,)))


def _bdot(a, b, dims=NN):
    return lax.dot_general(a.astype(bf16), b.astype(bf16), dims, preferred_element_type=f32)


def _split2(x):
    hi = x.astype(bf16)
    lo = (x - hi.astype(f32)).astype(bf16)
    return hi, lo


def _split3(x):
    hi = x.astype(bf16)
    r = x - hi.astype(f32)
    mid = r.astype(bf16)
    lo = (r - mid.astype(f32)).astype(bf16)
    return hi, mid, lo


def _dot01_left(m01, x):
    hi, mid, lo = _split3(x)
    d = lambda p: lax.dot_general(m01, p, NN, preferred_element_type=f32)
    return d(hi) + d(mid) + d(lo)


def _dot01_right(x, m01):
    hi, lo = _split2(x)
    d = lambda p: lax.dot_general(p, m01, NN, preferred_element_type=f32)
    return d(hi) + d(lo)


def _softplus(x):
    return jnp.maximum(x, 0.0) + jnp.log(1.0 + jnp.exp(-jnp.abs(x)))


def _sigmoid(x):
    return 0.5 * jnp.tanh(0.5 * x) + 0.5


def _silu(x):
    return x * _sigmoid(x)


def _gelu(x):
    return 0.5 * x * (1.0 + lax.erf(x * (1.0 / math.sqrt(2.0))))


def _rms(x, w):
    return x * lax.rsqrt(jnp.mean(x * x, axis=-1, keepdims=True) + NORM_EPS) * w


TM_A = 512
N_SLAB = 768


def _start_row_copies(idx_ref, base, src_hbm, dst, sem, r0, n, inline):
    def one(r):
        p = idx_ref[base + r]
        pltpu.make_async_copy(src_hbm.at[pl.ds(p, 1), :], dst.at[pl.ds(r, 1), :], sem).start()

    if inline:
        for k in range(n):
            one(r0 + k)
    else:
        def body(k, c):
            one(r0 + k)
            return c
        lax.fori_loop(0, n, body, 0, unroll=DMA_UNROLL)


def _wait_row_copies(src_hbm, dst, sem):
    pltpu.make_async_copy(src_hbm.at[pl.ds(0, dst.shape[0]), :], dst, sem).wait()


def _inproj_kernel(*refs, split):
    if split:
        xp_ref, xs_ref, nw_ref, w_ref, proj_ref, h8_ref, hscr = refs
        x = jnp.where(pl.program_id(0) < T_PROMPT // TM_A, xp_ref[...], xs_ref[...])
    else:
        x_ref, nw_ref, w_ref, proj_ref, h8_ref, hscr = refs
        x = x_ref[...]
    _inproj_body(x, nw_ref, w_ref, proj_ref, h8_ref, hscr)


def _inproj_gather_kernel(pos_ref, ys_hbm, nw_ref, w_ref, proj_ref, h8_ref, x_ref, hscr, xbuf, sem):
    i = pl.program_id(0)
    n = pl.num_programs(0)
    slot = i % 2

    @pl.when(i == 0)
    def _():
        _start_row_copies(pos_ref, 0, ys_hbm, xbuf.at[0], sem.at[0], 0, TM_A, inline=False)

    _wait_row_copies(ys_hbm, xbuf.at[slot], sem.at[slot])
    x = xbuf[slot]
    x_ref[...] = x
    nxt = jnp.minimum(i + 1, n - 1)
    _start_row_copies(pos_ref, nxt * TM_A, ys_hbm, xbuf.at[1 - slot], sem.at[1 - slot], 0, TM_A, inline=True)
    _inproj_body(x, nw_ref, w_ref, proj_ref, h8_ref, hscr)

    @pl.when(i == n - 1)
    def _():
        _wait_row_copies(ys_hbm, xbuf.at[1 - slot], sem.at[1 - slot])


def _inproj_body(x, nw_ref, w_ref, proj_ref, h8_ref, hscr):
    h = _rms(x, nw_ref[...])
    for k in range(D_MODEL // 128):
        hscr[k] = h[:, k * 128:(k + 1) * 128]
        h8_ref[:, pl.ds(k * 128, 128)] = hscr[k, pl.ds(SEQ_PAD - 1, TM_A // SEQ_PAD, stride=SEQ_PAD), :]
    hb = h.astype(bf16)
    for n in range(NP_IN // N_SLAB):
        sl = pl.ds(n * N_SLAB, N_SLAB)
        proj_ref[:, sl] = jnp.dot(hb, w_ref[0, :, sl], preferred_element_type=f32)


def _inproj_gather(pos, ys, nw, w_pad, l):
    return pl.pallas_call(
        _inproj_gather_kernel,
        grid_spec=pltpu.PrefetchScalarGridSpec(
            num_scalar_prefetch=1,
            grid=(T_ALL // TM_A,),
            in_specs=[pl.BlockSpec(memory_space=pl.ANY),
                      pl.BlockSpec((1, D_MODEL), lambda i, pos: (0, 0)),
                      pl.BlockSpec((1, D_MODEL, NP_IN), lambda i, pos: (l, 0, 0))],
            out_specs=[pl.BlockSpec((TM_A, NP_IN), lambda i, pos: (i, 0)),
                       pl.BlockSpec((TM_A // SEQ_PAD, D_MODEL), lambda i, pos: (i, 0)),
                       pl.BlockSpec((TM_A, D_MODEL), lambda i, pos: (i, 0))],
            scratch_shapes=[pltpu.VMEM((D_MODEL // 128, TM_A, 128), f32),
                            pltpu.VMEM((2, TM_A, D_MODEL), f32),
                            pltpu.SemaphoreType.DMA((2,))]),
        out_shape=[jax.ShapeDtypeStruct((T_ALL, NP_IN), f32),
                   jax.ShapeDtypeStruct((T_ALL // SEQ_PAD, D_MODEL), f32),
                   jax.ShapeDtypeStruct((T_ALL, D_MODEL), f32)],
        compiler_params=pltpu.CompilerParams(dimension_semantics=("arbitrary",),
                                             vmem_limit_bytes=VMEM_LIMIT),
        name="inproj_gather",
    )(pos, ys, nw, w_pad)


def _inproj(xs, nw, w_pad, l):
    split = len(xs) == 2
    npt = T_PROMPT // TM_A
    if split:
        x_specs = [pl.BlockSpec((TM_A, D_MODEL), lambda i: (jnp.minimum(i, npt - 1), 0)),
                   pl.BlockSpec((TM_A, D_MODEL), lambda i: (jnp.maximum(i - npt, 0), 0))]
    else:
        x_specs = [pl.BlockSpec((TM_A, D_MODEL), lambda i: (i, 0))]
    return pl.pallas_call(
        functools.partial(_inproj_kernel, split=split),
        grid=(T_ALL // TM_A,),
        in_specs=x_specs + [pl.BlockSpec((1, D_MODEL), lambda i: (0, 0)),
                            pl.BlockSpec((1, D_MODEL, NP_IN), lambda i: (l, 0, 0))],
        out_specs=[pl.BlockSpec((TM_A, NP_IN), lambda i: (i, 0)),
                   pl.BlockSpec((TM_A // SEQ_PAD, D_MODEL), lambda i: (i, 0))],
        out_shape=[jax.ShapeDtypeStruct((T_ALL, NP_IN), f32),
                   jax.ShapeDtypeStruct((T_ALL // SEQ_PAD, D_MODEL), f32)],
        scratch_shapes=[pltpu.VMEM((D_MODEL // 128, TM_A, 128), f32)],
        compiler_params=pltpu.CompilerParams(dimension_semantics=("arbitrary",),
                                             vmem_limit_bytes=VMEM_LIMIT),
        name="inproj",
    )(*xs, nw, w_pad)


def _mm_kernel(a_ref, b_ref, o_ref):
    o_ref[...] = jnp.dot(a_ref[...].astype(bf16), b_ref[...], preferred_element_type=f32)


def _mm(a, b):
    return pl.pallas_call(
        _mm_kernel,
        out_shape=jax.ShapeDtypeStruct((a.shape[0], b.shape[1]), f32),
        compiler_params=pltpu.CompilerParams(vmem_limit_bytes=VMEM_LIMIT),
        name="shift_proj",
    )(a, b)


def _cat(parts, axis):
    return parts[0] if len(parts) == 1 else jnp.concatenate(parts, axis=axis)


def _mixer_kernel(*refs, TB, C, G, per_seq, n_alias=0):
    it = iter(refs)
    proj_ref = next(it)
    if per_seq:
        gdn_in, conv_in, rwkv_in, pcf_in = next(it), next(it), next(it), next(it)
    convw_ref, v128_ref, v256_ref, sguw_ref, sgub_ref, mu_ref, wup_ref, aup_ref, gup_ref = (
        next(it) for _ in range(9))
    for _ in range(n_alias):
        next(it)
    mix_ref, gdn_out, conv_out, rwkv_out = next(it), next(it), next(it), next(it)
    cv_out = next(it) if per_seq else None
    xp, pp = next(it), next(it)

    nchunk = TB // C
    ngrp = C // G
    iters = int(math.log2(G)) - 1

    if per_seq:
        xp[pl.ds(0, 8), :] = jnp.zeros((8, 3 * D_A), f32)
        pp[pl.ds(0, 8), :] = jnp.zeros((8, NC_PAD), f32)
    else:
        @pl.when(pl.program_id(1) == 0)
        def _():
            xp[pl.ds(0, 8), :] = jnp.zeros((8, 3 * D_A), f32)
            pp[pl.ds(0, 8), :] = jnp.zeros((8, NC_PAD), f32)
            gdn_out[...] = jnp.zeros(gdn_out.shape, f32)
            rwkv_out[...] = jnp.zeros(rwkv_out.shape, f32)
    xp[pl.ds(8, TB), :] = proj_ref[:, pl.ds(OFF_Q, 3 * D_A)]
    pp[pl.ds(8, TB), :] = proj_ref[:, pl.ds(OFF_C, NC_PAD)]
    if per_seq:
        for s in range(TB // SEQ_PAD):
            r0 = 8 + s * SEQ_PAD
            xp[pl.ds(r0 + SEQ_LEAD - (CONV_W - 1), CONV_W - 1), :] = conv_in[s]
            pp[pl.ds(r0 + SEQ_LEAD - 1, 1), :] = pcf_in[s]

    rowi = lax.broadcasted_iota(i32, (TB, 1), 0)
    live = (rowi % SEQ_PAD) >= SEQ_LEAD if per_seq else None

    ii = lax.broadcasted_iota(i32, (C, C), 0)
    jj = lax.broadcasted_iota(i32, (C, C), 1)
    same = (ii // G) == (jj // G)
    causal = (ii >= jj) & same
    strict = (ii > jj) & same
    eye = ii == jj
    m_cum = causal.astype(bf16)
    m_grp = same.astype(bf16)

    def conv_cols(r0, c0):
        cs = pl.ds(c0, 128)
        acc = xp[pl.ds(r0 + 8, C), cs] * convw_ref[pl.ds(3, 1), cs]
        for j in range(CONV_W - 1):
            acc = acc + xp[pl.ds(r0 + 5 + j, C), cs] * convw_ref[pl.ds(j, 1), cs]
        return _silu(acc)

    alog_row = v128_ref[pl.ds(0, 1), :]
    dtb_row = v128_ref[pl.ds(1, 1), :]
    gnorm_w = v128_ref[pl.ds(2, 1), :]
    ln_w, ln_b = v256_ref[pl.ds(0, 1), :], v256_ref[pl.ds(1, 1), :]
    w0, a0 = v256_ref[pl.ds(2, 1), :], v256_ref[pl.ds(3, 1), :]
    k_k, k_a, r_k = v256_ref[pl.ds(4, 1), :], v256_ref[pl.ds(5, 1), :], v256_ref[pl.ds(6, 1), :]
    rln_w, rln_b = v256_ref[pl.ds(7, 1), :], v256_ref[pl.ds(8, 1), :]

    l64i = lax.broadcasted_iota(i32, (D_C, D_C), 0) // N_C
    l64j = lax.broadcasted_iota(i32, (D_C, D_C), 1) // N_C
    seg64 = (l64i == l64j).astype(bf16)
    gdn, rwk, lmats = [None] * nchunk, [None] * nchunk, [None] * (8 * nchunk)

    def phase1(c):
        R0 = c * C
        rows = pl.ds(R0, C)
        live_c = live[R0:R0 + C] if per_seq else None

        ba = proj_ref[rows, pl.ds(OFF_BA, 128)]
        beta_all = _sigmoid(ba)
        g_all = -jnp.exp(alog_row) * _softplus(ba + dtb_row)
        if per_seq:
            beta_all = jnp.where(live_c, beta_all, 0.0)
            g_all = jnp.where(live_c, g_all, 0.0)
        pcur = pp[pl.ds(R0 + 8, C), :]
        pprev = pp[pl.ds(R0 + 7, C), :]
        pm = pcur + mu_ref[...] * (pprev - pcur)
        r_ = pm[:, 0:D_C]
        kc = pm[:, D_C:2 * D_C]
        vc = pm[:, 2 * D_C:3 * D_C]
        wd = pm[:, 3 * D_C:3 * D_C + 128]
        ad = pm[:, 3 * D_C + 128:3 * D_C + 256]
        gd = pm[:, 3 * D_C + 256:3 * D_C + 384]
        wl_mm = _bdot(jnp.tanh(wd), wup_ref[...])
        a_mm = _bdot(ad, aup_ref[...])
        gate = _bdot(_sigmoid(gd), gup_ref[...])
        kk = kc * k_k
        kk_ss = _dot01_right(kk * kk, seg64)
        yield

        gc_all = _dot01_left(m_cum, g_all)
        total = lambda x: jnp.sum(x, axis=0, keepdims=True) if ngrp == 1 else _dot01_left(m_grp, x)
        gl_all = total(g_all)
        w_log = -_softplus(-(w0 + wl_mm)) - 0.5
        logw = -jnp.exp(w_log)
        a_ = _sigmoid(a0 + a_mm)
        kk = kk * lax.rsqrt(kk_ss + 1e-6)
        kc2 = kc * (1.0 + (a_ - 1.0) * k_a)
        if per_seq:
            logw = jnp.where(live_c, logw, 0.0)
            kk = jnp.where(live_c, kk, 0.0)
            kc2 = jnp.where(live_c, kc2, 0.0)
        b_ = kk * a_
        lw_hi, lw_lo = _split2(logw)
        Gc = (lax.dot_general(m_cum, lw_hi, NN, preferred_element_type=f32)
              + lax.dot_general(m_cum, lw_lo, NN, preferred_element_type=f32))
        Gl = total(logw)
        bonus_ss = _dot01_right(r_ * kc2 * r_k, seg64)
        yield

        heads = []
        for h in range(H_A):
            q = conv_cols(R0, OFF_Q + h * HD_A)
            k = conv_cols(R0, OFF_K + h * HD_A)
            v = conv_cols(R0, OFF_V + h * HD_A)
            q = q * lax.rsqrt(jnp.sum(q * q, axis=-1, keepdims=True) + 1e-6) * (HD_A ** -0.5)
            k = k * lax.rsqrt(jnp.sum(k * k, axis=-1, keepdims=True) + 1e-6)
            if per_seq:
                k = jnp.where(live_c, k, 0.0)
            beta = beta_all[:, h:h + 1]
            gcol = gc_all[:, 4 + h:5 + h]
            glr = gl_all[:, 4 + h:5 + h]
            grow = jnp.sum(jnp.where(eye, jnp.broadcast_to(gcol, (C, C)), 0.0), axis=0, keepdims=True)
            decay = jnp.where(causal, jnp.exp(jnp.where(causal, gcol - grow, 0.0)), 0.0)
            eg = jnp.exp(gcol)
            kb = k * beta
            kq = _bdot(jnp.concatenate([kb, q], axis=0), k, NT)
            heads.append(dict(kq=kq, decay=decay,
                              rhs=jnp.concatenate([v * beta, kb * eg], axis=1),
                              q_dec=q * eg, k_dec=k * jnp.exp(glr - gcol), glr=glr))
            yield
        gdn[c] = heads
        e_neg = jnp.exp(-Gc)
        e_rem = jnp.exp(Gl - Gc)
        rG = r_ * jnp.exp(Gc)
        kkG = kk * jnp.exp(Gc - logw)
        kN = kc2 * e_neg
        bN = b_ * e_neg
        aalls = []
        for h in range(H_C):
            hs = slice(h * N_C, (h + 1) * N_C)
            aalls.append(_bdot(jnp.concatenate([kkG[:, hs], rG[:, hs]], axis=0),
                               jnp.concatenate([bN[:, hs], kN[:, hs]], axis=0), NT))
        yield

        for h in range(H_A):
            d = gdn[c][h]
            kq = d.pop('kq')
            decay = d.pop('decay')
            lmats[8 * c + h] = jnp.where(strict, kq[:C] * decay, 0.0)
            d['attn'] = kq[C:] * decay
        yield
        heads = []
        for h in range(H_C):
            hs = slice(h * N_C, (h + 1) * N_C)
            aall = aalls[h]
            lmats[8 * c + H_A + h] = jnp.where(strict, aall[:C, :C], 0.0)
            akk_k = jnp.where(strict, aall[:C, C:], 0.0)
            ar = jnp.concatenate([jnp.where(causal, aall[C:, C:], 0.0),
                                  -jnp.where(causal, aall[C:, :C], 0.0)], axis=1)
            heads.append(dict(x1=_bdot(akk_k, vc[:, hs]), ar=ar))
        rwk[c] = dict(heads=heads, vc=vc, rG=rG, kkG=kkG, kdec=kc2 * e_rem, bdec=b_ * e_rem,
                      e_last=jnp.exp(Gl), gate=gate, bonus=bonus_ss * vc)
        yield

    def par(*gens):
        gens = list(gens)
        while gens:
            alive = []
            for g in gens:
                try:
                    next(g)
                    alive.append(g)
                except StopIteration:
                    pass
            gens = alive
            yield

    def seq(*gens):
        for g in gens:
            yield from g

    def lockstep(gens):
        for _ in par(*gens):
            pass

    nmats = [None] * (8 * nchunk)

    def inverses(chunks):
        idx = [8 * c + k for c in chunks for k in range(8)]
        bmm = lambda a, b: lax.dot_general(a.astype(bf16), b.astype(bf16), BNN, preferred_element_type=f32)
        pack = 2 * C <= 128
        if pack:
            N = -jnp.stack([jnp.concatenate([lmats[i], lmats[j]], axis=1) for i, j in zip(idx[0::2], idx[1::2])])
            left = lax.broadcasted_iota(i32, N.shape, 2) < C
            rhs = lambda X: jnp.concatenate([jnp.where(left, X, 0.0), jnp.where(left, 0.0, X)], axis=1)
        else:
            N = -jnp.stack([lmats[i] for i in idx])
            rhs = lambda X: X
        Q = bmm(N, rhs(N))
        yield
        for _ in range(iters - 1):
            R = bmm(jnp.concatenate([N, Q], axis=1), rhs(Q))
            N = N + Q + R[:, :C]
            Q = R[:, C:]
            yield
        N = N + Q + bmm(N, rhs(Q))
        if pack:
            for p, (i, j) in enumerate(zip(idx[0::2], idx[1::2])):
                nmats[i], nmats[j] = N[p][:, :C], N[p][:, C:]
        else:
            for p, i in enumerate(idx):
                nmats[i] = N[p]
        yield

    def phase3(c):
        for h in range(H_A):
            d = gdn[c][h]
            sol = d['rhs'] + _bdot(nmats[8 * c + h], d['rhs'])
            d['u'], d['w'] = sol[:, :HD_A], sol[:, HD_A:]
        for h in range(H_C):
            hs = slice(h * N_C, (h + 1) * N_C)
            d = rwk[c]['heads'][h]
            both = jnp.concatenate([d['x1'], rwk[c]['kkG'][:, hs]], axis=1)
            both = both + _bdot(nmats[8 * c + H_A + h], both)
            d['u_p'], d['w_p'] = both[:, :N_C], both[:, N_C:]
            d['both'] = both
        yield
        if per_seq:
            return
        for h in range(H_A):
            d = gdn[c][h]
            wu = jnp.concatenate([d['w'], d['u']], axis=1)
            a_wu = _bdot(d['attn'], wu)
            k_wu = _bdot(d['k_dec'], wu, TN)
            d['qe'] = d['q_dec'] - a_wu[:, :HD_A]
            d['o0'] = a_wu[:, HD_A:]
            d['m'] = -k_wu[:, :HD_A]
            d['b'] = k_wu[:, HD_A:]
        yield
        for h in range(H_C):
            hs = slice(h * N_C, (h + 1) * N_C)
            rc = rwk[c]
            d = rc['heads'][h]
            V = rc['vc'][:, hs]
            x = _bdot(d['ar'][:, C:], d['both'])
            d['y0'] = _bdot(d['ar'][:, :C], V) + x[:, :N_C]
            d['re'] = rc['rG'][:, hs] + x[:, N_C:]
            d['m'] = _bdot(d['w_p'], rc['bdec'][:, hs], TN)
            d['b'] = _bdot(jnp.concatenate([V, -d['u_p']], axis=0),
                           jnp.concatenate([rc['kdec'][:, hs], rc['bdec'][:, hs]], axis=0), TN)
        yield

    def gdn_head_seq(c, h):
        rows = pl.ds(c * C, C)
        d = gdn[c][h]
        S = gdn_out[0, h]
        Sb = S.astype(bf16)
        gdn_out[0, h] = S * jnp.exp(d['glr'][-1:, :]) + _bdot(d['m'], Sb) + d['b']
        o = d['o0'] + _bdot(d['qe'], Sb)
        yield
        o = o * lax.rsqrt(jnp.mean(o * o, axis=-1, keepdims=True) + NORM_EPS) * gnorm_w
        z = proj_ref[rows, pl.ds(OFF_Z + h * HD_A, HD_A)]
        mix_ref[rows, pl.ds(h * HD_A, HD_A)] = o * _silu(z)

    def rwkv_head_seq(c, h, ys):
        rc = rwk[c]
        hs = slice(h * N_C, (h + 1) * N_C)
        d = rc['heads'][h]
        S = rwkv_out[0, h]
        Sb = S.astype(bf16)
        rwkv_out[0, h] = S * rc['e_last'][-1:, hs] + d['b'] - _bdot(Sb, d['m'])
        ys[h] = d['y0'] + _bdot(d['re'], Sb, NT)
        yield

    def gdn_head(c, h):
        rows = pl.ds(c * C, C)
        d = gdn[c][h]
        rq, vn, s_old = [], [], []
        for s in range(ngrp):
            gs = slice(s * G, (s + 1) * G)
            S = gdn_in[s, h] if per_seq else gdn_out[0, h]
            s_old.append(S)
            R = _bdot(jnp.concatenate([d['w'][gs], d['q_dec'][gs]], axis=0), S)
            vn.append(d['u'][gs] - R[:G])
            rq.append(R[G:])
        yield
        v_new = _cat(vn, 0)
        o = _cat(rq, 0) + _bdot(d['attn'], v_new)
        for s in range(ngrp):
            gs = slice(s * G, (s + 1) * G)
            g_last = jnp.exp(d['glr'][s * G + G - 1:s * G + G, :])
            S_new = s_old[s] * g_last + _bdot(d['k_dec'][gs], v_new[gs], TN)
            if per_seq:
                gdn_out[s, h] = S_new
            else:
                gdn_out[0, h] = S_new
        yield
        o = o * lax.rsqrt(jnp.mean(o * o, axis=-1, keepdims=True) + NORM_EPS) * gnorm_w
        z = proj_ref[rows, pl.ds(OFF_Z + h * HD_A, HD_A)]
        mix_ref[rows, pl.ds(h * HD_A, HD_A)] = o * _silu(z)

    def rwkv_head(c, h, ys):
        rc = rwk[c]
        hs = slice(h * N_C, (h + 1) * N_C)
        d = rc['heads'][h]
        V = rc['vc'][:, hs]
        rr, ut, s_old = [], [], []
        for s in range(ngrp):
            gs = slice(s * G, (s + 1) * G)
            S = rwkv_in[s, h] if per_seq else rwkv_out[0, h]
            s_old.append(S)
            R = _bdot(jnp.concatenate([d['w_p'][gs], rc['rG'][gs, hs]], axis=0), S, NT)
            ut.append(d['u_p'][gs] + R[:G])
            rr.append(R[G:])
        yield
        Ut = _cat(ut, 0)
        ys[h] = _cat(rr, 0) + _bdot(d['ar'], jnp.concatenate([V, Ut], axis=0))
        for s in range(ngrp):
            gs = slice(s * G, (s + 1) * G)
            upd = _bdot(jnp.concatenate([V[gs], -Ut[gs]], axis=0),
                        jnp.concatenate([rc['kdec'][gs, hs], rc['bdec'][gs, hs]], axis=0), TN)
            S_new = s_old[s] * rc['e_last'][s * G + G - 1:s * G + G, hs] + upd
            if per_seq:
                rwkv_out[s, h] = S_new
            else:
                rwkv_out[0, h] = S_new
        yield

    def phase4(c):
        rows = pl.ds(c * C, C)
        rc = rwk[c]
        ys = [None] * H_C
        if per_seq:
            heads = [gdn_head(c, h) for h in range(H_A)] + [rwkv_head(c, h, ys) for h in range(H_C)]
        else:
            heads = [gdn_head_seq(c, h) for h in range(H_A)] + [rwkv_head_seq(c, h, ys) for h in range(H_C)]
        yield from par(*heads)
        y = jnp.concatenate(ys, axis=1)
        mu_y = _dot01_right(y, seg64) * (1.0 / N_C)
        dy = y - mu_y
        var_y = _dot01_right(dy * dy, seg64) * (1.0 / N_C)
        y = dy * lax.rsqrt(var_y + GN_EPS) * rln_w + rln_b
        mix_ref[rows, pl.ds(D_A + D_B, D_C)] = (y + rc['bonus']) * rc['gate']
        yield

    def sgu(sc):
        rows = pl.ds(sc * SGU_CHUNK, SGU_CHUNK)
        ug = _gelu(proj_ref[rows, pl.ds(OFF_U, D_B)])
        vs = _gelu(proj_ref[rows, pl.ds(OFF_VB, D_B)])
        yield
        mu_v = _dot01_right(vs, seg64) * (1.0 / DH_B)
        dv = vs - mu_v
        var_v = _dot01_right(dv * dv, seg64) * (1.0 / DH_B)
        vs = dv * lax.rsqrt(var_v + LN_EPS) * ln_w + ln_b
        if per_seq:
            cv_out[rows, :] = vs
        yield
        vsb = vs.astype(bf16)
        outs = [lax.dot_general(sguw_ref[h], vsb[:, h * DH_B:(h + 1) * DH_B], NN, preferred_element_type=f32)
                for h in range(4)]
        mixed = jnp.concatenate(outs, axis=1) + sgub_ref[...]
        mix_ref[rows, pl.ds(D_A, D_B)] = ug * mixed
        yield

    first = list(range(nchunk))[:max(nchunk // 2, 1)]
    second = list(range(nchunk))[len(first):]
    sgus = [sgu(sc) for sc in range(TB // SGU_CHUNK)]
    lockstep([phase1(c) for c in first])
    lockstep([phase1(c) for c in second]
             + [seq(inverses(first), par(*[phase3(c) for c in first]))])
    if second:
        lockstep([seq(inverses(second), par(*[phase3(c) for c in second])),
                  seq(*[phase4(c) for c in first]), seq(*sgus)])
        lockstep([seq(*[phase4(c) for c in second])])
    else:
        lockstep([seq(*[phase4(c) for c in first]), seq(*sgus)])

    if per_seq:
        for s in range(TB // SEQ_PAD):
            conv_out[s] = xp[pl.ds(8 + (s + 1) * SEQ_PAD - (CONV_W - 1), CONV_W - 1), :]
    else:
        conv_out[0] = xp[pl.ds(8 + TB - (CONV_W - 1), CONV_W - 1), :]
        xp[pl.ds(0, 8), :] = xp[pl.ds(TB, 8), :]
        pp[pl.ds(0, 8), :] = pp[pl.ds(TB, 8), :]


def _mixer_weight_specs(nidx):
    z2 = (lambda b, j: (0, 0)) if nidx == 2 else (lambda i: (0, 0))
    z3 = (lambda b, j: (0, 0, 0)) if nidx == 2 else (lambda i: (0, 0, 0))
    return [pl.BlockSpec((CONV_W, 3 * D_A), z2),
            pl.BlockSpec((8, 128), z2),
            pl.BlockSpec((16, D_C), z2),
            pl.BlockSpec((4, SGU_CHUNK, SGU_CHUNK), z3),
            pl.BlockSpec((SGU_CHUNK, D_B), z2),
            pl.BlockSpec((1, NC_PAD), z2),
            pl.BlockSpec((128, D_C), z2),
            pl.BlockSpec((128, D_C), z2),
            pl.BlockSpec((128, D_C), z2)]


def _mixer_prompt(proj, mw):
    nj = SEQ // TB_P
    return pl.pallas_call(
        functools.partial(_mixer_kernel, TB=TB_P, C=GDN_CHUNK, G=GDN_CHUNK, per_seq=False),
        grid=(BATCH, nj),
        in_specs=[pl.BlockSpec((TB_P, NP_IN), lambda b, j: (b * nj + j, 0))] + _mixer_weight_specs(2),
        out_specs=[pl.BlockSpec((TB_P, D_MODEL), lambda b, j: (b * nj + j, 0)),
                   pl.BlockSpec((1, H_A, HD_A, HD_A), lambda b, j: (b, 0, 0, 0)),
                   pl.BlockSpec((1, CONV_W - 1, 3 * D_A), lambda b, j: (b, 0, 0)),
                   pl.BlockSpec((1, H_C, N_C, N_C), lambda b, j: (b, 0, 0, 0))],
        out_shape=[jax.ShapeDtypeStruct((T_PROMPT, D_MODEL), f32),
                   jax.ShapeDtypeStruct((BATCH, H_A, HD_A, HD_A), f32),
                   jax.ShapeDtypeStruct((BATCH, CONV_W - 1, 3 * D_A), f32),
                   jax.ShapeDtypeStruct((BATCH, H_C, N_C, N_C), f32)],
        scratch_shapes=[pltpu.VMEM((TB_P + 8, 3 * D_A), f32), pltpu.VMEM((TB_P + 8, NC_PAD), f32)],
        compiler_params=pltpu.CompilerParams(dimension_semantics=("arbitrary", "arbitrary"),
                                             vmem_limit_bytes=VMEM_LIMIT),
        name="mixer_prompt",
    )(proj, *mw)


def _mixer_sample(proj, s_gdn, s_conv, s_rwkv, pcf, mw, l, prev_states):
    nseq = TB_S // SEQ_PAD
    base = T_PROMPT // TB_S
    n_in = 5 + len(mw)
    lay4 = lambda i: (l, i, 0, 0, 0)
    lay3 = lambda i: (l, i, 0, 0)
    return pl.pallas_call(
        functools.partial(_mixer_kernel, TB=TB_S, C=TB_S, G=SEQ_PAD, per_seq=True, n_alias=len(prev_states)),
        grid=(T_SAMPLE // TB_S,),
        in_specs=[pl.BlockSpec((TB_S, NP_IN), lambda i: (base + i, 0)),
                  pl.BlockSpec((None, nseq, H_A, HD_A, HD_A), lay4),
                  pl.BlockSpec((None, nseq, CONV_W - 1, 3 * D_A), lay3),
                  pl.BlockSpec((None, nseq, H_C, N_C, N_C), lay4),
                  pl.BlockSpec((nseq, 1, NC_PAD), lambda i: (i, 0, 0))] + _mixer_weight_specs(1)
                 + [pl.BlockSpec(memory_space=pl.ANY)] * len(prev_states),
        out_specs=[pl.BlockSpec((TB_S, D_MODEL), lambda i: (i, 0)),
                   pl.BlockSpec((None, nseq, H_A, HD_A, HD_A), lay4),
                   pl.BlockSpec((nseq, CONV_W - 1, 3 * D_A), lambda i: (i, 0, 0)),
                   pl.BlockSpec((None, nseq, H_C, N_C, N_C), lay4),
                   pl.BlockSpec((TB_S, D_B), lambda i: (i, 0))],
        out_shape=[jax.ShapeDtypeStruct((T_SAMPLE, D_MODEL), f32),
                   jax.ShapeDtypeStruct((DEPTH, DEC_BATCH, H_A, HD_A, HD_A), f32),
                   jax.ShapeDtypeStruct((DEC_BATCH, CONV_W - 1, 3 * D_A), f32),
                   jax.ShapeDtypeStruct((DEPTH, DEC_BATCH, H_C, N_C, N_C), f32),
                   jax.ShapeDtypeStruct((T_SAMPLE, D_B), f32)],
        scratch_shapes=[pltpu.VMEM((TB_S + 8, 3 * D_A), f32), pltpu.VMEM((TB_S + 8, NC_PAD), f32)],
        input_output_aliases={n_in + k: o for k, o in enumerate((1, 3)[:len(prev_states)])},
        compiler_params=pltpu.CompilerParams(dimension_semantics=("arbitrary",),
                                             vmem_limit_bytes=VMEM_LIMIT),
        name="mixer_sample",
    )(proj, s_gdn, s_conv, s_rwkv, pcf, *mw, *prev_states)


TM_C = 512


def _outproj_kernel(*refs, split):
    i = pl.program_id(0)
    in_prompt = i < T_PROMPT // TM_C
    if split:
        xp_ref, xs_ref = refs[:2]
        x = jnp.where(in_prompt, xp_ref[...], xs_ref[...])
    else:
        x = refs[0][...]
    mixp_ref, mixs_ref, wout_ref, nw_ref, rwc_ref, tri_ref, rb_ref, xg_ref, cnt_ref, run_ref = refs[-10:]

    @pl.when(i == 0)
    def _():
        run_ref[...] = jnp.zeros(run_ref.shape, f32)

    mix = jnp.where(in_prompt, mixp_ref[...], mixs_ref[...])
    x2 = x + jnp.dot(mix.astype(bf16), wout_ref[0], preferred_element_type=f32)
    xg_ref[:, pl.ds(0, D_MODEL)] = x2
    h2 = _rms(x2, nw_ref[...])
    hh, hl = _split2(h2)
    d = lambda a, b: jnp.dot(a, b, preferred_element_type=f32)
    both = d(hh, rwc_ref[...])
    logits = both[:, :128] + both[:, 128:] + d(hl, rwc_ref[:, pl.ds(0, 128)]) + rb_ref[...]

    lane = lax.broadcasted_iota(i32, logits.shape, 1).astype(f32)
    neg = jnp.float32(-jnp.inf)
    is_g = lane < float(N_GROUPS)
    gl = jnp.where(is_g, logits, neg)
    gmax = jnp.max(gl, axis=-1, keepdims=True)
    gsel = jnp.min(jnp.where(gl == gmax, lane, 128.0), axis=-1, keepdims=True)
    gw = 1.0 / jnp.sum(jnp.where(is_g, jnp.exp(jnp.where(is_g, logits - gmax, 0.0)), 0.0),
                       axis=-1, keepdims=True)
    lo = LANE_E0 + float(EPG) * gsel
    in_grp = (lane >= lo) & (lane < lo + float(EPG))
    el = jnp.where(in_grp, logits, neg)
    t1 = jnp.max(el, axis=-1, keepdims=True)
    i1 = jnp.min(jnp.where(el == t1, lane, 128.0), axis=-1, keepdims=True)
    el2 = jnp.where(lane == i1, neg, el)
    t2 = jnp.max(el2, axis=-1, keepdims=True)
    i2 = jnp.min(jnp.where(el2 == t2, lane, 128.0), axis=-1, keepdims=True)
    e2 = jnp.exp(t2 - t1)
    den = 1.0 + e2
    gates = jnp.where(lane == i1, gw / den, 0.0) + jnp.where(lane == i2, gw * e2 / den, 0.0)

    ea, eb = i1 - lo, i2 - lo
    e_lo, e_hi = jnp.minimum(ea, eb), jnp.maximum(ea, eb)
    bucket = float(N_PAIRS) * gsel + e_lo * (7.0 - e_lo) * 0.5 + (e_hi - e_lo - 1.0)
    onehot = jnp.where(lane == bucket, 1.0, 0.0)
    before = lax.dot_general(tri_ref[...], onehot.astype(bf16), NN, preferred_element_type=f32)
    rank = jnp.sum((before + run_ref[...]) * onehot, axis=-1, keepdims=True)
    run_ref[...] += jnp.sum(onehot, axis=0, keepdims=True)
    cnt_ref[...] = jnp.broadcast_to(run_ref[...], cnt_ref.shape)
    xg_ref[:, pl.ds(D_MODEL, 128)] = (gates + jnp.where(lane == float(LANE_G), bucket, 0.0)
                                      + jnp.where(lane == float(LANE_RANK), rank, 0.0))


def _outproj(xs, mix_p, mix_s, wout, nw, rwh, rwl, rb, l):
    rwc = jnp.concatenate([rwh, rwl], axis=1)
    ids = jnp.arange(TM_C)
    tri = (ids[:, None] > ids[None, :]).astype(bf16)
    row = lambda i: (i, 0)
    fix = lambda i: (0, 0)
    npt = T_PROMPT // TM_C
    p_rows = pl.BlockSpec((TM_C, D_MODEL), lambda i: (jnp.minimum(i, npt - 1), 0))
    s_rows = pl.BlockSpec((TM_C, D_MODEL), lambda i: (jnp.maximum(i - npt, 0), 0))
    split = len(xs) == 2
    return pl.pallas_call(
        functools.partial(_outproj_kernel, split=split),
        grid=(T_ALL // TM_C,),
        in_specs=([p_rows, s_rows] if split else [pl.BlockSpec((TM_C, D_MODEL), row)]) + [
                  p_rows, s_rows,
                  pl.BlockSpec((1, D_MODEL, D_MODEL), lambda i: (l, 0, 0)), pl.BlockSpec((1, D_MODEL), fix),
                  pl.BlockSpec((D_MODEL, 256), fix), pl.BlockSpec((TM_C, TM_C), fix),
                  pl.BlockSpec((1, 128), fix)],
        out_specs=[pl.BlockSpec((TM_C, XG_W), row), pl.BlockSpec((8, 128), fix)],
        out_shape=[jax.ShapeDtypeStruct((T_ALL, XG_W), f32), jax.ShapeDtypeStruct((8, 128), f32)],
        scratch_shapes=[pltpu.VMEM((1, 128), f32)],
        compiler_params=pltpu.CompilerParams(dimension_semantics=("arbitrary",),
                                             vmem_limit_bytes=VMEM_LIMIT),
        name="outproj_router",
    )(*xs, mix_p, mix_s, wout, nw, rwc, tri, rb)


TM_E = 512
NT_E = T_ALL // TM_E + N_GROUPS
T_SORT = NT_E * TM_E
DMA_UNROLL = 8


def _invert_kernel(pos_ref, pad_lo_ref, pad_hi_ref, src_ref):
    for g in range(N_GROUPS):
        def fill(p, c):
            src_ref[p] = T_ALL - 1
            return c
        lax.fori_loop(pad_lo_ref[g], pad_hi_ref[g], fill, 0)

    def body(t, c):
        src_ref[pos_ref[t]] = t
        return c
    lax.fori_loop(0, T_ALL, body, 0, unroll=DMA_UNROLL)


def _invert(pos, pad_lo, pad_hi):
    smem = pl.BlockSpec(memory_space=pltpu.SMEM)
    return pl.pallas_call(
        _invert_kernel,
        in_specs=[smem, smem, smem],
        out_specs=smem,
        out_shape=jax.ShapeDtypeStruct((T_SORT,), i32),
        name="moe_invert",
    )(pos, pad_lo, pad_hi)


def _gather_rows(pos_ref, base, ys_hbm, o_ref, sem):
    def issue(r, c):
        p = pos_ref[base + r]
        pltpu.make_async_copy(ys_hbm.at[pl.ds(p, 1), :], o_ref.at[pl.ds(r, 1), :], sem).start()
        return c

    lax.fori_loop(0, TM_E, issue, 0, unroll=DMA_UNROLL)
    pltpu.make_async_copy(ys_hbm.at[pl.ds(0, TM_E), :], o_ref, sem).wait()


def _combine_kernel(pos_ref, ys_hbm, o_ref, sem):
    _gather_rows(pos_ref, pl.program_id(0) * TM_E, ys_hbm, o_ref, sem)


def _combine_split_kernel(pos_ref, ys_hbm, op_ref, os_ref, sem):
    i = pl.program_id(0)

    @pl.when(i < T_PROMPT // TM_E)
    def _():
        _gather_rows(pos_ref, i * TM_E, ys_hbm, op_ref, sem)

    @pl.when(i >= T_PROMPT // TM_E)
    def _():
        _gather_rows(pos_ref, i * TM_E, ys_hbm, os_ref, sem)


def _combine(pos, ys, split):
    npt = T_PROMPT // TM_E
    if split:
        out_specs = [pl.BlockSpec((TM_E, D_MODEL), lambda i, pos: (jnp.minimum(i, npt - 1), 0)),
                     pl.BlockSpec((TM_E, D_MODEL), lambda i, pos: (jnp.maximum(i - npt, 0), 0))]
        out_shape = [jax.ShapeDtypeStruct((T_PROMPT, D_MODEL), f32),
                     jax.ShapeDtypeStruct((T_SAMPLE, D_MODEL), f32)]
    else:
        out_specs = pl.BlockSpec((TM_E, D_MODEL), lambda i, pos: (i, 0))
        out_shape = jax.ShapeDtypeStruct((T_ALL, D_MODEL), f32)
    return pl.pallas_call(
        _combine_split_kernel if split else _combine_kernel,
        grid_spec=pltpu.PrefetchScalarGridSpec(
            num_scalar_prefetch=1,
            grid=(T_ALL // TM_E,),
            in_specs=[pl.BlockSpec(memory_space=pl.ANY)],
            out_specs=out_specs,
            scratch_shapes=[pltpu.SemaphoreType.DMA(())]),
        out_shape=out_shape,
        compiler_params=pltpu.CompilerParams(dimension_semantics=("arbitrary",),
                                             vmem_limit_bytes=VMEM_LIMIT),
        name="moe_combine",
    )(pos, ys)


ROWS_E = TM_E // 2
N_XBUF = 3


def _experts_kernel(tg_ref, tv_ref, ti_ref, tu_ref, tw_ref, src_ref, xg_hbm, nw_ref, wg_ref, wu_ref, wd_ref, nf_ref,
                    ys_ref, xbuf, hbuf, ybuf, gsem, *, final_norm):
    del tw_ref
    t = pl.program_id(0)
    e = pl.program_id(1)
    valid = tv_ref[t] == 1
    code = tu_ref[t * EPG + e]
    slot = t % N_XBUF
    ahead = ti_ref[jnp.minimum(t + 2, NT_E - 1)]
    aslot = (t + 2) % N_XBUF
    is_last = ti_ref[jnp.minimum(t + 1, NT_E - 1)] == t

    @pl.when((t == 0) & (e == 0))
    def _():
        _start_row_copies(src_ref, 0, xg_hbm, xbuf.at[0], gsem.at[0], 0, TM_E, inline=False)
        _start_row_copies(src_ref, ti_ref[1] * TM_E, xg_hbm, xbuf.at[1], gsem.at[1], 0, TM_E, inline=False)

    @pl.when(valid & (e == 0))
    def _():
        _wait_row_copies(xg_hbm, xbuf.at[slot], gsem.at[slot])
        x2 = xbuf[slot, :, pl.ds(0, D_MODEL)]
        hbuf[...] = _rms(x2, nw_ref[...]).astype(bf16)
        ybuf[...] = x2

    def expert_step():
        hb = hbuf[...]
        he = (_silu(jnp.dot(hb, wg_ref[0, 0], preferred_element_type=f32))
              * jnp.dot(hb, wu_ref[0, 0], preferred_element_type=f32))
        yd = jnp.dot(he.astype(bf16), wd_ref[0, 0], preferred_element_type=f32)
        g = xbuf[slot, :, pl.ds(D_MODEL, 128)]
        lane = lax.broadcasted_iota(i32, g.shape, 1)
        gcol = jnp.sum(jnp.where(lane == LANE_E0 + EPG * tg_ref[t] + e, g, 0.0), axis=-1, keepdims=True)
        ybuf[...] += gcol * yd

    @pl.when(valid & ((code == 1) | (code == 2)))
    def _():
        _start_row_copies(src_ref, ahead * TM_E, xg_hbm, xbuf.at[aslot], gsem.at[aslot],
                          (code - 1) * ROWS_E, ROWS_E, inline=True)
        expert_step()

    @pl.when(valid & (code == 3))
    def _():
        expert_step()

    @pl.when(e == EPG - 1)
    def _():
        y = ybuf[...]
        if final_norm:
            y = _rms(y, nf_ref[...])
        ys_ref[...] = y

    @pl.when(valid & is_last & (e == EPG - 1))
    def _():
        for k in (1, 2):
            s = (t + k) % N_XBUF
            _wait_row_copies(xg_hbm, xbuf.at[s], gsem.at[s])


def _experts(tile_group, tile_valid, tile_idx, tile_used, tile_wexp, src, xg, nw, wg, wu, wd, nf, l, final_norm):
    wsel = lambda t, e, tg, tv, ti, tu, tw, src: (l, tg[t] * EPG + tw[t * EPG + e], 0, 0)
    return pl.pallas_call(
        functools.partial(_experts_kernel, final_norm=final_norm),
        grid_spec=pltpu.PrefetchScalarGridSpec(
            num_scalar_prefetch=6,
            grid=(NT_E, EPG),
            in_specs=[pl.BlockSpec(memory_space=pl.ANY),
                      pl.BlockSpec((1, D_MODEL), lambda t, e, tg, tv, ti, tu, tw, src: (0, 0)),
                      pl.BlockSpec((1, 1, D_MODEL, D_FF_E), wsel),
                      pl.BlockSpec((1, 1, D_MODEL, D_FF_E), wsel),
                      pl.BlockSpec((1, 1, D_FF_E, D_MODEL), wsel),
                      pl.BlockSpec((1, D_MODEL), lambda t, e, tg, tv, ti, tu, tw, src: (0, 0))],
            out_specs=pl.BlockSpec((TM_E, D_MODEL), lambda t, e, tg, tv, ti, tu, tw, src: (ti[t], 0)),
            scratch_shapes=[pltpu.VMEM((N_XBUF, TM_E, XG_W), f32), pltpu.VMEM((TM_E, D_MODEL), bf16),
                            pltpu.VMEM((TM_E, D_MODEL), f32), pltpu.SemaphoreType.DMA((N_XBUF,))]),
        out_shape=jax.ShapeDtypeStruct((T_SORT, D_MODEL), f32),
        compiler_params=pltpu.CompilerParams(dimension_semantics=("arbitrary", "arbitrary"),
                                             vmem_limit_bytes=VMEM_LIMIT),
        name="moe_experts",
    )(tile_group, tile_valid, tile_idx, tile_used, tile_wexp, src, xg, nw, wg, wu, wd, nf)


def _route_meta(xg, cnt_rows):
    b = xg[:, D_MODEL + LANE_G].astype(i32)
    rank = xg[:, D_MODEL + LANE_RANK].astype(i32)
    bcnt = cnt_rows[0, :N_BUCKETS].astype(i32).reshape(N_GROUPS, N_PAIRS)
    cnt = jnp.sum(bcnt, axis=1)
    padded = ((cnt + TM_E - 1) // TM_E) * TM_E
    off_end = jnp.cumsum(padded)
    off = off_end - padded
    bstart = (off[:, None] + jnp.cumsum(bcnt, axis=1) - bcnt).reshape(N_BUCKETS)
    bend = bstart + bcnt.reshape(N_BUCKETS)
    bid = jnp.arange(N_BUCKETS, dtype=i32)
    pos = rank + jnp.sum(jnp.where(b[:, None] == bid[None, :], bstart[None, :], 0), axis=1)
    tile_start = jnp.arange(NT_E, dtype=i32) * TM_E
    tile_group = jnp.minimum(jnp.sum((tile_start[:, None] >= off_end[None, :]).astype(i32), axis=1), N_GROUPS - 1)
    tile_valid = (tile_start < off_end[-1]).astype(i32)
    n_used = off_end[-1] // TM_E
    tile_idx = jnp.minimum(jnp.arange(NT_E, dtype=i32), n_used - 1)
    tile_group = jnp.take(tile_group, tile_idx)
    ts = (tile_idx * TM_E)[:, None]
    overlap = (bstart[None, :] < ts + TM_E) & (bend[None, :] > ts) & (bend > bstart)[None, :]
    pair_has = jnp.array([[int(e in p) for e in range(EPG)] for p in PAIRS] * N_GROUPS, dtype=i32)
    used = jnp.sum(overlap.astype(i32)[:, :, None] * pair_has[None, :, :], axis=1) > 0
    eidx = jnp.arange(EPG, dtype=i32)[None, :]
    prev = lax.cummax(jnp.where(used, eidx, -1), axis=1)
    nxt = lax.cummin(jnp.where(used, eidx, EPG), axis=1, reverse=True)
    wexp = jnp.where(prev >= 0, prev, jnp.minimum(nxt, EPG - 1))
    nth = jnp.cumsum(used.astype(i32), axis=1)
    code = jnp.where(used, jnp.minimum(nth, 3), 0)
    return (pos, tile_group, tile_valid, tile_idx, off + cnt, off_end,
            code.reshape(-1), wexp.reshape(-1))


def _pad_cols(a, n):
    return jnp.pad(a, ((0, 0), (0, n - a.shape[1])))


def _pad_rows(a, n):
    return jnp.pad(a, ((0, n - a.shape[0]), (0, 0)))


def _prep_w_in(w):
    pad_last = lambda a, n: jnp.pad(a, ((0, 0), (0, 0), (0, n - a.shape[-1])))
    c = w[..., 2568:]
    parts = [w[..., 0:2048], pad_last(w[..., 2048:2056], 128), w[..., 2056:2568],
             c[..., 0:768], pad_last(c[..., 768:832], 128), pad_last(c[..., 832:896], 128), c[..., 896:1024]]
    return jnp.concatenate(parts, axis=-1).astype(bf16)


def _prep_mu(mu):
    m = mu[None, :]
    return jnp.concatenate([m[:, 0:768], _pad_cols(m[:, 768:832], 128), _pad_cols(m[:, 832:896], 128),
                            m[:, 896:1024]], axis=1)


def _sgu_mats(sgu_w, sgu_b):
    t = jnp.arange(SGU_CHUNK)
    wm = jnp.where(t[:, None] >= t[None, :], sgu_w, 0.0)
    bias_p = jnp.repeat(jnp.transpose(sgu_b), DH_B, axis=1)
    small = jnp.zeros((4, SEQ_PAD, SEQ_PAD), f32).at[:, SEQ_LEAD:, SEQ_LEAD:].set(wm[:, :DEC_SEQ, :DEC_SEQ])
    eye16 = jnp.eye(TB_S // SEQ_PAD, dtype=f32)
    wm_s = jnp.einsum('ab,hij->haibj', eye16, small).reshape(4, TB_S, TB_S)
    bias_small = jnp.zeros((SEQ_PAD, D_B), f32).at[SEQ_LEAD:].set(bias_p[:DEC_SEQ])
    bias_s = jnp.tile(bias_small, (TB_S // SEQ_PAD, 1))
    return wm.astype(bf16), bias_p, wm_s.astype(bf16), bias_s


def _row(a, n):
    return _pad_cols(a.reshape(1, -1), n)


def kernel(x_prompt, x_sample, state_gdn, state_gdn_conv, state_rwkv, state_rwkv_shift, norm_mix, norm_ffn, norm_final, w_in, gdn_conv_w, gdn_a_log, gdn_dt_bias, gdn_norm_w, sgu_ln_w, sgu_ln_b, sgu_w, sgu_b, rwkv_mu, rwkv_w0, rwkv_w_up, rwkv_a0, rwkv_a_up, rwkv_g_up, rwkv_k_k, rwkv_k_a, rwkv_r_k, rwkv_ln_w, rwkv_ln_b, w_out, router_group_w, router_group_b, router_expert_w, router_expert_b, expert_w_gate, expert_w_up, expert_w_down):
    x = (x_prompt.reshape(T_PROMPT, D_MODEL),
         jnp.pad(x_sample, ((0, 0), (SEQ_LEAD, 0), (0, 0))).reshape(T_SAMPLE, D_MODEL))
    sample_states = ()

    w_pad = _prep_w_in(w_in)
    w_out_b = w_out.astype(bf16)
    wg_b, wu_b, wd_b = expert_w_gate.astype(bf16), expert_w_up.astype(bf16), expert_w_down.astype(bf16)

    outs = {k: [] for k in ('gdn_p', 'conv_p', 'rwkv_p', 'shift_p', 'gdn_s', 'conv_s', 'rwkv_s', 'shift_s', 'cv_s')}
    for l in range(DEPTH):
        last = l == DEPTH - 1
        v128 = jnp.concatenate([
            jnp.pad(gdn_a_log[l].reshape(1, H_A), ((0, 0), (4, 120))),
            jnp.pad(gdn_dt_bias[l].reshape(1, H_A), ((0, 0), (4, 120))),
            gdn_norm_w[l].reshape(1, HD_A), jnp.zeros((5, 128), f32)], axis=0)
        v256 = jnp.concatenate([a.reshape(1, D_C) for a in (
            sgu_ln_w[l], sgu_ln_b[l], rwkv_w0[l], rwkv_a0[l], rwkv_k_k[l], rwkv_k_a[l], rwkv_r_k[l],
            rwkv_ln_w[l], rwkv_ln_b[l])] + [jnp.zeros((7, D_C), f32)], axis=0)
        wm_p, bias_p, wm_s, bias_s = _sgu_mats(sgu_w[l], sgu_b[l])
        common = (gdn_conv_w[l], v128, v256)
        tail = (_prep_mu(rwkv_mu[l]), _pad_rows(rwkv_w_up[l], 128).astype(bf16),
                _pad_rows(rwkv_a_up[l], 128).astype(bf16), rwkv_g_up[l].astype(bf16))
        mw_p = common + (wm_p, bias_p) + tail
        mw_s = common + (wm_s, bias_s) + tail

        if l == 0:
            proj, h = _inproj(x, norm_mix[l].reshape(1, D_MODEL), w_pad, l)
        else:
            proj, h, x_tok = _inproj_gather(pos, ys, norm_mix[l].reshape(1, D_MODEL), w_pad, l)
            x = (x_tok,)
        pcf = _mm(state_rwkv_shift[l], w_pad[l, :, OFF_C:]).reshape(DEC_BATCH, 1, NC_PAD)
        mix_p, gdn_p, conv_p, rwkv_p = _mixer_prompt(proj, mw_p)
        mix_s, gdn_s, conv_s, rwkv_s, cv_s = _mixer_sample(
            proj, state_gdn, state_gdn_conv, state_rwkv, pcf, mw_s, l, sample_states)
        sample_states = (gdn_s, rwkv_s)

        rw = _pad_cols(jnp.concatenate([router_group_w[l], router_expert_w[l]], axis=1), 128)
        rwh = rw.astype(bf16)
        rwl = (rw - rwh.astype(f32)).astype(bf16)
        rb = _row(jnp.concatenate([router_group_b[l], router_expert_b[l]]), 128)
        nw_ffn = norm_ffn[l].reshape(1, D_MODEL)
        xg, cnt_rows = _outproj(x, mix_p, mix_s, w_out_b, nw_ffn, rwh, rwl, rb, l)
        pos, tile_group, tile_valid, tile_idx, pad_lo, pad_hi, tile_used, tile_wexp = _route_meta(xg, cnt_rows)
        src = _invert(pos, pad_lo, pad_hi)
        ys = _experts(tile_group, tile_valid, tile_idx, tile_used, tile_wexp, src, xg, nw_ffn, wg_b, wu_b, wd_b,
                      norm_final.reshape(1, D_MODEL), l, final_norm=last)
        if last:
            x = _combine(pos, ys, split=True)

        outs['gdn_p'].append(gdn_p)
        outs['conv_p'].append(conv_p)
        outs['rwkv_p'].append(rwkv_p)
        outs['shift_p'].append(h[SEQ // SEQ_PAD - 1:T_PROMPT // SEQ_PAD:SEQ // SEQ_PAD])
        outs['conv_s'].append(conv_s)
        outs['shift_s'].append(h[T_PROMPT // SEQ_PAD:])
        outs['cv_s'].append(cv_s.reshape(DEC_BATCH, SEQ_PAD, D_B)[:, SEQ_LEAD:])

    y_prompt = x[0].reshape(BATCH, SEQ, D_MODEL)
    y_sample = x[1].reshape(DEC_BATCH, SEQ_PAD, D_MODEL)[:, SEQ_LEAD:]
    st = lambda k: jnp.stack(outs[k])
    return (y_prompt, y_sample, st('gdn_p'), st('conv_p'), st('rwkv_p'), st('shift_p'),
            sample_states[0], st('conv_s'), sample_states[1], st('shift_s'), st('cv_s'))
```

```python
import functools
import math

import jax
import jax.numpy as jnp
from jax import lax
from jax.experimental import pallas as pl
from jax.experimental.pallas import tpu as pltpu

f32 = jnp.float32
bf16 = jnp.bfloat16
i32 = jnp.int32

D_MODEL = 1024
BATCH = 8
SEQ = 2048
DEPTH = 2
DEC_BATCH = 128
DEC_SEQ = 4
H_A = 4
HD_A = 128
D_A = 512
CONV_W = 4
D_B = 256
DH_B = 64
SGU_CHUNK = 128
H_C = 4
N_C = 64
D_C = 256
N_GROUPS = 4
EPG = 4
D_FF_E = 512
NORM_EPS = 1e-6
LN_EPS = 1e-5
GN_EPS = 64e-5

SEQ_PAD = 8
SEQ_LEAD = SEQ_PAD - DEC_SEQ
T_PROMPT = BATCH * SEQ
T_SAMPLE = DEC_BATCH * SEQ_PAD
T_ALL = T_PROMPT + T_SAMPLE
TB_P = 512
TB_S = 128
GDN_CHUNK = 64

OFF_Q, OFF_K, OFF_V, OFF_Z, OFF_BA, OFF_U, OFF_VB, OFF_C = 0, 512, 1024, 1536, 2048, 2176, 2432, 2688
NP_IN = 3840
NC_PAD = NP_IN - OFF_C

N_PAIRS = 6
N_BUCKETS = N_GROUPS * N_PAIRS
PAIRS = ((0, 1), (0, 2), (0, 3), (1, 2), (1, 3), (2, 3))
XG_W = D_MODEL + 128
LANE_G, LANE_RANK, LANE_E0 = 0, 1, 4

VMEM_LIMIT = 48 * 1024 * 1024


NN = (((1,), (0,)), ((), ()))
NT = (((1,), (1,)), ((), ()))
TN = (((0,), (0,)), ((), ()))
BNN = (((2,), (1,)), ((0,), (0,)))


def _bdot(a, b, dims=NN):
    return lax.dot_general(a.astype(bf16), b.astype(bf16), dims, preferred_element_type=f32)


def _split2(x):
    hi = x.astype(bf16)
    lo = (x - hi.astype(f32)).astype(bf16)
    return hi, lo


def _split3(x):
    hi = x.astype(bf16)
    r = x - hi.astype(f32)
    mid = r.astype(bf16)
    lo = (r - mid.astype(f32)).astype(bf16)
    return hi, mid, lo


def _dot01_left(m01, x):
    hi, mid, lo = _split3(x)
    d = lambda p: lax.dot_general(m01, p, NN, preferred_element_type=f32)
    return d(hi) + d(mid) + d(lo)


def _dot01_right(x, m01):
    hi, lo = _split2(x)
    d = lambda p: lax.dot_general(p, m01, NN, preferred_element_type=f32)
    return d(hi) + d(lo)


def _softplus(x):
    return jnp.maximum(x, 0.0) + jnp.log(1.0 + jnp.exp(-jnp.abs(x)))


def _sigmoid(x):
    return 0.5 * jnp.tanh(0.5 * x) + 0.5


def _silu(x):
    return x * _sigmoid(x)


def _gelu(x):
    return 0.5 * x * (1.0 + lax.erf(x * (1.0 / math.sqrt(2.0))))


def _rms(x, w):
    return x * lax.rsqrt(jnp.mean(x * x, axis=-1, keepdims=True) + NORM_EPS) * w


TM_A = 512
N_SLAB = 768


def _start_row_copies(idx_ref, base, src_hbm, dst, sem, r0, n, inline):
    def one(r):
        p = idx_ref[base + r]
        pltpu.make_async_copy(src_hbm.at[pl.ds(p, 1), :], dst.at[pl.ds(r, 1), :], sem).start()

    if inline:
        for k in range(n):
            one(r0 + k)
    else:
        def body(k, c):
            one(r0 + k)
            return c
        lax.fori_loop(0, n, body, 0, unroll=DMA_UNROLL)


def _wait_row_copies(src_hbm, dst, sem):
    pltpu.make_async_copy(src_hbm.at[pl.ds(0, dst.shape[0]), :], dst, sem).wait()


def _inproj_kernel(*refs, split):
    if split:
        xp_ref, xs_ref, nw_ref, w_ref, proj_ref, h8_ref, hscr = refs
        x = jnp.where(pl.program_id(0) < T_PROMPT // TM_A, xp_ref[...], xs_ref[...])
    else:
        x_ref, nw_ref, w_ref, proj_ref, h8_ref, hscr = refs
        x = x_ref[...]
    _inproj_body(x, nw_ref, w_ref, proj_ref, h8_ref, hscr)


def _inproj_gather_kernel(pos_ref, ys_hbm, nw_ref, w_ref, proj_ref, h8_ref, x_ref, hscr, xbuf, sem):
    i = pl.program_id(0)
    n = pl.num_programs(0)
    slot = i % 2

    @pl.when(i == 0)
    def _():
        _start_row_copies(pos_ref, 0, ys_hbm, xbuf.at[0], sem.at[0], 0, TM_A, inline=False)

    _wait_row_copies(ys_hbm, xbuf.at[slot], sem.at[slot])
    x = xbuf[slot]
    x_ref[...] = x
    nxt = jnp.minimum(i + 1, n - 1)
    _start_row_copies(pos_ref, nxt * TM_A, ys_hbm, xbuf.at[1 - slot], sem.at[1 - slot], 0, TM_A, inline=True)
    _inproj_body(x, nw_ref, w_ref, proj_ref, h8_ref, hscr)

    @pl.when(i == n - 1)
    def _():
        _wait_row_copies(ys_hbm, xbuf.at[1 - slot], sem.at[1 - slot])


def _inproj_body(x, nw_ref, w_ref, proj_ref, h8_ref, hscr):
    h = _rms(x, nw_ref[...])
    for k in range(D_MODEL // 128):
        hscr[k] = h[:, k * 128:(k + 1) * 128]
        h8_ref[:, pl.ds(k * 128, 128)] = hscr[k, pl.ds(SEQ_PAD - 1, TM_A // SEQ_PAD, stride=SEQ_PAD), :]
    hb = h.astype(bf16)
    for n in range(NP_IN // N_SLAB):
        sl = pl.ds(n * N_SLAB, N_SLAB)
        proj_ref[:, sl] = jnp.dot(hb, w_ref[0, :, sl], preferred_element_type=f32)


def _inproj_gather(pos, ys, nw, w_pad, l):
    return pl.pallas_call(
        _inproj_gather_kernel,
        grid_spec=pltpu.PrefetchScalarGridSpec(
            num_scalar_prefetch=1,
            grid=(T_ALL // TM_A,),
            in_specs=[pl.BlockSpec(memory_space=pl.ANY),
                      pl.BlockSpec((1, D_MODEL), lambda i, pos: (0, 0)),
                      pl.BlockSpec((1, D_MODEL, NP_IN), lambda i, pos: (l, 0, 0))],
            out_specs=[pl.BlockSpec((TM_A, NP_IN), lambda i, pos: (i, 0)),
                       pl.BlockSpec((TM_A // SEQ_PAD, D_MODEL), lambda i, pos: (i, 0)),
                       pl.BlockSpec((TM_A, D_MODEL), lambda i, pos: (i, 0))],
            scratch_shapes=[pltpu.VMEM((D_MODEL // 128, TM_A, 128), f32),
                            pltpu.VMEM((2, TM_A, D_MODEL), f32),
                            pltpu.SemaphoreType.DMA((2,))]),
        out_shape=[jax.ShapeDtypeStruct((T_ALL, NP_IN), f32),
                   jax.ShapeDtypeStruct((T_ALL // SEQ_PAD, D_MODEL), f32),
                   jax.ShapeDtypeStruct((T_ALL, D_MODEL), f32)],
        compiler_params=pltpu.CompilerParams(dimension_semantics=("arbitrary",),
                                             vmem_limit_bytes=VMEM_LIMIT),
        name="inproj_gather",
    )(pos, ys, nw, w_pad)


def _inproj(xs, nw, w_pad, l):
    split = len(xs) == 2
    npt = T_PROMPT // TM_A
    if split:
        x_specs = [pl.BlockSpec((TM_A, D_MODEL), lambda i: (jnp.minimum(i, npt - 1), 0)),
                   pl.BlockSpec((TM_A, D_MODEL), lambda i: (jnp.maximum(i - npt, 0), 0))]
    else:
        x_specs = [pl.BlockSpec((TM_A, D_MODEL), lambda i: (i, 0))]
    return pl.pallas_call(
        functools.partial(_inproj_kernel, split=split),
        grid=(T_ALL // TM_A,),
        in_specs=x_specs + [pl.BlockSpec((1, D_MODEL), lambda i: (0, 0)),
                            pl.BlockSpec((1, D_MODEL, NP_IN), lambda i: (l, 0, 0))],
        out_specs=[pl.BlockSpec((TM_A, NP_IN), lambda i: (i, 0)),
                   pl.BlockSpec((TM_A // SEQ_PAD, D_MODEL), lambda i: (i, 0))],
        out_shape=[jax.ShapeDtypeStruct((T_ALL, NP_IN), f32),
                   jax.ShapeDtypeStruct((T_ALL // SEQ_PAD, D_MODEL), f32)],
        scratch_shapes=[pltpu.VMEM((D_MODEL // 128, TM_A, 128), f32)],
        compiler_params=pltpu.CompilerParams(dimension_semantics=("arbitrary",),
                                             vmem_limit_bytes=VMEM_LIMIT),
        name="inproj",
    )(*xs, nw, w_pad)


def _mm_kernel(a_ref, b_ref, o_ref):
    o_ref[...] = jnp.dot(a_ref[...].astype(bf16), b_ref[...], preferred_element_type=f32)


def _mm(a, b):
    return pl.pallas_call(
        _mm_kernel,
        out_shape=jax.ShapeDtypeStruct((a.shape[0], b.shape[1]), f32),
        compiler_params=pltpu.CompilerParams(vmem_limit_bytes=VMEM_LIMIT),
        name="shift_proj",
    )(a, b)


def _cat(parts, axis):
    return parts[0] if len(parts) == 1 else jnp.concatenate(parts, axis=axis)


def _mixer_kernel(*refs, TB, C, G, per_seq, n_alias=0):
    it = iter(refs)
    proj_ref = next(it)
    if per_seq:
        gdn_in, conv_in, rwkv_in, pcf_in = next(it), next(it), next(it), next(it)
    convw_ref, v128_ref, v256_ref, sguw_ref, sgub_ref, mu_ref, wup_ref, aup_ref, gup_ref = (
        next(it) for _ in range(9))
    for _ in range(n_alias):
        next(it)
    mix_ref, gdn_out, conv_out, rwkv_out = next(it), next(it), next(it), next(it)
    cv_out = next(it) if per_seq else None
    xp, pp = next(it), next(it)

    nchunk = TB // C
    ngrp = C // G
    iters = int(math.log2(G)) - 1

    if per_seq:
        xp[pl.ds(0, 8), :] = jnp.zeros((8, 3 * D_A), f32)
        pp[pl.ds(0, 8), :] = jnp.zeros((8, NC_PAD), f32)
    else:
        @pl.when(pl.program_id(1) == 0)
        def _():
            xp[pl.ds(0, 8), :] = jnp.zeros((8, 3 * D_A), f32)
            pp[pl.ds(0, 8), :] = jnp.zeros((8, NC_PAD), f32)
            gdn_out[...] = jnp.zeros(gdn_out.shape, f32)
            rwkv_out[...] = jnp.zeros(rwkv_out.shape, f32)
    xp[pl.ds(8, TB), :] = proj_ref[:, pl.ds(OFF_Q, 3 * D_A)]
    pp[pl.ds(8, TB), :] = proj_ref[:, pl.ds(OFF_C, NC_PAD)]
    if per_seq:
        for s in range(TB // SEQ_PAD):
            r0 = 8 + s * SEQ_PAD
            xp[pl.ds(r0 + SEQ_LEAD - (CONV_W - 1), CONV_W - 1), :] = conv_in[s]
            pp[pl.ds(r0 + SEQ_LEAD - 1, 1), :] = pcf_in[s]

    rowi = lax.broadcasted_iota(i32, (TB, 1), 0)
    live = (rowi % SEQ_PAD) >= SEQ_LEAD if per_seq else None

    ii = lax.broadcasted_iota(i32, (C, C), 0)
    jj = lax.broadcasted_iota(i32, (C, C), 1)
    same = (ii // G) == (jj // G)
    causal = (ii >= jj) & same
    strict = (ii > jj) & same
    eye = ii == jj
    m_cum = causal.astype(bf16)
    m_grp = same.astype(bf16)

    def conv_cols(r0, c0):
        cs = pl.ds(c0, 128)
        acc = xp[pl.ds(r0 + 8, C), cs] * convw_ref[pl.ds(3, 1), cs]
        for j in range(CONV_W - 1):
            acc = acc + xp[pl.ds(r0 + 5 + j, C), cs] * convw_ref[pl.ds(j, 1), cs]
        return _silu(acc)

    alog_row = v128_ref[pl.ds(0, 1), :]
    dtb_row = v128_ref[pl.ds(1, 1), :]
    gnorm_w = v128_ref[pl.ds(2, 1), :]
    ln_w, ln_b = v256_ref[pl.ds(0, 1), :], v256_ref[pl.ds(1, 1), :]
    w0, a0 = v256_ref[pl.ds(2, 1), :], v256_ref[pl.ds(3, 1), :]
    k_k, k_a, r_k = v256_ref[pl.ds(4, 1), :], v256_ref[pl.ds(5, 1), :], v256_ref[pl.ds(6, 1), :]
    rln_w, rln_b = v256_ref[pl.ds(7, 1), :], v256_ref[pl.ds(8, 1), :]

    l64i = lax.broadcasted_iota(i32, (D_C, D_C), 0) // N_C
    l64j = lax.broadcasted_iota(i32, (D_C, D_C), 1) // N_C
    seg64 = (l64i == l64j).astype(bf16)
    gdn, rwk, lmats = [None] * nchunk, [None] * nchunk, [None] * (8 * nchunk)

    def phase1(c):
        R0 = c * C
        rows = pl.ds(R0, C)
        live_c = live[R0:R0 + C] if per_seq else None

        ba = proj_ref[rows, pl.ds(OFF_BA, 128)]
        beta_all = _sigmoid(ba)
        g_all = -jnp.exp(alog_row) * _softplus(ba + dtb_row)
        if per_seq:
            beta_all = jnp.where(live_c, beta_all, 0.0)
            g_all = jnp.where(live_c, g_all, 0.0)
        pcur = pp[pl.ds(R0 + 8, C), :]
        pprev = pp[pl.ds(R0 + 7, C), :]
        pm = pcur + mu_ref[...] * (pprev - pcur)
        r_ = pm[:, 0:D_C]
        kc = pm[:, D_C:2 * D_C]
        vc = pm[:, 2 * D_C:3 * D_C]
        wd = pm[:, 3 * D_C:3 * D_C + 128]
        ad = pm[:, 3 * D_C + 128:3 * D_C + 256]
        gd = pm[:, 3 * D_C + 256:3 * D_C + 384]
        wl_mm = _bdot(jnp.tanh(wd), wup_ref[...])
        a_mm = _bdot(ad, aup_ref[...])
        gate = _bdot(_sigmoid(gd), gup_ref[...])
        kk = kc * k_k
        kk_ss = _dot01_right(kk * kk, seg64)
        yield

        gc_all = _dot01_left(m_cum, g_all)
        total = lambda x: jnp.sum(x, axis=0, keepdims=True) if ngrp == 1 else _dot01_left(m_grp, x)
        gl_all = total(g_all)
        w_log = -_softplus(-(w0 + wl_mm)) - 0.5
        logw = -jnp.exp(w_log)
        a_ = _sigmoid(a0 + a_mm)
        kk = kk * lax.rsqrt(kk_ss + 1e-6)
        kc2 = kc * (1.0 + (a_ - 1.0) * k_a)
        if per_seq:
            logw = jnp.where(live_c, logw, 0.0)
            kk = jnp.where(live_c, kk, 0.0)
            kc2 = jnp.where(live_c, kc2, 0.0)
        b_ = kk * a_
        lw_hi, lw_lo = _split2(logw)
        Gc = (lax.dot_general(m_cum, lw_hi, NN, preferred_element_type=f32)
              + lax.dot_general(m_cum, lw_lo, NN, preferred_element_type=f32))
        Gl = total(logw)
        bonus_ss = _dot01_right(r_ * kc2 * r_k, seg64)
        yield

        heads = []
        for h in range(H_A):
            q = conv_cols(R0, OFF_Q + h * HD_A)
            k = conv_cols(R0, OFF_K + h * HD_A)
            v = conv_cols(R0, OFF_V + h * HD_A)
            q = q * lax.rsqrt(jnp.sum(q * q, axis=-1, keepdims=True) + 1e-6) * (HD_A ** -0.5)
            k = k * lax.rsqrt(jnp.sum(k * k, axis=-1, keepdims=True) + 1e-6)
            if per_seq:
                k = jnp.where(live_c, k, 0.0)
            beta = beta_all[:, h:h + 1]
            gcol = gc_all[:, 4 + h:5 + h]
            glr = gl_all[:, 4 + h:5 + h]
            grow = jnp.sum(jnp.where(eye, jnp.broadcast_to(gcol, (C, C)), 0.0), axis=0, keepdims=True)
            decay = jnp.where(causal, jnp.exp(jnp.where(causal, gcol - grow, 0.0)), 0.0)
            eg = jnp.exp(gcol)
            kb = k * beta
            kq = _bdot(jnp.concatenate([kb, q], axis=0), k, NT)
            heads.append(dict(kq=kq, decay=decay,
                              rhs=jnp.concatenate([v * beta, kb * eg], axis=1),
                              q_dec=q * eg, k_dec=k * jnp.exp(glr - gcol), glr=glr))
            yield
        gdn[c] = heads
        e_neg = jnp.exp(-Gc)
        e_rem = jnp.exp(Gl - Gc)
        rG = r_ * jnp.exp(Gc)
        kkG = kk * jnp.exp(Gc - logw)
        kN = kc2 * e_neg
        bN = b_ * e_neg
        aalls = []
        for h in range(H_C):
            hs = slice(h * N_C, (h + 1) * N_C)
            aalls.append(_bdot(jnp.concatenate([kkG[:, hs], rG[:, hs]], axis=0),
                               jnp.concatenate([bN[:, hs], kN[:, hs]], axis=0), NT))
        yield

        for h in range(H_A):
            d = gdn[c][h]
            kq = d.pop('kq')
            decay = d.pop('decay')
            lmats[8 * c + h] = jnp.where(strict, kq[:C] * decay, 0.0)
            d['attn'] = kq[C:] * decay
        yield
        heads = []
        for h in range(H_C):
            hs = slice(h * N_C, (h + 1) * N_C)
            aall = aalls[h]
            lmats[8 * c + H_A + h] = jnp.where(strict, aall[:C, :C], 0.0)
            akk_k = jnp.where(strict, aall[:C, C:], 0.0)
            ar = jnp.concatenate([jnp.where(causal, aall[C:, C:], 0.0),
                                  -jnp.where(causal, aall[C:, :C], 0.0)], axis=1)
            heads.append(dict(x1=_bdot(akk_k, vc[:, hs]), ar=ar))
        rwk[c] = dict(heads=heads, vc=vc, rG=rG, kkG=kkG, kdec=kc2 * e_rem, bdec=b_ * e_rem,
                      e_last=jnp.exp(Gl), gate=gate, bonus=bonus_ss * vc)
        yield

    def par(*gens):
        gens = list(gens)
        while gens:
            alive = []
            for g in gens:
                try:
                    next(g)
                    alive.append(g)
                except StopIteration:
                    pass
            gens = alive
            yield

    def seq(*gens):
        for g in gens:
            yield from g

    def lockstep(gens):
        for _ in par(*gens):
            pass

    nmats = [None] * (8 * nchunk)

    def inverses(chunks):
        idx = [8 * c + k for c in chunks for k in range(8)]
        bmm = lambda a, b: lax.dot_general(a.astype(bf16), b.astype(bf16), BNN, preferred_element_type=f32)
        pack = 2 * C <= 128
        if pack:
            N = -jnp.stack([jnp.concatenate([lmats[i], lmats[j]], axis=1) for i, j in zip(idx[0::2], idx[1::2])])
            left = lax.broadcasted_iota(i32, N.shape, 2) < C
            rhs = lambda X: jnp.concatenate([jnp.where(left, X, 0.0), jnp.where(left, 0.0, X)], axis=1)
        else:
            N = -jnp.stack([lmats[i] for i in idx])
            rhs = lambda X: X
        Q = bmm(N, rhs(N))
        yield
        for _ in range(iters - 1):
            R = bmm(jnp.concatenate([N, Q], axis=1), rhs(Q))
            N = N + Q + R[:, :C]
            Q = R[:, C:]
            yield
        N = N + Q + bmm(N, rhs(Q))
        if pack:
            for p, (i, j) in enumerate(zip(idx[0::2], idx[1::2])):
                nmats[i], nmats[j] = N[p][:, :C], N[p][:, C:]
        else:
            for p, i in enumerate(idx):
                nmats[i] = N[p]
        yield

    def phase3(c):
        for h in range(H_A):
            d = gdn[c][h]
            sol = d['rhs'] + _bdot(nmats[8 * c + h], d['rhs'])
            d['u'], d['w'] = sol[:, :HD_A], sol[:, HD_A:]
        for h in range(H_C):
            hs = slice(h * N_C, (h + 1) * N_C)
            d = rwk[c]['heads'][h]
            both = jnp.concatenate([d['x1'], rwk[c]['kkG'][:, hs]], axis=1)
            both = both + _bdot(nmats[8 * c + H_A + h], both)
            d['u_p'], d['w_p'] = both[:, :N_C], both[:, N_C:]
            d['both'] = both
        yield
        if per_seq:
            return
        for h in range(H_A):
            d = gdn[c][h]
            wu = jnp.concatenate([d['w'], d['u']], axis=1)
            a_wu = _bdot(d['attn'], wu)
            k_wu = _bdot(d['k_dec'], wu, TN)
            d['qe'] = d['q_dec'] - a_wu[:, :HD_A]
            d['o0'] = a_wu[:, HD_A:]
            d['m'] = -k_wu[:, :HD_A]
            d['b'] = k_wu[:, HD_A:]
        yield
        for h in range(H_C):
            hs = slice(h * N_C, (h + 1) * N_C)
            rc = rwk[c]
            d = rc['heads'][h]
            V = rc['vc'][:, hs]
            x = _bdot(d['ar'][:, C:], d['both'])
            d['y0'] = _bdot(d['ar'][:, :C], V) + x[:, :N_C]
            d['re'] = rc['rG'][:, hs] + x[:, N_C:]
            d['m'] = _bdot(d['w_p'], rc['bdec'][:, hs], TN)
            d['b'] = _bdot(jnp.concatenate([V, -d['u_p']], axis=0),
                           jnp.concatenate([rc['kdec'][:, hs], rc['bdec'][:, hs]], axis=0), TN)
        yield

    def gdn_head_seq(c, h):
        rows = pl.ds(c * C, C)
        d = gdn[c][h]
        S = gdn_out[0, h]
        Sb = S.astype(bf16)
        gdn_out[0, h] = S * jnp.exp(d['glr'][-1:, :]) + _bdot(d['m'], Sb) + d['b']
        o = d['o0'] + _bdot(d['qe'], Sb)
        yield
        o = o * lax.rsqrt(jnp.mean(o * o, axis=-1, keepdims=True) + NORM_EPS) * gnorm_w
        z = proj_ref[rows, pl.ds(OFF_Z + h * HD_A, HD_A)]
        mix_ref[rows, pl.ds(h * HD_A, HD_A)] = o * _silu(z)

    def rwkv_head_seq(c, h, ys):
        rc = rwk[c]
        hs = slice(h * N_C, (h + 1) * N_C)
        d = rc['heads'][h]
        S = rwkv_out[0, h]
        Sb = S.astype(bf16)
        rwkv_out[0, h] = S * rc['e_last'][-1:, hs] + d['b'] - _bdot(Sb, d['m'])
        ys[h] = d['y0'] + _bdot(d['re'], Sb, NT)
        yield

    def gdn_head(c, h):
        rows = pl.ds(c * C, C)
        d = gdn[c][h]
        rq, vn, s_old = [], [], []
        for s in range(ngrp):
            gs = slice(s * G, (s + 1) * G)
            S = gdn_in[s, h] if per_seq else gdn_out[0, h]
            s_old.append(S)
            R = _bdot(jnp.concatenate([d['w'][gs], d['q_dec'][gs]], axis=0), S)
            vn.append(d['u'][gs] - R[:G])
            rq.append(R[G:])
        yield
        v_new = _cat(vn, 0)
        o = _cat(rq, 0) + _bdot(d['attn'], v_new)
        for s in range(ngrp):
            gs = slice(s * G, (s + 1) * G)
            g_last = jnp.exp(d['glr'][s * G + G - 1:s * G + G, :])
            S_new = s_old[s] * g_last + _bdot(d['k_dec'][gs], v_new[gs], TN)
            if per_seq:
                gdn_out[s, h] = S_new
            else:
                gdn_out[0, h] = S_new
        yield
        o = o * lax.rsqrt(jnp.mean(o * o, axis=-1, keepdims=True) + NORM_EPS) * gnorm_w
        z = proj_ref[rows, pl.ds(OFF_Z + h * HD_A, HD_A)]
        mix_ref[rows, pl.ds(h * HD_A, HD_A)] = o * _silu(z)

    def rwkv_head(c, h, ys):
        rc = rwk[c]
        hs = slice(h * N_C, (h + 1) * N_C)
        d = rc['heads'][h]
        V = rc['vc'][:, hs]
        rr, ut, s_old = [], [], []
        for s in range(ngrp):
            gs = slice(s * G, (s + 1) * G)
            S = rwkv_in[s, h] if per_seq else rwkv_out[0, h]
            s_old.append(S)
            R = _bdot(jnp.concatenate([d['w_p'][gs], rc['rG'][gs, hs]], axis=0), S, NT)
            ut.append(d['u_p'][gs] + R[:G])
            rr.append(R[G:])
        yield
        Ut = _cat(ut, 0)
        ys[h] = _cat(rr, 0) + _bdot(d['ar'], jnp.concatenate([V, Ut], axis=0))
        for s in range(ngrp):
            gs = slice(s * G, (s + 1) * G)
            upd = _bdot(jnp.concatenate([V[gs], -Ut[gs]], axis=0),
                        jnp.concatenate([rc['kdec'][gs, hs], rc['bdec'][gs, hs]], axis=0), TN)
            S_new = s_old[s] * rc['e_last'][s * G + G - 1:s * G + G, hs] + upd
            if per_seq:
                rwkv_out[s, h] = S_new
            else:
                rwkv_out[0, h] = S_new
        yield

    def phase4(c):
        rows = pl.ds(c * C, C)
        rc = rwk[c]
        ys = [None] * H_C
        if per_seq:
            heads = [gdn_head(c, h) for h in range(H_A)] + [rwkv_head(c, h, ys) for h in range(H_C)]
        else:
            heads = [gdn_head_seq(c, h) for h in range(H_A)] + [rwkv_head_seq(c, h, ys) for h in range(H_C)]
        yield from par(*heads)
        y = jnp.concatenate(ys, axis=1)
        mu_y = _dot01_right(y, seg64) * (1.0 / N_C)
        dy = y - mu_y
        var_y = _dot01_right(dy * dy, seg64) * (1.0 / N_C)
        y = dy * lax.rsqrt(var_y + GN_EPS) * rln_w + rln_b
        mix_ref[rows, pl.ds(D_A + D_B, D_C)] = (y + rc['bonus']) * rc['gate']
        yield

    def sgu(sc):
        rows = pl.ds(sc * SGU_CHUNK, SGU_CHUNK)
        ug = _gelu(proj_ref[rows, pl.ds(OFF_U, D_B)])
        vs = _gelu(proj_ref[rows, pl.ds(OFF_VB, D_B)])
        yield
        mu_v = _dot01_right(vs, seg64) * (1.0 / DH_B)
        dv = vs - mu_v
        var_v = _dot01_right(dv * dv, seg64) * (1.0 / DH_B)
        vs = dv * lax.rsqrt(var_v + LN_EPS) * ln_w + ln_b
        if per_seq:
            cv_out[rows, :] = vs
        yield
        vsb = vs.astype(bf16)
        outs = [lax.dot_general(sguw_ref[h], vsb[:, h * DH_B:(h + 1) * DH_B], NN, preferred_element_type=f32)
                for h in range(4)]
        mixed = jnp.concatenate(outs, axis=1) + sgub_ref[...]
        mix_ref[rows, pl.ds(D_A, D_B)] = ug * mixed
        yield

    first = list(range(nchunk))[:max(nchunk // 2, 1)]
    second = list(range(nchunk))[len(first):]
    sgus = [sgu(sc) for sc in range(TB // SGU_CHUNK)]
    lockstep([phase1(c) for c in first])
    lockstep([phase1(c) for c in second]
             + [seq(inverses(first), par(*[phase3(c) for c in first]))])
    if second:
        lockstep([seq(inverses(second), par(*[phase3(c) for c in second])),
                  seq(*[phase4(c) for c in first]), seq(*sgus)])
        lockstep([seq(*[phase4(c) for c in second])])
    else:
        lockstep([seq(*[phase4(c) for c in first]), seq(*sgus)])

    if per_seq:
        for s in range(TB // SEQ_PAD):
            conv_out[s] = xp[pl.ds(8 + (s + 1) * SEQ_PAD - (CONV_W - 1), CONV_W - 1), :]
    else:
        conv_out[0] = xp[pl.ds(8 + TB - (CONV_W - 1), CONV_W - 1), :]
        xp[pl.ds(0, 8), :] = xp[pl.ds(TB, 8), :]
        pp[pl.ds(0, 8), :] = pp[pl.ds(TB, 8), :]


def _mixer_weight_specs(nidx):
    z2 = (lambda b, j: (0, 0)) if nidx == 2 else (lambda i: (0, 0))
    z3 = (lambda b, j: (0, 0, 0)) if nidx == 2 else (lambda i: (0, 0, 0))
    return [pl.BlockSpec((CONV_W, 3 * D_A), z2),
            pl.BlockSpec((8, 128), z2),
            pl.BlockSpec((16, D_C), z2),
            pl.BlockSpec((4, SGU_CHUNK, SGU_CHUNK), z3),
            pl.BlockSpec((SGU_CHUNK, D_B), z2),
            pl.BlockSpec((1, NC_PAD), z2),
            pl.BlockSpec((128, D_C), z2),
            pl.BlockSpec((128, D_C), z2),
            pl.BlockSpec((128, D_C), z2)]


def _mixer_prompt(proj, mw):
    nj = SEQ // TB_P
    return pl.pallas_call(
        functools.partial(_mixer_kernel, TB=TB_P, C=GDN_CHUNK, G=GDN_CHUNK, per_seq=False),
        grid=(BATCH, nj),
        in_specs=[pl.BlockSpec((TB_P, NP_IN), lambda b, j: (b * nj + j, 0))] + _mixer_weight_specs(2),
        out_specs=[pl.BlockSpec((TB_P, D_MODEL), lambda b, j: (b * nj + j, 0)),
                   pl.BlockSpec((1, H_A, HD_A, HD_A), lambda b, j: (b, 0, 0, 0)),
                   pl.BlockSpec((1, CONV_W - 1, 3 * D_A), lambda b, j: (b, 0, 0)),
                   pl.BlockSpec((1, H_C, N_C, N_C), lambda b, j: (b, 0, 0, 0))],
        out_shape=[jax.ShapeDtypeStruct((T_PROMPT, D_MODEL), f32),
                   jax.ShapeDtypeStruct((BATCH, H_A, HD_A, HD_A), f32),
                   jax.ShapeDtypeStruct((BATCH, CONV_W - 1, 3 * D_A), f32),
                   jax.ShapeDtypeStruct((BATCH, H_C, N_C, N_C), f32)],
        scratch_shapes=[pltpu.VMEM((TB_P + 8, 3 * D_A), f32), pltpu.VMEM((TB_P + 8, NC_PAD), f32)],
        compiler_params=pltpu.CompilerParams(dimension_semantics=("arbitrary", "arbitrary"),
                                             vmem_limit_bytes=VMEM_LIMIT),
        name="mixer_prompt",
    )(proj, *mw)


def _mixer_sample(proj, s_gdn, s_conv, s_rwkv, pcf, mw, l, prev_states):
    nseq = TB_S // SEQ_PAD
    base = T_PROMPT // TB_S
    n_in = 5 + len(mw)
    lay4 = lambda i: (l, i, 0, 0, 0)
    lay3 = lambda i: (l, i, 0, 0)
    return pl.pallas_call(
        functools.partial(_mixer_kernel, TB=TB_S, C=TB_S, G=SEQ_PAD, per_seq=True, n_alias=len(prev_states)),
        grid=(T_SAMPLE // TB_S,),
        in_specs=[pl.BlockSpec((TB_S, NP_IN), lambda i: (base + i, 0)),
                  pl.BlockSpec((None, nseq, H_A, HD_A, HD_A), lay4),
                  pl.BlockSpec((None, nseq, CONV_W - 1, 3 * D_A), lay3),
                  pl.BlockSpec((None, nseq, H_C, N_C, N_C), lay4),
                  pl.BlockSpec((nseq, 1, NC_PAD), lambda i: (i, 0, 0))] + _mixer_weight_specs(1)
                 + [pl.BlockSpec(memory_space=pl.ANY)] * len(prev_states),
        out_specs=[pl.BlockSpec((TB_S, D_MODEL), lambda i: (i, 0)),
                   pl.BlockSpec((None, nseq, H_A, HD_A, HD_A), lay4),
                   pl.BlockSpec((nseq, CONV_W - 1, 3 * D_A), lambda i: (i, 0, 0)),
                   pl.BlockSpec((None, nseq, H_C, N_C, N_C), lay4),
                   pl.BlockSpec((TB_S, D_B), lambda i: (i, 0))],
        out_shape=[jax.ShapeDtypeStruct((T_SAMPLE, D_MODEL), f32),
                   jax.ShapeDtypeStruct((DEPTH, DEC_BATCH, H_A, HD_A, HD_A), f32),
                   jax.ShapeDtypeStruct((DEC_BATCH, CONV_W - 1, 3 * D_A), f32),
                   jax.ShapeDtypeStruct((DEPTH, DEC_BATCH, H_C, N_C, N_C), f32),
                   jax.ShapeDtypeStruct((T_SAMPLE, D_B), f32)],
        scratch_shapes=[pltpu.VMEM((TB_S + 8, 3 * D_A), f32), pltpu.VMEM((TB_S + 8, NC_PAD), f32)],
        input_output_aliases={n_in + k: o for k, o in enumerate((1, 3)[:len(prev_states)])},
        compiler_params=pltpu.CompilerParams(dimension_semantics=("arbitrary",),
                                             vmem_limit_bytes=VMEM_LIMIT),
        name="mixer_sample",
    )(proj, s_gdn, s_conv, s_rwkv, pcf, *mw, *prev_states)


TM_C = 512


def _outproj_kernel(*refs, split):
    i = pl.program_id(0)
    in_prompt = i < T_PROMPT // TM_C
    if split:
        xp_ref, xs_ref = refs[:2]
        x = jnp.where(in_prompt, xp_ref[...], xs_ref[...])
    else:
        x = refs[0][...]
    mixp_ref, mixs_ref, wout_ref, nw_ref, rwc_ref, tri_ref, rb_ref, xg_ref, cnt_ref, run_ref = refs[-10:]

    @pl.when(i == 0)
    def _():
        run_ref[...] = jnp.zeros(run_ref.shape, f32)

    mix = jnp.where(in_prompt, mixp_ref[...], mixs_ref[...])
    x2 = x + jnp.dot(mix.astype(bf16), wout_ref[0], preferred_element_type=f32)
    xg_ref[:, pl.ds(0, D_MODEL)] = x2
    h2 = _rms(x2, nw_ref[...])
    hh, hl = _split2(h2)
    d = lambda a, b: jnp.dot(a, b, preferred_element_type=f32)
    both = d(hh, rwc_ref[...])
    logits = both[:, :128] + both[:, 128:] + d(hl, rwc_ref[:, pl.ds(0, 128)]) + rb_ref[...]

    lane = lax.broadcasted_iota(i32, logits.shape, 1).astype(f32)
    neg = jnp.float32(-jnp.inf)
    is_g = lane < float(N_GROUPS)
    gl = jnp.where(is_g, logits, neg)
    gmax = jnp.max(gl, axis=-1, keepdims=True)
    gsel = jnp.min(jnp.where(gl == gmax, lane, 128.0), axis=-1, keepdims=True)
    gw = 1.0 / jnp.sum(jnp.where(is_g, jnp.exp(jnp.where(is_g, logits - gmax, 0.0)), 0.0),
                       axis=-1, keepdims=True)
    lo = LANE_E0 + float(EPG) * gsel
    in_grp = (lane >= lo) & (lane < lo + float(EPG))
    el = jnp.where(in_grp, logits, neg)
    t1 = jnp.max(el, axis=-1, keepdims=True)
    i1 = jnp.min(jnp.where(el == t1, lane, 128.0), axis=-1, keepdims=True)
    el2 = jnp.where(lane == i1, neg, el)
    t2 = jnp.max(el2, axis=-1, keepdims=True)
    i2 = jnp.min(jnp.where(el2 == t2, lane, 128.0), axis=-1, keepdims=True)
    e2 = jnp.exp(t2 - t1)
    den = 1.0 + e2
    gates = jnp.where(lane == i1, gw / den, 0.0) + jnp.where(lane == i2, gw * e2 / den, 0.0)

    ea, eb = i1 - lo, i2 - lo
    e_lo, e_hi = jnp.minimum(ea, eb), jnp.maximum(ea, eb)
    bucket = float(N_PAIRS) * gsel + e_lo * (7.0 - e_lo) * 0.5 + (e_hi - e_lo - 1.0)
    onehot = jnp.where(lane == bucket, 1.0, 0.0)
    before = lax.dot_general(tri_ref[...], onehot.astype(bf16), NN, preferred_element_type=f32)
    rank = jnp.sum((before + run_ref[...]) * onehot, axis=-1, keepdims=True)
    run_ref[...] += jnp.sum(onehot, axis=0, keepdims=True)
    cnt_ref[...] = jnp.broadcast_to(run_ref[...], cnt_ref.shape)
    xg_ref[:, pl.ds(D_MODEL, 128)] = (gates + jnp.where(lane == float(LANE_G), bucket, 0.0)
                                      + jnp.where(lane == float(LANE_RANK), rank, 0.0))


def _outproj(xs, mix_p, mix_s, wout, nw, rwh, rwl, rb, l):
    rwc = jnp.concatenate([rwh, rwl], axis=1)
    ids = jnp.arange(TM_C)
    tri = (ids[:, None] > ids[None, :]).astype(bf16)
    row = lambda i: (i, 0)
    fix = lambda i: (0, 0)
    npt = T_PROMPT // TM_C
    p_rows = pl.BlockSpec((TM_C, D_MODEL), lambda i: (jnp.minimum(i, npt - 1), 0))
    s_rows = pl.BlockSpec((TM_C, D_MODEL), lambda i: (jnp.maximum(i - npt, 0), 0))
    split = len(xs) == 2
    return pl.pallas_call(
        functools.partial(_outproj_kernel, split=split),
        grid=(T_ALL // TM_C,),
        in_specs=([p_rows, s_rows] if split else [pl.BlockSpec((TM_C, D_MODEL), row)]) + [
                  p_rows, s_rows,
                  pl.BlockSpec((1, D_MODEL, D_MODEL), lambda i: (l, 0, 0)), pl.BlockSpec((1, D_MODEL), fix),
                  pl.BlockSpec((D_MODEL, 256), fix), pl.BlockSpec((TM_C, TM_C), fix),
                  pl.BlockSpec((1, 128), fix)],
        out_specs=[pl.BlockSpec((TM_C, XG_W), row), pl.BlockSpec((8, 128), fix)],
        out_shape=[jax.ShapeDtypeStruct((T_ALL, XG_W), f32), jax.ShapeDtypeStruct((8, 128), f32)],
        scratch_shapes=[pltpu.VMEM((1, 128), f32)],
        compiler_params=pltpu.CompilerParams(dimension_semantics=("arbitrary",),
                                             vmem_limit_bytes=VMEM_LIMIT),
        name="outproj_router",
    )(*xs, mix_p, mix_s, wout, nw, rwc, tri, rb)


TM_E = 512
NT_E = T_ALL // TM_E + N_GROUPS
T_SORT = NT_E * TM_E
DMA_UNROLL = 8


def _invert_kernel(pos_ref, pad_lo_ref, pad_hi_ref, src_ref):
    for g in range(N_GROUPS):
        def fill(p, c):
            src_ref[p] = T_ALL - 1
            return c
        lax.fori_loop(pad_lo_ref[g], pad_hi_ref[g], fill, 0)

    def body(t, c):
        src_ref[pos_ref[t]] = t
        return c
    lax.fori_loop(0, T_ALL, body, 0, unroll=DMA_UNROLL)


def _invert(pos, pad_lo, pad_hi):
    smem = pl.BlockSpec(memory_space=pltpu.SMEM)
    return pl.pallas_call(
        _invert_kernel,
        in_specs=[smem, smem, smem],
        out_specs=smem,
        out_shape=jax.ShapeDtypeStruct((T_SORT,), i32),
        name="moe_invert",
    )(pos, pad_lo, pad_hi)


def _gather_rows(pos_ref, base, ys_hbm, o_ref, sem):
    _start_row_copies(pos_ref, base, ys_hbm, o_ref, sem, 0, TM_E, inline=False)
    _wait_row_copies(ys_hbm, o_ref, sem)


def _combine_kernel(pos_ref, ys_hbm, o_ref, sem):
    _gather_rows(pos_ref, pl.program_id(0) * TM_E, ys_hbm, o_ref, sem)


def _combine_split_kernel(pos_ref, ys_hbm, op_ref, os_ref, sem):
    i = pl.program_id(0)

    @pl.when(i < T_PROMPT // TM_E)
    def _():
        _gather_rows(pos_ref, i * TM_E, ys_hbm, op_ref, sem)

    @pl.when(i >= T_PROMPT // TM_E)
    def _():
        _gather_rows(pos_ref, i * TM_E, ys_hbm, os_ref, sem)


def _combine(pos, ys, split):
    npt = T_PROMPT // TM_E
    if split:
        out_specs = [pl.BlockSpec((TM_E, D_MODEL), lambda i, pos: (jnp.minimum(i, npt - 1), 0)),
                     pl.BlockSpec((TM_E, D_MODEL), lambda i, pos: (jnp.maximum(i - npt, 0), 0))]
        out_shape = [jax.ShapeDtypeStruct((T_PROMPT, D_MODEL), f32),
                     jax.ShapeDtypeStruct((T_SAMPLE, D_MODEL), f32)]
    else:
        out_specs = pl.BlockSpec((TM_E, D_MODEL), lambda i, pos: (i, 0))
        out_shape = jax.ShapeDtypeStruct((T_ALL, D_MODEL), f32)
    return pl.pallas_call(
        _combine_split_kernel if split else _combine_kernel,
        grid_spec=pltpu.PrefetchScalarGridSpec(
            num_scalar_prefetch=1,
            grid=(T_ALL // TM_E,),
            in_specs=[pl.BlockSpec(memory_space=pl.ANY)],
            out_specs=out_specs,
            scratch_shapes=[pltpu.SemaphoreType.DMA(())]),
        out_shape=out_shape,
        compiler_params=pltpu.CompilerParams(dimension_semantics=("arbitrary",),
                                             vmem_limit_bytes=VMEM_LIMIT),
        name="moe_combine",
    )(pos, ys)


ROWS_E = TM_E // 2
N_XBUF = 3


def _experts_kernel(tg_ref, tv_ref, ti_ref, tu_ref, tw_ref, src_ref, xg_hbm, nw_ref, wg_ref, wu_ref, wd_ref, nf_ref,
                    ys_ref, xbuf, hbuf, ybuf, gsem, *, final_norm):
    del tw_ref
    t = pl.program_id(0)
    e = pl.program_id(1)
    valid = tv_ref[t] == 1
    code = tu_ref[t * EPG + e]
    slot = t % N_XBUF
    ahead = ti_ref[jnp.minimum(t + 2, NT_E - 1)]
    aslot = (t + 2) % N_XBUF
    is_last = ti_ref[jnp.minimum(t + 1, NT_E - 1)] == t

    @pl.when((t == 0) & (e == 0))
    def _():
        _start_row_copies(src_ref, 0, xg_hbm, xbuf.at[0], gsem.at[0], 0, TM_E, inline=False)
        _start_row_copies(src_ref, ti_ref[1] * TM_E, xg_hbm, xbuf.at[1], gsem.at[1], 0, TM_E, inline=False)

    @pl.when(valid & (e == 0))
    def _():
        _wait_row_copies(xg_hbm, xbuf.at[slot], gsem.at[slot])
        x2 = xbuf[slot, :, pl.ds(0, D_MODEL)]
        hbuf[...] = _rms(x2, nw_ref[...]).astype(bf16)
        ybuf[...] = x2

    def expert_step():
        hb = hbuf[...]
        he = (_silu(jnp.dot(hb, wg_ref[0, 0], preferred_element_type=f32))
              * jnp.dot(hb, wu_ref[0, 0], preferred_element_type=f32))
        yd = jnp.dot(he.astype(bf16), wd_ref[0, 0], preferred_element_type=f32)
        g = xbuf[slot, :, pl.ds(D_MODEL, 128)]
        lane = lax.broadcasted_iota(i32, g.shape, 1)
        gcol = jnp.sum(jnp.where(lane == LANE_E0 + EPG * tg_ref[t] + e, g, 0.0), axis=-1, keepdims=True)
        ybuf[...] += gcol * yd

    @pl.when(valid & ((code == 1) | (code == 2)))
    def _():
        _start_row_copies(src_ref, ahead * TM_E, xg_hbm, xbuf.at[aslot], gsem.at[aslot],
                          (code - 1) * ROWS_E, ROWS_E, inline=True)
        expert_step()

    @pl.when(valid & (code == 3))
    def _():
        expert_step()

    @pl.when(e == EPG - 1)
    def _():
        y = ybuf[...]
        if final_norm:
            y = _rms(y, nf_ref[...])
        ys_ref[...] = y

    @pl.when(valid & is_last & (e == EPG - 1))
    def _():
        for k in (1, 2):
            s = (t + k) % N_XBUF
            _wait_row_copies(xg_hbm, xbuf.at[s], gsem.at[s])


def _experts(tile_group, tile_valid, tile_idx, tile_used, tile_wexp, src, xg, nw, wg, wu, wd, nf, l, final_norm):
    wsel = lambda t, e, tg, tv, ti, tu, tw, src: (l, tg[t] * EPG + tw[t * EPG + e], 0, 0)
    return pl.pallas_call(
        functools.partial(_experts_kernel, final_norm=final_norm),
        grid_spec=pltpu.PrefetchScalarGridSpec(
            num_scalar_prefetch=6,
            grid=(NT_E, EPG),
            in_specs=[pl.BlockSpec(memory_space=pl.ANY),
                      pl.BlockSpec((1, D_MODEL), lambda t, e, tg, tv, ti, tu, tw, src: (0, 0)),
                      pl.BlockSpec((1, 1, D_MODEL, D_FF_E), wsel),
                      pl.BlockSpec((1, 1, D_MODEL, D_FF_E), wsel),
                      pl.BlockSpec((1, 1, D_FF_E, D_MODEL), wsel),
                      pl.BlockSpec((1, D_MODEL), lambda t, e, tg, tv, ti, tu, tw, src: (0, 0))],
            out_specs=pl.BlockSpec((TM_E, D_MODEL), lambda t, e, tg, tv, ti, tu, tw, src: (ti[t], 0)),
            scratch_shapes=[pltpu.VMEM((N_XBUF, TM_E, XG_W), f32), pltpu.VMEM((TM_E, D_MODEL), bf16),
                            pltpu.VMEM((TM_E, D_MODEL), f32), pltpu.SemaphoreType.DMA((N_XBUF,))]),
        out_shape=jax.ShapeDtypeStruct((T_SORT, D_MODEL), f32),
        compiler_params=pltpu.CompilerParams(dimension_semantics=("arbitrary", "arbitrary"),
                                             vmem_limit_bytes=VMEM_LIMIT),
        name="moe_experts",
    )(tile_group, tile_valid, tile_idx, tile_used, tile_wexp, src, xg, nw, wg, wu, wd, nf)


def _route_meta(xg, cnt_rows):
    b = xg[:, D_MODEL + LANE_G].astype(i32)
    rank = xg[:, D_MODEL + LANE_RANK].astype(i32)
    bcnt = cnt_rows[0, :N_BUCKETS].astype(i32).reshape(N_GROUPS, N_PAIRS)
    cnt = jnp.sum(bcnt, axis=1)
    padded = ((cnt + TM_E - 1) // TM_E) * TM_E
    off_end = jnp.cumsum(padded)
    off = off_end - padded
    bstart = (off[:, None] + jnp.cumsum(bcnt, axis=1) - bcnt).reshape(N_BUCKETS)
    bend = bstart + bcnt.reshape(N_BUCKETS)
    bid = jnp.arange(N_BUCKETS, dtype=i32)
    pos = rank + jnp.sum(jnp.where(b[:, None] == bid[None, :], bstart[None, :], 0), axis=1)
    tile_start = jnp.arange(NT_E, dtype=i32) * TM_E
    tile_group = jnp.minimum(jnp.sum((tile_start[:, None] >= off_end[None, :]).astype(i32), axis=1), N_GROUPS - 1)
    tile_valid = (tile_start < off_end[-1]).astype(i32)
    n_used = off_end[-1] // TM_E
    tile_idx = jnp.minimum(jnp.arange(NT_E, dtype=i32), n_used - 1)
    tile_group = jnp.take(tile_group, tile_idx)
    ts = (tile_idx * TM_E)[:, None]
    overlap = (bstart[None, :] < ts + TM_E) & (bend[None, :] > ts) & (bend > bstart)[None, :]
    pair_has = jnp.array([[int(e in p) for e in range(EPG)] for p in PAIRS] * N_GROUPS, dtype=i32)
    used = jnp.sum(overlap.astype(i32)[:, :, None] * pair_has[None, :, :], axis=1) > 0
    eidx = jnp.arange(EPG, dtype=i32)[None, :]
    prev = lax.cummax(jnp.where(used, eidx, -1), axis=1)
    nxt = lax.cummin(jnp.where(used, eidx, EPG), axis=1, reverse=True)
    wexp = jnp.where(prev >= 0, prev, jnp.minimum(nxt, EPG - 1))
    nth = jnp.cumsum(used.astype(i32), axis=1)
    code = jnp.where(used, jnp.minimum(nth, 3), 0)
    return (pos, tile_group, tile_valid, tile_idx, off + cnt, off_end,
            code.reshape(-1), wexp.reshape(-1))


def _pad_cols(a, n):
    return jnp.pad(a, ((0, 0), (0, n - a.shape[1])))


def _pad_rows(a, n):
    return jnp.pad(a, ((0, n - a.shape[0]), (0, 0)))


def _prep_w_in(w):
    pad_last = lambda a, n: jnp.pad(a, ((0, 0), (0, 0), (0, n - a.shape[-1])))
    c = w[..., 2568:]
    parts = [w[..., 0:2048], pad_last(w[..., 2048:2056], 128), w[..., 2056:2568],
             c[..., 0:768], pad_last(c[..., 768:832], 128), pad_last(c[..., 832:896], 128), c[..., 896:1024]]
    return jnp.concatenate(parts, axis=-1).astype(bf16)


def _prep_mu(mu):
    m = mu[None, :]
    return jnp.concatenate([m[:, 0:768], _pad_cols(m[:, 768:832], 128), _pad_cols(m[:, 832:896], 128),
                            m[:, 896:1024]], axis=1)


def _sgu_mats(sgu_w, sgu_b):
    t = jnp.arange(SGU_CHUNK)
    wm = jnp.where(t[:, None] >= t[None, :], sgu_w, 0.0)
    bias_p = jnp.repeat(jnp.transpose(sgu_b), DH_B, axis=1)
    small = jnp.zeros((4, SEQ_PAD, SEQ_PAD), f32).at[:, SEQ_LEAD:, SEQ_LEAD:].set(wm[:, :DEC_SEQ, :DEC_SEQ])
    eye16 = jnp.eye(TB_S // SEQ_PAD, dtype=f32)
    wm_s = jnp.einsum('ab,hij->haibj', eye16, small).reshape(4, TB_S, TB_S)
    bias_small = jnp.zeros((SEQ_PAD, D_B), f32).at[SEQ_LEAD:].set(bias_p[:DEC_SEQ])
    bias_s = jnp.tile(bias_small, (TB_S // SEQ_PAD, 1))
    return wm.astype(bf16), bias_p, wm_s.astype(bf16), bias_s


def _row(a, n):
    return _pad_cols(a.reshape(1, -1), n)


def kernel(x_prompt, x_sample, state_gdn, state_gdn_conv, state_rwkv, state_rwkv_shift, norm_mix, norm_ffn, norm_final, w_in, gdn_conv_w, gdn_a_log, gdn_dt_bias, gdn_norm_w, sgu_ln_w, sgu_ln_b, sgu_w, sgu_b, rwkv_mu, rwkv_w0, rwkv_w_up, rwkv_a0, rwkv_a_up, rwkv_g_up, rwkv_k_k, rwkv_k_a, rwkv_r_k, rwkv_ln_w, rwkv_ln_b, w_out, router_group_w, router_group_b, router_expert_w, router_expert_b, expert_w_gate, expert_w_up, expert_w_down):
    x = (x_prompt.reshape(T_PROMPT, D_MODEL),
         jnp.pad(x_sample, ((0, 0), (SEQ_LEAD, 0), (0, 0))).reshape(T_SAMPLE, D_MODEL))
    sample_states = ()

    w_pad = _prep_w_in(w_in)
    w_out_b = w_out.astype(bf16)
    wg_b, wu_b, wd_b = expert_w_gate.astype(bf16), expert_w_up.astype(bf16), expert_w_down.astype(bf16)

    outs = {k: [] for k in ('gdn_p', 'conv_p', 'rwkv_p', 'shift_p', 'gdn_s', 'conv_s', 'rwkv_s', 'shift_s', 'cv_s')}
    for l in range(DEPTH):
        last = l == DEPTH - 1
        v128 = jnp.concatenate([
            jnp.pad(gdn_a_log[l].reshape(1, H_A), ((0, 0), (4, 120))),
            jnp.pad(gdn_dt_bias[l].reshape(1, H_A), ((0, 0), (4, 120))),
            gdn_norm_w[l].reshape(1, HD_A), jnp.zeros((5, 128), f32)], axis=0)
        v256 = jnp.concatenate([a.reshape(1, D_C) for a in (
            sgu_ln_w[l], sgu_ln_b[l], rwkv_w0[l], rwkv_a0[l], rwkv_k_k[l], rwkv_k_a[l], rwkv_r_k[l],
            rwkv_ln_w[l], rwkv_ln_b[l])] + [jnp.zeros((7, D_C), f32)], axis=0)
        wm_p, bias_p, wm_s, bias_s = _sgu_mats(sgu_w[l], sgu_b[l])
        common = (gdn_conv_w[l], v128, v256)
        tail = (_prep_mu(rwkv_mu[l]), _pad_rows(rwkv_w_up[l], 128).astype(bf16),
                _pad_rows(rwkv_a_up[l], 128).astype(bf16), rwkv_g_up[l].astype(bf16))
        mw_p = common + (wm_p, bias_p) + tail
        mw_s = common + (wm_s, bias_s) + tail

        if l == 0:
            proj, h = _inproj(x, norm_mix[l].reshape(1, D_MODEL), w_pad, l)
        else:
            proj, h, x_tok = _inproj_gather(pos, ys, norm_mix[l].reshape(1, D_MODEL), w_pad, l)
            x = (x_tok,)
        pcf = _mm(state_rwkv_shift[l], w_pad[l, :, OFF_C:]).reshape(DEC_BATCH, 1, NC_PAD)
        mix_p, gdn_p, conv_p, rwkv_p = _mixer_prompt(proj, mw_p)
        mix_s, gdn_s, conv_s, rwkv_s, cv_s = _mixer_sample(
            proj, state_gdn, state_gdn_conv, state_rwkv, pcf, mw_s, l, sample_states)
        sample_states = (gdn_s, rwkv_s)

        rw = _pad_cols(jnp.concatenate([router_group_w[l], router_expert_w[l]], axis=1), 128)
        rwh = rw.astype(bf16)
        rwl = (rw - rwh.astype(f32)).astype(bf16)
        rb = _row(jnp.concatenate([router_group_b[l], router_expert_b[l]]), 128)
        nw_ffn = norm_ffn[l].reshape(1, D_MODEL)
        xg, cnt_rows = _outproj(x, mix_p, mix_s, w_out_b, nw_ffn, rwh, rwl, rb, l)
        pos, tile_group, tile_valid, tile_idx, pad_lo, pad_hi, tile_used, tile_wexp = _route_meta(xg, cnt_rows)
        src = _invert(pos, pad_lo, pad_hi)
        ys = _experts(tile_group, tile_valid, tile_idx, tile_used, tile_wexp, src, xg, nw_ffn, wg_b, wu_b, wd_b,
                      norm_final.reshape(1, D_MODEL), l, final_norm=last)
        if last:
            x = _combine(pos, ys, split=True)

        outs['gdn_p'].append(gdn_p)
        outs['conv_p'].append(conv_p)
        outs['rwkv_p'].append(rwkv_p)
        outs['shift_p'].append(h[SEQ // SEQ_PAD - 1:T_PROMPT // SEQ_PAD:SEQ // SEQ_PAD])
        outs['conv_s'].append(conv_s)
        outs['shift_s'].append(h[T_PROMPT // SEQ_PAD:])
        outs['cv_s'].append(cv_s.reshape(DEC_BATCH, SEQ_PAD, D_B)[:, SEQ_LEAD:])

    y_prompt = x[0].reshape(BATCH, SEQ, D_MODEL)
    y_sample = x[1].reshape(DEC_BATCH, SEQ_PAD, D_MODEL)[:, SEQ_LEAD:]
    st = lambda k: jnp.stack(outs[k])
    return (y_prompt, y_sample, st('gdn_p'), st('conv_p'), st('rwkv_p'), st('shift_p'),
            sample_states[0], st('conv_s'), sample_states[1], st('shift_s'), st('cv_s'))
```

```python
import functools
import math

import jax
import jax.numpy as jnp
from jax import lax
from jax.experimental import pallas as pl
from jax.experimental.pallas import tpu as pltpu

f32 = jnp.float32
bf16 = jnp.bfloat16
i32 = jnp.int32

D_MODEL = 1024
BATCH = 8
SEQ = 2048
DEPTH = 2
DEC_BATCH = 128
DEC_SEQ = 4
H_A = 4
HD_A = 128
D_A = 512
CONV_W = 4
D_B = 256
DH_B = 64
SGU_CHUNK = 128
H_C = 4
N_C = 64
D_C = 256
N_GROUPS = 4
EPG = 4
D_FF_E = 512
NORM_EPS = 1e-6
LN_EPS = 1e-5
GN_EPS = 64e-5

SEQ_PAD = 8
SEQ_LEAD = SEQ_PAD - DEC_SEQ
T_PROMPT = BATCH * SEQ
T_SAMPLE = DEC_BATCH * SEQ_PAD
T_ALL = T_PROMPT + T_SAMPLE
TB_P = 512
TB_S = 128
GDN_CHUNK = 64

OFF_Q, OFF_K, OFF_V, OFF_Z, OFF_BA, OFF_U, OFF_VB, OFF_C = 0, 512, 1024, 1536, 2048, 2176, 2432, 2688
NP_IN = 3840
NC_PAD = NP_IN - OFF_C

N_PAIRS = 6
N_BUCKETS = N_GROUPS * N_PAIRS
PAIRS = ((0, 1), (0, 2), (0, 3), (1, 2), (1, 3), (2, 3))
XG_W = D_MODEL + 128
LANE_G, LANE_RANK, LANE_E0 = 0, 1, 4

VMEM_LIMIT = 48 * 1024 * 1024


NN = (((1,), (0,)), ((), ()))
NT = (((1,), (1,)), ((), ()))
TN = (((0,), (0,)), ((), ()))
BNN = (((2,), (1,)), ((0,), (0,)))


def _bdot(a, b, dims=NN):
    return lax.dot_general(a.astype(bf16), b.astype(bf16), dims, preferred_element_type=f32)


def _split2(x):
    hi = x.astype(bf16)
    lo = (x - hi.astype(f32)).astype(bf16)
    return hi, lo


def _split3(x):
    hi = x.astype(bf16)
    r = x - hi.astype(f32)
    mid = r.astype(bf16)
    lo = (r - mid.astype(f32)).astype(bf16)
    return hi, mid, lo


def _dot01_left(m01, x):
    hi, mid, lo = _split3(x)
    d = lambda p: lax.dot_general(m01, p, NN, preferred_element_type=f32)
    return d(hi) + d(mid) + d(lo)


def _dot01_right(x, m01):
    hi, lo = _split2(x)
    d = lambda p: lax.dot_general(p, m01, NN, preferred_element_type=f32)
    return d(hi) + d(lo)


def _softplus(x):
    return jnp.maximum(x, 0.0) + jnp.log(1.0 + jnp.exp(-jnp.abs(x)))


def _sigmoid(x):
    return 0.5 * jnp.tanh(0.5 * x) + 0.5


def _silu(x):
    return x * _sigmoid(x)


def _gelu(x):
    return 0.5 * x * (1.0 + lax.erf(x * (1.0 / math.sqrt(2.0))))


def _rms(x, w):
    return x * lax.rsqrt(jnp.mean(x * x, axis=-1, keepdims=True) + NORM_EPS) * w


TM_A = 512
N_SLAB = 768


def _start_row_copies(idx_ref, base, src_hbm, dst, sem, r0, n, inline, prio=None):
    def one(r, k):
        p = idx_ref[base + r]
        pltpu.make_async_copy(src_hbm.at[pl.ds(p, 1), :], dst.at[pl.ds(r, 1), :], sem).start(
            priority=k % 2 if prio is None else prio)

    if inline:
        for k in range(n):
            one(r0 + k, k)
    else:
        def body(j, c):
            one(r0 + 2 * j, 0)
            one(r0 + 2 * j + 1, 1)
            return c
        lax.fori_loop(0, n // 2, body, 0, unroll=DMA_UNROLL // 2)


def _wait_row_copies(src_hbm, dst, sem):
    pltpu.make_async_copy(src_hbm.at[pl.ds(0, dst.shape[0]), :], dst, sem).wait()


def _inproj_kernel(*refs, split):
    if split:
        xp_ref, xs_ref, nw_ref, w_ref, proj_ref, h8_ref, hscr = refs
        x = jnp.where(pl.program_id(0) < T_PROMPT // TM_A, xp_ref[...], xs_ref[...])
    else:
        x_ref, nw_ref, w_ref, proj_ref, h8_ref, hscr = refs
        x = x_ref[...]
    _inproj_body(x, nw_ref, w_ref, proj_ref, h8_ref, hscr)


def _inproj_gather_kernel(pos_ref, ys_hbm, nw_ref, w_ref, proj_ref, h8_ref, x_ref, hscr, xbuf, sem):
    i = pl.program_id(0)
    n = pl.num_programs(0)
    slot = i % 2

    @pl.when(i == 0)
    def _():
        _start_row_copies(pos_ref, 0, ys_hbm, xbuf.at[0], sem.at[0], 0, TM_A, inline=False)

    _wait_row_copies(ys_hbm, xbuf.at[slot], sem.at[slot])
    x = xbuf[slot]
    x_ref[...] = x
    nxt = jnp.minimum(i + 1, n - 1)
    _start_row_copies(pos_ref, nxt * TM_A, ys_hbm, xbuf.at[1 - slot], sem.at[1 - slot], 0, TM_A, inline=True)
    _inproj_body(x, nw_ref, w_ref, proj_ref, h8_ref, hscr)

    @pl.when(i == n - 1)
    def _():
        _wait_row_copies(ys_hbm, xbuf.at[1 - slot], sem.at[1 - slot])


def _inproj_body(x, nw_ref, w_ref, proj_ref, h8_ref, hscr):
    h = _rms(x, nw_ref[...])
    for k in range(D_MODEL // 128):
        hscr[k] = h[:, k * 128:(k + 1) * 128]
        h8_ref[:, pl.ds(k * 128, 128)] = hscr[k, pl.ds(SEQ_PAD - 1, TM_A // SEQ_PAD, stride=SEQ_PAD), :]
    hb = h.astype(bf16)
    for n in range(NP_IN // N_SLAB):
        sl = pl.ds(n * N_SLAB, N_SLAB)
        proj_ref[:, sl] = jnp.dot(hb, w_ref[0, :, sl], preferred_element_type=f32)


def _inproj_gather(pos, ys, nw, w_pad, l):
    return pl.pallas_call(
        _inproj_gather_kernel,
        grid_spec=pltpu.PrefetchScalarGridSpec(
            num_scalar_prefetch=1,
            grid=(T_ALL // TM_A,),
            in_specs=[pl.BlockSpec(memory_space=pl.ANY),
                      pl.BlockSpec((1, D_MODEL), lambda i, pos: (0, 0)),
                      pl.BlockSpec((1, D_MODEL, NP_IN), lambda i, pos: (l, 0, 0))],
            out_specs=[pl.BlockSpec((TM_A, NP_IN), lambda i, pos: (i, 0)),
                       pl.BlockSpec((TM_A // SEQ_PAD, D_MODEL), lambda i, pos: (i, 0)),
                       pl.BlockSpec((TM_A, D_MODEL), lambda i, pos: (i, 0))],
            scratch_shapes=[pltpu.VMEM((D_MODEL // 128, TM_A, 128), f32),
                            pltpu.VMEM((2, TM_A, D_MODEL), f32),
                            pltpu.SemaphoreType.DMA((2,))]),
        out_shape=[jax.ShapeDtypeStruct((T_ALL, NP_IN), f32),
                   jax.ShapeDtypeStruct((T_ALL // SEQ_PAD, D_MODEL), f32),
                   jax.ShapeDtypeStruct((T_ALL, D_MODEL), f32)],
        compiler_params=pltpu.CompilerParams(dimension_semantics=("arbitrary",),
                                             vmem_limit_bytes=VMEM_LIMIT),
        name="inproj_gather",
    )(pos, ys, nw, w_pad)


def _inproj(xs, nw, w_pad, l):
    split = len(xs) == 2
    npt = T_PROMPT // TM_A
    if split:
        x_specs = [pl.BlockSpec((TM_A, D_MODEL), lambda i: (jnp.minimum(i, npt - 1), 0)),
                   pl.BlockSpec((TM_A, D_MODEL), lambda i: (jnp.maximum(i - npt, 0), 0))]
    else:
        x_specs = [pl.BlockSpec((TM_A, D_MODEL), lambda i: (i, 0))]
    return pl.pallas_call(
        functools.partial(_inproj_kernel, split=split),
        grid=(T_ALL // TM_A,),
        in_specs=x_specs + [pl.BlockSpec((1, D_MODEL), lambda i: (0, 0)),
                            pl.BlockSpec((1, D_MODEL, NP_IN), lambda i: (l, 0, 0))],
        out_specs=[pl.BlockSpec((TM_A, NP_IN), lambda i: (i, 0)),
                   pl.BlockSpec((TM_A // SEQ_PAD, D_MODEL), lambda i: (i, 0))],
        out_shape=[jax.ShapeDtypeStruct((T_ALL, NP_IN), f32),
                   jax.ShapeDtypeStruct((T_ALL // SEQ_PAD, D_MODEL), f32)],
        scratch_shapes=[pltpu.VMEM((D_MODEL // 128, TM_A, 128), f32)],
        compiler_params=pltpu.CompilerParams(dimension_semantics=("arbitrary",),
                                             vmem_limit_bytes=VMEM_LIMIT),
        name="inproj",
    )(*xs, nw, w_pad)


def _mm_kernel(a_ref, b_ref, o_ref):
    o_ref[...] = jnp.dot(a_ref[...].astype(bf16), b_ref[...], preferred_element_type=f32)


def _mm(a, b):
    return pl.pallas_call(
        _mm_kernel,
        out_shape=jax.ShapeDtypeStruct((a.shape[0], b.shape[1]), f32),
        compiler_params=pltpu.CompilerParams(vmem_limit_bytes=VMEM_LIMIT),
        name="shift_proj",
    )(a, b)


def _cat(parts, axis):
    return parts[0] if len(parts) == 1 else jnp.concatenate(parts, axis=axis)


def _mixer_kernel(*refs, TB, C, G, per_seq, n_alias=0):
    it = iter(refs)
    proj_ref = next(it)
    if per_seq:
        gdn_in, conv_in, rwkv_in, pcf_in = next(it), next(it), next(it), next(it)
    convw_ref, v128_ref, v256_ref, sguw_ref, sgub_ref, mu_ref, wup_ref, aup_ref, gup_ref = (
        next(it) for _ in range(9))
    for _ in range(n_alias):
        next(it)
    mix_ref, gdn_out, conv_out, rwkv_out = next(it), next(it), next(it), next(it)
    cv_out = next(it) if per_seq else None
    xp, pp = next(it), next(it)

    nchunk = TB // C
    ngrp = C // G
    iters = int(math.log2(G)) - 1

    if per_seq:
        xp[pl.ds(0, 8), :] = jnp.zeros((8, 3 * D_A), f32)
        pp[pl.ds(0, 8), :] = jnp.zeros((8, NC_PAD), f32)
    else:
        @pl.when(pl.program_id(1) == 0)
        def _():
            xp[pl.ds(0, 8), :] = jnp.zeros((8, 3 * D_A), f32)
            pp[pl.ds(0, 8), :] = jnp.zeros((8, NC_PAD), f32)
            gdn_out[...] = jnp.zeros(gdn_out.shape, f32)
            rwkv_out[...] = jnp.zeros(rwkv_out.shape, f32)
    xp[pl.ds(8, TB), :] = proj_ref[:, pl.ds(OFF_Q, 3 * D_A)]
    pp[pl.ds(8, TB), :] = proj_ref[:, pl.ds(OFF_C, NC_PAD)]
    if per_seq:
        for s in range(TB // SEQ_PAD):
            r0 = 8 + s * SEQ_PAD
            xp[pl.ds(r0 + SEQ_LEAD - (CONV_W - 1), CONV_W - 1), :] = conv_in[s]
            pp[pl.ds(r0 + SEQ_LEAD - 1, 1), :] = pcf_in[s]

    rowi = lax.broadcasted_iota(i32, (TB, 1), 0)
    live = (rowi % SEQ_PAD) >= SEQ_LEAD if per_seq else None

    ii = lax.broadcasted_iota(i32, (C, C), 0)
    jj = lax.broadcasted_iota(i32, (C, C), 1)
    same = (ii // G) == (jj // G)
    causal = (ii >= jj) & same
    strict = (ii > jj) & same
    eye = ii == jj
    m_cum = causal.astype(bf16)
    m_grp = same.astype(bf16)

    def conv_cols(r0, c0):
        cs = pl.ds(c0, 128)
        acc = xp[pl.ds(r0 + 8, C), cs] * convw_ref[pl.ds(3, 1), cs]
        for j in range(CONV_W - 1):
            acc = acc + xp[pl.ds(r0 + 5 + j, C), cs] * convw_ref[pl.ds(j, 1), cs]
        return _silu(acc)

    alog_row = v128_ref[pl.ds(0, 1), :]
    dtb_row = v128_ref[pl.ds(1, 1), :]
    gnorm_w = v128_ref[pl.ds(2, 1), :]
    ln_w, ln_b = v256_ref[pl.ds(0, 1), :], v256_ref[pl.ds(1, 1), :]
    w0, a0 = v256_ref[pl.ds(2, 1), :], v256_ref[pl.ds(3, 1), :]
    k_k, k_a, r_k = v256_ref[pl.ds(4, 1), :], v256_ref[pl.ds(5, 1), :], v256_ref[pl.ds(6, 1), :]
    rln_w, rln_b = v256_ref[pl.ds(7, 1), :], v256_ref[pl.ds(8, 1), :]

    l64i = lax.broadcasted_iota(i32, (D_C, D_C), 0) // N_C
    l64j = lax.broadcasted_iota(i32, (D_C, D_C), 1) // N_C
    seg64 = (l64i == l64j).astype(bf16)
    gdn, rwk, lmats = [None] * nchunk, [None] * nchunk, [None] * (8 * nchunk)

    def phase1(c):
        R0 = c * C
        rows = pl.ds(R0, C)
        live_c = live[R0:R0 + C] if per_seq else None

        ba = proj_ref[rows, pl.ds(OFF_BA, 128)]
        beta_all = _sigmoid(ba)
        g_all = -jnp.exp(alog_row) * _softplus(ba + dtb_row)
        if per_seq:
            beta_all = jnp.where(live_c, beta_all, 0.0)
            g_all = jnp.where(live_c, g_all, 0.0)
        pcur = pp[pl.ds(R0 + 8, C), :]
        pprev = pp[pl.ds(R0 + 7, C), :]
        pm = pcur + mu_ref[...] * (pprev - pcur)
        r_ = pm[:, 0:D_C]
        kc = pm[:, D_C:2 * D_C]
        vc = pm[:, 2 * D_C:3 * D_C]
        wd = pm[:, 3 * D_C:3 * D_C + 128]
        ad = pm[:, 3 * D_C + 128:3 * D_C + 256]
        gd = pm[:, 3 * D_C + 256:3 * D_C + 384]
        wl_mm = _bdot(jnp.tanh(wd), wup_ref[...])
        a_mm = _bdot(ad, aup_ref[...])
        gate = _bdot(_sigmoid(gd), gup_ref[...])
        kk = kc * k_k
        kk_ss = _dot01_right(kk * kk, seg64)
        yield

        gc_all = _dot01_left(m_cum, g_all)
        total = lambda x: jnp.sum(x, axis=0, keepdims=True) if ngrp == 1 else _dot01_left(m_grp, x)
        gl_all = total(g_all)
        w_log = -_softplus(-(w0 + wl_mm)) - 0.5
        logw = -jnp.exp(w_log)
        a_ = _sigmoid(a0 + a_mm)
        kk = kk * lax.rsqrt(kk_ss + 1e-6)
        kc2 = kc * (1.0 + (a_ - 1.0) * k_a)
        if per_seq:
            logw = jnp.where(live_c, logw, 0.0)
            kk = jnp.where(live_c, kk, 0.0)
            kc2 = jnp.where(live_c, kc2, 0.0)
        b_ = kk * a_
        lw_hi, lw_lo = _split2(logw)
        Gc = (lax.dot_general(m_cum, lw_hi, NN, preferred_element_type=f32)
              + lax.dot_general(m_cum, lw_lo, NN, preferred_element_type=f32))
        Gl = total(logw)
        bonus_ss = _dot01_right(r_ * kc2 * r_k, seg64)
        yield

        heads = []
        for h in range(H_A):
            q = conv_cols(R0, OFF_Q + h * HD_A)
            k = conv_cols(R0, OFF_K + h * HD_A)
            v = conv_cols(R0, OFF_V + h * HD_A)
            q = q * lax.rsqrt(jnp.sum(q * q, axis=-1, keepdims=True) + 1e-6) * (HD_A ** -0.5)
            k = k * lax.rsqrt(jnp.sum(k * k, axis=-1, keepdims=True) + 1e-6)
            if per_seq:
                k = jnp.where(live_c, k, 0.0)
            beta = beta_all[:, h:h + 1]
            gcol = gc_all[:, 4 + h:5 + h]
            glr = gl_all[:, 4 + h:5 + h]
            grow = jnp.sum(jnp.where(eye, jnp.broadcast_to(gcol, (C, C)), 0.0), axis=0, keepdims=True)
            decay = jnp.where(causal, jnp.exp(jnp.where(causal, gcol - grow, 0.0)), 0.0)
            eg = jnp.exp(gcol)
            kb = k * beta
            kq = _bdot(jnp.concatenate([kb, q], axis=0), k, NT)
            heads.append(dict(kq=kq, decay=decay,
                              rhs=jnp.concatenate([v * beta, kb * eg], axis=1),
                              q_dec=q * eg, k_dec=k * jnp.exp(glr - gcol), glr=glr))
            yield
        gdn[c] = heads
        e_neg = jnp.exp(-Gc)
        e_rem = jnp.exp(Gl - Gc)
        rG = r_ * jnp.exp(Gc)
        kkG = kk * jnp.exp(Gc - logw)
        kN = kc2 * e_neg
        bN = b_ * e_neg
        aalls = []
        for h in range(H_C):
            hs = slice(h * N_C, (h + 1) * N_C)
            aalls.append(_bdot(jnp.concatenate([kkG[:, hs], rG[:, hs]], axis=0),
                               jnp.concatenate([bN[:, hs], kN[:, hs]], axis=0), NT))
        yield

        for h in range(H_A):
            d = gdn[c][h]
            kq = d.pop('kq')
            decay = d.pop('decay')
            lmats[8 * c + h] = jnp.where(strict, kq[:C] * decay, 0.0)
            d['attn'] = kq[C:] * decay
        yield
        heads = []
        for h in range(H_C):
            hs = slice(h * N_C, (h + 1) * N_C)
            aall = aalls[h]
            lmats[8 * c + H_A + h] = jnp.where(strict, aall[:C, :C], 0.0)
            akk_k = jnp.where(strict, aall[:C, C:], 0.0)
            ar = jnp.concatenate([jnp.where(causal, aall[C:, C:], 0.0),
                                  -jnp.where(causal, aall[C:, :C], 0.0)], axis=1)
            heads.append(dict(x1=_bdot(akk_k, vc[:, hs]), ar=ar))
        rwk[c] = dict(heads=heads, vc=vc, rG=rG, kkG=kkG, kdec=kc2 * e_rem, bdec=b_ * e_rem,
                      e_last=jnp.exp(Gl), gate=gate, bonus=bonus_ss * vc)
        yield

    def par(*gens):
        gens = list(gens)
        while gens:
            alive = []
            for g in gens:
                try:
                    next(g)
                    alive.append(g)
                except StopIteration:
                    pass
            gens = alive
            yield

    def seq(*gens):
        for g in gens:
            yield from g

    def lockstep(gens):
        for _ in par(*gens):
            pass

    nmats = [None] * (8 * nchunk)

    def inverses(chunks):
        idx = [8 * c + k for c in chunks for k in range(8)]
        bmm = lambda a, b: lax.dot_general(a.astype(bf16), b.astype(bf16), BNN, preferred_element_type=f32)
        pack = 2 * C <= 128
        if pack:
            N = -jnp.stack([jnp.concatenate([lmats[i], lmats[j]], axis=1) for i, j in zip(idx[0::2], idx[1::2])])
            left = lax.broadcasted_iota(i32, N.shape, 2) < C
            rhs = lambda X: jnp.concatenate([jnp.where(left, X, 0.0), jnp.where(left, 0.0, X)], axis=1)
        else:
            N = -jnp.stack([lmats[i] for i in idx])
            rhs = lambda X: X
        Q = bmm(N, rhs(N))
        yield
        for _ in range(iters - 1):
            R = bmm(jnp.concatenate([N, Q], axis=1), rhs(Q))
            N = N + Q + R[:, :C]
            Q = R[:, C:]
            yield
        N = N + Q + bmm(N, rhs(Q))
        if pack:
            for p, (i, j) in enumerate(zip(idx[0::2], idx[1::2])):
                nmats[i], nmats[j] = N[p][:, :C], N[p][:, C:]
        else:
            for p, i in enumerate(idx):
                nmats[i] = N[p]
        yield

    def phase3(c):
        for h in range(H_A):
            d = gdn[c][h]
            sol = d['rhs'] + _bdot(nmats[8 * c + h], d['rhs'])
            d['u'], d['w'] = sol[:, :HD_A], sol[:, HD_A:]
        for h in range(H_C):
            hs = slice(h * N_C, (h + 1) * N_C)
            d = rwk[c]['heads'][h]
            both = jnp.concatenate([d['x1'], rwk[c]['kkG'][:, hs]], axis=1)
            both = both + _bdot(nmats[8 * c + H_A + h], both)
            d['u_p'], d['w_p'] = both[:, :N_C], both[:, N_C:]
            d['both'] = both
        yield
        if per_seq:
            return
        for h in range(H_A):
            d = gdn[c][h]
            wu = jnp.concatenate([d['w'], d['u']], axis=1)
            a_wu = _bdot(d['attn'], wu)
            k_wu = _bdot(d['k_dec'], wu, TN)
            d['qe'] = d['q_dec'] - a_wu[:, :HD_A]
            d['o0'] = a_wu[:, HD_A:]
            d['m'] = -k_wu[:, :HD_A]
            d['b'] = k_wu[:, HD_A:]
        yield
        for h in range(H_C):
            hs = slice(h * N_C, (h + 1) * N_C)
            rc = rwk[c]
            d = rc['heads'][h]
            V = rc['vc'][:, hs]
            x = _bdot(d['ar'][:, C:], d['both'])
            d['y0'] = _bdot(d['ar'][:, :C], V) + x[:, :N_C]
            d['re'] = rc['rG'][:, hs] + x[:, N_C:]
            d['m'] = _bdot(d['w_p'], rc['bdec'][:, hs], TN)
            d['b'] = _bdot(jnp.concatenate([V, -d['u_p']], axis=0),
                           jnp.concatenate([rc['kdec'][:, hs], rc['bdec'][:, hs]], axis=0), TN)
        yield

    def gdn_head_seq(c, h):
        rows = pl.ds(c * C, C)
        d = gdn[c][h]
        S = gdn_out[0, h]
        Sb = S.astype(bf16)
        gdn_out[0, h] = S * jnp.exp(d['glr'][-1:, :]) + _bdot(d['m'], Sb) + d['b']
        o = d['o0'] + _bdot(d['qe'], Sb)
        yield
        o = o * lax.rsqrt(jnp.mean(o * o, axis=-1, keepdims=True) + NORM_EPS) * gnorm_w
        z = proj_ref[rows, pl.ds(OFF_Z + h * HD_A, HD_A)]
        mix_ref[rows, pl.ds(h * HD_A, HD_A)] = o * _silu(z)

    def rwkv_head_seq(c, h, ys):
        rc = rwk[c]
        hs = slice(h * N_C, (h + 1) * N_C)
        d = rc['heads'][h]
        S = rwkv_out[0, h]
        Sb = S.astype(bf16)
        rwkv_out[0, h] = S * rc['e_last'][-1:, hs] + d['b'] - _bdot(Sb, d['m'])
        ys[h] = d['y0'] + _bdot(d['re'], Sb, NT)
        yield

    def gdn_head(c, h):
        rows = pl.ds(c * C, C)
        d = gdn[c][h]
        rq, vn, s_old = [], [], []
        for s in range(ngrp):
            gs = slice(s * G, (s + 1) * G)
            S = gdn_in[s, h] if per_seq else gdn_out[0, h]
            s_old.append(S)
            R = _bdot(jnp.concatenate([d['w'][gs], d['q_dec'][gs]], axis=0), S)
            vn.append(d['u'][gs] - R[:G])
            rq.append(R[G:])
        yield
        v_new = _cat(vn, 0)
        o = _cat(rq, 0) + _bdot(d['attn'], v_new)
        for s in range(ngrp):
            gs = slice(s * G, (s + 1) * G)
            g_last = jnp.exp(d['glr'][s * G + G - 1:s * G + G, :])
            S_new = s_old[s] * g_last + _bdot(d['k_dec'][gs], v_new[gs], TN)
            if per_seq:
                gdn_out[s, h] = S_new
            else:
                gdn_out[0, h] = S_new
        yield
        o = o * lax.rsqrt(jnp.mean(o * o, axis=-1, keepdims=True) + NORM_EPS) * gnorm_w
        z = proj_ref[rows, pl.ds(OFF_Z + h * HD_A, HD_A)]
        mix_ref[rows, pl.ds(h * HD_A, HD_A)] = o * _silu(z)

    def rwkv_head(c, h, ys):
        rc = rwk[c]
        hs = slice(h * N_C, (h + 1) * N_C)
        d = rc['heads'][h]
        V = rc['vc'][:, hs]
        rr, ut, s_old = [], [], []
        for s in range(ngrp):
            gs = slice(s * G, (s + 1) * G)
            S = rwkv_in[s, h] if per_seq else rwkv_out[0, h]
            s_old.append(S)
            R = _bdot(jnp.concatenate([d['w_p'][gs], rc['rG'][gs, hs]], axis=0), S, NT)
            ut.append(d['u_p'][gs] + R[:G])
            rr.append(R[G:])
        yield
        Ut = _cat(ut, 0)
        ys[h] = _cat(rr, 0) + _bdot(d['ar'], jnp.concatenate([V, Ut], axis=0))
        for s in range(ngrp):
            gs = slice(s * G, (s + 1) * G)
            upd = _bdot(jnp.concatenate([V[gs], -Ut[gs]], axis=0),
                        jnp.concatenate([rc['kdec'][gs, hs], rc['bdec'][gs, hs]], axis=0), TN)
            S_new = s_old[s] * rc['e_last'][s * G + G - 1:s * G + G, hs] + upd
            if per_seq:
                rwkv_out[s, h] = S_new
            else:
                rwkv_out[0, h] = S_new
        yield

    def phase4(c):
        rows = pl.ds(c * C, C)
        rc = rwk[c]
        ys = [None] * H_C
        if per_seq:
            heads = [gdn_head(c, h) for h in range(H_A)] + [rwkv_head(c, h, ys) for h in range(H_C)]
        else:
            heads = [gdn_head_seq(c, h) for h in range(H_A)] + [rwkv_head_seq(c, h, ys) for h in range(H_C)]
        yield from par(*heads)
        y = jnp.concatenate(ys, axis=1)
        mu_y = _dot01_right(y, seg64) * (1.0 / N_C)
        dy = y - mu_y
        var_y = _dot01_right(dy * dy, seg64) * (1.0 / N_C)
        y = dy * lax.rsqrt(var_y + GN_EPS) * rln_w + rln_b
        mix_ref[rows, pl.ds(D_A + D_B, D_C)] = (y + rc['bonus']) * rc['gate']
        yield

    def sgu(sc):
        rows = pl.ds(sc * SGU_CHUNK, SGU_CHUNK)
        ug = _gelu(proj_ref[rows, pl.ds(OFF_U, D_B)])
        vs = _gelu(proj_ref[rows, pl.ds(OFF_VB, D_B)])
        yield
        mu_v = _dot01_right(vs, seg64) * (1.0 / DH_B)
        dv = vs - mu_v
        var_v = _dot01_right(dv * dv, seg64) * (1.0 / DH_B)
        vs = dv * lax.rsqrt(var_v + LN_EPS) * ln_w + ln_b
        if per_seq:
            cv_out[rows, :] = vs
        yield
        vsb = vs.astype(bf16)
        outs = [lax.dot_general(sguw_ref[h], vsb[:, h * DH_B:(h + 1) * DH_B], NN, preferred_element_type=f32)
                for h in range(4)]
        mixed = jnp.concatenate(outs, axis=1) + sgub_ref[...]
        mix_ref[rows, pl.ds(D_A, D_B)] = ug * mixed
        yield

    first = list(range(nchunk))[:max(nchunk // 2, 1)]
    second = list(range(nchunk))[len(first):]
    sgus = [sgu(sc) for sc in range(TB // SGU_CHUNK)]
    lockstep([phase1(c) for c in first])
    lockstep([phase1(c) for c in second]
             + [seq(inverses(first), par(*[phase3(c) for c in first]))])
    if second:
        lockstep([seq(inverses(second), par(*[phase3(c) for c in second])),
                  seq(*[phase4(c) for c in first]), seq(*sgus)])
        lockstep([seq(*[phase4(c) for c in second])])
    else:
        lockstep([seq(*[phase4(c) for c in first]), seq(*sgus)])

    if per_seq:
        for s in range(TB // SEQ_PAD):
            conv_out[s] = xp[pl.ds(8 + (s + 1) * SEQ_PAD - (CONV_W - 1), CONV_W - 1), :]
    else:
        conv_out[0] = xp[pl.ds(8 + TB - (CONV_W - 1), CONV_W - 1), :]
        xp[pl.ds(0, 8), :] = xp[pl.ds(TB, 8), :]
        pp[pl.ds(0, 8), :] = pp[pl.ds(TB, 8), :]


def _mixer_weight_specs(nidx):
    z2 = (lambda b, j: (0, 0)) if nidx == 2 else (lambda i: (0, 0))
    z3 = (lambda b, j: (0, 0, 0)) if nidx == 2 else (lambda i: (0, 0, 0))
    return [pl.BlockSpec((CONV_W, 3 * D_A), z2),
            pl.BlockSpec((8, 128), z2),
            pl.BlockSpec((16, D_C), z2),
            pl.BlockSpec((4, SGU_CHUNK, SGU_CHUNK), z3),
            pl.BlockSpec((SGU_CHUNK, D_B), z2),
            pl.BlockSpec((1, NC_PAD), z2),
            pl.BlockSpec((128, D_C), z2),
            pl.BlockSpec((128, D_C), z2),
            pl.BlockSpec((128, D_C), z2)]


def _mixer_prompt(proj, mw):
    nj = SEQ // TB_P
    return pl.pallas_call(
        functools.partial(_mixer_kernel, TB=TB_P, C=GDN_CHUNK, G=GDN_CHUNK, per_seq=False),
        grid=(BATCH, nj),
        in_specs=[pl.BlockSpec((TB_P, NP_IN), lambda b, j: (b * nj + j, 0))] + _mixer_weight_specs(2),
        out_specs=[pl.BlockSpec((TB_P, D_MODEL), lambda b, j: (b * nj + j, 0)),
                   pl.BlockSpec((1, H_A, HD_A, HD_A), lambda b, j: (b, 0, 0, 0)),
                   pl.BlockSpec((1, CONV_W - 1, 3 * D_A), lambda b, j: (b, 0, 0)),
                   pl.BlockSpec((1, H_C, N_C, N_C), lambda b, j: (b, 0, 0, 0))],
        out_shape=[jax.ShapeDtypeStruct((T_PROMPT, D_MODEL), f32),
                   jax.ShapeDtypeStruct((BATCH, H_A, HD_A, HD_A), f32),
                   jax.ShapeDtypeStruct((BATCH, CONV_W - 1, 3 * D_A), f32),
                   jax.ShapeDtypeStruct((BATCH, H_C, N_C, N_C), f32)],
        scratch_shapes=[pltpu.VMEM((TB_P + 8, 3 * D_A), f32), pltpu.VMEM((TB_P + 8, NC_PAD), f32)],
        compiler_params=pltpu.CompilerParams(dimension_semantics=("arbitrary", "arbitrary"),
                                             vmem_limit_bytes=VMEM_LIMIT),
        name="mixer_prompt",
    )(proj, *mw)


def _mixer_sample(proj, s_gdn, s_conv, s_rwkv, pcf, mw, l, prev_states):
    nseq = TB_S // SEQ_PAD
    base = T_PROMPT // TB_S
    n_in = 5 + len(mw)
    lay4 = lambda i: (l, i, 0, 0, 0)
    lay3 = lambda i: (l, i, 0, 0)
    return pl.pallas_call(
        functools.partial(_mixer_kernel, TB=TB_S, C=TB_S, G=SEQ_PAD, per_seq=True, n_alias=len(prev_states)),
        grid=(T_SAMPLE // TB_S,),
        in_specs=[pl.BlockSpec((TB_S, NP_IN), lambda i: (base + i, 0)),
                  pl.BlockSpec((None, nseq, H_A, HD_A, HD_A), lay4),
                  pl.BlockSpec((None, nseq, CONV_W - 1, 3 * D_A), lay3),
                  pl.BlockSpec((None, nseq, H_C, N_C, N_C), lay4),
                  pl.BlockSpec((nseq, 1, NC_PAD), lambda i: (i, 0, 0))] + _mixer_weight_specs(1)
                 + [pl.BlockSpec(memory_space=pl.ANY)] * len(prev_states),
        out_specs=[pl.BlockSpec((TB_S, D_MODEL), lambda i: (i, 0)),
                   pl.BlockSpec((None, nseq, H_A, HD_A, HD_A), lay4),
                   pl.BlockSpec((nseq, CONV_W - 1, 3 * D_A), lambda i: (i, 0, 0)),
                   pl.BlockSpec((None, nseq, H_C, N_C, N_C), lay4),
                   pl.BlockSpec((TB_S, D_B), lambda i: (i, 0))],
        out_shape=[jax.ShapeDtypeStruct((T_SAMPLE, D_MODEL), f32),
                   jax.ShapeDtypeStruct((DEPTH, DEC_BATCH, H_A, HD_A, HD_A), f32),
                   jax.ShapeDtypeStruct((DEC_BATCH, CONV_W - 1, 3 * D_A), f32),
                   jax.ShapeDtypeStruct((DEPTH, DEC_BATCH, H_C, N_C, N_C), f32),
                   jax.ShapeDtypeStruct((T_SAMPLE, D_B), f32)],
        scratch_shapes=[pltpu.VMEM((TB_S + 8, 3 * D_A), f32), pltpu.VMEM((TB_S + 8, NC_PAD), f32)],
        input_output_aliases={n_in + k: o for k, o in enumerate((1, 3)[:len(prev_states)])},
        compiler_params=pltpu.CompilerParams(dimension_semantics=("arbitrary",),
                                             vmem_limit_bytes=VMEM_LIMIT),
        name="mixer_sample",
    )(proj, s_gdn, s_conv, s_rwkv, pcf, *mw, *prev_states)


TM_C = 512


def _outproj_kernel(*refs, split):
    i = pl.program_id(0)
    in_prompt = i < T_PROMPT // TM_C
    if split:
        xp_ref, xs_ref = refs[:2]
        x = jnp.where(in_prompt, xp_ref[...], xs_ref[...])
    else:
        x = refs[0][...]
    mixp_ref, mixs_ref, wout_ref, nw_ref, rwc_ref, tri_ref, rb_ref, xg_ref, cnt_ref, run_ref = refs[-10:]

    @pl.when(i == 0)
    def _():
        run_ref[...] = jnp.zeros(run_ref.shape, f32)

    mix = jnp.where(in_prompt, mixp_ref[...], mixs_ref[...])
    x2 = x + jnp.dot(mix.astype(bf16), wout_ref[0], preferred_element_type=f32)
    xg_ref[:, pl.ds(0, D_MODEL)] = x2
    h2 = _rms(x2, nw_ref[...])
    hh, hl = _split2(h2)
    d = lambda a, b: jnp.dot(a, b, preferred_element_type=f32)
    both = d(hh, rwc_ref[...])
    logits = both[:, :128] + both[:, 128:] + d(hl, rwc_ref[:, pl.ds(0, 128)]) + rb_ref[...]

    lane = lax.broadcasted_iota(i32, logits.shape, 1).astype(f32)
    neg = jnp.float32(-jnp.inf)
    is_g = lane < float(N_GROUPS)
    gl = jnp.where(is_g, logits, neg)
    gmax = jnp.max(gl, axis=-1, keepdims=True)
    gsel = jnp.min(jnp.where(gl == gmax, lane, 128.0), axis=-1, keepdims=True)
    gw = 1.0 / jnp.sum(jnp.where(is_g, jnp.exp(jnp.where(is_g, logits - gmax, 0.0)), 0.0),
                       axis=-1, keepdims=True)
    lo = LANE_E0 + float(EPG) * gsel
    in_grp = (lane >= lo) & (lane < lo + float(EPG))
    el = jnp.where(in_grp, logits, neg)
    t1 = jnp.max(el, axis=-1, keepdims=True)
    i1 = jnp.min(jnp.where(el == t1, lane, 128.0), axis=-1, keepdims=True)
    el2 = jnp.where(lane == i1, neg, el)
    t2 = jnp.max(el2, axis=-1, keepdims=True)
    i2 = jnp.min(jnp.where(el2 == t2, lane, 128.0), axis=-1, keepdims=True)
    e2 = jnp.exp(t2 - t1)
    den = 1.0 + e2
    gates = jnp.where(lane == i1, gw / den, 0.0) + jnp.where(lane == i2, gw * e2 / den, 0.0)

    ea, eb = i1 - lo, i2 - lo
    e_lo, e_hi = jnp.minimum(ea, eb), jnp.maximum(ea, eb)
    bucket = float(N_PAIRS) * gsel + e_lo * (7.0 - e_lo) * 0.5 + (e_hi - e_lo - 1.0)
    onehot = jnp.where(lane == bucket, 1.0, 0.0)
    before = lax.dot_general(tri_ref[...], onehot.astype(bf16), NN, preferred_element_type=f32)
    rank = jnp.sum((before + run_ref[...]) * onehot, axis=-1, keepdims=True)
    run_ref[...] += jnp.sum(onehot, axis=0, keepdims=True)
    cnt_ref[...] = jnp.broadcast_to(run_ref[...], cnt_ref.shape)
    xg_ref[:, pl.ds(D_MODEL, 128)] = (gates + jnp.where(lane == float(LANE_G), bucket, 0.0)
                                      + jnp.where(lane == float(LANE_RANK), rank, 0.0))


def _outproj(xs, mix_p, mix_s, wout, nw, rwh, rwl, rb, l):
    rwc = jnp.concatenate([rwh, rwl], axis=1)
    ids = jnp.arange(TM_C)
    tri = (ids[:, None] > ids[None, :]).astype(bf16)
    row = lambda i: (i, 0)
    fix = lambda i: (0, 0)
    npt = T_PROMPT // TM_C
    p_rows = pl.BlockSpec((TM_C, D_MODEL), lambda i: (jnp.minimum(i, npt - 1), 0))
    s_rows = pl.BlockSpec((TM_C, D_MODEL), lambda i: (jnp.maximum(i - npt, 0), 0))
    split = len(xs) == 2
    return pl.pallas_call(
        functools.partial(_outproj_kernel, split=split),
        grid=(T_ALL // TM_C,),
        in_specs=([p_rows, s_rows] if split else [pl.BlockSpec((TM_C, D_MODEL), row)]) + [
                  p_rows, s_rows,
                  pl.BlockSpec((1, D_MODEL, D_MODEL), lambda i: (l, 0, 0)), pl.BlockSpec((1, D_MODEL), fix),
                  pl.BlockSpec((D_MODEL, 256), fix), pl.BlockSpec((TM_C, TM_C), fix),
                  pl.BlockSpec((1, 128), fix)],
        out_specs=[pl.BlockSpec((TM_C, XG_W), row), pl.BlockSpec((8, 128), fix)],
        out_shape=[jax.ShapeDtypeStruct((T_ALL, XG_W), f32), jax.ShapeDtypeStruct((8, 128), f32)],
        scratch_shapes=[pltpu.VMEM((1, 128), f32)],
        compiler_params=pltpu.CompilerParams(dimension_semantics=("arbitrary",),
                                             vmem_limit_bytes=VMEM_LIMIT),
        name="outproj_router",
    )(*xs, mix_p, mix_s, wout, nw, rwc, tri, rb)


TM_E = 512
NT_E = T_ALL // TM_E + N_GROUPS
T_SORT = NT_E * TM_E
DMA_UNROLL = 8


def _invert_kernel(pos_ref, pad_lo_ref, pad_hi_ref, src_ref):
    for g in range(N_GROUPS):
        def fill(p, c):
            src_ref[p] = T_ALL - 1
            return c
        lax.fori_loop(pad_lo_ref[g], pad_hi_ref[g], fill, 0)

    def body(t, c):
        src_ref[pos_ref[t]] = t
        return c
    lax.fori_loop(0, T_ALL, body, 0, unroll=DMA_UNROLL)


def _invert(pos, pad_lo, pad_hi):
    smem = pl.BlockSpec(memory_space=pltpu.SMEM)
    return pl.pallas_call(
        _invert_kernel,
        in_specs=[smem, smem, smem],
        out_specs=smem,
        out_shape=jax.ShapeDtypeStruct((T_SORT,), i32),
        name="moe_invert",
    )(pos, pad_lo, pad_hi)


def _gather_rows(pos_ref, base, ys_hbm, o_ref, sem):
    _start_row_copies(pos_ref, base, ys_hbm, o_ref, sem, 0, TM_E, inline=False)
    _wait_row_copies(ys_hbm, o_ref, sem)


def _combine_kernel(pos_ref, ys_hbm, o_ref, sem):
    _gather_rows(pos_ref, pl.program_id(0) * TM_E, ys_hbm, o_ref, sem)


def _combine_split_kernel(pos_ref, ys_hbm, op_ref, os_ref, sem):
    i = pl.program_id(0)

    @pl.when(i < T_PROMPT // TM_E)
    def _():
        _gather_rows(pos_ref, i * TM_E, ys_hbm, op_ref, sem)

    @pl.when(i >= T_PROMPT // TM_E)
    def _():
        _gather_rows(pos_ref, i * TM_E, ys_hbm, os_ref, sem)


def _combine(pos, ys, split):
    npt = T_PROMPT // TM_E
    if split:
        out_specs = [pl.BlockSpec((TM_E, D_MODEL), lambda i, pos: (jnp.minimum(i, npt - 1), 0)),
                     pl.BlockSpec((TM_E, D_MODEL), lambda i, pos: (jnp.maximum(i - npt, 0), 0))]
        out_shape = [jax.ShapeDtypeStruct((T_PROMPT, D_MODEL), f32),
                     jax.ShapeDtypeStruct((T_SAMPLE, D_MODEL), f32)]
    else:
        out_specs = pl.BlockSpec((TM_E, D_MODEL), lambda i, pos: (i, 0))
        out_shape = jax.ShapeDtypeStruct((T_ALL, D_MODEL), f32)
    return pl.pallas_call(
        _combine_split_kernel if split else _combine_kernel,
        grid_spec=pltpu.PrefetchScalarGridSpec(
            num_scalar_prefetch=1,
            grid=(T_ALL // TM_E,),
            in_specs=[pl.BlockSpec(memory_space=pl.ANY)],
            out_specs=out_specs,
            scratch_shapes=[pltpu.SemaphoreType.DMA(())]),
        out_shape=out_shape,
        compiler_params=pltpu.CompilerParams(dimension_semantics=("arbitrary",),
                                             vmem_limit_bytes=VMEM_LIMIT),
        name="moe_combine",
    )(pos, ys)


ROWS_E = TM_E // 2
N_XBUF = 3


def _experts_kernel(tg_ref, tv_ref, ti_ref, tu_ref, tw_ref, src_ref, xg_hbm, nw_ref, wg_ref, wu_ref, wd_ref, nf_ref,
                    ys_ref, xbuf, hbuf, ybuf, gsem, *, final_norm):
    del tw_ref
    t = pl.program_id(0)
    e = pl.program_id(1)
    valid = tv_ref[t] == 1
    code = tu_ref[t * EPG + e]
    slot = t % N_XBUF
    ahead = ti_ref[jnp.minimum(t + 2, NT_E - 1)]
    aslot = (t + 2) % N_XBUF
    is_last = ti_ref[jnp.minimum(t + 1, NT_E - 1)] == t

    @pl.when((t == 0) & (e == 0))
    def _():
        _start_row_copies(src_ref, 0, xg_hbm, xbuf.at[0], gsem.at[0], 0, TM_E, inline=False)
        _start_row_copies(src_ref, ti_ref[1] * TM_E, xg_hbm, xbuf.at[1], gsem.at[1], 0, TM_E, inline=False)

    @pl.when(valid & (e == 0))
    def _():
        _wait_row_copies(xg_hbm, xbuf.at[slot], gsem.at[slot])
        x2 = xbuf[slot, :, pl.ds(0, D_MODEL)]
        hbuf[...] = _rms(x2, nw_ref[...]).astype(bf16)
        ybuf[...] = x2

    def expert_step():
        hb = hbuf[...]
        he = (_silu(jnp.dot(hb, wg_ref[0, 0], preferred_element_type=f32))
              * jnp.dot(hb, wu_ref[0, 0], preferred_element_type=f32))
        yd = jnp.dot(he.astype(bf16), wd_ref[0, 0], preferred_element_type=f32)
        g = xbuf[slot, :, pl.ds(D_MODEL, 128)]
        lane = lax.broadcasted_iota(i32, g.shape, 1)
        gcol = jnp.sum(jnp.where(lane == LANE_E0 + EPG * tg_ref[t] + e, g, 0.0), axis=-1, keepdims=True)
        ybuf[...] += gcol * yd

    @pl.when(valid & ((code == 1) | (code == 2)))
    def _():
        _start_row_copies(src_ref, ahead * TM_E, xg_hbm, xbuf.at[aslot], gsem.at[aslot],
                          (code - 1) * ROWS_E, ROWS_E, inline=True, prio=1)
        expert_step()

    @pl.when(valid & (code == 3))
    def _():
        expert_step()

    @pl.when(e == EPG - 1)
    def _():
        y = ybuf[...]
        if final_norm:
            y = _rms(y, nf_ref[...])
        ys_ref[...] = y

    @pl.when(valid & is_last & (e == EPG - 1))
    def _():
        for k in (1, 2):
            s = (t + k) % N_XBUF
            _wait_row_copies(xg_hbm, xbuf.at[s], gsem.at[s])


def _experts(tile_group, tile_valid, tile_idx, tile_used, tile_wexp, src, xg, nw, wg, wu, wd, nf, l, final_norm):
    wsel = lambda t, e, tg, tv, ti, tu, tw, src: (l, tg[t] * EPG + tw[t * EPG + e], 0, 0)
    return pl.pallas_call(
        functools.partial(_experts_kernel, final_norm=final_norm),
        grid_spec=pltpu.PrefetchScalarGridSpec(
            num_scalar_prefetch=6,
            grid=(NT_E, EPG),
            in_specs=[pl.BlockSpec(memory_space=pl.ANY),
                      pl.BlockSpec((1, D_MODEL), lambda t, e, tg, tv, ti, tu, tw, src: (0, 0)),
                      pl.BlockSpec((1, 1, D_MODEL, D_FF_E), wsel),
                      pl.BlockSpec((1, 1, D_MODEL, D_FF_E), wsel),
                      pl.BlockSpec((1, 1, D_FF_E, D_MODEL), wsel),
                      pl.BlockSpec((1, D_MODEL), lambda t, e, tg, tv, ti, tu, tw, src: (0, 0))],
            out_specs=pl.BlockSpec((TM_E, D_MODEL), lambda t, e, tg, tv, ti, tu, tw, src: (ti[t], 0)),
            scratch_shapes=[pltpu.VMEM((N_XBUF, TM_E, XG_W), f32), pltpu.VMEM((TM_E, D_MODEL), bf16),
                            pltpu.VMEM((TM_E, D_MODEL), f32), pltpu.SemaphoreType.DMA((N_XBUF,))]),
        out_shape=jax.ShapeDtypeStruct((T_SORT, D_MODEL), f32),
        compiler_params=pltpu.CompilerParams(dimension_semantics=("arbitrary", "arbitrary"),
                                             vmem_limit_bytes=VMEM_LIMIT),
        name="moe_experts",
    )(tile_group, tile_valid, tile_idx, tile_used, tile_wexp, src, xg, nw, wg, wu, wd, nf)


def _route_meta(xg, cnt_rows):
    b = xg[:, D_MODEL + LANE_G].astype(i32)
    rank = xg[:, D_MODEL + LANE_RANK].astype(i32)
    bcnt = cnt_rows[0, :N_BUCKETS].astype(i32).reshape(N_GROUPS, N_PAIRS)
    cnt = jnp.sum(bcnt, axis=1)
    padded = ((cnt + TM_E - 1) // TM_E) * TM_E
    off_end = jnp.cumsum(padded)
    off = off_end - padded
    bstart = (off[:, None] + jnp.cumsum(bcnt, axis=1) - bcnt).reshape(N_BUCKETS)
    bend = bstart + bcnt.reshape(N_BUCKETS)
    bid = jnp.arange(N_BUCKETS, dtype=i32)
    pos = rank + jnp.sum(jnp.where(b[:, None] == bid[None, :], bstart[None, :], 0), axis=1)
    tile_start = jnp.arange(NT_E, dtype=i32) * TM_E
    tile_group = jnp.minimum(jnp.sum((tile_start[:, None] >= off_end[None, :]).astype(i32), axis=1), N_GROUPS - 1)
    tile_valid = (tile_start < off_end[-1]).astype(i32)
    n_used = off_end[-1] // TM_E
    tile_idx = jnp.minimum(jnp.arange(NT_E, dtype=i32), n_used - 1)
    tile_group = jnp.take(tile_group, tile_idx)
    ts = (tile_idx * TM_E)[:, None]
    overlap = (bstart[None, :] < ts + TM_E) & (bend[None, :] > ts) & (bend > bstart)[None, :]
    pair_has = jnp.array([[int(e in p) for e in range(EPG)] for p in PAIRS] * N_GROUPS, dtype=i32)
    used = jnp.sum(overlap.astype(i32)[:, :, None] * pair_has[None, :, :], axis=1) > 0
    eidx = jnp.arange(EPG, dtype=i32)[None, :]
    prev = lax.cummax(jnp.where(used, eidx, -1), axis=1)
    nxt = lax.cummin(jnp.where(used, eidx, EPG), axis=1, reverse=True)
    wexp = jnp.where(prev >= 0, prev, jnp.minimum(nxt, EPG - 1))
    nth = jnp.cumsum(used.astype(i32), axis=1)
    code = jnp.where(used, jnp.minimum(nth, 3), 0)
    return (pos, tile_group, tile_valid, tile_idx, off + cnt, off_end,
            code.reshape(-1), wexp.reshape(-1))


def _pad_cols(a, n):
    return jnp.pad(a, ((0, 0), (0, n - a.shape[1])))


def _pad_rows(a, n):
    return jnp.pad(a, ((0, n - a.shape[0]), (0, 0)))


def _prep_w_in(w):
    pad_last = lambda a, n: jnp.pad(a, ((0, 0), (0, 0), (0, n - a.shape[-1])))
    c = w[..., 2568:]
    parts = [w[..., 0:2048], pad_last(w[..., 2048:2056], 128), w[..., 2056:2568],
             c[..., 0:768], pad_last(c[..., 768:832], 128), pad_last(c[..., 832:896], 128), c[..., 896:1024]]
    return jnp.concatenate(parts, axis=-1).astype(bf16)


def _prep_mu(mu):
    m = mu[None, :]
    return jnp.concatenate([m[:, 0:768], _pad_cols(m[:, 768:832], 128), _pad_cols(m[:, 832:896], 128),
                            m[:, 896:1024]], axis=1)


def _sgu_mats(sgu_w, sgu_b):
    t = jnp.arange(SGU_CHUNK)
    wm = jnp.where(t[:, None] >= t[None, :], sgu_w, 0.0)
    bias_p = jnp.repeat(jnp.transpose(sgu_b), DH_B, axis=1)
    small = jnp.zeros((4, SEQ_PAD, SEQ_PAD), f32).at[:, SEQ_LEAD:, SEQ_LEAD:].set(wm[:, :DEC_SEQ, :DEC_SEQ])
    eye16 = jnp.eye(TB_S // SEQ_PAD, dtype=f32)
    wm_s = jnp.einsum('ab,hij->haibj', eye16, small).reshape(4, TB_S, TB_S)
    bias_small = jnp.zeros((SEQ_PAD, D_B), f32).at[SEQ_LEAD:].set(bias_p[:DEC_SEQ])
    bias_s = jnp.tile(bias_small, (TB_S // SEQ_PAD, 1))
    return wm.astype(bf16), bias_p, wm_s.astype(bf16), bias_s


def _row(a, n):
    return _pad_cols(a.reshape(1, -1), n)


def kernel(x_prompt, x_sample, state_gdn, state_gdn_conv, state_rwkv, state_rwkv_shift, norm_mix, norm_ffn, norm_final, w_in, gdn_conv_w, gdn_a_log, gdn_dt_bias, gdn_norm_w, sgu_ln_w, sgu_ln_b, sgu_w, sgu_b, rwkv_mu, rwkv_w0, rwkv_w_up, rwkv_a0, rwkv_a_up, rwkv_g_up, rwkv_k_k, rwkv_k_a, rwkv_r_k, rwkv_ln_w, rwkv_ln_b, w_out, router_group_w, router_group_b, router_expert_w, router_expert_b, expert_w_gate, expert_w_up, expert_w_down):
    x = (x_prompt.reshape(T_PROMPT, D_MODEL),
         jnp.pad(x_sample, ((0, 0), (SEQ_LEAD, 0), (0, 0))).reshape(T_SAMPLE, D_MODEL))
    sample_states = ()

    w_pad = _prep_w_in(w_in)
    w_out_b = w_out.astype(bf16)
    wg_b, wu_b, wd_b = expert_w_gate.astype(bf16), expert_w_up.astype(bf16), expert_w_down.astype(bf16)

    outs = {k: [] for k in ('gdn_p', 'conv_p', 'rwkv_p', 'shift_p', 'gdn_s', 'conv_s', 'rwkv_s', 'shift_s', 'cv_s')}
    for l in range(DEPTH):
        last = l == DEPTH - 1
        v128 = jnp.concatenate([
            jnp.pad(gdn_a_log[l].reshape(1, H_A), ((0, 0), (4, 120))),
            jnp.pad(gdn_dt_bias[l].reshape(1, H_A), ((0, 0), (4, 120))),
            gdn_norm_w[l].reshape(1, HD_A), jnp.zeros((5, 128), f32)], axis=0)
        v256 = jnp.concatenate([a.reshape(1, D_C) for a in (
            sgu_ln_w[l], sgu_ln_b[l], rwkv_w0[l], rwkv_a0[l], rwkv_k_k[l], rwkv_k_a[l], rwkv_r_k[l],
            rwkv_ln_w[l], rwkv_ln_b[l])] + [jnp.zeros((7, D_C), f32)], axis=0)
        wm_p, bias_p, wm_s, bias_s = _sgu_mats(sgu_w[l], sgu_b[l])
        common = (gdn_conv_w[l], v128, v256)
        tail = (_prep_mu(rwkv_mu[l]), _pad_rows(rwkv_w_up[l], 128).astype(bf16),
                _pad_rows(rwkv_a_up[l], 128).astype(bf16), rwkv_g_up[l].astype(bf16))
        mw_p = common + (wm_p, bias_p) + tail
        mw_s = common + (wm_s, bias_s) + tail

        if l == 0:
            proj, h = _inproj(x, norm_mix[l].reshape(1, D_MODEL), w_pad, l)
        else:
            proj, h, x_tok = _inproj_gather(pos, ys, norm_mix[l].reshape(1, D_MODEL), w_pad, l)
            x = (x_tok,)
        pcf = _mm(state_rwkv_shift[l], w_pad[l, :, OFF_C:]).reshape(DEC_BATCH, 1, NC_PAD)
        mix_p, gdn_p, conv_p, rwkv_p = _mixer_prompt(proj, mw_p)
        mix_s, gdn_s, conv_s, rwkv_s, cv_s = _mixer_sample(
            proj, state_gdn, state_gdn_conv, state_rwkv, pcf, mw_s, l, sample_states)
        sample_states = (gdn_s, rwkv_s)

        rw = _pad_cols(jnp.concatenate([router_group_w[l], router_expert_w[l]], axis=1), 128)
        rwh = rw.astype(bf16)
        rwl = (rw - rwh.astype(f32)).astype(bf16)
        rb = _row(jnp.concatenate([router_group_b[l], router_expert_b[l]]), 128)
        nw_ffn = norm_ffn[l].reshape(1, D_MODEL)
        xg, cnt_rows = _outproj(x, mix_p, mix_s, w_out_b, nw_ffn, rwh, rwl, rb, l)
        pos, tile_group, tile_valid, tile_idx, pad_lo, pad_hi, tile_used, tile_wexp = _route_meta(xg, cnt_rows)
        src = _invert(pos, pad_lo, pad_hi)
        ys = _experts(tile_group, tile_valid, tile_idx, tile_used, tile_wexp, src, xg, nw_ffn, wg_b, wu_b, wd_b,
                      norm_final.reshape(1, D_MODEL), l, final_norm=last)
        if last:
            x = _combine(pos, ys, split=True)

        outs['gdn_p'].append(gdn_p)
        outs['conv_p'].append(conv_p)
        outs['rwkv_p'].append(rwkv_p)
        outs['shift_p'].append(h[SEQ // SEQ_PAD - 1:T_PROMPT // SEQ_PAD:SEQ // SEQ_PAD])
        outs['conv_s'].append(conv_s)
        outs['shift_s'].append(h[T_PROMPT // SEQ_PAD:])
        outs['cv_s'].append(cv_s.reshape(DEC_BATCH, SEQ_PAD, D_B)[:, SEQ_LEAD:])

    y_prompt = x[0].reshape(BATCH, SEQ, D_MODEL)
    y_sample = x[1].reshape(DEC_BATCH, SEQ_PAD, D_MODEL)[:, SEQ_LEAD:]
    st = lambda k: jnp.stack(outs[k])
    return (y_prompt, y_sample, st('gdn_p'), st('conv_p'), st('rwkv_p'), st('shift_p'),
            sample_states[0], st('conv_s'), sample_states[1], st('shift_s'), st('cv_s'))
```
